```python
import math
import jax, jax.numpy as jnp
from jax import lax
import numpy as np

D_MODEL = 1024
BATCH = 8
SEQ = 2048
DEPTH = 1

CHUNK = 64
Q_BLOCK = 128
HEAD_DIM = 64
N_HEADS_DIFF = D_MODEL // 256
DIFF_V_DIM = 2 * HEAD_DIM
N_HEADS_CHUNK = D_MODEL // 128
LEFT_CHUNKS = 8
BAND = (LEFT_CHUNKS + 1) * CHUNK
REL_CLIP = 128
N_MEM = 256
N_HEADS_MEM = 4
MEM_HEAD_DIM = D_MODEL // N_HEADS_MEM
D_FF = 4 * D_MODEL
ROPE_THETA = 10000.0
LN_EPS = 1e-5
NEG_INF = -1e30
DEEPNORM_ALPHA = (2.0 * DEPTH) ** 0.25
DEEPNORM_BETA = (8.0 * DEPTH) ** -0.25

W_DIFF_QK = N_HEADS_DIFF * 2 * HEAD_DIM
W_DIFF_V = N_HEADS_DIFF * DIFF_V_DIM
W_CHUNK = N_HEADS_CHUNK * HEAD_DIM
MIX_WIDTH = W_DIFF_V + W_CHUNK
IN_SPLITS = [W_DIFF_QK, W_DIFF_QK, W_DIFF_V, W_CHUNK, W_CHUNK, W_CHUNK]
IN_WIDTH = sum(IN_SPLITS)

kernel_name = "hybrid_diffattn_chunkrel_stream_layer"


def layer_norm(x, g, b):
    xf = x.astype(jnp.float32)
    mu = jnp.mean(xf, axis=-1, keepdims=True)
    var = jnp.mean(jnp.square(xf - mu), axis=-1, keepdims=True)
    y = (xf - mu) * lax.rsqrt(var + LN_EPS) * g.astype(jnp.float32) + b.astype(jnp.float32)
    return y.astype(x.dtype)


def rms_norm(x, g):
    xf = x.astype(jnp.float32)
    y = xf * lax.rsqrt(jnp.mean(jnp.square(xf), axis=-1, keepdims=True) + LN_EPS)
    return (y * g.astype(jnp.float32)).astype(x.dtype)


def rope(x, positions):
    d = x.shape[-1]
    inv_freq = 1.0 / (ROPE_THETA ** (jnp.arange(0, d, 2, dtype=jnp.float32) / d))
    ang = positions.astype(jnp.float32)[..., None] * inv_freq
    ang = ang.reshape(ang.shape[:2] + (1,) * (x.ndim - 3) + ang.shape[-1:])
    cos, sin = jnp.cos(ang), jnp.sin(ang)
    xf = x.astype(jnp.float32)
    x1, x2 = xf[..., : d // 2], xf[..., d // 2:]
    return jnp.concatenate([x1 * cos - x2 * sin, x2 * cos + x1 * sin], axis=-1).astype(x.dtype)


def diff_attention(q, k, v, lam_vecs, lambda_init, subln_g):
    B, S, H, _, d = q.shape
    nb = S // Q_BLOCK
    lam = (jnp.exp(jnp.sum(lam_vecs[0].astype(jnp.float32) * lam_vecs[1].astype(jnp.float32)))
           - jnp.exp(jnp.sum(lam_vecs[2].astype(jnp.float32) * lam_vecs[3].astype(jnp.float32)))
           + lambda_init)
    q = q * (d ** -0.5)
    q_blocks = jnp.moveaxis(q.reshape(B, nb, Q_BLOCK, H, 2, d), 1, 0)
    q_chunk = (jnp.arange(S) // CHUNK).reshape(nb, Q_BLOCK)
    k_chunk = jnp.arange(S) // CHUNK

    def one_block(args):
        qb, qc = args
        s = jnp.einsum('bqhtd,bkhtd->bthqk', qb, k).astype(jnp.float32)
        allowed = k_chunk[None, :] <= qc[:, None]
        p = jax.nn.softmax(jnp.where(allowed, s, NEG_INF), axis=-1)
        a = p[:, 0] - lam * p[:, 1]
        return jnp.einsum('bhqk,bkhe->bqhe', a.astype(v.dtype), v)

    o = lax.map(one_block, (q_blocks, q_chunk))
    o = jnp.moveaxis(o, 0, 1).reshape(B, S, H, v.shape[-1])
    return rms_norm(o, subln_g) * (1.0 - lambda_init)


def band_chunks(t, nc):
    B, S, H, d = t.shape
    tc = t.reshape(B, nc, CHUNK, H, d)
    tp = jnp.pad(tc, ((0, 0), (LEFT_CHUNKS, 0), (0, 0), (0, 0), (0, 0)))
    return jnp.concatenate([tp[:, j:j + nc] for j in range(LEFT_CHUNKS + 1)], axis=2)


def chunk_rel_attention(q, k, v, rel_bias):
    B, S, H, d = q.shape
    nc = S // CHUNK
    qc = (q * (d ** -0.5)).reshape(B, nc, CHUNK, H, d)
    kband = band_chunks(k, nc)
    vband = band_chunks(v, nc)
    s = jnp.einsum('bnqhd,bnkhd->bnhqk', qc, kband).astype(jnp.float32)
    qi = np.arange(CHUNK)[:, None]
    kj = np.arange(BAND)[None, :]
    rel_idx = np.clip(qi + LEFT_CHUNKS * CHUNK - kj, -REL_CLIP, REL_CLIP) + REL_CLIP
    bias = rel_bias[:, rel_idx].astype(jnp.float32)
    key_pos = (jnp.arange(nc)[:, None] - LEFT_CHUNKS) * CHUNK + jnp.arange(BAND)[None, :]
    valid = key_pos >= 0
    s = jnp.where(valid[None, :, None, None, :], s + bias[None, None], NEG_INF)
    p = jax.nn.softmax(s, axis=-1)
    o = jnp.einsum('bnhqk,bnkhd->bnqhd', p.astype(v.dtype), vband)
    return o.reshape(B, S, H, d)


def hybrid_mixer(x, positions, w_in, lam_vecs, lambda_init, subln_g, rel_bias, w_o):
    B, S, _ = x.shape
    h = x @ w_in
    qa, ka, va, qb, kb, vb = jnp.split(h, np.cumsum(IN_SPLITS)[:-1], axis=-1)
    qa = rope(qa.reshape(B, S, N_HEADS_DIFF, 2, HEAD_DIM), positions)
    ka = rope(ka.reshape(B, S, N_HEADS_DIFF, 2, HEAD_DIM), positions)
    va = va.reshape(B, S, N_HEADS_DIFF, DIFF_V_DIM)
    ya = diff_attention(qa, ka, va, lam_vecs, lambda_init, subln_g)
    yb = chunk_rel_attention(qb.reshape(B, S, N_HEADS_CHUNK, HEAD_DIM),
                             kb.reshape(B, S, N_HEADS_CHUNK, HEAD_DIM),
                             vb.reshape(B, S, N_HEADS_CHUNK, HEAD_DIM), rel_bias)
    y = jnp.concatenate([ya.reshape(B, S, W_DIFF_V), yb.reshape(B, S, W_CHUNK)], axis=-1)
    return y @ w_o


def memory_cross_attention(x, mem, w_mq, w_mk, w_mv, w_mo):
    B, S, _ = x.shape
    M = mem.shape[1]
    q = (x @ w_mq).reshape(B, S, N_HEADS_MEM, MEM_HEAD_DIM) * (MEM_HEAD_DIM ** -0.5)
    k = (mem @ w_mk).reshape(B, M, N_HEADS_MEM, MEM_HEAD_DIM)
    v = (mem @ w_mv).reshape(B, M, N_HEADS_MEM, MEM_HEAD_DIM)
    s = jnp.einsum('bshd,bmhd->bhsm', q, k).astype(jnp.float32)
    p = jax.nn.softmax(s, axis=-1)
    o = jnp.einsum('bhsm,bmhd->bshd', p.astype(v.dtype), v).reshape(B, S, D_MODEL)
    return o @ w_mo


def sq_relu_mlp(x, w_up, w_down):
    return jnp.square(jax.nn.relu(x @ w_up)) @ w_down


def setup_inputs(seed: int = 0) -> dict:
    key = jax.random.key(seed)
    ks = jax.random.split(key, 24)
    f32 = jnp.float32
    beta = DEEPNORM_BETA
    x = jax.random.normal(ks[0], (BATCH, SEQ, D_MODEL), f32)
    mem = jax.random.normal(ks[1], (BATCH, N_MEM, D_MODEL), f32)
    start = jax.random.randint(ks[2], (BATCH, 1), 0, 64, dtype=jnp.int32) * CHUNK
    positions = (start + jnp.arange(SEQ, dtype=jnp.int32)[None, :]).astype(jnp.int32)
    col_scale = jnp.concatenate([
        jnp.ones((2 * W_DIFF_QK,), f32), jnp.full((W_DIFF_V,), beta, f32),
        jnp.ones((2 * W_CHUNK,), f32), jnp.full((W_CHUNK,), beta, f32)])
    w_in = jax.random.normal(ks[3], (DEPTH, D_MODEL, IN_WIDTH), f32) * (D_MODEL ** -0.5) * col_scale
    diff_lambda = jax.random.normal(ks[4], (DEPTH, 4, HEAD_DIM), f32) * 0.1
    subln_g = 1.0 + 0.02 * jax.random.normal(ks[5], (DEPTH, DIFF_V_DIM), f32)
    rel_bias = 0.2 * jax.random.normal(ks[6], (DEPTH, N_HEADS_CHUNK, 2 * REL_CLIP + 1), f32)
    w_o = jax.random.normal(ks[7], (DEPTH, MIX_WIDTH, D_MODEL), f32) * (MIX_WIDTH ** -0.5) * beta
    ln1_g = 1.0 + 0.02 * jax.random.normal(ks[8], (DEPTH, D_MODEL), f32)
    ln1_b = 0.02 * jax.random.normal(ks[9], (DEPTH, D_MODEL), f32)
    w_mq = jax.random.normal(ks[10], (DEPTH, D_MODEL, D_MODEL), f32) * (D_MODEL ** -0.5)
    w_mk = jax.random.normal(ks[11], (DEPTH, D_MODEL, D_MODEL), f32) * (D_MODEL ** -0.5)
    w_mv = jax.random.normal(ks[12], (DEPTH, D_MODEL, D_MODEL), f32) * (D_MODEL ** -0.5) * beta
    w_mo = jax.random.normal(ks[13], (DEPTH, D_MODEL, D_MODEL), f32) * (D_MODEL ** -0.5) * beta
    ln2_g = 1.0 + 0.02 * jax.random.normal(ks[14], (DEPTH, D_MODEL), f32)
    ln2_b = 0.02 * jax.random.normal(ks[15], (DEPTH, D_MODEL), f32)
    w_up = jax.random.normal(ks[16], (DEPTH, D_MODEL, D_FF), f32) * (D_MODEL ** -0.5) * beta
    w_down = jax.random.normal(ks[17], (DEPTH, D_FF, D_MODEL), f32) * (D_FF ** -0.5) * beta
    ln3_g = 1.0 + 0.02 * jax.random.normal(ks[18], (DEPTH, D_MODEL), f32)
    ln3_b = 0.02 * jax.random.normal(ks[19], (DEPTH, D_MODEL), f32)
    return {"x": x, "mem": mem, "positions": positions, "w_in": w_in,
            "diff_lambda": diff_lambda, "subln_g": subln_g, "rel_bias": rel_bias,
            "w_o": w_o, "ln1_g": ln1_g, "ln1_b": ln1_b,
            "w_mq": w_mq, "w_mk": w_mk, "w_mv": w_mv, "w_mo": w_mo,
            "ln2_g": ln2_g, "ln2_b": ln2_b, "w_up": w_up, "w_down": w_down,
            "ln3_g": ln3_g, "ln3_b": ln3_b}


def reference(x, mem, positions, w_in, diff_lambda, subln_g, rel_bias, w_o, ln1_g, ln1_b,
              w_mq, w_mk, w_mv, w_mo, ln2_g, ln2_b, w_up, w_down, ln3_g, ln3_b):
    alpha = DEEPNORM_ALPHA
    for l in range(DEPTH):
        lambda_init = 0.8 - 0.6 * math.exp(-0.3 * l)
        y = hybrid_mixer(x, positions, w_in[l], diff_lambda[l], lambda_init, subln_g[l],
                         rel_bias[l], w_o[l])
        x = layer_norm(alpha * x + y, ln1_g[l], ln1_b[l])
        y = memory_cross_attention(x, mem, w_mq[l], w_mk[l], w_mv[l], w_mo[l])
        x = layer_norm(alpha * x + y, ln2_g[l], ln2_b[l])
        y = sq_relu_mlp(x, w_up[l], w_down[l])
        x = layer_norm(alpha * x + y, ln3_g[l], ln3_b[l])
    return x
```

```python
import functools
import math

import jax
import jax.numpy as jnp
import numpy as np
from jax import lax
from jax.experimental import pallas as pl
from jax.experimental.pallas import tpu as pltpu

D_MODEL = 1024
CHUNK = 64
HEAD_DIM = 64
N_HEADS_DIFF = 4
DIFF_V_DIM = 128
N_HEADS_CHUNK = 8
LEFT_CHUNKS = 8
REL_CLIP = 128
N_HEADS_MEM = 4
MEM_HEAD_DIM = 256
D_FF = 4096
ROPE_THETA = 10000.0
LN_EPS = 1e-5
NEG_INF = -1e30
DEPTH = 1
DEEPNORM_ALPHA = (2.0 * DEPTH) ** 0.25
IN_WIDTH = 3072

LANES = 128
VMEM_LIMIT = 56 * 1024 * 1024

F32 = jnp.float32
BF16 = jnp.bfloat16

QA_BLK, KA_BLK, VA_BLK, QB_BLK, KB_BLK, VB_BLK = 0, 4, 8, 12, 16, 20


def _params(sem):
    return pltpu.CompilerParams(dimension_semantics=sem, vmem_limit_bytes=VMEM_LIMIT)


def _dot(a, b):
    return jnp.dot(a, b, preferred_element_type=F32)


def _dot_nt(a, b):
    return lax.dot_general(a, b, (((1,), (1,)), ((), ())), preferred_element_type=F32)


def _layer_norm(z, g, b):
    mu = jnp.mean(z, axis=-1, keepdims=True)
    zc = z - mu
    var = jnp.mean(zc * zc, axis=-1, keepdims=True)
    return zc * lax.rsqrt(var + LN_EPS) * g + b


def _lam_kernel(dl_ref, o_ref, *, lambda_init):
    dl = dl_ref[...]
    s1 = jnp.sum(dl[0:1, :] * dl[1:2, :], axis=1, keepdims=True)
    s2 = jnp.sum(dl[2:3, :] * dl[3:4, :], axis=1, keepdims=True)
    lam = jnp.exp(s1) - jnp.exp(s2) + lambda_init
    o_ref[...] = jnp.broadcast_to(lam, o_ref.shape)


def _lam_call(diff_lambda, lambda_init):
    return pl.pallas_call(
        functools.partial(_lam_kernel, lambda_init=lambda_init),
        out_shape=jax.ShapeDtypeStruct((1, LANES), F32),
        name="lam",
    )(diff_lambda)


def _inproj_kernel(pos_ref, invf_ref, x_ref, w_ref, o_ref):
    xb = x_ref[...].astype(BF16)
    ang = pos_ref[...].astype(F32) * invf_ref[...]
    cos = jnp.cos(ang)
    sin = jnp.sin(ang)
    lane = lax.broadcasted_iota(jnp.int32, ang.shape, 1)
    upper = (lane % HEAD_DIM) >= (HEAD_DIM // 2)
    sin_up = jnp.where(upper, sin, 0.0)
    sin_lo = jnp.where(upper, 0.0, -sin)
    n_chunks = IN_WIDTH // 512
    for c in range(n_chunks):
        h = _dot(xb, w_ref[:, c * 512:(c + 1) * 512])
        if c < 2:
            slabs = []
            for s in range(4):
                sl = h[:, s * LANES:(s + 1) * LANES]
                slabs.append(sl * cos
                             + pltpu.roll(sl, HEAD_DIM // 2, 1) * sin_up
                             + pltpu.roll(sl, LANES - HEAD_DIM // 2, 1) * sin_lo)
            h = jnp.concatenate(slabs, axis=1)
        if c == 0 or c == 3:
            h = h * (HEAD_DIM ** -0.5)
        o_ref[:, c * 512:(c + 1) * 512] = h.astype(BF16)


def _inproj_call(pos, invf, x2d, w_in, tm):
    T = x2d.shape[0]
    return pl.pallas_call(
        _inproj_kernel,
        grid=(T // tm,),
        in_specs=[
            pl.BlockSpec((tm, 1), lambda i: (i, 0)),
            pl.BlockSpec((1, LANES), lambda i: (0, 0)),
            pl.BlockSpec((tm, D_MODEL), lambda i: (i, 0)),
            pl.BlockSpec((D_MODEL, IN_WIDTH), lambda i: (0, 0)),
        ],
        out_specs=pl.BlockSpec((tm, IN_WIDTH), lambda i: (i, 0)),
        out_shape=jax.ShapeDtypeStruct((T, IN_WIDTH), BF16),
        compiler_params=_params(("arbitrary",)),
        name="inproj",
    )(pos, invf, x2d, w_in)


def _stack_halves(q):
    lane = lax.broadcasted_iota(jnp.int32, q.shape, 1)
    zero = jnp.zeros_like(q)
    return jnp.concatenate([jnp.where(lane < HEAD_DIM, q, zero),
                            jnp.where(lane >= HEAD_DIM, q, zero)], axis=0)


def _diffattn_kernel(lam_ref, g_ref, q_ref, k_ref, v_ref, o_ref, *, tq, lambda_init):
    i = pl.program_id(2)
    qq = _stack_halves(q_ref[0])

    def update(carry, s, vblk):
        m, l, acc = carry
        m_new = jnp.maximum(m, jnp.max(s, axis=1, keepdims=True))
        alpha = jnp.exp(m - m_new)
        p = jnp.exp(s - m_new)
        l = alpha * l + jnp.sum(p, axis=1, keepdims=True)
        acc = alpha * acc + _dot(p.astype(BF16), vblk)
        return m_new, l, acc

    def body(j, carry):
        start = pl.multiple_of(j * tq, tq)
        s = _dot_nt(qq, k_ref[0, pl.ds(start, tq), :])
        return update(carry, s, v_ref[0, pl.ds(start, tq), :])

    init = (jnp.full((2 * tq, 1), NEG_INF, F32), jnp.zeros((2 * tq, 1), F32),
            jnp.zeros((2 * tq, DIFF_V_DIM), F32))
    carry = lax.fori_loop(0, i, body, init)

    start = pl.multiple_of(i * tq, tq)
    s = _dot_nt(qq, k_ref[0, pl.ds(start, tq), :])
    row = lax.broadcasted_iota(jnp.int32, s.shape, 0)
    col = lax.broadcasted_iota(jnp.int32, s.shape, 1)
    allowed = (col // CHUNK) <= ((row % tq) // CHUNK)
    s = jnp.where(allowed, s, NEG_INF)
    _, l, acc = update(carry, s, v_ref[0, pl.ds(start, tq), :])

    o = acc / l
    o = o[:tq] - lam_ref[...] * o[tq:]
    y = o * lax.rsqrt(jnp.mean(o * o, axis=-1, keepdims=True) + LN_EPS) * g_ref[...]
    o_ref[0] = (y * (1.0 - lambda_init)).astype(o_ref.dtype)


def _diffattn_call(lam, g, h3, tq, lambda_init):
    B, S, _ = h3.shape
    return pl.pallas_call(
        functools.partial(_diffattn_kernel, tq=tq, lambda_init=lambda_init),
        grid=(B, N_HEADS_DIFF, S // tq),
        in_specs=[
            pl.BlockSpec((1, LANES), lambda b, h, i: (0, 0)),
            pl.BlockSpec((1, DIFF_V_DIM), lambda b, h, i: (0, 0)),
            pl.BlockSpec((1, tq, LANES), lambda b, h, i: (b, i, QA_BLK + h)),
            pl.BlockSpec((1, S, LANES), lambda b, h, i: (b, 0, KA_BLK + h)),
            pl.BlockSpec((1, S, LANES), lambda b, h, i: (b, 0, VA_BLK + h)),
        ],
        out_specs=pl.BlockSpec((1, tq, LANES), lambda b, h, i: (b, i, h)),
        out_shape=jax.ShapeDtypeStruct((B, S, N_HEADS_DIFF * DIFF_V_DIM), BF16),
        compiler_params=_params(("arbitrary", "arbitrary", "arbitrary")),
        name="diffattn",
    )(lam, g, h3, h3, h3)


CA_TQ = 256
CA_NKB = 3


def _chunk_bias_table(rel_bias):
    r = np.arange(CA_TQ)[:, None]
    koff = np.arange(CA_NKB * CA_TQ)[None, :] - (CA_NKB - 1) * CA_TQ
    rel_idx = np.clip(r - koff, -REL_CLIP, REL_CLIP) + REL_CLIP
    qc = r // CHUNK
    kc = np.floor_divide(koff, CHUNK)
    allowed = (kc <= qc) & (kc >= qc - LEFT_CHUNKS)
    bias = rel_bias[:, rel_idx].astype(F32)
    bias = jnp.where(allowed[None], bias, NEG_INF)
    return bias.reshape(N_HEADS_CHUNK // 2, 2 * CA_TQ, CA_NKB * CA_TQ)


def _chunkattn_kernel(bias_ref, q_ref, k_ref, v_ref, o_ref):
    i = pl.program_id(2)
    qq = _stack_halves(q_ref[0])
    s_blocks, v_blocks = [], []
    for j in range(CA_NKB):
        kb = i - (CA_NKB - 1) + j
        start = pl.multiple_of(jnp.maximum(kb, 0) * CA_TQ, CA_TQ)
        s = _dot_nt(qq, k_ref[0, pl.ds(start, CA_TQ), :])
        s = s + bias_ref[0, :, j * CA_TQ:(j + 1) * CA_TQ]
        if j < CA_NKB - 1:
            s = s + jnp.where(kb < 0, NEG_INF, 0.0)
        s_blocks.append(s)
        v_blocks.append(v_ref[0, pl.ds(start, CA_TQ), :])
    m = s_blocks[0].max(axis=1, keepdims=True)
    for s in s_blocks[1:]:
        m = jnp.maximum(m, s.max(axis=1, keepdims=True))
    l = None
    acc = None
    for s, vblk in zip(s_blocks, v_blocks):
        p = jnp.exp(s - m)
        ps = jnp.sum(p, axis=1, keepdims=True)
        pv = _dot(p.astype(BF16), vblk)
        l = ps if l is None else l + ps
        acc = pv if acc is None else acc + pv
    o = acc / l
    lane = lax.broadcasted_iota(jnp.int32, (CA_TQ, LANES), 1)
    o_ref[0] = jnp.where(lane < HEAD_DIM, o[:CA_TQ], o[CA_TQ:]).astype(o_ref.dtype)


def _chunkattn_call(bias, h3):
    B, S, _ = h3.shape
    npairs = N_HEADS_CHUNK // 2
    return pl.pallas_call(
        _chunkattn_kernel,
        grid=(npairs, B, S // CA_TQ),
        in_specs=[
            pl.BlockSpec((1, 2 * CA_TQ, CA_NKB * CA_TQ), lambda p, b, i: (p, 0, 0)),
            pl.BlockSpec((1, CA_TQ, LANES), lambda p, b, i: (b, i, QB_BLK + p)),
            pl.BlockSpec((1, S, LANES), lambda p, b, i: (b, 0, KB_BLK + p)),
            pl.BlockSpec((1, S, LANES), lambda p, b, i: (b, 0, VB_BLK + p)),
        ],
        out_specs=pl.BlockSpec((1, CA_TQ, LANES), lambda p, b, i: (b, i, p)),
        out_shape=jax.ShapeDtypeStruct((B, S, N_HEADS_CHUNK * HEAD_DIM), BF16),
        compiler_params=_params(("arbitrary", "arbitrary", "arbitrary")),
        name="chunkattn",
    )(bias, h3, h3, h3)


def _outproj_kernel(x_ref, ya_ref, yb_ref, wo_ref, g_ref, b_ref, wq_ref, x1_ref, q_ref):
    half = ya_ref.shape[1]
    y = _dot(ya_ref[...], wo_ref[0:half, :]) + _dot(yb_ref[...], wo_ref[half:, :])
    x1 = _layer_norm(DEEPNORM_ALPHA * x_ref[...] + y, g_ref[...], b_ref[...])
    x1_ref[...] = x1
    q = _dot(x1.astype(BF16), wq_ref[...]) * (MEM_HEAD_DIM ** -0.5)
    q_ref[...] = q.astype(BF16)


def _outproj_call(x2d, ya, yb, w_o, g, b, w_mq, tm):
    T = x2d.shape[0]
    half = ya.shape[1]
    row = lambda i: (i, 0)
    fixed = lambda i: (0, 0)
    return pl.pallas_call(
        _outproj_kernel,
        grid=(T // tm,),
        in_specs=[
            pl.BlockSpec((tm, D_MODEL), row),
            pl.BlockSpec((tm, half), row),
            pl.BlockSpec((tm, half), row),
            pl.BlockSpec((D_MODEL, D_MODEL), fixed),
            pl.BlockSpec((1, D_MODEL), fixed),
            pl.BlockSpec((1, D_MODEL), fixed),
            pl.BlockSpec((D_MODEL, D_MODEL), fixed),
        ],
        out_specs=[pl.BlockSpec((tm, D_MODEL), row), pl.BlockSpec((tm, D_MODEL), row)],
        out_shape=[jax.ShapeDtypeStruct((T, D_MODEL), F32), jax.ShapeDtypeStruct((T, D_MODEL), BF16)],
        compiler_params=_params(("arbitrary",)),
        name="outproj",
    )(x2d, ya, yb, w_o, g, b, w_mq)


def _memkv_kernel(mem_ref, wk_ref, wv_ref, k_ref, v_ref):
    mb = mem_ref[0].astype(BF16)
    k_ref[0] = _dot(mb, wk_ref[...]).astype(BF16)
    v_ref[0] = _dot(mb, wv_ref[...]).astype(BF16)


def _memkv_call(mem, w_mk, w_mv):
    B, M, _ = mem.shape
    blk = pl.BlockSpec((1, M, D_MODEL), lambda b: (b, 0, 0))
    fixed = pl.BlockSpec((D_MODEL, D_MODEL), lambda b: (0, 0))
    return pl.pallas_call(
        _memkv_kernel,
        grid=(B,),
        in_specs=[blk, fixed, fixed],
        out_specs=[blk, blk],
        out_shape=[jax.ShapeDtypeStruct((B, M, D_MODEL), BF16)] * 2,
        compiler_params=_params(("arbitrary",)),
        name="memkv",
    )(mem, w_mk, w_mv)


def _memattn_kernel(x1_ref, q_ref, k_ref, v_ref, wo_ref, g_ref, b_ref, x2_ref):
    outs = []
    for h in range(N_HEADS_MEM):
        sl = slice(h * MEM_HEAD_DIM, (h + 1) * MEM_HEAD_DIM)
        s = _dot_nt(q_ref[0, :, sl], k_ref[0, :, sl])
        m = jnp.max(s, axis=1, keepdims=True)
        p = jnp.exp(s - m)
        l = jnp.sum(p, axis=1, keepdims=True)
        outs.append((_dot(p.astype(BF16), v_ref[0, :, sl]) / l).astype(BF16))
    o = jnp.concatenate(outs, axis=1)
    y = _dot(o, wo_ref[...])
    x2_ref[0] = _layer_norm(DEEPNORM_ALPHA * x1_ref[0] + y, g_ref[...], b_ref[...])


def _memattn_call(x1, q, k, v, w_mo, g, b, tm):
    B, S, _ = x1.shape
    M = k.shape[1]
    row = pl.BlockSpec((1, tm, D_MODEL), lambda bb, i: (bb, i, 0))
    kv = pl.BlockSpec((1, M, D_MODEL), lambda bb, i: (bb, 0, 0))
    vec = pl.BlockSpec((1, D_MODEL), lambda bb, i: (0, 0))
    return pl.pallas_call(
        _memattn_kernel,
        grid=(B, S // tm),
        in_specs=[row, row, kv, kv, pl.BlockSpec((D_MODEL, D_MODEL), lambda bb, i: (0, 0)), vec, vec],
        out_specs=row,
        out_shape=jax.ShapeDtypeStruct((B, S, D_MODEL), F32),
        compiler_params=_params(("arbitrary", "arbitrary")),
        name="memattn",
    )(x1, q, k, v, w_mo, g, b)


def _mlp_kernel(x_ref, wu_ref, wd_ref, g_ref, b_ref, o_ref, *, ff_chunk):
    x = x_ref[...]
    xb = x.astype(BF16)
    y = None
    for c in range(D_FF // ff_chunk):
        sl = slice(c * ff_chunk, (c + 1) * ff_chunk)
        h = jnp.maximum(_dot(xb, wu_ref[:, sl]), 0.0)
        part = _dot((h * h).astype(BF16), wd_ref[sl, :])
        y = part if y is None else y + part
    o_ref[...] = _layer_norm(DEEPNORM_ALPHA * x + y, g_ref[...], b_ref[...])


def _mlp_call(x2d, w_up, w_down, g, b, tm, ff_chunk):
    T = x2d.shape[0]
    row = pl.BlockSpec((tm, D_MODEL), lambda i: (i, 0))
    vec = pl.BlockSpec((1, D_MODEL), lambda i: (0, 0))
    return pl.pallas_call(
        functools.partial(_mlp_kernel, ff_chunk=ff_chunk),
        grid=(T // tm,),
        in_specs=[row,
                  pl.BlockSpec((D_MODEL, D_FF), lambda i: (0, 0), pipeline_mode=pl.Buffered(1)),
                  pl.BlockSpec((D_FF, D_MODEL), lambda i: (0, 0), pipeline_mode=pl.Buffered(1)),
                  vec, vec],
        out_specs=row,
        out_shape=jax.ShapeDtypeStruct((T, D_MODEL), F32),
        compiler_params=_params(("arbitrary",)),
        name="mlp",
    )(x2d, w_up, w_down, g, b)


def kernel(x, mem, positions, w_in, diff_lambda, subln_g, rel_bias, w_o, ln1_g, ln1_b,
           w_mq, w_mk, w_mv, w_mo, ln2_g, ln2_b, w_up, w_down, ln3_g, ln3_b):
    B, S, D = x.shape
    T = B * S
    depth = w_in.shape[0]
    assert depth == DEPTH and D == D_MODEL and S % CA_TQ == 0
    inv_freq = 1.0 / (ROPE_THETA ** (jnp.arange(0, HEAD_DIM, 2, dtype=F32) / HEAD_DIM))
    invf = jnp.tile(inv_freq, LANES // (HEAD_DIM // 2)).reshape(1, LANES)
    pos = positions.reshape(T, 1)
    vec = lambda a: a.reshape(1, -1)

    for l in range(depth):
        lambda_init = 0.8 - 0.6 * math.exp(-0.3 * l)
        lam = _lam_call(diff_lambda[l], lambda_init)
        h = _inproj_call(pos, invf, x.reshape(T, D), w_in[l].astype(BF16), tm=512)
        h3 = h.reshape(B, S, IN_WIDTH)
        ya = _diffattn_call(lam, vec(subln_g[l]), h3, tq=256, lambda_init=lambda_init)
        yb = _chunkattn_call(_chunk_bias_table(rel_bias[l]), h3)
        x1, qm = _outproj_call(x.reshape(T, D), ya.reshape(T, -1), yb.reshape(T, -1),
                               w_o[l].astype(BF16), vec(ln1_g[l]), vec(ln1_b[l]),
                               w_mq[l].astype(BF16), tm=512)
        km, vm = _memkv_call(mem, w_mk[l].astype(BF16), w_mv[l].astype(BF16))
        x2 = _memattn_call(x1.reshape(B, S, D), qm.reshape(B, S, D), km, vm,
                           w_mo[l].astype(BF16), vec(ln2_g[l]), vec(ln2_b[l]), tm=512)
        out = _mlp_call(x2.reshape(T, D), w_up[l].astype(BF16), w_down[l].astype(BF16),
                        vec(ln3_g[l]), vec(ln3_b[l]), tm=512, ff_chunk=1024)
        x = out.reshape(B, S, D)
    return x
```

```python
import functools
import math

import jax
import jax.numpy as jnp
import numpy as np
from jax import lax
from jax.experimental import pallas as pl
from jax.experimental.pallas import tpu as pltpu

D_MODEL = 1024
CHUNK = 64
HEAD_DIM = 64
N_HEADS_DIFF = 4
DIFF_V_DIM = 128
N_HEADS_CHUNK = 8
LEFT_CHUNKS = 8
REL_CLIP = 128
N_HEADS_MEM = 4
MEM_HEAD_DIM = 256
D_FF = 4096
ROPE_THETA = 10000.0
LN_EPS = 1e-5
NEG_INF = -1e30
DEPTH = 1
DEEPNORM_ALPHA = (2.0 * DEPTH) ** 0.25
IN_WIDTH = 3072

LANES = 128
VMEM_LIMIT = 56 * 1024 * 1024

F32 = jnp.float32
BF16 = jnp.bfloat16

QA_BLK, KA_BLK, VA_BLK, QB_BLK, KB_BLK, VB_BLK = 0, 4, 8, 12, 16, 20


def _params(sem):
    return pltpu.CompilerParams(dimension_semantics=sem, vmem_limit_bytes=VMEM_LIMIT)


def _dot(a, b):
    return jnp.dot(a, b, preferred_element_type=F32)


def _dot_nt(a, b):
    return lax.dot_general(a, b, (((1,), (1,)), ((), ())), preferred_element_type=F32)


def _layer_norm(z, g, b):
    mu = jnp.mean(z, axis=-1, keepdims=True)
    zc = z - mu
    var = jnp.mean(zc * zc, axis=-1, keepdims=True)
    return zc * lax.rsqrt(var + LN_EPS) * g + b


def _lam_kernel(dl_ref, o_ref, *, lambda_init):
    dl = dl_ref[...]
    s1 = jnp.sum(dl[0:1, :] * dl[1:2, :], axis=1, keepdims=True)
    s2 = jnp.sum(dl[2:3, :] * dl[3:4, :], axis=1, keepdims=True)
    lam = jnp.exp(s1) - jnp.exp(s2) + lambda_init
    o_ref[...] = jnp.broadcast_to(lam, o_ref.shape)


def _lam_call(diff_lambda, lambda_init):
    return pl.pallas_call(
        functools.partial(_lam_kernel, lambda_init=lambda_init),
        out_shape=jax.ShapeDtypeStruct((1, LANES), F32),
        name="lam",
    )(diff_lambda)


def _inproj_kernel(pos_ref, invf_ref, x_ref, w_ref, o_ref):
    xb = x_ref[...].astype(BF16)
    ang = pos_ref[...].astype(F32) * invf_ref[...]
    cos = jnp.cos(ang)
    sin = jnp.sin(ang)
    lane = lax.broadcasted_iota(jnp.int32, ang.shape, 1)
    upper = (lane % HEAD_DIM) >= (HEAD_DIM // 2)
    sin_up = jnp.where(upper, sin, 0.0)
    sin_lo = jnp.where(upper, 0.0, -sin)
    n_chunks = IN_WIDTH // 512
    for c in range(n_chunks):
        h = _dot(xb, w_ref[:, c * 512:(c + 1) * 512])
        if c < 2:
            slabs = []
            for s in range(4):
                sl = h[:, s * LANES:(s + 1) * LANES]
                slabs.append(sl * cos
                             + pltpu.roll(sl, HEAD_DIM // 2, 1) * sin_up
                             + pltpu.roll(sl, LANES - HEAD_DIM // 2, 1) * sin_lo)
            h = jnp.concatenate(slabs, axis=1)
        if c == 0 or c == 3:
            h = h * (HEAD_DIM ** -0.5)
        o_ref[:, c * 512:(c + 1) * 512] = h.astype(BF16)


def _inproj_call(pos, invf, x2d, w_in, tm):
    T = x2d.shape[0]
    return pl.pallas_call(
        _inproj_kernel,
        grid=(T // tm,),
        in_specs=[
            pl.BlockSpec((tm, 1), lambda i: (i, 0)),
            pl.BlockSpec((1, LANES), lambda i: (0, 0)),
            pl.BlockSpec((tm, D_MODEL), lambda i: (i, 0)),
            pl.BlockSpec((D_MODEL, IN_WIDTH), lambda i: (0, 0)),
        ],
        out_specs=pl.BlockSpec((tm, IN_WIDTH), lambda i: (i, 0)),
        out_shape=jax.ShapeDtypeStruct((T, IN_WIDTH), BF16),
        compiler_params=_params(("arbitrary",)),
        name="inproj",
    )(pos, invf, x2d, w_in)


def _stack_halves(q):
    lane = lax.broadcasted_iota(jnp.int32, q.shape, 1)
    zero = jnp.zeros_like(q)
    return jnp.concatenate([jnp.where(lane < HEAD_DIM, q, zero),
                            jnp.where(lane >= HEAD_DIM, q, zero)], axis=0)


def _fill_v_ones(vaug_ref, v):
    vaug_ref[:, :LANES] = v
    vaug_ref[:, LANES:] = jnp.ones(v.shape, v.dtype)


def _diffattn_block(nkb, lam_ref, g_ref, q_ref, k_ref, o_ref, vaug_ref, s_ref, p_ref, *, tq, lambda_init):
    qq = _stack_halves(q_ref[0])
    nl = tq // LANES

    mpart = None
    for j in range(nkb):
        s = _dot_nt(qq, k_ref[0, j * tq:(j + 1) * tq, :])
        if j == nkb - 1:
            row = lax.broadcasted_iota(jnp.int32, s.shape, 0)
            col = lax.broadcasted_iota(jnp.int32, s.shape, 1)
            s = jnp.where((col // CHUNK) <= ((row % tq) // CHUNK), s, NEG_INF)
        s_ref[j] = s
        for c in range(nl):
            sl = s[:, c * LANES:(c + 1) * LANES]
            mpart = sl if mpart is None else jnp.maximum(mpart, sl)
    m = jnp.broadcast_to(jnp.max(mpart, axis=1, keepdims=True), (2 * tq, LANES))

    for j in range(nkb):
        for c in range(nl):
            col0 = j * tq + c * LANES
            p_ref[:, col0:col0 + LANES] = jnp.exp(s_ref[j, :, c * LANES:(c + 1) * LANES] - m).astype(BF16)
    acc = _dot(p_ref[:, :nkb * tq], vaug_ref[:nkb * tq, :])

    o = acc[:, :LANES] / acc[:, LANES:]
    o = o[:tq] - lam_ref[...] * o[tq:]
    y = o * lax.rsqrt(jnp.mean(o * o, axis=-1, keepdims=True) + LN_EPS) * g_ref[...]
    o_ref[0] = (y * (1.0 - lambda_init)).astype(o_ref.dtype)


def _diffattn_kernel(lam_ref, g_ref, q_ref, k_ref, v_ref, o_ref, vaug_ref, s_ref, p_ref,
                     *, tq, nq, lambda_init):
    i = pl.program_id(2)

    @pl.when(i == 0)
    def _():
        _fill_v_ones(vaug_ref, v_ref[0])

    for c in range(nq):
        pl.when(i == c)(functools.partial(
            _diffattn_block, c + 1, lam_ref, g_ref, q_ref, k_ref, o_ref, vaug_ref, s_ref, p_ref,
            tq=tq, lambda_init=lambda_init))


def _diffattn_call(lam, g, h3, tq, lambda_init):
    B, S, _ = h3.shape
    return pl.pallas_call(
        functools.partial(_diffattn_kernel, tq=tq, nq=S // tq, lambda_init=lambda_init),
        grid=(B, N_HEADS_DIFF, S // tq),
        scratch_shapes=[pltpu.VMEM((S, 2 * LANES), BF16),
                        pltpu.VMEM((S // tq, 2 * tq, tq), F32),
                        pltpu.VMEM((2 * tq, S), BF16)],
        in_specs=[
            pl.BlockSpec((1, LANES), lambda b, h, i: (0, 0)),
            pl.BlockSpec((1, DIFF_V_DIM), lambda b, h, i: (0, 0)),
            pl.BlockSpec((1, tq, LANES), lambda b, h, i: (b, i, QA_BLK + h)),
            pl.BlockSpec((1, S, LANES), lambda b, h, i: (b, 0, KA_BLK + h)),
            pl.BlockSpec((1, S, LANES), lambda b, h, i: (b, 0, VA_BLK + h)),
        ],
        out_specs=pl.BlockSpec((1, tq, LANES), lambda b, h, i: (b, i, h)),
        out_shape=jax.ShapeDtypeStruct((B, S, N_HEADS_DIFF * DIFF_V_DIM), BF16),
        compiler_params=_params(("arbitrary", "arbitrary", "arbitrary")),
        name="diffattn",
    )(lam, g, h3, h3, h3)


CA_TQ = 256
CA_NKB = 3


def _chunk_bias_table(rel_bias):
    nh = rel_bias.shape[0]
    width = CA_NKB * CA_TQ
    n_far = width - 1 - REL_CLIP + 1
    n_neg = (CA_TQ - 1) - REL_CLIP + 1
    diag = jnp.concatenate([
        jnp.broadcast_to(rel_bias[:, 2 * REL_CLIP:], (nh, n_far)),
        rel_bias[:, 1:2 * REL_CLIP][:, ::-1],
        jnp.broadcast_to(rel_bias[:, :1], (nh, n_neg + 1)),
    ], axis=1).astype(F32)
    period = width + CA_TQ
    assert diag.shape[1] == period
    hankel = jnp.tile(diag, (1, CA_TQ + 1))[:, :CA_TQ * (period + 1)].reshape(nh, CA_TQ, period + 1)
    bias = hankel[:, ::-1, :width]
    r = np.arange(CA_TQ)[:, None]
    koff = np.arange(width)[None, :] - (CA_NKB - 1) * CA_TQ
    qc = r // CHUNK
    kc = np.floor_divide(koff, CHUNK)
    allowed = (kc <= qc) & (kc >= qc - LEFT_CHUNKS)
    bias = jnp.where(allowed[None], bias, NEG_INF)
    return bias.reshape(nh // 2, 2 * CA_TQ, width)


def _chunkattn_kernel(bias_ref, q_ref, k_ref, v_ref, o_ref, vaug_ref):
    i = pl.program_id(2)

    @pl.when(i == 0)
    def _():
        _fill_v_ones(vaug_ref, v_ref[0])

    qq = _stack_halves(q_ref[0])
    s_blocks, starts = [], []
    for j in range(CA_NKB):
        kb = i - (CA_NKB - 1) + j
        start = pl.multiple_of(jnp.maximum(kb, 0) * CA_TQ, CA_TQ)
        s = _dot_nt(qq, k_ref[0, pl.ds(start, CA_TQ), :])
        s = s + bias_ref[0, :, j * CA_TQ:(j + 1) * CA_TQ]
        if j < CA_NKB - 1:
            s = s + jnp.where(kb < 0, NEG_INF, 0.0)
        s_blocks.append(s)
        starts.append(start)
    mpart = None
    for s in s_blocks:
        for c in range(CA_TQ // LANES):
            sl = s[:, c * LANES:(c + 1) * LANES]
            mpart = sl if mpart is None else jnp.maximum(mpart, sl)
    m = jnp.broadcast_to(jnp.max(mpart, axis=1, keepdims=True), mpart.shape)
    acc = None
    for s, start in zip(s_blocks, starts):
        p = jnp.concatenate([jnp.exp(s[:, c * LANES:(c + 1) * LANES] - m) for c in range(CA_TQ // LANES)],
                            axis=1).astype(BF16)
        pv = _dot(p, vaug_ref[pl.ds(start, CA_TQ), :])
        acc = pv if acc is None else acc + pv
    o = acc[:, :LANES] / acc[:, LANES:]
    lane = lax.broadcasted_iota(jnp.int32, (CA_TQ, LANES), 1)
    o_ref[0] = jnp.where(lane < HEAD_DIM, o[:CA_TQ], o[CA_TQ:]).astype(o_ref.dtype)


def _chunkattn_call(bias, h3):
    B, S, _ = h3.shape
    npairs = N_HEADS_CHUNK // 2
    return pl.pallas_call(
        _chunkattn_kernel,
        grid=(npairs, B, S // CA_TQ),
        in_specs=[
            pl.BlockSpec((1, 2 * CA_TQ, CA_NKB * CA_TQ), lambda p, b, i: (p, 0, 0)),
            pl.BlockSpec((1, CA_TQ, LANES), lambda p, b, i: (b, i, QB_BLK + p)),
            pl.BlockSpec((1, S, LANES), lambda p, b, i: (b, 0, KB_BLK + p)),
            pl.BlockSpec((1, S, LANES), lambda p, b, i: (b, 0, VB_BLK + p)),
        ],
        out_specs=pl.BlockSpec((1, CA_TQ, LANES), lambda p, b, i: (b, i, p)),
        out_shape=jax.ShapeDtypeStruct((B, S, N_HEADS_CHUNK * HEAD_DIM), BF16),
        scratch_shapes=[pltpu.VMEM((S, 2 * LANES), BF16)],
        compiler_params=_params(("arbitrary", "arbitrary", "arbitrary")),
        name="chunkattn",
    )(bias, h3, h3, h3)


def _outproj_kernel(x_ref, ya_ref, yb_ref, wo_ref, g_ref, b_ref, wq_ref, x1_ref, q_ref):
    half = ya_ref.shape[1]
    y = _dot(ya_ref[...], wo_ref[0:half, :]) + _dot(yb_ref[...], wo_ref[half:, :])
    x1 = _layer_norm(DEEPNORM_ALPHA * x_ref[...] + y, g_ref[...], b_ref[...])
    x1_ref[...] = x1
    q = _dot(x1.astype(BF16), wq_ref[...]) * (MEM_HEAD_DIM ** -0.5)
    q_ref[...] = q.astype(BF16)


def _outproj_call(x2d, ya, yb, w_o, g, b, w_mq, tm):
    T = x2d.shape[0]
    half = ya.shape[1]
    row = lambda i: (i, 0)
    fixed = lambda i: (0, 0)
    return pl.pallas_call(
        _outproj_kernel,
        grid=(T // tm,),
        in_specs=[
            pl.BlockSpec((tm, D_MODEL), row),
            pl.BlockSpec((tm, half), row),
            pl.BlockSpec((tm, half), row),
            pl.BlockSpec((D_MODEL, D_MODEL), fixed),
            pl.BlockSpec((1, D_MODEL), fixed),
            pl.BlockSpec((1, D_MODEL), fixed),
            pl.BlockSpec((D_MODEL, D_MODEL), fixed),
        ],
        out_specs=[pl.BlockSpec((tm, D_MODEL), row), pl.BlockSpec((tm, D_MODEL), row)],
        out_shape=[jax.ShapeDtypeStruct((T, D_MODEL), F32), jax.ShapeDtypeStruct((T, D_MODEL), BF16)],
        compiler_params=_params(("arbitrary",)),
        name="outproj",
    )(x2d, ya, yb, w_o, g, b, w_mq)


def _memkv_kernel(mem_ref, wk_ref, wv_ref, k_ref, v_ref):
    mb = mem_ref[0].astype(BF16)
    k_ref[0] = _dot(mb, wk_ref[...]).astype(BF16)
    v_ref[0] = _dot(mb, wv_ref[...]).astype(BF16)


def _memkv_call(mem, w_mk, w_mv):
    B, M, _ = mem.shape
    blk = pl.BlockSpec((1, M, D_MODEL), lambda b: (b, 0, 0))
    fixed = pl.BlockSpec((D_MODEL, D_MODEL), lambda b: (0, 0))
    return pl.pallas_call(
        _memkv_kernel,
        grid=(B,),
        in_specs=[blk, fixed, fixed],
        out_specs=[blk, blk],
        out_shape=[jax.ShapeDtypeStruct((B, M, D_MODEL), BF16)] * 2,
        compiler_params=_params(("arbitrary",)),
        name="memkv",
    )(mem, w_mk, w_mv)


def _memattn_kernel(x1_ref, q_ref, k_ref, v_ref, wo_ref, g_ref, b_ref, x2_ref):
    outs = []
    for h in range(N_HEADS_MEM):
        sl = slice(h * MEM_HEAD_DIM, (h + 1) * MEM_HEAD_DIM)
        s = _dot_nt(q_ref[0, :, sl], k_ref[0, :, sl])
        m = jnp.max(s, axis=1, keepdims=True)
        p = jnp.exp(s - m)
        l = jnp.sum(p, axis=1, keepdims=True)
        outs.append((_dot(p.astype(BF16), v_ref[0, :, sl]) / l).astype(BF16))
    o = jnp.concatenate(outs, axis=1)
    y = _dot(o, wo_ref[...])
    x2_ref[0] = _layer_norm(DEEPNORM_ALPHA * x1_ref[0] + y, g_ref[...], b_ref[...])


def _memattn_call(x1, q, k, v, w_mo, g, b, tm):
    B, S, _ = x1.shape
    M = k.shape[1]
    row = pl.BlockSpec((1, tm, D_MODEL), lambda bb, i: (bb, i, 0))
    kv = pl.BlockSpec((1, M, D_MODEL), lambda bb, i: (bb, 0, 0))
    vec = pl.BlockSpec((1, D_MODEL), lambda bb, i: (0, 0))
    return pl.pallas_call(
        _memattn_kernel,
        grid=(B, S // tm),
        in_specs=[row, row, kv, kv, pl.BlockSpec((D_MODEL, D_MODEL), lambda bb, i: (0, 0)), vec, vec],
        out_specs=row,
        out_shape=jax.ShapeDtypeStruct((B, S, D_MODEL), F32),
        compiler_params=_params(("arbitrary", "arbitrary")),
        name="memattn",
    )(x1, q, k, v, w_mo, g, b)


def _mlp_kernel(x_ref, wu_ref, wd_ref, g_ref, b_ref, o_ref, *, ff_chunk):
    x = x_ref[...]
    xb = x.astype(BF16)
    y = None
    for c in range(D_FF // ff_chunk):
        sl = slice(c * ff_chunk, (c + 1) * ff_chunk)
        h = jnp.maximum(_dot(xb, wu_ref[:, sl]), 0.0)
        part = _dot((h * h).astype(BF16), wd_ref[sl, :])
        y = part if y is None else y + part
    o_ref[...] = _layer_norm(DEEPNORM_ALPHA * x + y, g_ref[...], b_ref[...])


def _mlp_call(x2d, w_up, w_down, g, b, tm, ff_chunk):
    T = x2d.shape[0]
    row = pl.BlockSpec((tm, D_MODEL), lambda i: (i, 0))
    vec = pl.BlockSpec((1, D_MODEL), lambda i: (0, 0))
    return pl.pallas_call(
        functools.partial(_mlp_kernel, ff_chunk=ff_chunk),
        grid=(T // tm,),
        in_specs=[row,
                  pl.BlockSpec((D_MODEL, D_FF), lambda i: (0, 0), pipeline_mode=pl.Buffered(1)),
                  pl.BlockSpec((D_FF, D_MODEL), lambda i: (0, 0), pipeline_mode=pl.Buffered(1)),
                  vec, vec],
        out_specs=row,
        out_shape=jax.ShapeDtypeStruct((T, D_MODEL), F32),
        compiler_params=_params(("arbitrary",)),
        name="mlp",
    )(x2d, w_up, w_down, g, b)


def kernel(x, mem, positions, w_in, diff_lambda, subln_g, rel_bias, w_o, ln1_g, ln1_b,
           w_mq, w_mk, w_mv, w_mo, ln2_g, ln2_b, w_up, w_down, ln3_g, ln3_b):
    B, S, D = x.shape
    T = B * S
    depth = w_in.shape[0]
    assert depth == DEPTH and D == D_MODEL and S % CA_TQ == 0
    inv_freq = 1.0 / (ROPE_THETA ** (jnp.arange(0, HEAD_DIM, 2, dtype=F32) / HEAD_DIM))
    invf = jnp.tile(inv_freq, LANES // (HEAD_DIM // 2)).reshape(1, LANES)
    pos = positions.reshape(T, 1)
    vec = lambda a: a.reshape(1, -1)

    for l in range(depth):
        lambda_init = 0.8 - 0.6 * math.exp(-0.3 * l)
        lam = _lam_call(diff_lambda[l], lambda_init)
        h = _inproj_call(pos, invf, x.reshape(T, D), w_in[l].astype(BF16), tm=512)
        h3 = h.reshape(B, S, IN_WIDTH)
        ya = _diffattn_call(lam, vec(subln_g[l]), h3, tq=256, lambda_init=lambda_init)
        yb = _chunkattn_call(_chunk_bias_table(rel_bias[l]), h3)
        x1, qm = _outproj_call(x.reshape(T, D), ya.reshape(T, -1), yb.reshape(T, -1),
                               w_o[l].astype(BF16), vec(ln1_g[l]), vec(ln1_b[l]),
                               w_mq[l].astype(BF16), tm=512)
        km, vm = _memkv_call(mem, w_mk[l].astype(BF16), w_mv[l].astype(BF16))
        x2 = _memattn_call(x1.reshape(B, S, D), qm.reshape(B, S, D), km, vm,
                           w_mo[l].astype(BF16), vec(ln2_g[l]), vec(ln2_b[l]), tm=512)
        out = _mlp_call(x2.reshape(T, D), w_up[l].astype(BF16), w_down[l].astype(BF16),
                        vec(ln3_g[l]), vec(ln3_b[l]), tm=512, ff_chunk=1024)
        x = out.reshape(B, S, D)
    return x
```

```python
import functools
import math

import jax
import jax.numpy as jnp
import numpy as np
from jax import lax
from jax.experimental import pallas as pl
from jax.experimental.pallas import tpu as pltpu

D_MODEL = 1024
CHUNK = 64
HEAD_DIM = 64
N_HEADS_DIFF = 4
DIFF_V_DIM = 128
N_HEADS_CHUNK = 8
LEFT_CHUNKS = 8
REL_CLIP = 128
N_HEADS_MEM = 4
MEM_HEAD_DIM = 256
D_FF = 4096
ROPE_THETA = 10000.0
LN_EPS = 1e-5
NEG_INF = -1e30
DEPTH = 1
DEEPNORM_ALPHA = (2.0 * DEPTH) ** 0.25
IN_WIDTH = 3072

LANES = 128
VMEM_LIMIT = 56 * 1024 * 1024

F32 = jnp.float32
BF16 = jnp.bfloat16

QA_BLK, KA_BLK, VA_BLK, QB_BLK, KB_BLK, VB_BLK = 0, 4, 8, 12, 16, 20


def _params(sem):
    return pltpu.CompilerParams(dimension_semantics=sem, vmem_limit_bytes=VMEM_LIMIT)


def _dot(a, b):
    return jnp.dot(a, b, preferred_element_type=F32)


def _dot_nt(a, b):
    return lax.dot_general(a, b, (((1,), (1,)), ((), ())), preferred_element_type=F32)


def _layer_norm(z, g, b):
    mu = jnp.mean(z, axis=-1, keepdims=True)
    zc = z - mu
    var = jnp.mean(zc * zc, axis=-1, keepdims=True)
    return zc * lax.rsqrt(var + LN_EPS) * g + b


def _lam_kernel(dl_ref, o_ref, *, lambda_init):
    dl = dl_ref[...]
    s1 = jnp.sum(dl[0:1, :] * dl[1:2, :], axis=1, keepdims=True)
    s2 = jnp.sum(dl[2:3, :] * dl[3:4, :], axis=1, keepdims=True)
    lam = jnp.exp(s1) - jnp.exp(s2) + lambda_init
    o_ref[...] = jnp.broadcast_to(lam, o_ref.shape)


def _lam_call(diff_lambda, lambda_init):
    return pl.pallas_call(
        functools.partial(_lam_kernel, lambda_init=lambda_init),
        out_shape=jax.ShapeDtypeStruct((1, LANES), F32),
        name="lam",
    )(diff_lambda)


def _inproj_kernel(pos_ref, invf_ref, x_ref, w_ref, o_ref):
    xb = x_ref[...].astype(BF16)
    ang = pos_ref[...].astype(F32) * invf_ref[...]
    cos = jnp.cos(ang)
    sin = jnp.sin(ang)
    lane = lax.broadcasted_iota(jnp.int32, ang.shape, 1)
    upper = (lane % HEAD_DIM) >= (HEAD_DIM // 2)
    sin_up = jnp.where(upper, sin, 0.0)
    sin_lo = jnp.where(upper, 0.0, -sin)
    n_chunks = IN_WIDTH // 512
    for c in range(n_chunks):
        h = _dot(xb, w_ref[:, c * 512:(c + 1) * 512])
        if c < 2:
            slabs = []
            for s in range(4):
                sl = h[:, s * LANES:(s + 1) * LANES]
                slabs.append(sl * cos
                             + pltpu.roll(sl, HEAD_DIM // 2, 1) * sin_up
                             + pltpu.roll(sl, LANES - HEAD_DIM // 2, 1) * sin_lo)
            h = jnp.concatenate(slabs, axis=1)
        if c == 0 or c == 3:
            h = h * (HEAD_DIM ** -0.5)
        o_ref[:, c * 512:(c + 1) * 512] = h.astype(BF16)


def _inproj_call(pos, invf, x2d, w_in, tm):
    T = x2d.shape[0]
    return pl.pallas_call(
        _inproj_kernel,
        grid=(T // tm,),
        in_specs=[
            pl.BlockSpec((tm, 1), lambda i: (i, 0)),
            pl.BlockSpec((1, LANES), lambda i: (0, 0)),
            pl.BlockSpec((tm, D_MODEL), lambda i: (i, 0)),
            pl.BlockSpec((D_MODEL, IN_WIDTH), lambda i: (0, 0)),
        ],
        out_specs=pl.BlockSpec((tm, IN_WIDTH), lambda i: (i, 0)),
        out_shape=jax.ShapeDtypeStruct((T, IN_WIDTH), BF16),
        compiler_params=_params(("arbitrary",)),
        name="inproj",
    )(pos, invf, x2d, w_in)


def _stack_halves(q):
    lane = lax.broadcasted_iota(jnp.int32, q.shape, 1)
    zero = jnp.zeros_like(q)
    return jnp.concatenate([jnp.where(lane < HEAD_DIM, q, zero),
                            jnp.where(lane >= HEAD_DIM, q, zero)], axis=0)


def _fill_v_ones(vaug_ref, v):
    vaug_ref[:, :LANES] = v
    vaug_ref[:, LANES:] = jnp.ones(v.shape, v.dtype)


def _diff_scores(nkb, q, k_ref, s_ref, m_ref, *, tq):
    qq = _stack_halves(q)
    mpart = None
    for j in range(nkb):
        s = _dot_nt(qq, k_ref[0, j * tq:(j + 1) * tq, :])
        if j == nkb - 1:
            row = lax.broadcasted_iota(jnp.int32, s.shape, 0)
            col = lax.broadcasted_iota(jnp.int32, s.shape, 1)
            s = jnp.where((col // CHUNK) <= ((row % tq) // CHUNK), s, NEG_INF)
        s_ref[j] = s
        for c in range(tq // LANES):
            sl = s[:, c * LANES:(c + 1) * LANES]
            mpart = sl if mpart is None else jnp.maximum(mpart, sl)
    m_ref[...] = jnp.broadcast_to(jnp.max(mpart, axis=1, keepdims=True), (2 * tq, LANES))


def _diff_output(nkb, s_ref, m_ref, vaug_ref, p_ref, lam_ref, g_ref, o_ref, *, tq, lambda_init):
    m = m_ref[...]
    for j in range(nkb):
        for c in range(tq // LANES):
            col0 = j * tq + c * LANES
            p_ref[:, col0:col0 + LANES] = jnp.exp(s_ref[j, :, c * LANES:(c + 1) * LANES] - m).astype(BF16)
    acc = _dot(p_ref[:, :nkb * tq], vaug_ref[:nkb * tq, :])
    o = acc[:, :LANES] / acc[:, LANES:]
    o = o[:tq] - lam_ref[...] * o[tq:]
    y = o * lax.rsqrt(jnp.mean(o * o, axis=-1, keepdims=True) + LN_EPS) * g_ref[...]
    o_ref[0] = (y * (1.0 - lambda_init)).astype(o_ref.dtype)


def _diffattn_kernel(lam_ref, g_ref, q_ref, k_ref, v_ref, o_ref, vaug_ref, s_ref, m_ref, p_ref,
                     *, tq, nq, lambda_init):
    i = pl.program_id(2)

    def step(c):
        if c == 0:
            _fill_v_ones(vaug_ref, v_ref[0])
            _diff_scores(1, q_ref[0, 0:tq, :], k_ref, s_ref.at[0], m_ref.at[0], tq=tq)
        if c + 1 < nq:
            nxt = (c + 1) % 2
            _diff_scores(c + 2, q_ref[0, (c + 1) * tq:(c + 2) * tq, :], k_ref, s_ref.at[nxt], m_ref.at[nxt],
                         tq=tq)
        _diff_output(c + 1, s_ref.at[c % 2], m_ref.at[c % 2], vaug_ref, p_ref, lam_ref, g_ref, o_ref,
                     tq=tq, lambda_init=lambda_init)

    for c in range(nq):
        pl.when(i == c)(functools.partial(step, c))


def _diffattn_call(lam, g, h3, tq, lambda_init):
    B, S, _ = h3.shape
    nq = S // tq
    return pl.pallas_call(
        functools.partial(_diffattn_kernel, tq=tq, nq=nq, lambda_init=lambda_init),
        grid=(B, N_HEADS_DIFF, nq),
        scratch_shapes=[pltpu.VMEM((S, 2 * LANES), BF16),
                        pltpu.VMEM((2, nq, 2 * tq, tq), F32),
                        pltpu.VMEM((2, 2 * tq, LANES), F32),
                        pltpu.VMEM((2 * tq, S), BF16)],
        in_specs=[
            pl.BlockSpec((1, LANES), lambda b, h, i: (0, 0)),
            pl.BlockSpec((1, DIFF_V_DIM), lambda b, h, i: (0, 0)),
            pl.BlockSpec((1, S, LANES), lambda b, h, i: (b, 0, QA_BLK + h)),
            pl.BlockSpec((1, S, LANES), lambda b, h, i: (b, 0, KA_BLK + h)),
            pl.BlockSpec((1, S, LANES), lambda b, h, i: (b, 0, VA_BLK + h)),
        ],
        out_specs=pl.BlockSpec((1, tq, LANES), lambda b, h, i: (b, i, h)),
        out_shape=jax.ShapeDtypeStruct((B, S, N_HEADS_DIFF * DIFF_V_DIM), BF16),
        compiler_params=_params(("arbitrary", "arbitrary", "arbitrary")),
        name="diffattn",
    )(lam, g, h3, h3, h3)


CA_TQ = 256
CA_NKB = 3


def _chunk_bias_table(rel_bias):
    nh = rel_bias.shape[0]
    width = CA_NKB * CA_TQ
    n_far = width - 1 - REL_CLIP + 1
    n_neg = (CA_TQ - 1) - REL_CLIP + 1
    diag = jnp.concatenate([
        jnp.broadcast_to(rel_bias[:, 2 * REL_CLIP:], (nh, n_far)),
        rel_bias[:, 1:2 * REL_CLIP][:, ::-1],
        jnp.broadcast_to(rel_bias[:, :1], (nh, n_neg + 1)),
    ], axis=1).astype(F32)
    period = width + CA_TQ
    assert diag.shape[1] == period
    diag = jnp.concatenate([diag[:, CA_TQ - 1:], diag[:, :CA_TQ - 1]], axis=1)
    bias = jnp.tile(diag, (1, CA_TQ))[:, :CA_TQ * (period - 1)].reshape(nh, CA_TQ, period - 1)[:, :, :width]
    r = np.arange(CA_TQ)[:, None]
    koff = np.arange(width)[None, :] - (CA_NKB - 1) * CA_TQ
    qc = r // CHUNK
    kc = np.floor_divide(koff, CHUNK)
    allowed = (kc <= qc) & (kc >= qc - LEFT_CHUNKS)
    bias = jnp.where(allowed[None], bias, NEG_INF)
    return bias.reshape(nh // 2, 2 * CA_TQ, width)


def _chunkattn_kernel(bias_ref, q_ref, k_ref, v_ref, o_ref, vaug_ref, s_ref, m_ref, p_ref):
    i = pl.program_id(2)
    nq = q_ref.shape[1] // CA_TQ
    nl = CA_TQ // LANES

    def key_start(blk, j):
        kb = blk - (CA_NKB - 1) + j
        return kb, pl.multiple_of(jnp.maximum(kb, 0) * CA_TQ, CA_TQ)

    def scores(blk, slot):
        qstart = pl.multiple_of(blk * CA_TQ, CA_TQ)
        qq = _stack_halves(q_ref[0, pl.ds(qstart, CA_TQ), :])
        mpart = None
        for j in range(CA_NKB):
            kb, start = key_start(blk, j)
            s = _dot_nt(qq, k_ref[0, pl.ds(start, CA_TQ), :])
            s = s + bias_ref[0, :, j * CA_TQ:(j + 1) * CA_TQ]
            if j < CA_NKB - 1:
                s = s + jnp.where(kb < 0, NEG_INF, 0.0)
            s_ref[slot, j] = s
            for c in range(nl):
                sl = s[:, c * LANES:(c + 1) * LANES]
                mpart = sl if mpart is None else jnp.maximum(mpart, sl)
        m_ref[slot] = jnp.broadcast_to(jnp.max(mpart, axis=1, keepdims=True), mpart.shape)

    def output(blk, slot):
        m = m_ref[slot]
        acc = None
        for j in range(CA_NKB):
            _, start = key_start(blk, j)
            for c in range(nl):
                p_ref[:, j * CA_TQ + c * LANES:j * CA_TQ + (c + 1) * LANES] = jnp.exp(
                    s_ref[slot, j, :, c * LANES:(c + 1) * LANES] - m).astype(BF16)
            pv = _dot(p_ref[:, j * CA_TQ:(j + 1) * CA_TQ], vaug_ref[pl.ds(start, CA_TQ), :])
            acc = pv if acc is None else acc + pv
        o = acc[:, :LANES] / acc[:, LANES:]
        lane = lax.broadcasted_iota(jnp.int32, (CA_TQ, LANES), 1)
        o_ref[0] = jnp.where(lane < HEAD_DIM, o[:CA_TQ], o[CA_TQ:]).astype(o_ref.dtype)

    @pl.when(i == 0)
    def _():
        _fill_v_ones(vaug_ref, v_ref[0])
        scores(0, 0)

    for parity in range(2):
        @pl.when(jnp.logical_and(i % 2 == parity, i < nq - 1))
        def _(parity=parity):
            scores(i + 1, 1 - parity)
            output(i, parity)

    @pl.when(i == nq - 1)
    def _():
        output(i, (nq - 1) % 2)


def _chunkattn_call(bias, h3):
    B, S, _ = h3.shape
    npairs = N_HEADS_CHUNK // 2
    return pl.pallas_call(
        _chunkattn_kernel,
        grid=(npairs, B, S // CA_TQ),
        in_specs=[
            pl.BlockSpec((1, 2 * CA_TQ, CA_NKB * CA_TQ), lambda p, b, i: (p, 0, 0)),
            pl.BlockSpec((1, S, LANES), lambda p, b, i: (b, 0, QB_BLK + p)),
            pl.BlockSpec((1, S, LANES), lambda p, b, i: (b, 0, KB_BLK + p)),
            pl.BlockSpec((1, S, LANES), lambda p, b, i: (b, 0, VB_BLK + p)),
        ],
        out_specs=pl.BlockSpec((1, CA_TQ, LANES), lambda p, b, i: (b, i, p)),
        out_shape=jax.ShapeDtypeStruct((B, S, N_HEADS_CHUNK * HEAD_DIM), BF16),
        scratch_shapes=[pltpu.VMEM((S, 2 * LANES), BF16),
                        pltpu.VMEM((2, CA_NKB, 2 * CA_TQ, CA_TQ), F32),
                        pltpu.VMEM((2, 2 * CA_TQ, LANES), F32),
                        pltpu.VMEM((2 * CA_TQ, CA_NKB * CA_TQ), BF16)],
        compiler_params=_params(("arbitrary", "arbitrary", "arbitrary")),
        name="chunkattn",
    )(bias, h3, h3, h3)


def _outproj_kernel(x_ref, ya_ref, yb_ref, wo_ref, g_ref, b_ref, wq_ref, x1_ref, q_ref):
    half = ya_ref.shape[1]
    y = _dot(ya_ref[...], wo_ref[0:half, :]) + _dot(yb_ref[...], wo_ref[half:, :])
    x1 = _layer_norm(DEEPNORM_ALPHA * x_ref[...] + y, g_ref[...], b_ref[...])
    x1_ref[...] = x1
    q = _dot(x1.astype(BF16), wq_ref[...]) * (MEM_HEAD_DIM ** -0.5)
    q_ref[...] = q.astype(BF16)


def _outproj_call(x2d, ya, yb, w_o, g, b, w_mq, tm):
    T = x2d.shape[0]
    half = ya.shape[1]
    row = lambda i: (i, 0)
    fixed = lambda i: (0, 0)
    return pl.pallas_call(
        _outproj_kernel,
        grid=(T // tm,),
        in_specs=[
            pl.BlockSpec((tm, D_MODEL), row),
            pl.BlockSpec((tm, half), row),
            pl.BlockSpec((tm, half), row),
            pl.BlockSpec((D_MODEL, D_MODEL), fixed),
            pl.BlockSpec((1, D_MODEL), fixed),
            pl.BlockSpec((1, D_MODEL), fixed),
            pl.BlockSpec((D_MODEL, D_MODEL), fixed),
        ],
        out_specs=[pl.BlockSpec((tm, D_MODEL), row), pl.BlockSpec((tm, D_MODEL), row)],
        out_shape=[jax.ShapeDtypeStruct((T, D_MODEL), F32), jax.ShapeDtypeStruct((T, D_MODEL), BF16)],
        compiler_params=_params(("arbitrary",)),
        name="outproj",
    )(x2d, ya, yb, w_o, g, b, w_mq)


def _memkv_kernel(mem_ref, wk_ref, wv_ref, k_ref, v_ref):
    mb = mem_ref[0].astype(BF16)
    k_ref[0] = _dot(mb, wk_ref[...]).astype(BF16)
    v_ref[0] = _dot(mb, wv_ref[...]).astype(BF16)


def _memkv_call(mem, w_mk, w_mv):
    B, M, _ = mem.shape
    blk = pl.BlockSpec((1, M, D_MODEL), lambda b: (b, 0, 0))
    fixed = pl.BlockSpec((D_MODEL, D_MODEL), lambda b: (0, 0))
    return pl.pallas_call(
        _memkv_kernel,
        grid=(B,),
        in_specs=[blk, fixed, fixed],
        out_specs=[blk, blk],
        out_shape=[jax.ShapeDtypeStruct((B, M, D_MODEL), BF16)] * 2,
        compiler_params=_params(("arbitrary",)),
        name="memkv",
    )(mem, w_mk, w_mv)


def _memattn_kernel(x1_ref, q_ref, k_ref, v_ref, wo_ref, g_ref, b_ref, x2_ref):
    outs = []
    for h in range(N_HEADS_MEM):
        sl = slice(h * MEM_HEAD_DIM, (h + 1) * MEM_HEAD_DIM)
        s = _dot_nt(q_ref[0, :, sl], k_ref[0, :, sl])
        m = jnp.max(s, axis=1, keepdims=True)
        p = jnp.exp(s - m)
        l = jnp.sum(p, axis=1, keepdims=True)
        outs.append((_dot(p.astype(BF16), v_ref[0, :, sl]) / l).astype(BF16))
    o = jnp.concatenate(outs, axis=1)
    y = _dot(o, wo_ref[...])
    x2_ref[0] = _layer_norm(DEEPNORM_ALPHA * x1_ref[0] + y, g_ref[...], b_ref[...])


def _memattn_call(x1, q, k, v, w_mo, g, b, tm):
    B, S, _ = x1.shape
    M = k.shape[1]
    row = pl.BlockSpec((1, tm, D_MODEL), lambda bb, i: (bb, i, 0))
    kv = pl.BlockSpec((1, M, D_MODEL), lambda bb, i: (bb, 0, 0))
    vec = pl.BlockSpec((1, D_MODEL), lambda bb, i: (0, 0))
    return pl.pallas_call(
        _memattn_kernel,
        grid=(B, S // tm),
        in_specs=[row, row, kv, kv, pl.BlockSpec((D_MODEL, D_MODEL), lambda bb, i: (0, 0)), vec, vec],
        out_specs=row,
        out_shape=jax.ShapeDtypeStruct((B, S, D_MODEL), F32),
        compiler_params=_params(("arbitrary", "arbitrary")),
        name="memattn",
    )(x1, q, k, v, w_mo, g, b)


def _mlp_kernel(x_ref, wu_ref, wd_ref, g_ref, b_ref, o_ref, *, ff_chunk):
    x = x_ref[...]
    xb = x.astype(BF16)
    y = None
    for c in range(D_FF // ff_chunk):
        sl = slice(c * ff_chunk, (c + 1) * ff_chunk)
        h = jnp.maximum(_dot(xb, wu_ref[:, sl]), 0.0)
        part = _dot((h * h).astype(BF16), wd_ref[sl, :])
        y = part if y is None else y + part
    o_ref[...] = _layer_norm(DEEPNORM_ALPHA * x + y, g_ref[...], b_ref[...])


def _mlp_call(x2d, w_up, w_down, g, b, tm, ff_chunk):
    T = x2d.shape[0]
    row = pl.BlockSpec((tm, D_MODEL), lambda i: (i, 0))
    vec = pl.BlockSpec((1, D_MODEL), lambda i: (0, 0))
    return pl.pallas_call(
        functools.partial(_mlp_kernel, ff_chunk=ff_chunk),
        grid=(T // tm,),
        in_specs=[row,
                  pl.BlockSpec((D_MODEL, D_FF), lambda i: (0, 0), pipeline_mode=pl.Buffered(1)),
                  pl.BlockSpec((D_FF, D_MODEL), lambda i: (0, 0), pipeline_mode=pl.Buffered(1)),
                  vec, vec],
        out_specs=row,
        out_shape=jax.ShapeDtypeStruct((T, D_MODEL), F32),
        compiler_params=_params(("arbitrary",)),
        name="mlp",
    )(x2d, w_up, w_down, g, b)


def kernel(x, mem, positions, w_in, diff_lambda, subln_g, rel_bias, w_o, ln1_g, ln1_b,
           w_mq, w_mk, w_mv, w_mo, ln2_g, ln2_b, w_up, w_down, ln3_g, ln3_b):
    B, S, D = x.shape
    T = B * S
    depth = w_in.shape[0]
    assert depth == DEPTH and D == D_MODEL and S % CA_TQ == 0
    inv_freq = 1.0 / (ROPE_THETA ** (jnp.arange(0, HEAD_DIM, 2, dtype=F32) / HEAD_DIM))
    invf = jnp.tile(inv_freq, LANES // (HEAD_DIM // 2)).reshape(1, LANES)
    pos = positions.reshape(T, 1)
    vec = lambda a: a.reshape(1, -1)

    for l in range(depth):
        lambda_init = 0.8 - 0.6 * math.exp(-0.3 * l)
        lam = _lam_call(diff_lambda[l], lambda_init)
        h = _inproj_call(pos, invf, x.reshape(T, D), w_in[l].astype(BF16), tm=512)
        h3 = h.reshape(B, S, IN_WIDTH)
        ya = _diffattn_call(lam, vec(subln_g[l]), h3, tq=256, lambda_init=lambda_init)
        yb = _chunkattn_call(_chunk_bias_table(rel_bias[l]), h3)
        x1, qm = _outproj_call(x.reshape(T, D), ya.reshape(T, -1), yb.reshape(T, -1),
                               w_o[l].astype(BF16), vec(ln1_g[l]), vec(ln1_b[l]),
                               w_mq[l].astype(BF16), tm=512)
        km, vm = _memkv_call(mem, w_mk[l].astype(BF16), w_mv[l].astype(BF16))
        x2 = _memattn_call(x1.reshape(B, S, D), qm.reshape(B, S, D), km, vm,
                           w_mo[l].astype(BF16), vec(ln2_g[l]), vec(ln2_b[l]), tm=512)
        out = _mlp_call(x2.reshape(T, D), w_up[l].astype(BF16), w_down[l].astype(BF16),
                        vec(ln3_g[l]), vec(ln3_b[l]), tm=512, ff_chunk=1024)
        x = out.reshape(B, S, D)
    return x
```

```python
import functools
import math

import jax
import jax.numpy as jnp
import numpy as np
from jax import lax
from jax.experimental import pallas as pl
from jax.experimental.pallas import tpu as pltpu

D_MODEL = 1024
CHUNK = 64
HEAD_DIM = 64
N_HEADS_DIFF = 4
DIFF_V_DIM = 128
N_HEADS_CHUNK = 8
LEFT_CHUNKS = 8
REL_CLIP = 128
N_HEADS_MEM = 4
MEM_HEAD_DIM = 256
D_FF = 4096
ROPE_THETA = 10000.0
LN_EPS = 1e-5
NEG_INF = -1e30
DEPTH = 1
DEEPNORM_ALPHA = (2.0 * DEPTH) ** 0.25
IN_WIDTH = 3072

LANES = 128
VMEM_LIMIT = 56 * 1024 * 1024

F32 = jnp.float32
BF16 = jnp.bfloat16

QA_BLK, KA_BLK, VA_BLK, QB_BLK, KB_BLK, VB_BLK = 0, 4, 8, 12, 16, 20


def _params(sem):
    return pltpu.CompilerParams(dimension_semantics=sem, vmem_limit_bytes=VMEM_LIMIT)


def _dot(a, b):
    return jnp.dot(a, b, preferred_element_type=F32)


def _dot_nt(a, b):
    return lax.dot_general(a, b, (((1,), (1,)), ((), ())), preferred_element_type=F32)


def _layer_norm(z, g, b):
    mu = jnp.mean(z, axis=-1, keepdims=True)
    zc = z - mu
    var = jnp.mean(zc * zc, axis=-1, keepdims=True)
    return zc * lax.rsqrt(var + LN_EPS) * g + b


def _lam_kernel(dl_ref, o_ref, *, lambda_init):
    dl = dl_ref[...]
    s1 = jnp.sum(dl[0:1, :] * dl[1:2, :], axis=1, keepdims=True)
    s2 = jnp.sum(dl[2:3, :] * dl[3:4, :], axis=1, keepdims=True)
    lam = jnp.exp(s1) - jnp.exp(s2) + lambda_init
    o_ref[...] = jnp.broadcast_to(lam, o_ref.shape)


def _lam_call(diff_lambda, lambda_init):
    return pl.pallas_call(
        functools.partial(_lam_kernel, lambda_init=lambda_init),
        out_shape=jax.ShapeDtypeStruct((1, LANES), F32),
        name="lam",
    )(diff_lambda)


def _inproj_kernel(pos_ref, invf_ref, x_ref, w_ref, o_ref):
    xb = x_ref[...].astype(BF16)
    ang = pos_ref[...].astype(F32) * invf_ref[...]
    cos = jnp.cos(ang)
    sin = jnp.sin(ang)
    lane = lax.broadcasted_iota(jnp.int32, ang.shape, 1)
    upper = (lane % HEAD_DIM) >= (HEAD_DIM // 2)
    sin_up = jnp.where(upper, sin, 0.0)
    sin_lo = jnp.where(upper, 0.0, -sin)
    n_chunks = IN_WIDTH // 512
    for c in range(n_chunks):
        h = _dot(xb, w_ref[:, c * 512:(c + 1) * 512])
        if c < 2:
            slabs = []
            for s in range(4):
                sl = h[:, s * LANES:(s + 1) * LANES]
                slabs.append(sl * cos
                             + pltpu.roll(sl, HEAD_DIM // 2, 1) * sin_up
                             + pltpu.roll(sl, LANES - HEAD_DIM // 2, 1) * sin_lo)
            h = jnp.concatenate(slabs, axis=1)
        if c == 0 or c == 3:
            h = h * (HEAD_DIM ** -0.5)
        o_ref[:, c * 512:(c + 1) * 512] = h.astype(BF16)


def _inproj_call(pos, invf, x2d, w_in, tm):
    T = x2d.shape[0]
    return pl.pallas_call(
        _inproj_kernel,
        grid=(T // tm,),
        in_specs=[
            pl.BlockSpec((tm, 1), lambda i: (i, 0)),
            pl.BlockSpec((1, LANES), lambda i: (0, 0)),
            pl.BlockSpec((tm, D_MODEL), lambda i: (i, 0)),
            pl.BlockSpec((D_MODEL, IN_WIDTH), lambda i: (0, 0)),
        ],
        out_specs=pl.BlockSpec((tm, IN_WIDTH), lambda i: (i, 0)),
        out_shape=jax.ShapeDtypeStruct((T, IN_WIDTH), BF16),
        compiler_params=_params(("arbitrary",)),
        name="inproj",
    )(pos, invf, x2d, w_in)


def _stack_halves(q):
    lane = lax.broadcasted_iota(jnp.int32, q.shape, 1)
    zero = jnp.zeros_like(q)
    return jnp.concatenate([jnp.where(lane < HEAD_DIM, q, zero),
                            jnp.where(lane >= HEAD_DIM, q, zero)], axis=0)


def _fill_v_ones(vaug_ref, v):
    vaug_ref[:, :LANES] = v
    vaug_ref[:, LANES:] = jnp.ones(v.shape, v.dtype)


def _diff_scores(nkb, q, k_ref, s_ref, m_ref, *, tq):
    qq = _stack_halves(q)
    mpart = None
    for j in range(nkb):
        s = _dot_nt(qq, k_ref[0, j * tq:(j + 1) * tq, :])
        if j == nkb - 1:
            row = lax.broadcasted_iota(jnp.int32, s.shape, 0)
            col = lax.broadcasted_iota(jnp.int32, s.shape, 1)
            s = jnp.where((col // CHUNK) <= ((row % tq) // CHUNK), s, NEG_INF)
        s_ref[j] = s
        for c in range(tq // LANES):
            sl = s[:, c * LANES:(c + 1) * LANES]
            mpart = sl if mpart is None else jnp.maximum(mpart, sl)
    m_ref[...] = jnp.broadcast_to(jnp.max(mpart, axis=1, keepdims=True), (2 * tq, LANES))


def _diff_output(nkb, s_ref, m_ref, vaug_ref, p_ref, lam_ref, g_ref, o_ref, *, tq, lambda_init):
    m = m_ref[...]
    for j in range(nkb):
        for c in range(tq // LANES):
            col0 = j * tq + c * LANES
            p_ref[:, col0:col0 + LANES] = jnp.exp(s_ref[j, :, c * LANES:(c + 1) * LANES] - m).astype(BF16)
    acc = _dot(p_ref[:, :nkb * tq], vaug_ref[:nkb * tq, :])
    o = acc[:, :LANES] / acc[:, LANES:]
    o = o[:tq] - lam_ref[...] * o[tq:]
    y = o * lax.rsqrt(jnp.mean(o * o, axis=-1, keepdims=True) + LN_EPS) * g_ref[...]
    o_ref[...] = (y * (1.0 - lambda_init)).astype(o_ref.dtype)


def _diffattn_kernel(lam_ref, g_ref, q_ref, k_ref, v_ref, o_ref, vaug_ref, s_ref, m_ref, p_ref,
                     *, tq, nq, lambda_init):
    _fill_v_ones(vaug_ref, v_ref[0])
    _diff_scores(1, q_ref[0, 0:tq, :], k_ref, s_ref.at[0], m_ref.at[0], tq=tq)
    for c in range(nq):
        if c + 1 < nq:
            nxt = (c + 1) % 2
            _diff_scores(c + 2, q_ref[0, (c + 1) * tq:(c + 2) * tq, :], k_ref, s_ref.at[nxt], m_ref.at[nxt],
                         tq=tq)
        _diff_output(c + 1, s_ref.at[c % 2], m_ref.at[c % 2], vaug_ref, p_ref.at[c % 2], lam_ref, g_ref,
                     o_ref.at[0, c * tq:(c + 1) * tq, :], tq=tq, lambda_init=lambda_init)


def _diffattn_call(lam, g, h3, tq, lambda_init):
    B, S, _ = h3.shape
    nq = S // tq
    return pl.pallas_call(
        functools.partial(_diffattn_kernel, tq=tq, nq=nq, lambda_init=lambda_init),
        grid=(B, N_HEADS_DIFF),
        scratch_shapes=[pltpu.VMEM((S, 2 * LANES), BF16),
                        pltpu.VMEM((2, nq, 2 * tq, tq), F32),
                        pltpu.VMEM((2, 2 * tq, LANES), F32),
                        pltpu.VMEM((2, 2 * tq, S), BF16)],
        in_specs=[
            pl.BlockSpec((1, LANES), lambda b, h: (0, 0)),
            pl.BlockSpec((1, DIFF_V_DIM), lambda b, h: (0, 0)),
            pl.BlockSpec((1, S, LANES), lambda b, h: (b, 0, QA_BLK + h)),
            pl.BlockSpec((1, S, LANES), lambda b, h: (b, 0, KA_BLK + h)),
            pl.BlockSpec((1, S, LANES), lambda b, h: (b, 0, VA_BLK + h)),
        ],
        out_specs=pl.BlockSpec((1, S, LANES), lambda b, h: (b, 0, h)),
        out_shape=jax.ShapeDtypeStruct((B, S, N_HEADS_DIFF * DIFF_V_DIM), BF16),
        compiler_params=_params(("arbitrary", "arbitrary")),
        name="diffattn",
    )(lam, g, h3, h3, h3)


CA_TQ = 256
CA_NKB = 3


def _chunk_bias_table(rel_bias):
    nh = rel_bias.shape[0]
    width = CA_NKB * CA_TQ
    n_far = width - 1 - REL_CLIP + 1
    n_neg = (CA_TQ - 1) - REL_CLIP + 1
    diag = jnp.concatenate([
        jnp.broadcast_to(rel_bias[:, 2 * REL_CLIP:], (nh, n_far)),
        rel_bias[:, 1:2 * REL_CLIP][:, ::-1],
        jnp.broadcast_to(rel_bias[:, :1], (nh, n_neg + 1)),
    ], axis=1).astype(F32)
    period = width + CA_TQ
    assert diag.shape[1] == period
    diag = jnp.concatenate([diag[:, CA_TQ - 1:], diag[:, :CA_TQ - 1]], axis=1)
    bias = jnp.tile(diag, (1, CA_TQ))[:, :CA_TQ * (period - 1)].reshape(nh, CA_TQ, period - 1)[:, :, :width]
    r = np.arange(CA_TQ)[:, None]
    koff = np.arange(width)[None, :] - (CA_NKB - 1) * CA_TQ
    qc = r // CHUNK
    kc = np.floor_divide(koff, CHUNK)
    allowed = (kc <= qc) & (kc >= qc - LEFT_CHUNKS)
    bias = jnp.where(allowed[None], bias, NEG_INF)
    return bias.reshape(nh // 2, 2 * CA_TQ, width)


def _chunkattn_kernel(bias_ref, q_ref, k_ref, v_ref, o_ref, vaug_ref, s_ref, m_ref, p_ref):
    nq = q_ref.shape[1] // CA_TQ
    nl = CA_TQ // LANES

    def key_blocks(blk):
        return [(j, blk - (CA_NKB - 1) + j) for j in range(CA_NKB) if blk - (CA_NKB - 1) + j >= 0]

    def scores(blk, slot):
        qq = _stack_halves(q_ref[0, blk * CA_TQ:(blk + 1) * CA_TQ, :])
        mpart = None
        for j, kb in key_blocks(blk):
            s = _dot_nt(qq, k_ref[0, kb * CA_TQ:(kb + 1) * CA_TQ, :])
            s = s + bias_ref[0, :, j * CA_TQ:(j + 1) * CA_TQ]
            s_ref[slot, j] = s
            for c in range(nl):
                sl = s[:, c * LANES:(c + 1) * LANES]
                mpart = sl if mpart is None else jnp.maximum(mpart, sl)
        m_ref[slot] = jnp.broadcast_to(jnp.max(mpart, axis=1, keepdims=True), mpart.shape)

    def output(blk, slot):
        m = m_ref[slot]
        blocks = key_blocks(blk)
        for j, _ in blocks:
            for c in range(nl):
                col0 = j * CA_TQ + c * LANES
                p_ref[slot, :, col0:col0 + LANES] = jnp.exp(
                    s_ref[slot, j, :, c * LANES:(c + 1) * LANES] - m).astype(BF16)
        j0, kb0 = blocks[0]
        acc = _dot(p_ref[slot, :, j0 * CA_TQ:], vaug_ref[kb0 * CA_TQ:(blk + 1) * CA_TQ, :])
        o = acc[:, :LANES] / acc[:, LANES:]
        lane = lax.broadcasted_iota(jnp.int32, (CA_TQ, LANES), 1)
        o_ref[0, blk * CA_TQ:(blk + 1) * CA_TQ, :] = jnp.where(
            lane < HEAD_DIM, o[:CA_TQ], o[CA_TQ:]).astype(o_ref.dtype)

    _fill_v_ones(vaug_ref, v_ref[0])
    scores(0, 0)
    for blk in range(nq):
        if blk + 1 < nq:
            scores(blk + 1, (blk + 1) % 2)
        output(blk, blk % 2)


def _chunkattn_call(bias, h3):
    B, S, _ = h3.shape
    npairs = N_HEADS_CHUNK // 2
    return pl.pallas_call(
        _chunkattn_kernel,
        grid=(npairs, B),
        in_specs=[
            pl.BlockSpec((1, 2 * CA_TQ, CA_NKB * CA_TQ), lambda p, b: (p, 0, 0)),
            pl.BlockSpec((1, S, LANES), lambda p, b: (b, 0, QB_BLK + p)),
            pl.BlockSpec((1, S, LANES), lambda p, b: (b, 0, KB_BLK + p)),
            pl.BlockSpec((1, S, LANES), lambda p, b: (b, 0, VB_BLK + p)),
        ],
        out_specs=pl.BlockSpec((1, S, LANES), lambda p, b: (b, 0, p)),
        out_shape=jax.ShapeDtypeStruct((B, S, N_HEADS_CHUNK * HEAD_DIM), BF16),
        scratch_shapes=[pltpu.VMEM((S, 2 * LANES), BF16),
                        pltpu.VMEM((2, CA_NKB, 2 * CA_TQ, CA_TQ), F32),
                        pltpu.VMEM((2, 2 * CA_TQ, LANES), F32),
                        pltpu.VMEM((2, 2 * CA_TQ, CA_NKB * CA_TQ), BF16)],
        compiler_params=_params(("arbitrary", "arbitrary")),
        name="chunkattn",
    )(bias, h3, h3, h3)


def _outproj_kernel(x_ref, ya_ref, yb_ref, wo_ref, g_ref, b_ref, wq_ref, x1_ref, q_ref):
    half = ya_ref.shape[1]
    y = _dot(ya_ref[...], wo_ref[0:half, :]) + _dot(yb_ref[...], wo_ref[half:, :])
    x1 = _layer_norm(DEEPNORM_ALPHA * x_ref[...] + y, g_ref[...], b_ref[...])
    x1_ref[...] = x1
    q = _dot(x1.astype(BF16), wq_ref[...]) * (MEM_HEAD_DIM ** -0.5)
    q_ref[...] = q.astype(BF16)


def _outproj_call(x2d, ya, yb, w_o, g, b, w_mq, tm):
    T = x2d.shape[0]
    half = ya.shape[1]
    row = lambda i: (i, 0)
    fixed = lambda i: (0, 0)
    return pl.pallas_call(
        _outproj_kernel,
        grid=(T // tm,),
        in_specs=[
            pl.BlockSpec((tm, D_MODEL), row),
            pl.BlockSpec((tm, half), row),
            pl.BlockSpec((tm, half), row),
            pl.BlockSpec((D_MODEL, D_MODEL), fixed),
            pl.BlockSpec((1, D_MODEL), fixed),
            pl.BlockSpec((1, D_MODEL), fixed),
            pl.BlockSpec((D_MODEL, D_MODEL), fixed),
        ],
        out_specs=[pl.BlockSpec((tm, D_MODEL), row), pl.BlockSpec((tm, D_MODEL), row)],
        out_shape=[jax.ShapeDtypeStruct((T, D_MODEL), F32), jax.ShapeDtypeStruct((T, D_MODEL), BF16)],
        compiler_params=_params(("arbitrary",)),
        name="outproj",
    )(x2d, ya, yb, w_o, g, b, w_mq)


def _memkv_kernel(mem_ref, wk_ref, wv_ref, k_ref, v_ref):
    mb = mem_ref[0].astype(BF16)
    k_ref[0] = _dot(mb, wk_ref[...]).astype(BF16)
    v_ref[0] = _dot(mb, wv_ref[...]).astype(BF16)


def _memkv_call(mem, w_mk, w_mv):
    B, M, _ = mem.shape
    blk = pl.BlockSpec((1, M, D_MODEL), lambda b: (b, 0, 0))
    fixed = pl.BlockSpec((D_MODEL, D_MODEL), lambda b: (0, 0))
    return pl.pallas_call(
        _memkv_kernel,
        grid=(B,),
        in_specs=[blk, fixed, fixed],
        out_specs=[blk, blk],
        out_shape=[jax.ShapeDtypeStruct((B, M, D_MODEL), BF16)] * 2,
        compiler_params=_params(("arbitrary",)),
        name="memkv",
    )(mem, w_mk, w_mv)


def _memattn_kernel(x1_ref, q_ref, k_ref, v_ref, wo_ref, g_ref, b_ref, x2_ref):
    outs = []
    for h in range(N_HEADS_MEM):
        sl = slice(h * MEM_HEAD_DIM, (h + 1) * MEM_HEAD_DIM)
        s = _dot_nt(q_ref[0, :, sl], k_ref[0, :, sl])
        m = jnp.max(s, axis=1, keepdims=True)
        p = jnp.exp(s - m)
        l = jnp.sum(p, axis=1, keepdims=True)
        outs.append((_dot(p.astype(BF16), v_ref[0, :, sl]) / l).astype(BF16))
    o = jnp.concatenate(outs, axis=1)
    y = _dot(o, wo_ref[...])
    x2_ref[0] = _layer_norm(DEEPNORM_ALPHA * x1_ref[0] + y, g_ref[...], b_ref[...])


def _memattn_call(x1, q, k, v, w_mo, g, b, tm):
    B, S, _ = x1.shape
    M = k.shape[1]
    row = pl.BlockSpec((1, tm, D_MODEL), lambda bb, i: (bb, i, 0))
    kv = pl.BlockSpec((1, M, D_MODEL), lambda bb, i: (bb, 0, 0))
    vec = pl.BlockSpec((1, D_MODEL), lambda bb, i: (0, 0))
    return pl.pallas_call(
        _memattn_kernel,
        grid=(B, S // tm),
        in_specs=[row, row, kv, kv, pl.BlockSpec((D_MODEL, D_MODEL), lambda bb, i: (0, 0)), vec, vec],
        out_specs=row,
        out_shape=jax.ShapeDtypeStruct((B, S, D_MODEL), F32),
        compiler_params=_params(("arbitrary", "arbitrary")),
        name="memattn",
    )(x1, q, k, v, w_mo, g, b)


def _mlp_kernel(x_ref, wu_ref, wd_ref, g_ref, b_ref, o_ref, *, ff_chunk):
    x = x_ref[...]
    xb = x.astype(BF16)
    y = None
    for c in range(D_FF // ff_chunk):
        sl = slice(c * ff_chunk, (c + 1) * ff_chunk)
        h = jnp.maximum(_dot(xb, wu_ref[:, sl]), 0.0)
        part = _dot((h * h).astype(BF16), wd_ref[sl, :])
        y = part if y is None else y + part
    o_ref[...] = _layer_norm(DEEPNORM_ALPHA * x + y, g_ref[...], b_ref[...])


def _mlp_call(x2d, w_up, w_down, g, b, tm, ff_chunk):
    T = x2d.shape[0]
    row = pl.BlockSpec((tm, D_MODEL), lambda i: (i, 0))
    vec = pl.BlockSpec((1, D_MODEL), lambda i: (0, 0))
    return pl.pallas_call(
        functools.partial(_mlp_kernel, ff_chunk=ff_chunk),
        grid=(T // tm,),
        in_specs=[row,
                  pl.BlockSpec((D_MODEL, D_FF), lambda i: (0, 0), pipeline_mode=pl.Buffered(1)),
                  pl.BlockSpec((D_FF, D_MODEL), lambda i: (0, 0), pipeline_mode=pl.Buffered(1)),
                  vec, vec],
        out_specs=row,
        out_shape=jax.ShapeDtypeStruct((T, D_MODEL), F32),
        compiler_params=_params(("arbitrary",)),
        name="mlp",
    )(x2d, w_up, w_down, g, b)


def kernel(x, mem, positions, w_in, diff_lambda, subln_g, rel_bias, w_o, ln1_g, ln1_b,
           w_mq, w_mk, w_mv, w_mo, ln2_g, ln2_b, w_up, w_down, ln3_g, ln3_b):
    B, S, D = x.shape
    T = B * S
    depth = w_in.shape[0]
    assert depth == DEPTH and D == D_MODEL and S % CA_TQ == 0
    inv_freq = 1.0 / (ROPE_THETA ** (jnp.arange(0, HEAD_DIM, 2, dtype=F32) / HEAD_DIM))
    invf = jnp.tile(inv_freq, LANES // (HEAD_DIM // 2)).reshape(1, LANES)
    pos = positions.reshape(T, 1)
    vec = lambda a: a.reshape(1, -1)

    for l in range(depth):
        lambda_init = 0.8 - 0.6 * math.exp(-0.3 * l)
        lam = _lam_call(diff_lambda[l], lambda_init)
        h = _inproj_call(pos, invf, x.reshape(T, D), w_in[l].astype(BF16), tm=512)
        h3 = h.reshape(B, S, IN_WIDTH)
        ya = _diffattn_call(lam, vec(subln_g[l]), h3, tq=256, lambda_init=lambda_init)
        yb = _chunkattn_call(_chunk_bias_table(rel_bias[l]), h3)
        x1, qm = _outproj_call(x.reshape(T, D), ya.reshape(T, -1), yb.reshape(T, -1),
                               w_o[l].astype(BF16), vec(ln1_g[l]), vec(ln1_b[l]),
                               w_mq[l].astype(BF16), tm=512)
        km, vm = _memkv_call(mem, w_mk[l].astype(BF16), w_mv[l].astype(BF16))
        x2 = _memattn_call(x1.reshape(B, S, D), qm.reshape(B, S, D), km, vm,
                           w_mo[l].astype(BF16), vec(ln2_g[l]), vec(ln2_b[l]), tm=512)
        out = _mlp_call(x2.reshape(T, D), w_up[l].astype(BF16), w_down[l].astype(BF16),
                        vec(ln3_g[l]), vec(ln3_b[l]), tm=512, ff_chunk=1024)
        x = out.reshape(B, S, D)
    return x
```

```python
import functools
import math

import jax
import jax.numpy as jnp
import numpy as np
from jax import lax
from jax.experimental import pallas as pl
from jax.experimental.pallas import tpu as pltpu

D_MODEL = 1024
CHUNK = 64
HEAD_DIM = 64
N_HEADS_DIFF = 4
DIFF_V_DIM = 128
N_HEADS_CHUNK = 8
LEFT_CHUNKS = 8
REL_CLIP = 128
N_HEADS_MEM = 4
MEM_HEAD_DIM = 256
D_FF = 4096
ROPE_THETA = 10000.0
LN_EPS = 1e-5
NEG_INF = -1e30
DEPTH = 1
DEEPNORM_ALPHA = (2.0 * DEPTH) ** 0.25
IN_WIDTH = 3072

LANES = 128
VMEM_LIMIT = 56 * 1024 * 1024

F32 = jnp.float32
BF16 = jnp.bfloat16

QA_BLK, KA_BLK, VA_BLK, QB_BLK, KB_BLK, VB_BLK = 0, 4, 8, 12, 16, 20


def _params(sem):
    return pltpu.CompilerParams(dimension_semantics=sem, vmem_limit_bytes=VMEM_LIMIT)


def _dot(a, b):
    return jnp.dot(a, b, preferred_element_type=F32)


def _dot_nt(a, b):
    return lax.dot_general(a, b, (((1,), (1,)), ((), ())), preferred_element_type=F32)


SUB_ROWS = 512


def _sub_tiles(n_rows):
    return [slice(r, r + SUB_ROWS) for r in range(0, n_rows, SUB_ROWS)]


def _layer_norm(z, g, b):
    mu = jnp.mean(z, axis=-1, keepdims=True)
    zc = z - mu
    var = jnp.mean(zc * zc, axis=-1, keepdims=True)
    return zc * lax.rsqrt(var + LN_EPS) * g + b


def _lam_kernel(dl_ref, o_ref, *, lambda_init):
    dl = dl_ref[...]
    s1 = jnp.sum(dl[0:1, :] * dl[1:2, :], axis=1, keepdims=True)
    s2 = jnp.sum(dl[2:3, :] * dl[3:4, :], axis=1, keepdims=True)
    lam = jnp.exp(s1) - jnp.exp(s2) + lambda_init
    o_ref[...] = jnp.broadcast_to(lam, o_ref.shape)


def _lam_call(diff_lambda, lambda_init):
    return pl.pallas_call(
        functools.partial(_lam_kernel, lambda_init=lambda_init),
        out_shape=jax.ShapeDtypeStruct((1, LANES), F32),
        name="lam",
    )(diff_lambda)


def _inproj_kernel(pos_ref, invf_ref, x_ref, w_ref, o_ref):
    xb = x_ref[...].astype(BF16)
    ang = pos_ref[...].astype(F32) * invf_ref[...]
    cos = jnp.cos(ang)
    sin = jnp.sin(ang)
    lane = lax.broadcasted_iota(jnp.int32, ang.shape, 1)
    upper = (lane % HEAD_DIM) >= (HEAD_DIM // 2)
    sin_up = jnp.where(upper, sin, 0.0)
    sin_lo = jnp.where(upper, 0.0, -sin)
    n_chunks = IN_WIDTH // 512
    for c in range(n_chunks):
        h = _dot(xb, w_ref[:, c * 512:(c + 1) * 512])
        if c < 2:
            slabs = []
            for s in range(4):
                sl = h[:, s * LANES:(s + 1) * LANES]
                slabs.append(sl * cos
                             + pltpu.roll(sl, HEAD_DIM // 2, 1) * sin_up
                             + pltpu.roll(sl, LANES - HEAD_DIM // 2, 1) * sin_lo)
            h = jnp.concatenate(slabs, axis=1)
        if c == 0 or c == 3:
            h = h * (HEAD_DIM ** -0.5)
        o_ref[:, c * 512:(c + 1) * 512] = h.astype(BF16)


def _inproj_call(pos, invf, x2d, w_in, tm):
    T = x2d.shape[0]
    return pl.pallas_call(
        _inproj_kernel,
        grid=(T // tm,),
        in_specs=[
            pl.BlockSpec((tm, 1), lambda i: (i, 0)),
            pl.BlockSpec((1, LANES), lambda i: (0, 0)),
            pl.BlockSpec((tm, D_MODEL), lambda i: (i, 0)),
            pl.BlockSpec((D_MODEL, IN_WIDTH), lambda i: (0, 0)),
        ],
        out_specs=pl.BlockSpec((tm, IN_WIDTH), lambda i: (i, 0)),
        out_shape=jax.ShapeDtypeStruct((T, IN_WIDTH), BF16),
        compiler_params=_params(("arbitrary",)),
        name="inproj",
    )(pos, invf, x2d, w_in)


def _stack_halves(q):
    lane = lax.broadcasted_iota(jnp.int32, q.shape, 1)
    zero = jnp.zeros_like(q)
    return jnp.concatenate([jnp.where(lane < HEAD_DIM, q, zero),
                            jnp.where(lane >= HEAD_DIM, q, zero)], axis=0)


def _fill_v_ones(vaug_ref, v):
    vaug_ref[:, :LANES] = v
    vaug_ref[:, LANES:] = jnp.ones(v.shape, v.dtype)


def _diff_scores(nkb, q, k_ref, s_ref, m_ref, *, tq):
    qq = _stack_halves(q)
    mpart = None
    for j in range(nkb):
        s = _dot_nt(qq, k_ref[0, j * tq:(j + 1) * tq, :])
        if j == nkb - 1:
            row = lax.broadcasted_iota(jnp.int32, s.shape, 0)
            col = lax.broadcasted_iota(jnp.int32, s.shape, 1)
            s = jnp.where((col // CHUNK) <= ((row % tq) // CHUNK), s, NEG_INF)
        s_ref[j] = s
        for c in range(tq // LANES):
            sl = s[:, c * LANES:(c + 1) * LANES]
            mpart = sl if mpart is None else jnp.maximum(mpart, sl)
    m_ref[...] = jnp.broadcast_to(jnp.max(mpart, axis=1, keepdims=True), (2 * tq, LANES))


def _diff_output(nkb, s_ref, m_ref, vaug_ref, p_ref, lam_ref, g_ref, o_ref, *, tq, lambda_init):
    m = m_ref[...]
    for j in range(nkb):
        for c in range(tq // LANES):
            col0 = j * tq + c * LANES
            p_ref[:, col0:col0 + LANES] = jnp.exp(s_ref[j, :, c * LANES:(c + 1) * LANES] - m).astype(BF16)
    acc = _dot(p_ref[:, :nkb * tq], vaug_ref[:nkb * tq, :])
    o = acc[:, :LANES] / acc[:, LANES:]
    o = o[:tq] - lam_ref[...] * o[tq:]
    y = o * lax.rsqrt(jnp.mean(o * o, axis=-1, keepdims=True) + LN_EPS) * g_ref[...]
    o_ref[...] = (y * (1.0 - lambda_init)).astype(o_ref.dtype)


def _diffattn_kernel(lam_ref, g_ref, q_ref, k_ref, v_ref, o_ref, vaug_ref, s_ref, m_ref, p_ref,
                     *, tq, nq, lambda_init):
    _fill_v_ones(vaug_ref, v_ref[0])
    _diff_scores(1, q_ref[0, 0:tq, :], k_ref, s_ref.at[0], m_ref.at[0], tq=tq)
    for c in range(nq):
        if c + 1 < nq:
            nxt = (c + 1) % 2
            _diff_scores(c + 2, q_ref[0, (c + 1) * tq:(c + 2) * tq, :], k_ref, s_ref.at[nxt], m_ref.at[nxt],
                         tq=tq)
        _diff_output(c + 1, s_ref.at[c % 2], m_ref.at[c % 2], vaug_ref, p_ref.at[c % 2], lam_ref, g_ref,
                     o_ref.at[0, c * tq:(c + 1) * tq, :], tq=tq, lambda_init=lambda_init)


def _diffattn_call(lam, g, h3, tq, lambda_init):
    B, S, _ = h3.shape
    nq = S // tq
    return pl.pallas_call(
        functools.partial(_diffattn_kernel, tq=tq, nq=nq, lambda_init=lambda_init),
        grid=(B, N_HEADS_DIFF),
        scratch_shapes=[pltpu.VMEM((S, 2 * LANES), BF16),
                        pltpu.VMEM((2, nq, 2 * tq, tq), F32),
                        pltpu.VMEM((2, 2 * tq, LANES), F32),
                        pltpu.VMEM((2, 2 * tq, S), BF16)],
        in_specs=[
            pl.BlockSpec((1, LANES), lambda b, h: (0, 0)),
            pl.BlockSpec((1, DIFF_V_DIM), lambda b, h: (0, 0)),
            pl.BlockSpec((1, S, LANES), lambda b, h: (b, 0, QA_BLK + h)),
            pl.BlockSpec((1, S, LANES), lambda b, h: (b, 0, KA_BLK + h)),
            pl.BlockSpec((1, S, LANES), lambda b, h: (b, 0, VA_BLK + h)),
        ],
        out_specs=pl.BlockSpec((1, S, LANES), lambda b, h: (b, 0, h)),
        out_shape=jax.ShapeDtypeStruct((B, S, N_HEADS_DIFF * DIFF_V_DIM), BF16),
        compiler_params=_params(("arbitrary", "arbitrary")),
        name="diffattn",
    )(lam, g, h3, h3, h3)


CA_TQ = 256
CA_NKB = 3


def _chunk_bias_table(rel_bias):
    nh = rel_bias.shape[0]
    width = CA_NKB * CA_TQ
    n_far = width - 1 - REL_CLIP + 1
    n_neg = (CA_TQ - 1) - REL_CLIP + 1
    diag = jnp.concatenate([
        jnp.broadcast_to(rel_bias[:, 2 * REL_CLIP:], (nh, n_far)),
        rel_bias[:, 1:2 * REL_CLIP][:, ::-1],
        jnp.broadcast_to(rel_bias[:, :1], (nh, n_neg + 1)),
    ], axis=1).astype(F32)
    period = width + CA_TQ
    assert diag.shape[1] == period
    diag = jnp.concatenate([diag[:, CA_TQ - 1:], diag[:, :CA_TQ - 1]], axis=1)
    bias = jnp.tile(diag, (1, CA_TQ))[:, :CA_TQ * (period - 1)].reshape(nh, CA_TQ, period - 1)[:, :, :width]
    r = np.arange(CA_TQ)[:, None]
    koff = np.arange(width)[None, :] - (CA_NKB - 1) * CA_TQ
    qc = r // CHUNK
    kc = np.floor_divide(koff, CHUNK)
    allowed = (kc <= qc) & (kc >= qc - LEFT_CHUNKS)
    bias = jnp.where(allowed[None], bias, NEG_INF)
    return bias.reshape(nh // 2, 2 * CA_TQ, width)


def _chunkattn_kernel(bias_ref, q_ref, k_ref, v_ref, o_ref, vaug_ref, s_ref, m_ref, p_ref):
    nq = q_ref.shape[1] // CA_TQ
    nl = CA_TQ // LANES

    def key_blocks(blk):
        return [(j, blk - (CA_NKB - 1) + j) for j in range(CA_NKB) if blk - (CA_NKB - 1) + j >= 0]

    def scores(blk, slot):
        qq = _stack_halves(q_ref[0, blk * CA_TQ:(blk + 1) * CA_TQ, :])
        mpart = None
        for j, kb in key_blocks(blk):
            s = _dot_nt(qq, k_ref[0, kb * CA_TQ:(kb + 1) * CA_TQ, :])
            s = s + bias_ref[0, :, j * CA_TQ:(j + 1) * CA_TQ]
            s_ref[slot, j] = s
            for c in range(nl):
                sl = s[:, c * LANES:(c + 1) * LANES]
                mpart = sl if mpart is None else jnp.maximum(mpart, sl)
        m_ref[slot] = jnp.broadcast_to(jnp.max(mpart, axis=1, keepdims=True), mpart.shape)

    def output(blk, slot):
        m = m_ref[slot]
        blocks = key_blocks(blk)
        for j, _ in blocks:
            for c in range(nl):
                col0 = j * CA_TQ + c * LANES
                p_ref[slot, :, col0:col0 + LANES] = jnp.exp(
                    s_ref[slot, j, :, c * LANES:(c + 1) * LANES] - m).astype(BF16)
        j0, kb0 = blocks[0]
        acc = _dot(p_ref[slot, :, j0 * CA_TQ:], vaug_ref[kb0 * CA_TQ:(blk + 1) * CA_TQ, :])
        o = acc[:, :LANES] / acc[:, LANES:]
        lane = lax.broadcasted_iota(jnp.int32, (CA_TQ, LANES), 1)
        o_ref[0, blk * CA_TQ:(blk + 1) * CA_TQ, :] = jnp.where(
            lane < HEAD_DIM, o[:CA_TQ], o[CA_TQ:]).astype(o_ref.dtype)

    _fill_v_ones(vaug_ref, v_ref[0])
    scores(0, 0)
    for blk in range(nq):
        if blk + 1 < nq:
            scores(blk + 1, (blk + 1) % 2)
        output(blk, blk % 2)


def _chunkattn_call(bias, h3):
    B, S, _ = h3.shape
    npairs = N_HEADS_CHUNK // 2
    return pl.pallas_call(
        _chunkattn_kernel,
        grid=(npairs, B),
        in_specs=[
            pl.BlockSpec((1, 2 * CA_TQ, CA_NKB * CA_TQ), lambda p, b: (p, 0, 0)),
            pl.BlockSpec((1, S, LANES), lambda p, b: (b, 0, QB_BLK + p)),
            pl.BlockSpec((1, S, LANES), lambda p, b: (b, 0, KB_BLK + p)),
            pl.BlockSpec((1, S, LANES), lambda p, b: (b, 0, VB_BLK + p)),
        ],
        out_specs=pl.BlockSpec((1, S, LANES), lambda p, b: (b, 0, p)),
        out_shape=jax.ShapeDtypeStruct((B, S, N_HEADS_CHUNK * HEAD_DIM), BF16),
        scratch_shapes=[pltpu.VMEM((S, 2 * LANES), BF16),
                        pltpu.VMEM((2, CA_NKB, 2 * CA_TQ, CA_TQ), F32),
                        pltpu.VMEM((2, 2 * CA_TQ, LANES), F32),
                        pltpu.VMEM((2, 2 * CA_TQ, CA_NKB * CA_TQ), BF16)],
        compiler_params=_params(("arbitrary", "arbitrary")),
        name="chunkattn",
    )(bias, h3, h3, h3)


def _outproj_kernel(x_ref, ya_ref, yb_ref, wo_ref, g_ref, b_ref, wq_ref, x1_ref, q_ref):
    half = ya_ref.shape[1]
    tiles = _sub_tiles(x_ref.shape[0])
    ys = [_dot(ya_ref[rows, :], wo_ref[0:half, :]) + _dot(yb_ref[rows, :], wo_ref[half:, :]) for rows in tiles]
    for rows, y in zip(tiles, ys):
        x1 = _layer_norm(DEEPNORM_ALPHA * x_ref[rows, :] + y, g_ref[...], b_ref[...])
        x1_ref[rows, :] = x1
        q = _dot(x1.astype(BF16), wq_ref[...]) * (MEM_HEAD_DIM ** -0.5)
        q_ref[rows, :] = q.astype(BF16)


def _outproj_call(x2d, ya, yb, w_o, g, b, w_mq, tm):
    T = x2d.shape[0]
    half = ya.shape[1]
    row = lambda i: (i, 0)
    fixed = lambda i: (0, 0)
    return pl.pallas_call(
        _outproj_kernel,
        grid=(T // tm,),
        in_specs=[
            pl.BlockSpec((tm, D_MODEL), row),
            pl.BlockSpec((tm, half), row),
            pl.BlockSpec((tm, half), row),
            pl.BlockSpec((D_MODEL, D_MODEL), fixed),
            pl.BlockSpec((1, D_MODEL), fixed),
            pl.BlockSpec((1, D_MODEL), fixed),
            pl.BlockSpec((D_MODEL, D_MODEL), fixed),
        ],
        out_specs=[pl.BlockSpec((tm, D_MODEL), row), pl.BlockSpec((tm, D_MODEL), row)],
        out_shape=[jax.ShapeDtypeStruct((T, D_MODEL), F32), jax.ShapeDtypeStruct((T, D_MODEL), BF16)],
        compiler_params=_params(("arbitrary",)),
        name="outproj",
    )(x2d, ya, yb, w_o, g, b, w_mq)


def _memkv_kernel(mem_ref, wk_ref, wv_ref, k_ref, v_ref):
    mb = mem_ref[0].astype(BF16)
    k_ref[0] = _dot(mb, wk_ref[...]).astype(BF16)
    v_ref[0] = _dot(mb, wv_ref[...]).astype(BF16)


def _memkv_call(mem, w_mk, w_mv):
    B, M, _ = mem.shape
    blk = pl.BlockSpec((1, M, D_MODEL), lambda b: (b, 0, 0))
    fixed = pl.BlockSpec((D_MODEL, D_MODEL), lambda b: (0, 0))
    return pl.pallas_call(
        _memkv_kernel,
        grid=(B,),
        in_specs=[blk, fixed, fixed],
        out_specs=[blk, blk],
        out_shape=[jax.ShapeDtypeStruct((B, M, D_MODEL), BF16)] * 2,
        compiler_params=_params(("arbitrary",)),
        name="memkv",
    )(mem, w_mk, w_mv)


def _memattn_kernel(x1_ref, q_ref, k_ref, v_ref, wo_ref, g_ref, b_ref, x2_ref):
    tiles = _sub_tiles(x1_ref.shape[1])
    attn = []
    for rows in tiles:
        outs = []
        for h in range(N_HEADS_MEM):
            sl = slice(h * MEM_HEAD_DIM, (h + 1) * MEM_HEAD_DIM)
            s = _dot_nt(q_ref[0, rows, sl], k_ref[0, :, sl])
            m = jnp.max(s, axis=1, keepdims=True)
            p = jnp.exp(s - m)
            l = jnp.sum(p, axis=1, keepdims=True)
            outs.append((_dot(p.astype(BF16), v_ref[0, :, sl]) / l).astype(BF16))
        attn.append(jnp.concatenate(outs, axis=1))
    ys = [_dot(o, wo_ref[...]) for o in attn]
    for rows, y in zip(tiles, ys):
        x2_ref[0, rows, :] = _layer_norm(DEEPNORM_ALPHA * x1_ref[0, rows, :] + y, g_ref[...], b_ref[...])


def _memattn_call(x1, q, k, v, w_mo, g, b, tm):
    B, S, _ = x1.shape
    M = k.shape[1]
    row = pl.BlockSpec((1, tm, D_MODEL), lambda bb, i: (bb, i, 0))
    kv = pl.BlockSpec((1, M, D_MODEL), lambda bb, i: (bb, 0, 0))
    vec = pl.BlockSpec((1, D_MODEL), lambda bb, i: (0, 0))
    return pl.pallas_call(
        _memattn_kernel,
        grid=(B, S // tm),
        in_specs=[row, row, kv, kv, pl.BlockSpec((D_MODEL, D_MODEL), lambda bb, i: (0, 0)), vec, vec],
        out_specs=row,
        out_shape=jax.ShapeDtypeStruct((B, S, D_MODEL), F32),
        compiler_params=_params(("arbitrary", "arbitrary")),
        name="memattn",
    )(x1, q, k, v, w_mo, g, b)


def _mlp_kernel(x_ref, wu_ref, wd_ref, g_ref, b_ref, o_ref, *, ff_chunk):
    def finish(rows, y):
        o_ref[rows, :] = _layer_norm(DEEPNORM_ALPHA * x_ref[rows, :] + y, g_ref[...], b_ref[...])

    pending = None
    for rows in _sub_tiles(x_ref.shape[0]):
        xb = x_ref[rows, :].astype(BF16)
        y = None
        for c in range(D_FF // ff_chunk):
            sl = slice(c * ff_chunk, (c + 1) * ff_chunk)
            h = jnp.maximum(_dot(xb, wu_ref[:, sl]), 0.0)
            part = _dot((h * h).astype(BF16), wd_ref[sl, :])
            y = part if y is None else y + part
            if c == 0 and pending is not None:
                finish(*pending)
                pending = None
        pending = (rows, y)
    finish(*pending)


def _mlp_call(x2d, w_up, w_down, g, b, tm, ff_chunk):
    T = x2d.shape[0]
    row = pl.BlockSpec((tm, D_MODEL), lambda i: (i, 0))
    vec = pl.BlockSpec((1, D_MODEL), lambda i: (0, 0))
    return pl.pallas_call(
        functools.partial(_mlp_kernel, ff_chunk=ff_chunk),
        grid=(T // tm,),
        in_specs=[row,
                  pl.BlockSpec((D_MODEL, D_FF), lambda i: (0, 0), pipeline_mode=pl.Buffered(1)),
                  pl.BlockSpec((D_FF, D_MODEL), lambda i: (0, 0), pipeline_mode=pl.Buffered(1)),
                  vec, vec],
        out_specs=row,
        out_shape=jax.ShapeDtypeStruct((T, D_MODEL), F32),
        compiler_params=_params(("arbitrary",)),
        name="mlp",
    )(x2d, w_up, w_down, g, b)


def kernel(x, mem, positions, w_in, diff_lambda, subln_g, rel_bias, w_o, ln1_g, ln1_b,
           w_mq, w_mk, w_mv, w_mo, ln2_g, ln2_b, w_up, w_down, ln3_g, ln3_b):
    B, S, D = x.shape
    T = B * S
    depth = w_in.shape[0]
    assert depth == DEPTH and D == D_MODEL and S % CA_TQ == 0
    inv_freq = 1.0 / (ROPE_THETA ** (jnp.arange(0, HEAD_DIM, 2, dtype=F32) / HEAD_DIM))
    invf = jnp.tile(inv_freq, LANES // (HEAD_DIM // 2)).reshape(1, LANES)
    pos = positions.reshape(T, 1)
    vec = lambda a: a.reshape(1, -1)

    for l in range(depth):
        lambda_init = 0.8 - 0.6 * math.exp(-0.3 * l)
        lam = _lam_call(diff_lambda[l], lambda_init)
        h = _inproj_call(pos, invf, x.reshape(T, D), w_in[l].astype(BF16), tm=512)
        h3 = h.reshape(B, S, IN_WIDTH)
        ya = _diffattn_call(lam, vec(subln_g[l]), h3, tq=256, lambda_init=lambda_init)
        yb = _chunkattn_call(_chunk_bias_table(rel_bias[l]), h3)
        x1, qm = _outproj_call(x.reshape(T, D), ya.reshape(T, -1), yb.reshape(T, -1),
                               w_o[l].astype(BF16), vec(ln1_g[l]), vec(ln1_b[l]),
                               w_mq[l].astype(BF16), tm=1024)
        km, vm = _memkv_call(mem, w_mk[l].astype(BF16), w_mv[l].astype(BF16))
        x2 = _memattn_call(x1.reshape(B, S, D), qm.reshape(B, S, D), km, vm,
                           w_mo[l].astype(BF16), vec(ln2_g[l]), vec(ln2_b[l]), tm=1024)
        out = _mlp_call(x2.reshape(T, D), w_up[l].astype(BF16), w_down[l].astype(BF16),
                        vec(ln3_g[l]), vec(ln3_b[l]), tm=1024, ff_chunk=1024)
        x = out.reshape(B, S, D)
    return x
```

```python
import functools
import math

import jax
import jax.numpy as jnp
import numpy as np
from jax import lax
from jax.experimental import pallas as pl
from jax.experimental.pallas import tpu as pltpu

D_MODEL = 1024
CHUNK = 64
HEAD_DIM = 64
N_HEADS_DIFF = 4
DIFF_V_DIM = 128
N_HEADS_CHUNK = 8
LEFT_CHUNKS = 8
REL_CLIP = 128
N_HEADS_MEM = 4
MEM_HEAD_DIM = 256
D_FF = 4096
ROPE_THETA = 10000.0
LN_EPS = 1e-5
NEG_INF = -1e30
DEPTH = 1
DEEPNORM_ALPHA = (2.0 * DEPTH) ** 0.25
IN_WIDTH = 3072

LANES = 128
VMEM_LIMIT = 56 * 1024 * 1024

F32 = jnp.float32
BF16 = jnp.bfloat16

QA_BLK, KA_BLK, VA_BLK, QB_BLK, KB_BLK, VB_BLK = 0, 4, 8, 12, 16, 20


def _params(sem):
    return pltpu.CompilerParams(dimension_semantics=sem, vmem_limit_bytes=VMEM_LIMIT)


def _dot(a, b):
    return jnp.dot(a, b, preferred_element_type=F32)


def _dot_nt(a, b):
    return lax.dot_general(a, b, (((1,), (1,)), ((), ())), preferred_element_type=F32)


SUB_ROWS = 512


def _sub_tiles(n_rows):
    return [slice(r, r + SUB_ROWS) for r in range(0, n_rows, SUB_ROWS)]


def _layer_norm(z, g, b):
    mu = jnp.mean(z, axis=-1, keepdims=True)
    zc = z - mu
    var = jnp.mean(zc * zc, axis=-1, keepdims=True)
    return zc * lax.rsqrt(var + LN_EPS) * g + b


def _lam_kernel(dl_ref, o_ref, *, lambda_init):
    dl = dl_ref[...]
    s1 = jnp.sum(dl[0:1, :] * dl[1:2, :], axis=1, keepdims=True)
    s2 = jnp.sum(dl[2:3, :] * dl[3:4, :], axis=1, keepdims=True)
    lam = jnp.exp(s1) - jnp.exp(s2) + lambda_init
    o_ref[...] = jnp.broadcast_to(lam, o_ref.shape)


def _lam_call(diff_lambda, lambda_init, width):
    return pl.pallas_call(
        functools.partial(_lam_kernel, lambda_init=lambda_init),
        out_shape=jax.ShapeDtypeStruct((1, width), F32),
        name="lam",
    )(diff_lambda)


def _inproj_kernel(pos_ref, invf_ref, x_ref, w_ref, o_ref):
    xb = x_ref[...].astype(BF16)
    ang = pos_ref[...].astype(F32) * invf_ref[...]
    cos = jnp.cos(ang)
    sin = jnp.sin(ang)
    lane = lax.broadcasted_iota(jnp.int32, ang.shape, 1)
    upper = (lane % HEAD_DIM) >= (HEAD_DIM // 2)
    sin_up = jnp.where(upper, sin, 0.0)
    sin_lo = jnp.where(upper, 0.0, -sin)
    n_chunks = IN_WIDTH // 512
    for c in range(n_chunks):
        h = _dot(xb, w_ref[:, c * 512:(c + 1) * 512])
        if c < 2:
            slabs = []
            for s in range(4):
                sl = h[:, s * LANES:(s + 1) * LANES]
                slabs.append(sl * cos
                             + pltpu.roll(sl, HEAD_DIM // 2, 1) * sin_up
                             + pltpu.roll(sl, LANES - HEAD_DIM // 2, 1) * sin_lo)
            h = jnp.concatenate(slabs, axis=1)
        if c == 0 or c == 3:
            h = h * (HEAD_DIM ** -0.5)
        o_ref[:, c * 512:(c + 1) * 512] = h.astype(BF16)


def _inproj_call(pos, invf, x2d, w_in, tm):
    T = x2d.shape[0]
    return pl.pallas_call(
        _inproj_kernel,
        grid=(T // tm,),
        in_specs=[
            pl.BlockSpec((tm, 1), lambda i: (i, 0)),
            pl.BlockSpec((1, LANES), lambda i: (0, 0)),
            pl.BlockSpec((tm, D_MODEL), lambda i: (i, 0)),
            pl.BlockSpec((D_MODEL, IN_WIDTH), lambda i: (0, 0)),
        ],
        out_specs=pl.BlockSpec((tm, IN_WIDTH), lambda i: (i, 0)),
        out_shape=jax.ShapeDtypeStruct((T, IN_WIDTH), BF16),
        compiler_params=_params(("arbitrary",)),
        name="inproj",
    )(pos, invf, x2d, w_in)


def _stack_halves(q):
    lane = lax.broadcasted_iota(jnp.int32, q.shape, 1)
    zero = jnp.zeros_like(q)
    return jnp.concatenate([jnp.where(lane < HEAD_DIM, q, zero),
                            jnp.where(lane >= HEAD_DIM, q, zero)], axis=0)


def _fill_v_ones(vaug_ref, v):
    vaug_ref[:, :LANES] = v
    vaug_ref[:, LANES:] = jnp.ones(v.shape, v.dtype)


DA_TQ = 256
VT_ONES = 16


def _fill_vt_ones(vt_ref, v):
    nv = v.shape[1]
    vt_ref[0:nv, :] = v.astype(F32).T.astype(vt_ref.dtype)
    vt_ref[nv:, :] = jnp.ones((vt_ref.shape[0] - nv, vt_ref.shape[1]), vt_ref.dtype)


def _diff_scores(nkb, q, k_ref, s_ref, m_ref, *, tq):
    qq = _stack_halves(q)
    m = None
    for j in range(nkb):
        s = _dot_nt(k_ref[0, j * tq:(j + 1) * tq, :], qq)
        if j == nkb - 1:
            row = lax.broadcasted_iota(jnp.int32, s.shape, 0)
            col = lax.broadcasted_iota(jnp.int32, s.shape, 1)
            s = jnp.where((row // CHUNK) <= ((col % tq) // CHUNK), s, NEG_INF)
        s_ref[j * tq:(j + 1) * tq, :] = s
        mj = jnp.max(s, axis=0, keepdims=True)
        m = mj if m is None else jnp.maximum(m, mj)
    m_ref[...] = m


def _diff_output(nkb, s_ref, m_ref, vt_ref, p_ref, lam_ref, g_ref, o_ref, *, tq, lambda_init):
    m = m_ref[...]
    acc = None
    for j in range(nkb):
        rows = slice(j * tq, (j + 1) * tq)
        part = _dot(vt_ref[:, rows], jnp.exp(s_ref[rows, :] - m).astype(BF16))
        acc = part if acc is None else acc + part
    o = acc[:DIFF_V_DIM, :] / acc[DIFF_V_DIM:DIFF_V_DIM + 1, :]
    o = o[:, :tq] - lam_ref[...] * o[:, tq:]
    y = o * lax.rsqrt(jnp.mean(o * o, axis=0, keepdims=True) + LN_EPS) * g_ref[...]
    o_ref[...] = (y * (1.0 - lambda_init)).T.astype(o_ref.dtype)


def _diffattn_kernel(lam_ref, g_ref, q_ref, k_ref, v_ref, o_ref, vt_ref, s_ref, m_ref, p_ref,
                     *, tq, nq, lambda_init):
    _fill_vt_ones(vt_ref, v_ref[0])
    _diff_scores(1, q_ref[0, 0:tq, :], k_ref, s_ref.at[0], m_ref.at[0], tq=tq)
    for c in range(nq):
        if c + 1 < nq:
            nxt = (c + 1) % 2
            _diff_scores(c + 2, q_ref[0, (c + 1) * tq:(c + 2) * tq, :], k_ref, s_ref.at[nxt], m_ref.at[nxt],
                         tq=tq)
        _diff_output(c + 1, s_ref.at[c % 2], m_ref.at[c % 2], vt_ref, p_ref.at[c % 2], lam_ref, g_ref,
                     o_ref.at[0, c * tq:(c + 1) * tq, :], tq=tq, lambda_init=lambda_init)


def _diffattn_call(lam, g, h3, tq, lambda_init):
    B, S, _ = h3.shape
    nq = S // tq
    return pl.pallas_call(
        functools.partial(_diffattn_kernel, tq=tq, nq=nq, lambda_init=lambda_init),
        grid=(B, N_HEADS_DIFF),
        scratch_shapes=[pltpu.VMEM((DIFF_V_DIM + VT_ONES, S), BF16),
                        pltpu.VMEM((2, S, 2 * tq), F32),
                        pltpu.VMEM((2, 1, 2 * tq), F32),
                        pltpu.VMEM((2, S, 2 * tq), BF16)],
        in_specs=[
            pl.BlockSpec((1, tq), lambda b, h: (0, 0)),
            pl.BlockSpec((DIFF_V_DIM, 1), lambda b, h: (0, 0)),
            pl.BlockSpec((1, S, LANES), lambda b, h: (b, 0, QA_BLK + h)),
            pl.BlockSpec((1, S, LANES), lambda b, h: (b, 0, KA_BLK + h)),
            pl.BlockSpec((1, S, LANES), lambda b, h: (b, 0, VA_BLK + h)),
        ],
        out_specs=pl.BlockSpec((1, S, LANES), lambda b, h: (b, 0, h)),
        out_shape=jax.ShapeDtypeStruct((B, S, N_HEADS_DIFF * DIFF_V_DIM), BF16),
        compiler_params=_params(("arbitrary", "arbitrary")),
        name="diffattn",
    )(lam, g, h3, h3, h3)


CA_TQ = 256
CA_NKB = 3


def _chunk_bias_table(rel_bias):
    nh = rel_bias.shape[0]
    width = CA_NKB * CA_TQ
    n_far = width - 1 - REL_CLIP + 1
    n_neg = (CA_TQ - 1) - REL_CLIP + 1
    diag = jnp.concatenate([
        jnp.broadcast_to(rel_bias[:, 2 * REL_CLIP:], (nh, n_far)),
        rel_bias[:, 1:2 * REL_CLIP][:, ::-1],
        jnp.broadcast_to(rel_bias[:, :1], (nh, n_neg + 1)),
    ], axis=1).astype(F32)
    period = width + CA_TQ
    assert diag.shape[1] == period
    diag = jnp.concatenate([diag[:, CA_TQ - 1:], diag[:, :CA_TQ - 1]], axis=1)
    bias = jnp.tile(diag, (1, CA_TQ))[:, :CA_TQ * (period - 1)].reshape(nh, CA_TQ, period - 1)[:, :, :width]
    r = np.arange(CA_TQ)[:, None]
    koff = np.arange(width)[None, :] - (CA_NKB - 1) * CA_TQ
    qc = r // CHUNK
    kc = np.floor_divide(koff, CHUNK)
    allowed = (kc <= qc) & (kc >= qc - LEFT_CHUNKS)
    bias = jnp.where(allowed[None], bias, NEG_INF)
    return bias.reshape(nh // 2, 2 * CA_TQ, width)


def _chunkattn_kernel(bias_ref, q_ref, k_ref, v_ref, o_ref, vaug_ref, s_ref, m_ref, p_ref):
    nq = q_ref.shape[1] // CA_TQ
    nl = CA_TQ // LANES

    def key_blocks(blk):
        return [(j, blk - (CA_NKB - 1) + j) for j in range(CA_NKB) if blk - (CA_NKB - 1) + j >= 0]

    def scores(blk, slot):
        qq = _stack_halves(q_ref[0, blk * CA_TQ:(blk + 1) * CA_TQ, :])
        mpart = None
        for j, kb in key_blocks(blk):
            s = _dot_nt(qq, k_ref[0, kb * CA_TQ:(kb + 1) * CA_TQ, :])
            s = s + bias_ref[0, :, j * CA_TQ:(j + 1) * CA_TQ]
            s_ref[slot, j] = s
            for c in range(nl):
                sl = s[:, c * LANES:(c + 1) * LANES]
                mpart = sl if mpart is None else jnp.maximum(mpart, sl)
        m_ref[slot] = jnp.broadcast_to(jnp.max(mpart, axis=1, keepdims=True), mpart.shape)

    def output(blk, slot):
        m = m_ref[slot]
        blocks = key_blocks(blk)
        for j, _ in blocks:
            for c in range(nl):
                col0 = j * CA_TQ + c * LANES
                p_ref[slot, :, col0:col0 + LANES] = jnp.exp(
                    s_ref[slot, j, :, c * LANES:(c + 1) * LANES] - m).astype(BF16)
        j0, kb0 = blocks[0]
        acc = _dot(p_ref[slot, :, j0 * CA_TQ:], vaug_ref[kb0 * CA_TQ:(blk + 1) * CA_TQ, :])
        o = acc[:, :LANES] / acc[:, LANES:]
        lane = lax.broadcasted_iota(jnp.int32, (CA_TQ, LANES), 1)
        o_ref[0, blk * CA_TQ:(blk + 1) * CA_TQ, :] = jnp.where(
            lane < HEAD_DIM, o[:CA_TQ], o[CA_TQ:]).astype(o_ref.dtype)

    _fill_v_ones(vaug_ref, v_ref[0])
    scores(0, 0)
    for blk in range(nq):
        if blk + 1 < nq:
            scores(blk + 1, (blk + 1) % 2)
        output(blk, blk % 2)


def _chunkattn_call(bias, h3):
    B, S, _ = h3.shape
    npairs = N_HEADS_CHUNK // 2
    return pl.pallas_call(
        _chunkattn_kernel,
        grid=(npairs, B),
        in_specs=[
            pl.BlockSpec((1, 2 * CA_TQ, CA_NKB * CA_TQ), lambda p, b: (p, 0, 0)),
            pl.BlockSpec((1, S, LANES), lambda p, b: (b, 0, QB_BLK + p)),
            pl.BlockSpec((1, S, LANES), lambda p, b: (b, 0, KB_BLK + p)),
            pl.BlockSpec((1, S, LANES), lambda p, b: (b, 0, VB_BLK + p)),
        ],
        out_specs=pl.BlockSpec((1, S, LANES), lambda p, b: (b, 0, p)),
        out_shape=jax.ShapeDtypeStruct((B, S, N_HEADS_CHUNK * HEAD_DIM), BF16),
        scratch_shapes=[pltpu.VMEM((S, 2 * LANES), BF16),
                        pltpu.VMEM((2, CA_NKB, 2 * CA_TQ, CA_TQ), F32),
                        pltpu.VMEM((2, 2 * CA_TQ, LANES), F32),
                        pltpu.VMEM((2, 2 * CA_TQ, CA_NKB * CA_TQ), BF16)],
        compiler_params=_params(("arbitrary", "arbitrary")),
        name="chunkattn",
    )(bias, h3, h3, h3)


def _outproj_kernel(x_ref, ya_ref, yb_ref, wo_ref, g_ref, b_ref, wq_ref, x1_ref, q_ref):
    half = ya_ref.shape[1]
    tiles = _sub_tiles(x_ref.shape[0])
    ys = [_dot(ya_ref[rows, :], wo_ref[0:half, :]) + _dot(yb_ref[rows, :], wo_ref[half:, :]) for rows in tiles]
    for rows, y in zip(tiles, ys):
        x1 = _layer_norm(DEEPNORM_ALPHA * x_ref[rows, :] + y, g_ref[...], b_ref[...])
        x1_ref[rows, :] = x1
        q = _dot(x1.astype(BF16), wq_ref[...]) * (MEM_HEAD_DIM ** -0.5)
        q_ref[rows, :] = q.astype(BF16)


def _outproj_call(x2d, ya, yb, w_o, g, b, w_mq, tm):
    T = x2d.shape[0]
    half = ya.shape[1]
    row = lambda i: (i, 0)
    fixed = lambda i: (0, 0)
    return pl.pallas_call(
        _outproj_kernel,
        grid=(T // tm,),
        in_specs=[
            pl.BlockSpec((tm, D_MODEL), row),
            pl.BlockSpec((tm, half), row),
            pl.BlockSpec((tm, half), row),
            pl.BlockSpec((D_MODEL, D_MODEL), fixed),
            pl.BlockSpec((1, D_MODEL), fixed),
            pl.BlockSpec((1, D_MODEL), fixed),
            pl.BlockSpec((D_MODEL, D_MODEL), fixed),
        ],
        out_specs=[pl.BlockSpec((tm, D_MODEL), row), pl.BlockSpec((tm, D_MODEL), row)],
        out_shape=[jax.ShapeDtypeStruct((T, D_MODEL), F32), jax.ShapeDtypeStruct((T, D_MODEL), BF16)],
        compiler_params=_params(("arbitrary",)),
        name="outproj",
    )(x2d, ya, yb, w_o, g, b, w_mq)


def _memkv_kernel(mem_ref, wk_ref, wv_ref, k_ref, v_ref):
    mb = mem_ref[0].astype(BF16)
    k_ref[0] = _dot(mb, wk_ref[...]).astype(BF16)
    v_ref[0] = _dot(mb, wv_ref[...]).astype(BF16)


def _memkv_call(mem, w_mk, w_mv):
    B, M, _ = mem.shape
    blk = pl.BlockSpec((1, M, D_MODEL), lambda b: (b, 0, 0))
    fixed = pl.BlockSpec((D_MODEL, D_MODEL), lambda b: (0, 0))
    return pl.pallas_call(
        _memkv_kernel,
        grid=(B,),
        in_specs=[blk, fixed, fixed],
        out_specs=[blk, blk],
        out_shape=[jax.ShapeDtypeStruct((B, M, D_MODEL), BF16)] * 2,
        compiler_params=_params(("arbitrary",)),
        name="memkv",
    )(mem, w_mk, w_mv)


def _memattn_kernel(x1_ref, q_ref, k_ref, v_ref, wo_ref, g_ref, b_ref, x2_ref):
    tiles = _sub_tiles(x1_ref.shape[1])
    attn = []
    for rows in tiles:
        outs = []
        for h in range(N_HEADS_MEM):
            sl = slice(h * MEM_HEAD_DIM, (h + 1) * MEM_HEAD_DIM)
            s = _dot_nt(q_ref[0, rows, sl], k_ref[0, :, sl])
            m = jnp.max(s, axis=1, keepdims=True)
            p = jnp.exp(s - m)
            l = jnp.sum(p, axis=1, keepdims=True)
            outs.append((_dot(p.astype(BF16), v_ref[0, :, sl]) / l).astype(BF16))
        attn.append(jnp.concatenate(outs, axis=1))
    ys = [_dot(o, wo_ref[...]) for o in attn]
    for rows, y in zip(tiles, ys):
        x2_ref[0, rows, :] = _layer_norm(DEEPNORM_ALPHA * x1_ref[0, rows, :] + y, g_ref[...], b_ref[...])


def _memattn_call(x1, q, k, v, w_mo, g, b, tm):
    B, S, _ = x1.shape
    M = k.shape[1]
    row = pl.BlockSpec((1, tm, D_MODEL), lambda bb, i: (bb, i, 0))
    kv = pl.BlockSpec((1, M, D_MODEL), lambda bb, i: (bb, 0, 0))
    vec = pl.BlockSpec((1, D_MODEL), lambda bb, i: (0, 0))
    return pl.pallas_call(
        _memattn_kernel,
        grid=(B, S // tm),
        in_specs=[row, row, kv, kv, pl.BlockSpec((D_MODEL, D_MODEL), lambda bb, i: (0, 0)), vec, vec],
        out_specs=row,
        out_shape=jax.ShapeDtypeStruct((B, S, D_MODEL), F32),
        compiler_params=_params(("arbitrary", "arbitrary")),
        name="memattn",
    )(x1, q, k, v, w_mo, g, b)


def _mlp_kernel(x_ref, wu_ref, wd_ref, g_ref, b_ref, o_ref, *, ff_chunk):
    def finish(rows, y):
        o_ref[rows, :] = _layer_norm(DEEPNORM_ALPHA * x_ref[rows, :] + y, g_ref[...], b_ref[...])

    pending = None
    for rows in _sub_tiles(x_ref.shape[0]):
        xb = x_ref[rows, :].astype(BF16)
        y = None
        for c in range(D_FF // ff_chunk):
            sl = slice(c * ff_chunk, (c + 1) * ff_chunk)
            h = jnp.maximum(_dot(xb, wu_ref[:, sl]), 0.0)
            part = _dot((h * h).astype(BF16), wd_ref[sl, :])
            y = part if y is None else y + part
            if c == 0 and pending is not None:
                finish(*pending)
                pending = None
        pending = (rows, y)
    finish(*pending)


def _mlp_call(x2d, w_up, w_down, g, b, tm, ff_chunk):
    T = x2d.shape[0]
    row = pl.BlockSpec((tm, D_MODEL), lambda i: (i, 0))
    vec = pl.BlockSpec((1, D_MODEL), lambda i: (0, 0))
    return pl.pallas_call(
        functools.partial(_mlp_kernel, ff_chunk=ff_chunk),
        grid=(T // tm,),
        in_specs=[row,
                  pl.BlockSpec((D_MODEL, D_FF), lambda i: (0, 0), pipeline_mode=pl.Buffered(1)),
                  pl.BlockSpec((D_FF, D_MODEL), lambda i: (0, 0), pipeline_mode=pl.Buffered(1)),
                  vec, vec],
        out_specs=row,
        out_shape=jax.ShapeDtypeStruct((T, D_MODEL), F32),
        compiler_params=_params(("arbitrary",)),
        name="mlp",
    )(x2d, w_up, w_down, g, b)


def kernel(x, mem, positions, w_in, diff_lambda, subln_g, rel_bias, w_o, ln1_g, ln1_b,
           w_mq, w_mk, w_mv, w_mo, ln2_g, ln2_b, w_up, w_down, ln3_g, ln3_b):
    B, S, D = x.shape
    T = B * S
    depth = w_in.shape[0]
    assert depth == DEPTH and D == D_MODEL and S % CA_TQ == 0
    inv_freq = 1.0 / (ROPE_THETA ** (jnp.arange(0, HEAD_DIM, 2, dtype=F32) / HEAD_DIM))
    invf = jnp.tile(inv_freq, LANES // (HEAD_DIM // 2)).reshape(1, LANES)
    pos = positions.reshape(T, 1)
    vec = lambda a: a.reshape(1, -1)

    for l in range(depth):
        lambda_init = 0.8 - 0.6 * math.exp(-0.3 * l)
        lam = _lam_call(diff_lambda[l], lambda_init, DA_TQ)
        h = _inproj_call(pos, invf, x.reshape(T, D), w_in[l].astype(BF16), tm=512)
        h3 = h.reshape(B, S, IN_WIDTH)
        ya = _diffattn_call(lam, subln_g[l].reshape(-1, 1), h3, tq=DA_TQ, lambda_init=lambda_init)
        yb = _chunkattn_call(_chunk_bias_table(rel_bias[l]), h3)
        x1, qm = _outproj_call(x.reshape(T, D), ya.reshape(T, -1), yb.reshape(T, -1),
                               w_o[l].astype(BF16), vec(ln1_g[l]), vec(ln1_b[l]),
                               w_mq[l].astype(BF16), tm=1024)
        km, vm = _memkv_call(mem, w_mk[l].astype(BF16), w_mv[l].astype(BF16))
        x2 = _memattn_call(x1.reshape(B, S, D), qm.reshape(B, S, D), km, vm,
                           w_mo[l].astype(BF16), vec(ln2_g[l]), vec(ln2_b[l]), tm=1024)
        out = _mlp_call(x2.reshape(T, D), w_up[l].astype(BF16), w_down[l].astype(BF16),
                        vec(ln3_g[l]), vec(ln3_b[l]), tm=1024, ff_chunk=1024)
        x = out.reshape(B, S, D)
    return x
```

```python
import functools
import math

import jax
import jax.numpy as jnp
import numpy as np
from jax import lax
from jax.experimental import pallas as pl
from jax.experimental.pallas import tpu as pltpu

D_MODEL = 1024
CHUNK = 64
HEAD_DIM = 64
N_HEADS_DIFF = 4
DIFF_V_DIM = 128
N_HEADS_CHUNK = 8
LEFT_CHUNKS = 8
REL_CLIP = 128
N_HEADS_MEM = 4
MEM_HEAD_DIM = 256
D_FF = 4096
ROPE_THETA = 10000.0
LN_EPS = 1e-5
NEG_INF = -1e30
LOG2E = math.log2(math.e)
DEPTH = 1
DEEPNORM_ALPHA = (2.0 * DEPTH) ** 0.25
IN_WIDTH = 3072

LANES = 128
VMEM_LIMIT = 56 * 1024 * 1024

F32 = jnp.float32
BF16 = jnp.bfloat16

QA_BLK, KA_BLK, VA_BLK, QB_BLK, KB_BLK, VB_BLK = 0, 4, 8, 12, 16, 20


def _params(sem):
    return pltpu.CompilerParams(dimension_semantics=sem, vmem_limit_bytes=VMEM_LIMIT)


def _dot(a, b):
    return jnp.dot(a, b, preferred_element_type=F32)


def _dot_nt(a, b):
    return lax.dot_general(a, b, (((1,), (1,)), ((), ())), preferred_element_type=F32)


SUB_ROWS = 512


def _sub_tiles(n_rows):
    return [slice(r, r + SUB_ROWS) for r in range(0, n_rows, SUB_ROWS)]


def _layer_norm(z, g, b):
    mu = jnp.mean(z, axis=-1, keepdims=True)
    zc = z - mu
    var = jnp.mean(zc * zc, axis=-1, keepdims=True)
    return zc * lax.rsqrt(var + LN_EPS) * g + b


def _lam_kernel(dl_ref, o_ref, *, lambda_init):
    dl = dl_ref[...]
    s1 = jnp.sum(dl[0:1, :] * dl[1:2, :], axis=1, keepdims=True)
    s2 = jnp.sum(dl[2:3, :] * dl[3:4, :], axis=1, keepdims=True)
    lam = jnp.exp(s1) - jnp.exp(s2) + lambda_init
    o_ref[...] = jnp.broadcast_to(lam, o_ref.shape)


def _lam_call(diff_lambda, lambda_init, width):
    return pl.pallas_call(
        functools.partial(_lam_kernel, lambda_init=lambda_init),
        out_shape=jax.ShapeDtypeStruct((1, width), F32),
        name="lam",
    )(diff_lambda)


def _inproj_kernel(pos_ref, invf_ref, x_ref, w_ref, o_ref):
    xb = x_ref[...].astype(BF16)
    ang = pos_ref[...].astype(F32) * invf_ref[...]
    cos = jnp.cos(ang)
    sin = jnp.sin(ang)
    lane = lax.broadcasted_iota(jnp.int32, ang.shape, 1)
    upper = (lane % HEAD_DIM) >= (HEAD_DIM // 2)
    sin_up = jnp.where(upper, sin, 0.0)
    sin_lo = jnp.where(upper, 0.0, -sin)
    n_chunks = IN_WIDTH // 512
    for c in range(n_chunks):
        h = _dot(xb, w_ref[:, c * 512:(c + 1) * 512])
        if c < 2:
            slabs = []
            for s in range(4):
                sl = h[:, s * LANES:(s + 1) * LANES]
                slabs.append(sl * cos
                             + pltpu.roll(sl, HEAD_DIM // 2, 1) * sin_up
                             + pltpu.roll(sl, LANES - HEAD_DIM // 2, 1) * sin_lo)
            h = jnp.concatenate(slabs, axis=1)
        if c == 0 or c == 3:
            h = h * (HEAD_DIM ** -0.5 * LOG2E)
        o_ref[:, c * 512:(c + 1) * 512] = h.astype(BF16)


def _inproj_call(pos, invf, x2d, w_in, tm):
    T = x2d.shape[0]
    return pl.pallas_call(
        _inproj_kernel,
        grid=(T // tm,),
        in_specs=[
            pl.BlockSpec((tm, 1), lambda i: (i, 0)),
            pl.BlockSpec((1, LANES), lambda i: (0, 0)),
            pl.BlockSpec((tm, D_MODEL), lambda i: (i, 0)),
            pl.BlockSpec((D_MODEL, IN_WIDTH), lambda i: (0, 0)),
        ],
        out_specs=pl.BlockSpec((tm, IN_WIDTH), lambda i: (i, 0)),
        out_shape=jax.ShapeDtypeStruct((T, IN_WIDTH), BF16),
        compiler_params=_params(("arbitrary",)),
        name="inproj",
    )(pos, invf, x2d, w_in)


def _stack_halves(q):
    lane = lax.broadcasted_iota(jnp.int32, q.shape, 1)
    zero = jnp.zeros_like(q)
    return jnp.concatenate([jnp.where(lane < HEAD_DIM, q, zero),
                            jnp.where(lane >= HEAD_DIM, q, zero)], axis=0)


DA_TQ = 256
VT_ONES = 16


def _fill_vt_ones(vt_ref, v):
    nv = v.shape[1]
    vt_ref[0:nv, :] = v.astype(F32).T.astype(vt_ref.dtype)
    vt_ref[nv:, :] = jnp.ones((vt_ref.shape[0] - nv, vt_ref.shape[1]), vt_ref.dtype)


def _diff_scores(nkb, q, k_ref, s_ref, m_ref, *, tq):
    qq = _stack_halves(q)
    m = None
    for j in range(nkb):
        s = _dot_nt(k_ref[0, j * tq:(j + 1) * tq, :], qq)
        if j == nkb - 1:
            row = lax.broadcasted_iota(jnp.int32, s.shape, 0)
            col = lax.broadcasted_iota(jnp.int32, s.shape, 1)
            s = jnp.where((row // CHUNK) <= ((col % tq) // CHUNK), s, NEG_INF)
        s_ref[j * tq:(j + 1) * tq, :] = s
        mj = jnp.max(s, axis=0, keepdims=True)
        m = mj if m is None else jnp.maximum(m, mj)
    m_ref[...] = m


def _diff_output(nkb, s_ref, m_ref, vt_ref, lam_ref, g_ref, o_ref, *, tq, lambda_init):
    m = m_ref[...]
    acc = None
    for j in range(nkb):
        rows = slice(j * tq, (j + 1) * tq)
        part = _dot(vt_ref[:, rows], jnp.exp2(s_ref[rows, :] - m).astype(BF16))
        acc = part if acc is None else acc + part
    o = acc[:DIFF_V_DIM, :] / acc[DIFF_V_DIM:DIFF_V_DIM + 1, :]
    o = o[:, :tq] - lam_ref[...] * o[:, tq:]
    y = o * lax.rsqrt(jnp.mean(o * o, axis=0, keepdims=True) + LN_EPS) * g_ref[...]
    o_ref[...] = (y * (1.0 - lambda_init)).T.astype(o_ref.dtype)


def _diffattn_kernel(lam_ref, g_ref, q_ref, k_ref, v_ref, o_ref, vt_ref, s_ref, m_ref,
                     *, tq, nq, lambda_init):
    _fill_vt_ones(vt_ref, v_ref[0])
    _diff_scores(1, q_ref[0, 0:tq, :], k_ref, s_ref.at[0], m_ref.at[0], tq=tq)
    for c in range(nq):
        if c + 1 < nq:
            nxt = (c + 1) % 2
            _diff_scores(c + 2, q_ref[0, (c + 1) * tq:(c + 2) * tq, :], k_ref, s_ref.at[nxt], m_ref.at[nxt],
                         tq=tq)
        _diff_output(c + 1, s_ref.at[c % 2], m_ref.at[c % 2], vt_ref, lam_ref, g_ref,
                     o_ref.at[0, c * tq:(c + 1) * tq, :], tq=tq, lambda_init=lambda_init)


def _diffattn_call(lam, g, h3, tq, lambda_init):
    B, S, _ = h3.shape
    nq = S // tq
    return pl.pallas_call(
        functools.partial(_diffattn_kernel, tq=tq, nq=nq, lambda_init=lambda_init),
        grid=(B, N_HEADS_DIFF),
        scratch_shapes=[pltpu.VMEM((DIFF_V_DIM + VT_ONES, S), BF16),
                        pltpu.VMEM((2, S, 2 * tq), F32),
                        pltpu.VMEM((2, 1, 2 * tq), F32)],
        in_specs=[
            pl.BlockSpec((1, tq), lambda b, h: (0, 0)),
            pl.BlockSpec((DIFF_V_DIM, 1), lambda b, h: (0, 0)),
            pl.BlockSpec((1, S, LANES), lambda b, h: (b, 0, QA_BLK + h)),
            pl.BlockSpec((1, S, LANES), lambda b, h: (b, 0, KA_BLK + h)),
            pl.BlockSpec((1, S, LANES), lambda b, h: (b, 0, VA_BLK + h)),
        ],
        out_specs=pl.BlockSpec((1, S, LANES), lambda b, h: (b, 0, h)),
        out_shape=jax.ShapeDtypeStruct((B, S, N_HEADS_DIFF * DIFF_V_DIM), BF16),
        compiler_params=_params(("arbitrary", "arbitrary")),
        name="diffattn",
    )(lam, g, h3, h3, h3)


CA_TQ = 256
CA_NKB = 3


def _chunk_bias_table(rel_bias):
    nh = rel_bias.shape[0]
    width = CA_NKB * CA_TQ
    n_far = width - 1 - REL_CLIP + 1
    n_neg = (CA_TQ - 1) - REL_CLIP + 1
    diag = jnp.concatenate([
        jnp.broadcast_to(rel_bias[:, 2 * REL_CLIP:], (nh, n_far)),
        rel_bias[:, 1:2 * REL_CLIP][:, ::-1],
        jnp.broadcast_to(rel_bias[:, :1], (nh, n_neg + 1)),
    ], axis=1).astype(F32)
    period = width + CA_TQ
    assert diag.shape[1] == period
    diag = jnp.concatenate([diag[:, CA_TQ - 1:], diag[:, :CA_TQ - 1]], axis=1)
    bias = jnp.tile(diag, (1, CA_TQ))[:, :CA_TQ * (period - 1)].reshape(nh, CA_TQ, period - 1)[:, :, :width]
    r = np.arange(CA_TQ)[:, None]
    koff = np.arange(width)[None, :] - (CA_NKB - 1) * CA_TQ
    qc = r // CHUNK
    kc = np.floor_divide(koff, CHUNK)
    allowed = (kc <= qc) & (kc >= qc - LEFT_CHUNKS)
    bias = jnp.where(allowed[None], bias * LOG2E, NEG_INF)
    return bias.reshape(nh // 2, 2, CA_TQ, width).transpose(0, 3, 1, 2).reshape(nh // 2, width, 2 * CA_TQ)


def _chunkattn_kernel(bias_ref, q_ref, k_ref, v_ref, o_ref, vt_ref, s_ref, m_ref):
    nq = q_ref.shape[1] // CA_TQ

    def key_blocks(blk):
        return [(j, blk - (CA_NKB - 1) + j) for j in range(CA_NKB) if blk - (CA_NKB - 1) + j >= 0]

    def scores(blk, slot):
        qq = _stack_halves(q_ref[0, blk * CA_TQ:(blk + 1) * CA_TQ, :])
        m = None
        for j, kb in key_blocks(blk):
            rows = slice(j * CA_TQ, (j + 1) * CA_TQ)
            s = _dot_nt(k_ref[0, kb * CA_TQ:(kb + 1) * CA_TQ, :], qq) + bias_ref[0, rows, :]
            s_ref[slot, rows, :] = s
            mj = jnp.max(s, axis=0, keepdims=True)
            m = mj if m is None else jnp.maximum(m, mj)
        m_ref[slot] = m

    def output(blk, slot):
        m = m_ref[slot]
        acc = None
        for j, kb in key_blocks(blk):
            p = jnp.exp2(s_ref[slot, j * CA_TQ:(j + 1) * CA_TQ, :] - m).astype(BF16)
            part = _dot(vt_ref[:, kb * CA_TQ:(kb + 1) * CA_TQ], p)
            acc = part if acc is None else acc + part
        l = acc[LANES:LANES + 1, :]
        o_t = jnp.concatenate([acc[:HEAD_DIM, :CA_TQ] / l[:, :CA_TQ],
                               acc[HEAD_DIM:LANES, CA_TQ:] / l[:, CA_TQ:]], axis=0)
        o_ref[0, blk * CA_TQ:(blk + 1) * CA_TQ, :] = o_t.T.astype(o_ref.dtype)

    _fill_vt_ones(vt_ref, v_ref[0])
    scores(0, 0)
    for blk in range(nq):
        if blk + 1 < nq:
            scores(blk + 1, (blk + 1) % 2)
        output(blk, blk % 2)


def _chunkattn_call(bias, h3):
    B, S, _ = h3.shape
    npairs = N_HEADS_CHUNK // 2
    return pl.pallas_call(
        _chunkattn_kernel,
        grid=(npairs, B),
        in_specs=[
            pl.BlockSpec((1, CA_NKB * CA_TQ, 2 * CA_TQ), lambda p, b: (p, 0, 0)),
            pl.BlockSpec((1, S, LANES), lambda p, b: (b, 0, QB_BLK + p)),
            pl.BlockSpec((1, S, LANES), lambda p, b: (b, 0, KB_BLK + p)),
            pl.BlockSpec((1, S, LANES), lambda p, b: (b, 0, VB_BLK + p)),
        ],
        out_specs=pl.BlockSpec((1, S, LANES), lambda p, b: (b, 0, p)),
        out_shape=jax.ShapeDtypeStruct((B, S, N_HEADS_CHUNK * HEAD_DIM), BF16),
        scratch_shapes=[pltpu.VMEM((LANES + VT_ONES, S), BF16),
                        pltpu.VMEM((2, CA_NKB * CA_TQ, 2 * CA_TQ), F32),
                        pltpu.VMEM((2, 1, 2 * CA_TQ), F32)],
        compiler_params=_params(("arbitrary", "arbitrary")),
        name="chunkattn",
    )(bias, h3, h3, h3)


def _outproj_kernel(x_ref, ya_ref, yb_ref, wo_ref, g_ref, b_ref, wq_ref, x1_ref, q_ref):
    half = ya_ref.shape[1]
    tiles = _sub_tiles(x_ref.shape[0])
    ys = [_dot(ya_ref[rows, :], wo_ref[0:half, :]) + _dot(yb_ref[rows, :], wo_ref[half:, :]) for rows in tiles]
    for rows, y in zip(tiles, ys):
        x1 = _layer_norm(DEEPNORM_ALPHA * x_ref[rows, :] + y, g_ref[...], b_ref[...])
        x1_ref[rows, :] = x1
        q = _dot(x1.astype(BF16), wq_ref[...]) * (MEM_HEAD_DIM ** -0.5 * LOG2E)
        q_ref[rows, :] = q.astype(BF16)


def _outproj_call(x2d, ya, yb, w_o, g, b, w_mq, tm):
    T = x2d.shape[0]
    half = ya.shape[1]
    row = lambda i: (i, 0)
    fixed = lambda i: (0, 0)
    return pl.pallas_call(
        _outproj_kernel,
        grid=(T // tm,),
        in_specs=[
            pl.BlockSpec((tm, D_MODEL), row),
            pl.BlockSpec((tm, half), row),
            pl.BlockSpec((tm, half), row),
            pl.BlockSpec((D_MODEL, D_MODEL), fixed),
            pl.BlockSpec((1, D_MODEL), fixed),
            pl.BlockSpec((1, D_MODEL), fixed),
            pl.BlockSpec((D_MODEL, D_MODEL), fixed),
        ],
        out_specs=[pl.BlockSpec((tm, D_MODEL), row), pl.BlockSpec((tm, D_MODEL), row)],
        out_shape=[jax.ShapeDtypeStruct((T, D_MODEL), F32), jax.ShapeDtypeStruct((T, D_MODEL), BF16)],
        compiler_params=_params(("arbitrary",)),
        name="outproj",
    )(x2d, ya, yb, w_o, g, b, w_mq)


def _memkv_kernel(mem_ref, wk_ref, wv_ref, k_ref, v_ref):
    mb = mem_ref[0].astype(BF16)
    k_ref[0] = _dot(mb, wk_ref[...]).astype(BF16)
    v_ref[0] = _dot(mb, wv_ref[...]).astype(BF16)


def _memkv_call(mem, w_mk, w_mv):
    B, M, _ = mem.shape
    blk = pl.BlockSpec((1, M, D_MODEL), lambda b: (b, 0, 0))
    fixed = pl.BlockSpec((D_MODEL, D_MODEL), lambda b: (0, 0))
    return pl.pallas_call(
        _memkv_kernel,
        grid=(B,),
        in_specs=[blk, fixed, fixed],
        out_specs=[blk, blk],
        out_shape=[jax.ShapeDtypeStruct((B, M, D_MODEL), BF16)] * 2,
        compiler_params=_params(("arbitrary",)),
        name="memkv",
    )(mem, w_mk, w_mv)


def _memattn_kernel(x1_ref, q_ref, k_ref, v_ref, wo_ref, g_ref, b_ref, x2_ref):
    tiles = _sub_tiles(x1_ref.shape[1])
    attn = []
    for rows in tiles:
        outs = []
        for h in range(N_HEADS_MEM):
            sl = slice(h * MEM_HEAD_DIM, (h + 1) * MEM_HEAD_DIM)
            s = _dot_nt(q_ref[0, rows, sl], k_ref[0, :, sl])
            m = jnp.max(s, axis=1, keepdims=True)
            p = jnp.exp2(s - m)
            l = jnp.sum(p, axis=1, keepdims=True)
            outs.append((_dot(p.astype(BF16), v_ref[0, :, sl]) / l).astype(BF16))
        attn.append(jnp.concatenate(outs, axis=1))
    ys = [_dot(o, wo_ref[...]) for o in attn]
    for rows, y in zip(tiles, ys):
        x2_ref[0, rows, :] = _layer_norm(DEEPNORM_ALPHA * x1_ref[0, rows, :] + y, g_ref[...], b_ref[...])


def _memattn_call(x1, q, k, v, w_mo, g, b, tm):
    B, S, _ = x1.shape
    M = k.shape[1]
    row = pl.BlockSpec((1, tm, D_MODEL), lambda bb, i: (bb, i, 0))
    kv = pl.BlockSpec((1, M, D_MODEL), lambda bb, i: (bb, 0, 0))
    vec = pl.BlockSpec((1, D_MODEL), lambda bb, i: (0, 0))
    return pl.pallas_call(
        _memattn_kernel,
        grid=(B, S // tm),
        in_specs=[row, row, kv, kv, pl.BlockSpec((D_MODEL, D_MODEL), lambda bb, i: (0, 0)), vec, vec],
        out_specs=row,
        out_shape=jax.ShapeDtypeStruct((B, S, D_MODEL), F32),
        compiler_params=_params(("arbitrary", "arbitrary")),
        name="memattn",
    )(x1, q, k, v, w_mo, g, b)


def _mlp_kernel(x_ref, wu_ref, wd_ref, g_ref, b_ref, o_ref, *, ff_chunk):
    def finish(rows, y):
        o_ref[rows, :] = _layer_norm(DEEPNORM_ALPHA * x_ref[rows, :] + y, g_ref[...], b_ref[...])

    pending = None
    for rows in _sub_tiles(x_ref.shape[0]):
        xb = x_ref[rows, :].astype(BF16)
        y = None
        for c in range(D_FF // ff_chunk):
            sl = slice(c * ff_chunk, (c + 1) * ff_chunk)
            h = jnp.maximum(_dot(xb, wu_ref[:, sl]), 0.0)
            part = _dot((h * h).astype(BF16), wd_ref[sl, :])
            y = part if y is None else y + part
            if c == 0 and pending is not None:
                finish(*pending)
                pending = None
        pending = (rows, y)
    finish(*pending)


def _mlp_call(x2d, w_up, w_down, g, b, tm, ff_chunk):
    T = x2d.shape[0]
    row = pl.BlockSpec((tm, D_MODEL), lambda i: (i, 0))
    vec = pl.BlockSpec((1, D_MODEL), lambda i: (0, 0))
    return pl.pallas_call(
        functools.partial(_mlp_kernel, ff_chunk=ff_chunk),
        grid=(T // tm,),
        in_specs=[row,
                  pl.BlockSpec((D_MODEL, D_FF), lambda i: (0, 0), pipeline_mode=pl.Buffered(1)),
                  pl.BlockSpec((D_FF, D_MODEL), lambda i: (0, 0), pipeline_mode=pl.Buffered(1)),
                  vec, vec],
        out_specs=row,
        out_shape=jax.ShapeDtypeStruct((T, D_MODEL), F32),
        compiler_params=_params(("arbitrary",)),
        name="mlp",
    )(x2d, w_up, w_down, g, b)


def kernel(x, mem, positions, w_in, diff_lambda, subln_g, rel_bias, w_o, ln1_g, ln1_b,
           w_mq, w_mk, w_mv, w_mo, ln2_g, ln2_b, w_up, w_down, ln3_g, ln3_b):
    B, S, D = x.shape
    T = B * S
    depth = w_in.shape[0]
    assert depth == DEPTH and D == D_MODEL and S % CA_TQ == 0
    inv_freq = 1.0 / (ROPE_THETA ** (jnp.arange(0, HEAD_DIM, 2, dtype=F32) / HEAD_DIM))
    invf = jnp.tile(inv_freq, LANES // (HEAD_DIM // 2)).reshape(1, LANES)
    pos = positions.reshape(T, 1)
    vec = lambda a: a.reshape(1, -1)

    for l in range(depth):
        lambda_init = 0.8 - 0.6 * math.exp(-0.3 * l)
        lam = _lam_call(diff_lambda[l], lambda_init, DA_TQ)
        h = _inproj_call(pos, invf, x.reshape(T, D), w_in[l].astype(BF16), tm=512)
        h3 = h.reshape(B, S, IN_WIDTH)
        ya = _diffattn_call(lam, subln_g[l].reshape(-1, 1), h3, tq=DA_TQ, lambda_init=lambda_init)
        yb = _chunkattn_call(_chunk_bias_table(rel_bias[l]), h3)
        x1, qm = _outproj_call(x.reshape(T, D), ya.reshape(T, -1), yb.reshape(T, -1),
                               w_o[l].astype(BF16), vec(ln1_g[l]), vec(ln1_b[l]),
                               w_mq[l].astype(BF16), tm=1024)
        km, vm = _memkv_call(mem, w_mk[l].astype(BF16), w_mv[l].astype(BF16))
        x2 = _memattn_call(x1.reshape(B, S, D), qm.reshape(B, S, D), km, vm,
                           w_mo[l].astype(BF16), vec(ln2_g[l]), vec(ln2_b[l]), tm=1024)
        out = _mlp_call(x2.reshape(T, D), w_up[l].astype(BF16), w_down[l].astype(BF16),
                        vec(ln3_g[l]), vec(ln3_b[l]), tm=1024, ff_chunk=1024)
        x = out.reshape(B, S, D)
    return x
```

```python
import functools
import math

import jax
import jax.numpy as jnp
import numpy as np
from jax import lax
from jax.experimental import pallas as pl
from jax.experimental.pallas import tpu as pltpu

D_MODEL = 1024
CHUNK = 64
HEAD_DIM = 64
N_HEADS_DIFF = 4
DIFF_V_DIM = 128
N_HEADS_CHUNK = 8
LEFT_CHUNKS = 8
REL_CLIP = 128
N_HEADS_MEM = 4
MEM_HEAD_DIM = 256
D_FF = 4096
ROPE_THETA = 10000.0
LN_EPS = 1e-5
NEG_INF = -1e30
LOG2E = math.log2(math.e)
DEPTH = 1
DEEPNORM_ALPHA = (2.0 * DEPTH) ** 0.25
IN_WIDTH = 3072

LANES = 128
VMEM_LIMIT = 56 * 1024 * 1024

F32 = jnp.float32
BF16 = jnp.bfloat16

QA_BLK, KA_BLK, VA_BLK, QB_BLK, KB_BLK, VB_BLK = 0, 4, 8, 12, 16, 20


def _params(sem):
    return pltpu.CompilerParams(dimension_semantics=sem, vmem_limit_bytes=VMEM_LIMIT)


def _dot(a, b):
    return jnp.dot(a, b, preferred_element_type=F32)


def _dot_nt(a, b):
    return lax.dot_general(a, b, (((1,), (1,)), ((), ())), preferred_element_type=F32)


SUB_ROWS = 512


def _sub_tiles(n_rows):
    return [slice(r, r + SUB_ROWS) for r in range(0, n_rows, SUB_ROWS)]


def _resident(shape):
    return pl.BlockSpec(shape, lambda *_: (0,) * len(shape), pipeline_mode=pl.Buffered(1))


def _cast_weights_once(first_step, pairs):
    @pl.when(first_step)
    def _():
        for src_ref, dst_ref in pairs:
            dst_ref[...] = src_ref[...].astype(dst_ref.dtype)


def _layer_norm(z, g, b):
    mu = jnp.mean(z, axis=-1, keepdims=True)
    zc = z - mu
    var = jnp.mean(zc * zc, axis=-1, keepdims=True)
    return zc * lax.rsqrt(var + LN_EPS) * g + b


def _lam_kernel(dl_ref, o_ref, *, lambda_init):
    dl = dl_ref[...]
    s1 = jnp.sum(dl[0:1, :] * dl[1:2, :], axis=1, keepdims=True)
    s2 = jnp.sum(dl[2:3, :] * dl[3:4, :], axis=1, keepdims=True)
    lam = jnp.exp(s1) - jnp.exp(s2) + lambda_init
    o_ref[...] = jnp.broadcast_to(lam, o_ref.shape)


def _lam_call(diff_lambda, lambda_init, width):
    return pl.pallas_call(
        functools.partial(_lam_kernel, lambda_init=lambda_init),
        out_shape=jax.ShapeDtypeStruct((1, width), F32),
        name="lam",
    )(diff_lambda)


def _inproj_kernel(pos_ref, invf_ref, x_ref, wf_ref, o_ref, w_ref):
    _cast_weights_once(pl.program_id(0) == 0, [(wf_ref, w_ref)])
    xb = x_ref[...].astype(BF16)
    ang = pos_ref[...].astype(F32) * invf_ref[...]
    cos = jnp.cos(ang)
    sin = jnp.sin(ang)
    lane = lax.broadcasted_iota(jnp.int32, ang.shape, 1)
    upper = (lane % HEAD_DIM) >= (HEAD_DIM // 2)
    sin_up = jnp.where(upper, sin, 0.0)
    sin_lo = jnp.where(upper, 0.0, -sin)
    n_chunks = IN_WIDTH // 512
    for c in range(n_chunks):
        h = _dot(xb, w_ref[:, c * 512:(c + 1) * 512])
        if c < 2:
            slabs = []
            for s in range(4):
                sl = h[:, s * LANES:(s + 1) * LANES]
                slabs.append(sl * cos
                             + pltpu.roll(sl, HEAD_DIM // 2, 1) * sin_up
                             + pltpu.roll(sl, LANES - HEAD_DIM // 2, 1) * sin_lo)
            h = jnp.concatenate(slabs, axis=1)
        if c == 0 or c == 3:
            h = h * (HEAD_DIM ** -0.5 * LOG2E)
        o_ref[:, c * 512:(c + 1) * 512] = h.astype(BF16)


def _inproj_call(pos, invf, x2d, w_in, tm):
    T = x2d.shape[0]
    return pl.pallas_call(
        _inproj_kernel,
        grid=(T // tm,),
        in_specs=[
            pl.BlockSpec((tm, 1), lambda i: (i, 0)),
            pl.BlockSpec((1, LANES), lambda i: (0, 0)),
            pl.BlockSpec((tm, D_MODEL), lambda i: (i, 0)),
            _resident((D_MODEL, IN_WIDTH)),
        ],
        out_specs=pl.BlockSpec((tm, IN_WIDTH), lambda i: (i, 0)),
        out_shape=jax.ShapeDtypeStruct((T, IN_WIDTH), BF16),
        scratch_shapes=[pltpu.VMEM((D_MODEL, IN_WIDTH), BF16)],
        compiler_params=_params(("arbitrary",)),
        name="inproj",
    )(pos, invf, x2d, w_in)


def _stack_halves(q):
    lane = lax.broadcasted_iota(jnp.int32, q.shape, 1)
    zero = jnp.zeros_like(q)
    return jnp.concatenate([jnp.where(lane < HEAD_DIM, q, zero),
                            jnp.where(lane >= HEAD_DIM, q, zero)], axis=0)


DA_TQ = 256
VT_ONES = 16


def _fill_vt_ones(vt_ref, v):
    nv = v.shape[1]
    vt_ref[0:nv, :] = v.astype(F32).T.astype(vt_ref.dtype)
    vt_ref[nv:, :] = jnp.ones((vt_ref.shape[0] - nv, vt_ref.shape[1]), vt_ref.dtype)


def _diff_scores(nkb, q, k_ref, s_ref, m_ref, *, tq):
    qq = _stack_halves(q)
    m = None
    for j in range(nkb):
        s = _dot_nt(k_ref[0, j * tq:(j + 1) * tq, :], qq)
        if j == nkb - 1:
            row = lax.broadcasted_iota(jnp.int32, s.shape, 0)
            col = lax.broadcasted_iota(jnp.int32, s.shape, 1)
            s = jnp.where((row // CHUNK) <= ((col % tq) // CHUNK), s, NEG_INF)
        s_ref[j * tq:(j + 1) * tq, :] = s
        mj = jnp.max(s, axis=0, keepdims=True)
        m = mj if m is None else jnp.maximum(m, mj)
    m_ref[...] = m


def _diff_output(nkb, s_ref, m_ref, vt_ref, lam_ref, g_ref, o_ref, *, tq, lambda_init):
    m = m_ref[...]
    acc = None
    for j in range(nkb):
        rows = slice(j * tq, (j + 1) * tq)
        part = _dot(vt_ref[:, rows], jnp.exp2(s_ref[rows, :] - m).astype(BF16))
        acc = part if acc is None else acc + part
    o = acc[:DIFF_V_DIM, :] / acc[DIFF_V_DIM:DIFF_V_DIM + 1, :]
    o = o[:, :tq] - lam_ref[...] * o[:, tq:]
    y = o * lax.rsqrt(jnp.mean(o * o, axis=0, keepdims=True) + LN_EPS) * g_ref[...]
    o_ref[...] = (y * (1.0 - lambda_init)).T.astype(o_ref.dtype)


def _diffattn_kernel(lam_ref, g_ref, q_ref, k_ref, v_ref, o_ref, vt_ref, s_ref, m_ref,
                     *, tq, nq, lambda_init):
    _fill_vt_ones(vt_ref, v_ref[0])
    _diff_scores(1, q_ref[0, 0:tq, :], k_ref, s_ref.at[0], m_ref.at[0], tq=tq)
    for c in range(nq):
        if c + 1 < nq:
            nxt = (c + 1) % 2
            _diff_scores(c + 2, q_ref[0, (c + 1) * tq:(c + 2) * tq, :], k_ref, s_ref.at[nxt], m_ref.at[nxt],
                         tq=tq)
        _diff_output(c + 1, s_ref.at[c % 2], m_ref.at[c % 2], vt_ref, lam_ref, g_ref,
                     o_ref.at[0, c * tq:(c + 1) * tq, :], tq=tq, lambda_init=lambda_init)


def _diffattn_call(lam, g, h3, tq, lambda_init):
    B, S, _ = h3.shape
    nq = S // tq
    return pl.pallas_call(
        functools.partial(_diffattn_kernel, tq=tq, nq=nq, lambda_init=lambda_init),
        grid=(B, N_HEADS_DIFF),
        scratch_shapes=[pltpu.VMEM((DIFF_V_DIM + VT_ONES, S), BF16),
                        pltpu.VMEM((2, S, 2 * tq), F32),
                        pltpu.VMEM((2, 1, 2 * tq), F32)],
        in_specs=[
            pl.BlockSpec((1, tq), lambda b, h: (0, 0)),
            pl.BlockSpec((DIFF_V_DIM, 1), lambda b, h: (0, 0)),
            pl.BlockSpec((1, S, LANES), lambda b, h: (b, 0, QA_BLK + h)),
            pl.BlockSpec((1, S, LANES), lambda b, h: (b, 0, KA_BLK + h)),
            pl.BlockSpec((1, S, LANES), lambda b, h: (b, 0, VA_BLK + h)),
        ],
        out_specs=pl.BlockSpec((1, S, LANES), lambda b, h: (b, 0, h)),
        out_shape=jax.ShapeDtypeStruct((B, S, N_HEADS_DIFF * DIFF_V_DIM), BF16),
        compiler_params=_params(("arbitrary", "arbitrary")),
        name="diffattn",
    )(lam, g, h3, h3, h3)


CA_TQ = 256
CA_NKB = 3


def _chunk_bias_table(rel_bias):
    nh = rel_bias.shape[0]
    n_neg = CA_TQ - REL_CLIP + 1
    n_far = CA_NKB * CA_TQ - REL_CLIP
    by_dist = jnp.concatenate([
        jnp.broadcast_to(rel_bias[:, :1], (nh, n_neg)),
        rel_bias[:, 1:2 * REL_CLIP],
        jnp.broadcast_to(rel_bias[:, 2 * REL_CLIP:], (nh, n_far)),
    ], axis=1).astype(F32) * LOG2E
    assert by_dist.shape[1] == (CA_NKB + 1) * CA_TQ
    return by_dist.reshape(nh // 2, 2, (CA_NKB + 1) * CA_TQ)


def _build_bias_table(dist_ref, bias_ref):
    row = lax.broadcasted_iota(jnp.int32, (CA_TQ, CA_TQ), 0) // CHUNK
    col = lax.broadcasted_iota(jnp.int32, (CA_TQ, CA_TQ), 1) // CHUNK
    for t in range(2):
        for j in range(CA_NKB):
            d0 = (CA_NKB - 1 - j) * CA_TQ
            g = jnp.concatenate([dist_ref[0, t:t + 1, d0 + CA_TQ:d0 + 2 * CA_TQ],
                                 dist_ref[0, t:t + 1, d0:d0 + CA_TQ]], axis=1)
            rolled = pltpu.roll(jnp.broadcast_to(g, (CA_TQ, 2 * CA_TQ)), 0, 1, stride=1, stride_axis=0)
            blk = rolled[:, :CA_TQ]
            if j == 0:
                blk = jnp.where(row >= col, blk, NEG_INF)
            elif j == CA_NKB - 1:
                blk = jnp.where(row <= col, blk, NEG_INF)
            bias_ref[j * CA_TQ:(j + 1) * CA_TQ, t * CA_TQ:(t + 1) * CA_TQ] = blk


def _chunkattn_kernel(dist_ref, q_ref, k_ref, v_ref, o_ref, bias_ref, vt_ref, s_ref, m_ref):
    nq = q_ref.shape[1] // CA_TQ

    @pl.when(pl.program_id(1) == 0)
    def _():
        _build_bias_table(dist_ref, bias_ref)


    def key_blocks(blk):
        return [(j, blk - (CA_NKB - 1) + j) for j in range(CA_NKB) if blk - (CA_NKB - 1) + j >= 0]

    def scores(blk, slot):
        qq = _stack_halves(q_ref[0, blk * CA_TQ:(blk + 1) * CA_TQ, :])
        m = None
        for j, kb in key_blocks(blk):
            rows = slice(j * CA_TQ, (j + 1) * CA_TQ)
            s = _dot_nt(k_ref[0, kb * CA_TQ:(kb + 1) * CA_TQ, :], qq) + bias_ref[rows, :]
            s_ref[slot, rows, :] = s
            mj = jnp.max(s, axis=0, keepdims=True)
            m = mj if m is None else jnp.maximum(m, mj)
        m_ref[slot] = m

    def output(blk, slot):
        m = m_ref[slot]
        acc = None
        for j, kb in key_blocks(blk):
            p = jnp.exp2(s_ref[slot, j * CA_TQ:(j + 1) * CA_TQ, :] - m).astype(BF16)
            part = _dot(vt_ref[:, kb * CA_TQ:(kb + 1) * CA_TQ], p)
            acc = part if acc is None else acc + part
        l = acc[LANES:LANES + 1, :]
        o_t = jnp.concatenate([acc[:HEAD_DIM, :CA_TQ] / l[:, :CA_TQ],
                               acc[HEAD_DIM:LANES, CA_TQ:] / l[:, CA_TQ:]], axis=0)
        o_ref[0, blk * CA_TQ:(blk + 1) * CA_TQ, :] = o_t.T.astype(o_ref.dtype)

    _fill_vt_ones(vt_ref, v_ref[0])
    scores(0, 0)
    for blk in range(nq):
        if blk + 1 < nq:
            scores(blk + 1, (blk + 1) % 2)
        output(blk, blk % 2)


def _chunkattn_call(bias, h3):
    B, S, _ = h3.shape
    npairs = N_HEADS_CHUNK // 2
    return pl.pallas_call(
        _chunkattn_kernel,
        grid=(npairs, B),
        in_specs=[
            pl.BlockSpec((1, 2, (CA_NKB + 1) * CA_TQ), lambda p, b: (p, 0, 0)),
            pl.BlockSpec((1, S, LANES), lambda p, b: (b, 0, QB_BLK + p)),
            pl.BlockSpec((1, S, LANES), lambda p, b: (b, 0, KB_BLK + p)),
            pl.BlockSpec((1, S, LANES), lambda p, b: (b, 0, VB_BLK + p)),
        ],
        out_specs=pl.BlockSpec((1, S, LANES), lambda p, b: (b, 0, p)),
        out_shape=jax.ShapeDtypeStruct((B, S, N_HEADS_CHUNK * HEAD_DIM), BF16),
        scratch_shapes=[pltpu.VMEM((CA_NKB * CA_TQ, 2 * CA_TQ), F32),
                        pltpu.VMEM((LANES + VT_ONES, S), BF16),
                        pltpu.VMEM((2, CA_NKB * CA_TQ, 2 * CA_TQ), F32),
                        pltpu.VMEM((2, 1, 2 * CA_TQ), F32)],
        compiler_params=_params(("arbitrary", "arbitrary")),
        name="chunkattn",
    )(bias, h3, h3, h3)


def _outproj_kernel(x_ref, ya_ref, yb_ref, wof_ref, g_ref, b_ref, wqf_ref, x1_ref, q_ref, wo_ref, wq_ref):
    _cast_weights_once(pl.program_id(0) == 0, [(wof_ref, wo_ref), (wqf_ref, wq_ref)])
    half = ya_ref.shape[1]
    tiles = _sub_tiles(x_ref.shape[0])
    ys = [_dot(ya_ref[rows, :], wo_ref[0:half, :]) + _dot(yb_ref[rows, :], wo_ref[half:, :]) for rows in tiles]
    for rows, y in zip(tiles, ys):
        x1 = _layer_norm(DEEPNORM_ALPHA * x_ref[rows, :] + y, g_ref[...], b_ref[...])
        x1_ref[rows, :] = x1
        q = _dot(x1.astype(BF16), wq_ref[...]) * (MEM_HEAD_DIM ** -0.5 * LOG2E)
        q_ref[rows, :] = q.astype(BF16)


def _outproj_call(x2d, ya, yb, w_o, g, b, w_mq, tm):
    T = x2d.shape[0]
    half = ya.shape[1]
    row = lambda i: (i, 0)
    fixed = lambda i: (0, 0)
    return pl.pallas_call(
        _outproj_kernel,
        grid=(T // tm,),
        in_specs=[
            pl.BlockSpec((tm, D_MODEL), row),
            pl.BlockSpec((tm, half), row),
            pl.BlockSpec((tm, half), row),
            _resident((D_MODEL, D_MODEL)),
            pl.BlockSpec((1, D_MODEL), fixed),
            pl.BlockSpec((1, D_MODEL), fixed),
            _resident((D_MODEL, D_MODEL)),
        ],
        out_specs=[pl.BlockSpec((tm, D_MODEL), row), pl.BlockSpec((tm, D_MODEL), row)],
        out_shape=[jax.ShapeDtypeStruct((T, D_MODEL), F32), jax.ShapeDtypeStruct((T, D_MODEL), BF16)],
        scratch_shapes=[pltpu.VMEM((D_MODEL, D_MODEL), BF16)] * 2,
        compiler_params=_params(("arbitrary",)),
        name="outproj",
    )(x2d, ya, yb, w_o, g, b, w_mq)


def _memkv_kernel(mem_ref, wkf_ref, wvf_ref, k_ref, v_ref, wk_ref, wv_ref):
    _cast_weights_once(pl.program_id(0) == 0, [(wkf_ref, wk_ref), (wvf_ref, wv_ref)])
    mb = mem_ref[0].astype(BF16)
    k_ref[0] = _dot(mb, wk_ref[...]).astype(BF16)
    v_ref[0] = _dot(mb, wv_ref[...]).astype(BF16)


def _memkv_call(mem, w_mk, w_mv):
    B, M, _ = mem.shape
    blk = pl.BlockSpec((1, M, D_MODEL), lambda b: (b, 0, 0))
    fixed = _resident((D_MODEL, D_MODEL))
    return pl.pallas_call(
        _memkv_kernel,
        grid=(B,),
        in_specs=[blk, fixed, fixed],
        out_specs=[blk, blk],
        out_shape=[jax.ShapeDtypeStruct((B, M, D_MODEL), BF16)] * 2,
        scratch_shapes=[pltpu.VMEM((D_MODEL, D_MODEL), BF16)] * 2,
        compiler_params=_params(("arbitrary",)),
        name="memkv",
    )(mem, w_mk, w_mv)


def _memattn_kernel(x1_ref, q_ref, k_ref, v_ref, wof_ref, g_ref, b_ref, x2_ref, wo_ref):
    first_step = jnp.logical_and(pl.program_id(0) == 0, pl.program_id(1) == 0)
    _cast_weights_once(first_step, [(wof_ref, wo_ref)])
    tiles = _sub_tiles(x1_ref.shape[1])
    attn = []
    for rows in tiles:
        outs = []
        for h in range(N_HEADS_MEM):
            sl = slice(h * MEM_HEAD_DIM, (h + 1) * MEM_HEAD_DIM)
            s = _dot_nt(q_ref[0, rows, sl], k_ref[0, :, sl])
            m = jnp.max(s, axis=1, keepdims=True)
            p = jnp.exp2(s - m)
            l = jnp.sum(p, axis=1, keepdims=True)
            outs.append((_dot(p.astype(BF16), v_ref[0, :, sl]) / l).astype(BF16))
        attn.append(jnp.concatenate(outs, axis=1))
    ys = [_dot(o, wo_ref[...]) for o in attn]
    for rows, y in zip(tiles, ys):
        x2_ref[0, rows, :] = _layer_norm(DEEPNORM_ALPHA * x1_ref[0, rows, :] + y, g_ref[...], b_ref[...])


def _memattn_call(x1, q, k, v, w_mo, g, b, tm):
    B, S, _ = x1.shape
    M = k.shape[1]
    row = pl.BlockSpec((1, tm, D_MODEL), lambda bb, i: (bb, i, 0))
    kv = pl.BlockSpec((1, M, D_MODEL), lambda bb, i: (bb, 0, 0))
    vec = pl.BlockSpec((1, D_MODEL), lambda bb, i: (0, 0))
    return pl.pallas_call(
        _memattn_kernel,
        grid=(B, S // tm),
        in_specs=[row, row, kv, kv, _resident((D_MODEL, D_MODEL)), vec, vec],
        out_specs=row,
        out_shape=jax.ShapeDtypeStruct((B, S, D_MODEL), F32),
        scratch_shapes=[pltpu.VMEM((D_MODEL, D_MODEL), BF16)],
        compiler_params=_params(("arbitrary", "arbitrary")),
        name="memattn",
    )(x1, q, k, v, w_mo, g, b)


def _mlp_kernel(x_ref, wu_ref, wd_ref, g_ref, b_ref, o_ref, *, ff_chunk):
    def finish(rows, y):
        o_ref[rows, :] = _layer_norm(DEEPNORM_ALPHA * x_ref[rows, :] + y, g_ref[...], b_ref[...])

    pending = None
    for rows in _sub_tiles(x_ref.shape[0]):
        xb = x_ref[rows, :].astype(BF16)
        y = None
        for c in range(D_FF // ff_chunk):
            sl = slice(c * ff_chunk, (c + 1) * ff_chunk)
            h = jnp.maximum(_dot(xb, wu_ref[:, sl]), 0.0)
            part = _dot((h * h).astype(BF16), wd_ref[sl, :])
            y = part if y is None else y + part
            if c == 0 and pending is not None:
                finish(*pending)
                pending = None
        pending = (rows, y)
    finish(*pending)


def _mlp_call(x2d, w_up, w_down, g, b, tm, ff_chunk):
    T = x2d.shape[0]
    row = pl.BlockSpec((tm, D_MODEL), lambda i: (i, 0))
    vec = pl.BlockSpec((1, D_MODEL), lambda i: (0, 0))
    return pl.pallas_call(
        functools.partial(_mlp_kernel, ff_chunk=ff_chunk),
        grid=(T // tm,),
        in_specs=[row,
                  pl.BlockSpec((D_MODEL, D_FF), lambda i: (0, 0), pipeline_mode=pl.Buffered(1)),
                  pl.BlockSpec((D_FF, D_MODEL), lambda i: (0, 0), pipeline_mode=pl.Buffered(1)),
                  vec, vec],
        out_specs=row,
        out_shape=jax.ShapeDtypeStruct((T, D_MODEL), F32),
        compiler_params=_params(("arbitrary",)),
        name="mlp",
    )(x2d, w_up, w_down, g, b)


def kernel(x, mem, positions, w_in, diff_lambda, subln_g, rel_bias, w_o, ln1_g, ln1_b,
           w_mq, w_mk, w_mv, w_mo, ln2_g, ln2_b, w_up, w_down, ln3_g, ln3_b):
    B, S, D = x.shape
    T = B * S
    depth = w_in.shape[0]
    assert depth == DEPTH and D == D_MODEL and S % CA_TQ == 0
    inv_freq = 1.0 / (ROPE_THETA ** (jnp.arange(0, HEAD_DIM, 2, dtype=F32) / HEAD_DIM))
    invf = jnp.tile(inv_freq, LANES // (HEAD_DIM // 2)).reshape(1, LANES)
    pos = positions.reshape(T, 1)
    vec = lambda a: a.reshape(1, -1)

    for l in range(depth):
        lambda_init = 0.8 - 0.6 * math.exp(-0.3 * l)
        lam = _lam_call(diff_lambda[l], lambda_init, DA_TQ)
        h = _inproj_call(pos, invf, x.reshape(T, D), w_in[l], tm=512)
        h3 = h.reshape(B, S, IN_WIDTH)
        ya = _diffattn_call(lam, subln_g[l].reshape(-1, 1), h3, tq=DA_TQ, lambda_init=lambda_init)
        yb = _chunkattn_call(_chunk_bias_table(rel_bias[l]), h3)
        x1, qm = _outproj_call(x.reshape(T, D), ya.reshape(T, -1), yb.reshape(T, -1),
                               w_o[l], vec(ln1_g[l]), vec(ln1_b[l]), w_mq[l], tm=1024)
        km, vm = _memkv_call(mem, w_mk[l], w_mv[l])
        x2 = _memattn_call(x1.reshape(B, S, D), qm.reshape(B, S, D), km, vm,
                           w_mo[l], vec(ln2_g[l]), vec(ln2_b[l]), tm=1024)
        out = _mlp_call(x2.reshape(T, D), w_up[l].astype(BF16), w_down[l].astype(BF16),
                        vec(ln3_g[l]), vec(ln3_b[l]), tm=1024, ff_chunk=1024)
        x = out.reshape(B, S, D)
    return x
```

```python
import functools
import math

import jax
import jax.numpy as jnp
import numpy as np
from jax import lax
from jax.experimental import pallas as pl
from jax.experimental.pallas import tpu as pltpu

D_MODEL = 1024
CHUNK = 64
HEAD_DIM = 64
N_HEADS_DIFF = 4
DIFF_V_DIM = 128
N_HEADS_CHUNK = 8
LEFT_CHUNKS = 8
REL_CLIP = 128
N_HEADS_MEM = 4
MEM_HEAD_DIM = 256
D_FF = 4096
ROPE_THETA = 10000.0
LN_EPS = 1e-5
NEG_INF = -1e30
LOG2E = math.log2(math.e)
DEPTH = 1
DEEPNORM_ALPHA = (2.0 * DEPTH) ** 0.25
IN_WIDTH = 3072

LANES = 128
VMEM_LIMIT = 56 * 1024 * 1024

F32 = jnp.float32
BF16 = jnp.bfloat16

QA_BLK, KA_BLK, VA_BLK, QB_BLK, KB_BLK, VB_BLK = 0, 4, 8, 12, 16, 20


def _params(sem):
    return pltpu.CompilerParams(dimension_semantics=sem, vmem_limit_bytes=VMEM_LIMIT)


def _dot(a, b):
    return jnp.dot(a, b, preferred_element_type=F32)


def _dot_nt(a, b):
    return lax.dot_general(a, b, (((1,), (1,)), ((), ())), preferred_element_type=F32)


SUB_ROWS = 512


def _sub_tiles(n_rows):
    return [slice(r, r + SUB_ROWS) for r in range(0, n_rows, SUB_ROWS)]


def _resident(shape):
    return pl.BlockSpec(shape, lambda *_: (0,) * len(shape), pipeline_mode=pl.Buffered(1))


def _cast_weights_once(first_step, pairs):
    @pl.when(first_step)
    def _():
        for src_ref, dst_ref in pairs:
            dst_ref[...] = src_ref[...].astype(dst_ref.dtype)


def _layer_norm(z, g, b):
    mu = jnp.mean(z, axis=-1, keepdims=True)
    zc = z - mu
    var = jnp.mean(zc * zc, axis=-1, keepdims=True)
    return zc * lax.rsqrt(var + LN_EPS) * g + b


def _lam_kernel(dl_ref, o_ref, *, lambda_init):
    dl = dl_ref[...]
    s1 = jnp.sum(dl[0:1, :] * dl[1:2, :], axis=1, keepdims=True)
    s2 = jnp.sum(dl[2:3, :] * dl[3:4, :], axis=1, keepdims=True)
    lam = jnp.exp(s1) - jnp.exp(s2) + lambda_init
    o_ref[...] = jnp.broadcast_to(lam, o_ref.shape)


def _lam_call(diff_lambda, lambda_init, width):
    return pl.pallas_call(
        functools.partial(_lam_kernel, lambda_init=lambda_init),
        out_shape=jax.ShapeDtypeStruct((1, width), F32),
        name="lam",
    )(diff_lambda)


IN_TM = 512
ROPE_PACK = LANES // (HEAD_DIM // 2)


def _spread_token_groups(t):
    n = t.shape[0]
    grp = lax.broadcasted_iota(jnp.int32, t.shape, 1) // (HEAD_DIM // 2)
    out = []
    for a in range(ROPE_PACK):
        spread = t
        for g in range(ROPE_PACK):
            if g != a:
                moved = pltpu.roll(t, ((g - a) % ROPE_PACK) * (HEAD_DIM // 2), 1)
                spread = jnp.where(grp == g, moved, spread)
        out.append(spread)
    return jnp.concatenate(out, axis=0)


def _inproj_kernel(pos_ref, invf_ref, x_ref, wf_ref, o_ref, w_ref):
    _cast_weights_once(pl.program_id(0) == 0, [(wf_ref, w_ref)])
    xb = x_ref[...].astype(BF16)
    ang = pos_ref[...].astype(F32) * invf_ref[...]
    cos = _spread_token_groups(jnp.cos(ang))
    sin = _spread_token_groups(jnp.sin(ang))
    lane = lax.broadcasted_iota(jnp.int32, cos.shape, 1)
    upper = (lane % HEAD_DIM) >= (HEAD_DIM // 2)
    sin_up = jnp.where(upper, sin, 0.0)
    sin_lo = jnp.where(upper, 0.0, -sin)
    n_chunks = IN_WIDTH // 512
    for c in range(n_chunks):
        h = _dot(xb, w_ref[:, c * 512:(c + 1) * 512])
        if c < 2:
            slabs = []
            for s in range(4):
                sl = h[:, s * LANES:(s + 1) * LANES]
                slabs.append(sl * cos
                             + pltpu.roll(sl, HEAD_DIM // 2, 1) * sin_up
                             + pltpu.roll(sl, LANES - HEAD_DIM // 2, 1) * sin_lo)
            h = jnp.concatenate(slabs, axis=1)
        if c == 0 or c == 3:
            h = h * (HEAD_DIM ** -0.5 * LOG2E)
        o_ref[:, c * 512:(c + 1) * 512] = h.astype(BF16)


def _inproj_call(pos, invf, x2d, w_in, tm):
    T = x2d.shape[0]
    return pl.pallas_call(
        _inproj_kernel,
        grid=(T // tm,),
        in_specs=[
            pl.BlockSpec((tm // ROPE_PACK, LANES), lambda i: (i, 0)),
            pl.BlockSpec((1, LANES), lambda i: (0, 0)),
            pl.BlockSpec((tm, D_MODEL), lambda i: (i, 0)),
            _resident((D_MODEL, IN_WIDTH)),
        ],
        out_specs=pl.BlockSpec((tm, IN_WIDTH), lambda i: (i, 0)),
        out_shape=jax.ShapeDtypeStruct((T, IN_WIDTH), BF16),
        scratch_shapes=[pltpu.VMEM((D_MODEL, IN_WIDTH), BF16)],
        compiler_params=_params(("arbitrary",)),
        name="inproj",
    )(pos, invf, x2d, w_in)


def _stack_halves(q):
    lane = lax.broadcasted_iota(jnp.int32, q.shape, 1)
    zero = jnp.zeros_like(q)
    return jnp.concatenate([jnp.where(lane < HEAD_DIM, q, zero),
                            jnp.where(lane >= HEAD_DIM, q, zero)], axis=0)


DA_TQ = 256
VT_ONES = 16


def _fill_vt_ones(vt_ref, v):
    nv = v.shape[1]
    vt_ref[0:nv, :] = v.astype(F32).T.astype(vt_ref.dtype)
    vt_ref[nv:, :] = jnp.ones((vt_ref.shape[0] - nv, vt_ref.shape[1]), vt_ref.dtype)


def _diff_scores(nkb, q, k_ref, s_ref, m_ref, *, tq):
    qq = _stack_halves(q)
    m = None
    for j in range(nkb):
        s = _dot_nt(k_ref[0, j * tq:(j + 1) * tq, :], qq)
        if j == nkb - 1:
            row = lax.broadcasted_iota(jnp.int32, s.shape, 0)
            col = lax.broadcasted_iota(jnp.int32, s.shape, 1)
            s = jnp.where((row // CHUNK) <= ((col % tq) // CHUNK), s, NEG_INF)
        s_ref[j * tq:(j + 1) * tq, :] = s
        mj = jnp.max(s, axis=0, keepdims=True)
        m = mj if m is None else jnp.maximum(m, mj)
    m_ref[...] = m


def _diff_output(nkb, s_ref, m_ref, vt_ref, lam_ref, g_ref, o_ref, *, tq, lambda_init):
    m = m_ref[...]
    acc = None
    for j in range(nkb):
        rows = slice(j * tq, (j + 1) * tq)
        part = _dot(vt_ref[:, rows], jnp.exp2(s_ref[rows, :] - m).astype(BF16))
        acc = part if acc is None else acc + part
    o = acc[:DIFF_V_DIM, :] / acc[DIFF_V_DIM:DIFF_V_DIM + 1, :]
    o = o[:, :tq] - lam_ref[...] * o[:, tq:]
    y = o * lax.rsqrt(jnp.mean(o * o, axis=0, keepdims=True) + LN_EPS) * g_ref[...]
    o_ref[...] = (y * (1.0 - lambda_init)).T.astype(o_ref.dtype)


def _diffattn_kernel(lam_ref, g_ref, q_ref, k_ref, v_ref, wf_ref, o_ref, wb_ref, vt_ref, s_ref, m_ref,
                     *, tq, nq, lambda_init):
    wb_ref[...] = wf_ref[...].astype(wb_ref.dtype)
    _fill_vt_ones(vt_ref, v_ref[0])
    _diff_scores(1, q_ref[0, 0:tq, :], k_ref, s_ref.at[0], m_ref.at[0], tq=tq)
    for c in range(nq):
        if c + 1 < nq:
            nxt = (c + 1) % 2
            _diff_scores(c + 2, q_ref[0, (c + 1) * tq:(c + 2) * tq, :], k_ref, s_ref.at[nxt], m_ref.at[nxt],
                         tq=tq)
        _diff_output(c + 1, s_ref.at[c % 2], m_ref.at[c % 2], vt_ref, lam_ref, g_ref,
                     o_ref.at[0, c * tq:(c + 1) * tq, :], tq=tq, lambda_init=lambda_init)


def _diffattn_call(lam, g, h3, w_cast, tq, lambda_init):
    B, S, _ = h3.shape
    nq = S // tq
    wrows, wcols = w_cast.shape
    wspec = pl.BlockSpec((wrows // (B * N_HEADS_DIFF), wcols), lambda b, h: (b * N_HEADS_DIFF + h, 0))
    return pl.pallas_call(
        functools.partial(_diffattn_kernel, tq=tq, nq=nq, lambda_init=lambda_init),
        grid=(B, N_HEADS_DIFF),
        scratch_shapes=[pltpu.VMEM((DIFF_V_DIM + VT_ONES, S), BF16),
                        pltpu.VMEM((2, S, 2 * tq), F32),
                        pltpu.VMEM((2, 1, 2 * tq), F32)],
        in_specs=[
            pl.BlockSpec((1, tq), lambda b, h: (0, 0)),
            pl.BlockSpec((DIFF_V_DIM, 1), lambda b, h: (0, 0)),
            pl.BlockSpec((1, S, LANES), lambda b, h: (b, 0, QA_BLK + h)),
            pl.BlockSpec((1, S, LANES), lambda b, h: (b, 0, KA_BLK + h)),
            pl.BlockSpec((1, S, LANES), lambda b, h: (b, 0, VA_BLK + h)),
            wspec,
        ],
        out_specs=[pl.BlockSpec((1, S, LANES), lambda b, h: (b, 0, h)), wspec],
        out_shape=[jax.ShapeDtypeStruct((B, S, N_HEADS_DIFF * DIFF_V_DIM), BF16),
                   jax.ShapeDtypeStruct(w_cast.shape, BF16)],
        compiler_params=_params(("arbitrary", "arbitrary")),
        name="diffattn",
    )(lam, g, h3, h3, h3, w_cast)


CA_TQ = 256
CA_NKB = 3


def _chunk_bias_table(rel_bias):
    nh = rel_bias.shape[0]
    n_neg = CA_TQ - REL_CLIP + 1
    n_far = CA_NKB * CA_TQ - REL_CLIP
    by_dist = jnp.concatenate([
        jnp.broadcast_to(rel_bias[:, :1], (nh, n_neg)),
        rel_bias[:, 1:2 * REL_CLIP],
        jnp.broadcast_to(rel_bias[:, 2 * REL_CLIP:], (nh, n_far)),
    ], axis=1).astype(F32) * LOG2E
    assert by_dist.shape[1] == (CA_NKB + 1) * CA_TQ
    return by_dist.reshape(nh // 2, 2, (CA_NKB + 1) * CA_TQ)


def _build_bias_table(dist_ref, bias_ref):
    row = lax.broadcasted_iota(jnp.int32, (CA_TQ, CA_TQ), 0) // CHUNK
    col = lax.broadcasted_iota(jnp.int32, (CA_TQ, CA_TQ), 1) // CHUNK
    for t in range(2):
        for j in range(CA_NKB):
            d0 = (CA_NKB - 1 - j) * CA_TQ
            g = jnp.concatenate([dist_ref[0, t:t + 1, d0 + CA_TQ:d0 + 2 * CA_TQ],
                                 dist_ref[0, t:t + 1, d0:d0 + CA_TQ]], axis=1)
            rolled = pltpu.roll(jnp.broadcast_to(g, (CA_TQ, 2 * CA_TQ)), 0, 1, stride=1, stride_axis=0)
            blk = rolled[:, :CA_TQ]
            if j == 0:
                blk = jnp.where(row >= col, blk, NEG_INF)
            elif j == CA_NKB - 1:
                blk = jnp.where(row <= col, blk, NEG_INF)
            bias_ref[j * CA_TQ:(j + 1) * CA_TQ, t * CA_TQ:(t + 1) * CA_TQ] = blk


def _chunkattn_kernel(dist_ref, q_ref, k_ref, v_ref, wf_ref, o_ref, wb_ref, bias_ref, vt_ref, s_ref, m_ref):
    nq = q_ref.shape[1] // CA_TQ
    wb_ref[...] = wf_ref[...].astype(wb_ref.dtype)

    @pl.when(pl.program_id(1) == 0)
    def _():
        _build_bias_table(dist_ref, bias_ref)

    def key_blocks(blk):
        return [(j, blk - (CA_NKB - 1) + j) for j in range(CA_NKB) if blk - (CA_NKB - 1) + j >= 0]

    def scores(blk, slot):
        qq = _stack_halves(q_ref[0, blk * CA_TQ:(blk + 1) * CA_TQ, :])
        m = None
        for j, kb in key_blocks(blk):
            rows = slice(j * CA_TQ, (j + 1) * CA_TQ)
            s = _dot_nt(k_ref[0, kb * CA_TQ:(kb + 1) * CA_TQ, :], qq) + bias_ref[rows, :]
            s_ref[slot, rows, :] = s
            mj = jnp.max(s, axis=0, keepdims=True)
            m = mj if m is None else jnp.maximum(m, mj)
        m_ref[slot] = m

    def output(blk, slot):
        m = m_ref[slot]
        acc = None
        for j, kb in key_blocks(blk):
            p = jnp.exp2(s_ref[slot, j * CA_TQ:(j + 1) * CA_TQ, :] - m).astype(BF16)
            part = _dot(vt_ref[:, kb * CA_TQ:(kb + 1) * CA_TQ], p)
            acc = part if acc is None else acc + part
        l = acc[LANES:LANES + 1, :]
        o_t = jnp.concatenate([acc[:HEAD_DIM, :CA_TQ] / l[:, :CA_TQ],
                               acc[HEAD_DIM:LANES, CA_TQ:] / l[:, CA_TQ:]], axis=0)
        o_ref[0, blk * CA_TQ:(blk + 1) * CA_TQ, :] = o_t.T.astype(o_ref.dtype)

    _fill_vt_ones(vt_ref, v_ref[0])
    scores(0, 0)
    for blk in range(nq):
        if blk + 1 < nq:
            scores(blk + 1, (blk + 1) % 2)
        output(blk, blk % 2)


def _chunkattn_call(bias, h3, w_cast):
    B, S, _ = h3.shape
    npairs = N_HEADS_CHUNK // 2
    wrows, wcols = w_cast.shape
    wspec = pl.BlockSpec((wrows // (npairs * B), wcols), lambda p, b: (p * B + b, 0))
    return pl.pallas_call(
        _chunkattn_kernel,
        grid=(npairs, B),
        in_specs=[
            pl.BlockSpec((1, 2, (CA_NKB + 1) * CA_TQ), lambda p, b: (p, 0, 0)),
            pl.BlockSpec((1, S, LANES), lambda p, b: (b, 0, QB_BLK + p)),
            pl.BlockSpec((1, S, LANES), lambda p, b: (b, 0, KB_BLK + p)),
            pl.BlockSpec((1, S, LANES), lambda p, b: (b, 0, VB_BLK + p)),
            wspec,
        ],
        out_specs=[pl.BlockSpec((1, S, LANES), lambda p, b: (b, 0, p)), wspec],
        out_shape=[jax.ShapeDtypeStruct((B, S, N_HEADS_CHUNK * HEAD_DIM), BF16),
                   jax.ShapeDtypeStruct(w_cast.shape, BF16)],
        scratch_shapes=[pltpu.VMEM((CA_NKB * CA_TQ, 2 * CA_TQ), F32),
                        pltpu.VMEM((LANES + VT_ONES, S), BF16),
                        pltpu.VMEM((2, CA_NKB * CA_TQ, 2 * CA_TQ), F32),
                        pltpu.VMEM((2, 1, 2 * CA_TQ), F32)],
        compiler_params=_params(("arbitrary", "arbitrary")),
        name="chunkattn",
    )(bias, h3, h3, h3, w_cast)


def _outproj_kernel(x_ref, ya_ref, yb_ref, wof_ref, g_ref, b_ref, wqf_ref, x1_ref, q_ref, wo_ref, wq_ref):
    _cast_weights_once(pl.program_id(0) == 0, [(wof_ref, wo_ref), (wqf_ref, wq_ref)])
    half = ya_ref.shape[1]
    tiles = _sub_tiles(x_ref.shape[0])
    ys = [_dot(ya_ref[rows, :], wo_ref[0:half, :]) + _dot(yb_ref[rows, :], wo_ref[half:, :]) for rows in tiles]
    for rows, y in zip(tiles, ys):
        x1 = _layer_norm(DEEPNORM_ALPHA * x_ref[rows, :] + y, g_ref[...], b_ref[...])
        x1_ref[rows, :] = x1
        q = _dot(x1.astype(BF16), wq_ref[...]) * (MEM_HEAD_DIM ** -0.5 * LOG2E)
        q_ref[rows, :] = q.astype(BF16)


def _outproj_call(x2d, ya, yb, w_o, g, b, w_mq, tm):
    T = x2d.shape[0]
    half = ya.shape[1]
    row = lambda i: (i, 0)
    fixed = lambda i: (0, 0)
    return pl.pallas_call(
        _outproj_kernel,
        grid=(T // tm,),
        in_specs=[
            pl.BlockSpec((tm, D_MODEL), row),
            pl.BlockSpec((tm, half), row),
            pl.BlockSpec((tm, half), row),
            _resident((D_MODEL, D_MODEL)),
            pl.BlockSpec((1, D_MODEL), fixed),
            pl.BlockSpec((1, D_MODEL), fixed),
            _resident((D_MODEL, D_MODEL)),
        ],
        out_specs=[pl.BlockSpec((tm, D_MODEL), row), pl.BlockSpec((tm, D_MODEL), row)],
        out_shape=[jax.ShapeDtypeStruct((T, D_MODEL), F32), jax.ShapeDtypeStruct((T, D_MODEL), BF16)],
        scratch_shapes=[pltpu.VMEM((D_MODEL, D_MODEL), BF16)] * 2,
        compiler_params=_params(("arbitrary",)),
        name="outproj",
    )(x2d, ya, yb, w_o, g, b, w_mq)


def _memkv_kernel(mem_ref, wkf_ref, wvf_ref, k_ref, v_ref, wk_ref, wv_ref):
    _cast_weights_once(pl.program_id(0) == 0, [(wkf_ref, wk_ref), (wvf_ref, wv_ref)])
    mb = mem_ref[0].astype(BF16)
    k_ref[0] = _dot(mb, wk_ref[...]).astype(BF16)
    v_ref[0] = _dot(mb, wv_ref[...]).astype(BF16)


def _memkv_call(mem, w_mk, w_mv):
    B, M, _ = mem.shape
    blk = pl.BlockSpec((1, M, D_MODEL), lambda b: (b, 0, 0))
    fixed = _resident((D_MODEL, D_MODEL))
    return pl.pallas_call(
        _memkv_kernel,
        grid=(B,),
        in_specs=[blk, fixed, fixed],
        out_specs=[blk, blk],
        out_shape=[jax.ShapeDtypeStruct((B, M, D_MODEL), BF16)] * 2,
        scratch_shapes=[pltpu.VMEM((D_MODEL, D_MODEL), BF16)] * 2,
        compiler_params=_params(("arbitrary",)),
        name="memkv",
    )(mem, w_mk, w_mv)


def _memattn_kernel(x1_ref, q_ref, k_ref, v_ref, wof_ref, g_ref, b_ref, x2_ref, wo_ref):
    first_step = jnp.logical_and(pl.program_id(0) == 0, pl.program_id(1) == 0)
    _cast_weights_once(first_step, [(wof_ref, wo_ref)])
    tiles = _sub_tiles(x1_ref.shape[1])
    attn = []
    for rows in tiles:
        outs = []
        for h in range(N_HEADS_MEM):
            sl = slice(h * MEM_HEAD_DIM, (h + 1) * MEM_HEAD_DIM)
            s = _dot_nt(q_ref[0, rows, sl], k_ref[0, :, sl])
            m = jnp.max(s, axis=1, keepdims=True)
            p = jnp.exp2(s - m)
            l = jnp.sum(p, axis=1, keepdims=True)
            outs.append((_dot(p.astype(BF16), v_ref[0, :, sl]) / l).astype(BF16))
        attn.append(jnp.concatenate(outs, axis=1))
    ys = [_dot(o, wo_ref[...]) for o in attn]
    for rows, y in zip(tiles, ys):
        x2_ref[0, rows, :] = _layer_norm(DEEPNORM_ALPHA * x1_ref[0, rows, :] + y, g_ref[...], b_ref[...])


def _memattn_call(x1, q, k, v, w_mo, g, b, tm):
    B, S, _ = x1.shape
    M = k.shape[1]
    row = pl.BlockSpec((1, tm, D_MODEL), lambda bb, i: (bb, i, 0))
    kv = pl.BlockSpec((1, M, D_MODEL), lambda bb, i: (bb, 0, 0))
    vec = pl.BlockSpec((1, D_MODEL), lambda bb, i: (0, 0))
    return pl.pallas_call(
        _memattn_kernel,
        grid=(B, S // tm),
        in_specs=[row, row, kv, kv, _resident((D_MODEL, D_MODEL)), vec, vec],
        out_specs=row,
        out_shape=jax.ShapeDtypeStruct((B, S, D_MODEL), F32),
        scratch_shapes=[pltpu.VMEM((D_MODEL, D_MODEL), BF16)],
        compiler_params=_params(("arbitrary", "arbitrary")),
        name="memattn",
    )(x1, q, k, v, w_mo, g, b)


def _mlp_kernel(x_ref, wu_ref, wd_ref, g_ref, b_ref, o_ref, *, ff_chunk):
    def finish(rows, y):
        o_ref[rows, :] = _layer_norm(DEEPNORM_ALPHA * x_ref[rows, :] + y, g_ref[...], b_ref[...])

    pending = None
    for rows in _sub_tiles(x_ref.shape[0]):
        xb = x_ref[rows, :].astype(BF16)
        y = None
        for c in range(D_FF // ff_chunk):
            sl = slice(c * ff_chunk, (c + 1) * ff_chunk)
            h = jnp.maximum(_dot(xb, wu_ref[:, sl]), 0.0)
            part = _dot((h * h).astype(BF16), wd_ref[sl, :])
            y = part if y is None else y + part
            if c == 0 and pending is not None:
                finish(*pending)
                pending = None
        pending = (rows, y)
    finish(*pending)


def _mlp_call(x2d, w_up, w_down, g, b, tm, ff_chunk):
    T = x2d.shape[0]
    row = pl.BlockSpec((tm, D_MODEL), lambda i: (i, 0))
    vec = pl.BlockSpec((1, D_MODEL), lambda i: (0, 0))
    return pl.pallas_call(
        functools.partial(_mlp_kernel, ff_chunk=ff_chunk),
        grid=(T // tm,),
        in_specs=[row,
                  pl.BlockSpec((D_MODEL, D_FF), lambda i: (0, 0), pipeline_mode=pl.Buffered(1)),
                  pl.BlockSpec((D_FF, D_MODEL), lambda i: (0, 0), pipeline_mode=pl.Buffered(1)),
                  vec, vec],
        out_specs=row,
        out_shape=jax.ShapeDtypeStruct((T, D_MODEL), F32),
        compiler_params=_params(("arbitrary",)),
        name="mlp",
    )(x2d, w_up, w_down, g, b)


def kernel(x, mem, positions, w_in, diff_lambda, subln_g, rel_bias, w_o, ln1_g, ln1_b,
           w_mq, w_mk, w_mv, w_mo, ln2_g, ln2_b, w_up, w_down, ln3_g, ln3_b):
    B, S, D = x.shape
    T = B * S
    depth = w_in.shape[0]
    assert depth == DEPTH and D == D_MODEL and S % CA_TQ == 0
    inv_freq = 1.0 / (ROPE_THETA ** (jnp.arange(0, HEAD_DIM, 2, dtype=F32) / HEAD_DIM))
    invf = jnp.tile(inv_freq, LANES // (HEAD_DIM // 2)).reshape(1, LANES)
    pos = positions.reshape(T // IN_TM, ROPE_PACK, IN_TM // ROPE_PACK).transpose(0, 2, 1)
    pos = jnp.repeat(pos, HEAD_DIM // 2, axis=2).reshape(T // ROPE_PACK, LANES)
    vec = lambda a: a.reshape(1, -1)

    for l in range(depth):
        lambda_init = 0.8 - 0.6 * math.exp(-0.3 * l)
        lam = _lam_call(diff_lambda[l], lambda_init, DA_TQ)
        h = _inproj_call(pos, invf, x.reshape(T, D), w_in[l], tm=IN_TM)
        h3 = h.reshape(B, S, IN_WIDTH)
        ya, w_up_b = _diffattn_call(lam, subln_g[l].reshape(-1, 1), h3, w_up[l], tq=DA_TQ,
                                    lambda_init=lambda_init)
        yb, w_down_b = _chunkattn_call(_chunk_bias_table(rel_bias[l]), h3, w_down[l])
        x1, qm = _outproj_call(x.reshape(T, D), ya.reshape(T, -1), yb.reshape(T, -1),
                               w_o[l], vec(ln1_g[l]), vec(ln1_b[l]), w_mq[l], tm=1024)
        km, vm = _memkv_call(mem, w_mk[l], w_mv[l])
        x2 = _memattn_call(x1.reshape(B, S, D), qm.reshape(B, S, D), km, vm,
                           w_mo[l], vec(ln2_g[l]), vec(ln2_b[l]), tm=1024)
        out = _mlp_call(x2.reshape(T, D), w_up_b, w_down_b,
                        vec(ln3_g[l]), vec(ln3_b[l]), tm=1024, ff_chunk=1024)
        x = out.reshape(B, S, D)
    return x
```

```python
import functools
import math

import jax
import jax.numpy as jnp
import numpy as np
from jax import lax
from jax.experimental import pallas as pl
from jax.experimental.pallas import tpu as pltpu

D_MODEL = 1024
CHUNK = 64
HEAD_DIM = 64
N_HEADS_DIFF = 4
DIFF_V_DIM = 128
N_HEADS_CHUNK = 8
LEFT_CHUNKS = 8
REL_CLIP = 128
N_HEADS_MEM = 4
MEM_HEAD_DIM = 256
D_FF = 4096
ROPE_THETA = 10000.0
LN_EPS = 1e-5
NEG_INF = -1e30
LOG2E = math.log2(math.e)
DEPTH = 1
DEEPNORM_ALPHA = (2.0 * DEPTH) ** 0.25
IN_WIDTH = 3072

LANES = 128
VMEM_LIMIT = 56 * 1024 * 1024

F32 = jnp.float32
BF16 = jnp.bfloat16

QA_BLK, KA_BLK, VA_BLK, QB_BLK, KB_BLK, VB_BLK = 0, 4, 8, 12, 16, 20


def _params(sem):
    return pltpu.CompilerParams(dimension_semantics=sem, vmem_limit_bytes=VMEM_LIMIT)


def _dot(a, b):
    return jnp.dot(a, b, preferred_element_type=F32)


def _dot_nt(a, b):
    return lax.dot_general(a, b, (((1,), (1,)), ((), ())), preferred_element_type=F32)


SUB_ROWS = 512


def _sub_tiles(n_rows):
    return [slice(r, r + SUB_ROWS) for r in range(0, n_rows, SUB_ROWS)]


def _resident(shape):
    return pl.BlockSpec(shape, lambda *_: (0,) * len(shape), pipeline_mode=pl.Buffered(1))


def _cast_weights_once(first_step, pairs):
    @pl.when(first_step)
    def _():
        for src_ref, dst_ref in pairs:
            dst_ref[...] = src_ref[...].astype(dst_ref.dtype)


def _layer_norm(z, g, b):
    mu = jnp.mean(z, axis=-1, keepdims=True)
    zc = z - mu
    var = jnp.mean(zc * zc, axis=-1, keepdims=True)
    return zc * lax.rsqrt(var + LN_EPS) * g + b


def _lam_kernel(dl_ref, o_ref, *, lambda_init):
    dl = dl_ref[...]
    s1 = jnp.sum(dl[0:1, :] * dl[1:2, :], axis=1, keepdims=True)
    s2 = jnp.sum(dl[2:3, :] * dl[3:4, :], axis=1, keepdims=True)
    lam = jnp.exp(s1) - jnp.exp(s2) + lambda_init
    o_ref[...] = jnp.broadcast_to(lam, o_ref.shape)


def _lam_call(diff_lambda, lambda_init, width):
    return pl.pallas_call(
        functools.partial(_lam_kernel, lambda_init=lambda_init),
        out_shape=jax.ShapeDtypeStruct((1, width), F32),
        name="lam",
    )(diff_lambda)


IN_TM = 512
ROPE_PACK = LANES // (HEAD_DIM // 2)


def _spread_token_groups(t):
    n = t.shape[0]
    grp = lax.broadcasted_iota(jnp.int32, t.shape, 1) // (HEAD_DIM // 2)
    out = []
    for a in range(ROPE_PACK):
        spread = t
        for g in range(ROPE_PACK):
            if g != a:
                moved = pltpu.roll(t, ((g - a) % ROPE_PACK) * (HEAD_DIM // 2), 1)
                spread = jnp.where(grp == g, moved, spread)
        out.append(spread)
    return jnp.concatenate(out, axis=0)


def _inproj_kernel(pos_ref, invf_ref, x_ref, wf_ref, o_ref, w_ref):
    _cast_weights_once(pl.program_id(0) == 0, [(wf_ref, w_ref)])
    xb = x_ref[...].astype(BF16)
    ang = pos_ref[...].astype(F32) * invf_ref[...]
    cos = _spread_token_groups(jnp.cos(ang))
    sin = _spread_token_groups(jnp.sin(ang))
    lane = lax.broadcasted_iota(jnp.int32, cos.shape, 1)
    upper = (lane % HEAD_DIM) >= (HEAD_DIM // 2)
    sin_up = jnp.where(upper, sin, 0.0)
    sin_lo = jnp.where(upper, 0.0, -sin)
    n_chunks = IN_WIDTH // 512
    for c in range(n_chunks):
        h = _dot(xb, w_ref[:, c * 512:(c + 1) * 512])
        if c < 2:
            slabs = []
            for s in range(4):
                sl = h[:, s * LANES:(s + 1) * LANES]
                slabs.append(sl * cos
                             + pltpu.roll(sl, HEAD_DIM // 2, 1) * sin_up
                             + pltpu.roll(sl, LANES - HEAD_DIM // 2, 1) * sin_lo)
            h = jnp.concatenate(slabs, axis=1)
        if c == 0 or c == 3:
            h = h * (HEAD_DIM ** -0.5 * LOG2E)
        o_ref[:, c * 512:(c + 1) * 512] = h.astype(BF16)


def _inproj_call(pos, invf, x2d, w_in, tm):
    T = x2d.shape[0]
    return pl.pallas_call(
        _inproj_kernel,
        grid=(T // tm,),
        in_specs=[
            pl.BlockSpec((tm // ROPE_PACK, LANES), lambda i: (i, 0)),
            pl.BlockSpec((1, LANES), lambda i: (0, 0)),
            pl.BlockSpec((tm, D_MODEL), lambda i: (i, 0)),
            _resident((D_MODEL, IN_WIDTH)),
        ],
        out_specs=pl.BlockSpec((tm, IN_WIDTH), lambda i: (i, 0)),
        out_shape=jax.ShapeDtypeStruct((T, IN_WIDTH), BF16),
        scratch_shapes=[pltpu.VMEM((D_MODEL, IN_WIDTH), BF16)],
        compiler_params=_params(("arbitrary",)),
        name="inproj",
    )(pos, invf, x2d, w_in)


def _stack_halves(q):
    lane = lax.broadcasted_iota(jnp.int32, q.shape, 1)
    zero = jnp.zeros_like(q)
    return jnp.concatenate([jnp.where(lane < HEAD_DIM, q, zero),
                            jnp.where(lane >= HEAD_DIM, q, zero)], axis=0)


DA_TQ = 256
VT_ONES = 16


def _fill_vt_ones(vt_ref, v):
    nv = v.shape[1]
    vt_ref[0:nv, :] = v.astype(F32).T.astype(vt_ref.dtype)
    vt_ref[nv:, :] = jnp.ones((vt_ref.shape[0] - nv, vt_ref.shape[1]), vt_ref.dtype)


def _diff_scores(nkb, q, k_ref, s_ref, m_ref, *, tq):
    qq = _stack_halves(q)
    m = None
    for j in range(nkb):
        s = _dot_nt(k_ref[0, j * tq:(j + 1) * tq, :], qq)
        if j == nkb - 1:
            row = lax.broadcasted_iota(jnp.int32, s.shape, 0)
            col = lax.broadcasted_iota(jnp.int32, s.shape, 1)
            s = jnp.where((row // CHUNK) <= ((col % tq) // CHUNK), s, NEG_INF)
        s_ref[j * tq:(j + 1) * tq, :] = s
        mj = jnp.max(s, axis=0, keepdims=True)
        m = mj if m is None else jnp.maximum(m, mj)
    m_ref[...] = m


def _diff_output(nkb, s_ref, m_ref, vt_ref, lam_ref, g_ref, o_ref, *, tq, lambda_init):
    m = m_ref[...]
    acc = None
    for j in range(nkb):
        rows = slice(j * tq, (j + 1) * tq)
        part = _dot(vt_ref[:, rows], jnp.exp2(s_ref[rows, :] - m).astype(BF16))
        acc = part if acc is None else acc + part
    o = acc[:DIFF_V_DIM, :] / acc[DIFF_V_DIM:DIFF_V_DIM + 1, :]
    o = o[:, :tq] - lam_ref[...] * o[:, tq:]
    y = o * lax.rsqrt(jnp.mean(o * o, axis=0, keepdims=True) + LN_EPS) * g_ref[...]
    o_ref[...] = (y * (1.0 - lambda_init)).T.astype(o_ref.dtype)


def _diffattn_kernel(lam_ref, g_ref, q_ref, k_ref, v_ref, wf_ref, o_ref, wb_ref, vt_ref, s_ref, m_ref,
                     *, tq, nq, lambda_init):
    wb_ref[...] = wf_ref[...].astype(wb_ref.dtype)
    _fill_vt_ones(vt_ref, v_ref[0])
    _diff_scores(1, q_ref[0, 0:tq, :], k_ref, s_ref.at[0], m_ref.at[0], tq=tq)
    for c in range(nq):
        if c + 1 < nq:
            nxt = (c + 1) % 2
            _diff_scores(c + 2, q_ref[0, (c + 1) * tq:(c + 2) * tq, :], k_ref, s_ref.at[nxt], m_ref.at[nxt],
                         tq=tq)
        _diff_output(c + 1, s_ref.at[c % 2], m_ref.at[c % 2], vt_ref, lam_ref, g_ref,
                     o_ref.at[0, c * tq:(c + 1) * tq, :], tq=tq, lambda_init=lambda_init)


def _diffattn_call(lam, g, h3, w_cast, tq, lambda_init):
    B, S, _ = h3.shape
    nq = S // tq
    wrows, wcols = w_cast.shape
    wspec = pl.BlockSpec((wrows // (B * N_HEADS_DIFF), wcols), lambda b, h: (b * N_HEADS_DIFF + h, 0))
    return pl.pallas_call(
        functools.partial(_diffattn_kernel, tq=tq, nq=nq, lambda_init=lambda_init),
        grid=(B, N_HEADS_DIFF),
        scratch_shapes=[pltpu.VMEM((DIFF_V_DIM + VT_ONES, S), BF16),
                        pltpu.VMEM((2, S, 2 * tq), F32),
                        pltpu.VMEM((2, 1, 2 * tq), F32)],
        in_specs=[
            pl.BlockSpec((1, tq), lambda b, h: (0, 0)),
            pl.BlockSpec((DIFF_V_DIM, 1), lambda b, h: (0, 0)),
            pl.BlockSpec((1, S, LANES), lambda b, h: (b, 0, QA_BLK + h)),
            pl.BlockSpec((1, S, LANES), lambda b, h: (b, 0, KA_BLK + h)),
            pl.BlockSpec((1, S, LANES), lambda b, h: (b, 0, VA_BLK + h)),
            wspec,
        ],
        out_specs=[pl.BlockSpec((1, S, LANES), lambda b, h: (b, 0, h)), wspec],
        out_shape=[jax.ShapeDtypeStruct((B, S, N_HEADS_DIFF * DIFF_V_DIM), BF16),
                   jax.ShapeDtypeStruct(w_cast.shape, BF16)],
        compiler_params=_params(("arbitrary", "arbitrary")),
        name="diffattn",
    )(lam, g, h3, h3, h3, w_cast)


CA_TQ = 256
CA_NKB = 3


def _chunk_bias_table(rel_bias):
    nh = rel_bias.shape[0]
    n_neg = CA_TQ - REL_CLIP + 1
    n_far = CA_NKB * CA_TQ - REL_CLIP
    by_dist = jnp.concatenate([
        jnp.broadcast_to(rel_bias[:, :1], (nh, n_neg)),
        rel_bias[:, 1:2 * REL_CLIP],
        jnp.broadcast_to(rel_bias[:, 2 * REL_CLIP:], (nh, n_far)),
    ], axis=1).astype(F32) * LOG2E
    assert by_dist.shape[1] == (CA_NKB + 1) * CA_TQ
    return by_dist.reshape(nh // 2, 2, (CA_NKB + 1) * CA_TQ)


def _build_bias_table(dist_ref, bias_ref):
    row = lax.broadcasted_iota(jnp.int32, (CA_TQ, CA_TQ), 0) // CHUNK
    col = lax.broadcasted_iota(jnp.int32, (CA_TQ, CA_TQ), 1) // CHUNK
    for t in range(2):
        for j in range(CA_NKB):
            d0 = (CA_NKB - 1 - j) * CA_TQ
            g = jnp.concatenate([dist_ref[0, t:t + 1, d0 + CA_TQ:d0 + 2 * CA_TQ],
                                 dist_ref[0, t:t + 1, d0:d0 + CA_TQ]], axis=1)
            rolled = pltpu.roll(jnp.broadcast_to(g, (CA_TQ, 2 * CA_TQ)), 0, 1, stride=1, stride_axis=0)
            blk = rolled[:, :CA_TQ]
            if j == 0:
                blk = jnp.where(row >= col, blk, NEG_INF)
            elif j == CA_NKB - 1:
                blk = jnp.where(row <= col, blk, NEG_INF)
            bias_ref[j * CA_TQ:(j + 1) * CA_TQ, t * CA_TQ:(t + 1) * CA_TQ] = blk


def _chunkattn_kernel(dist_ref, q_ref, k_ref, v_ref, wf_ref, o_ref, wb_ref, bias_ref, vt_ref, s_ref, m_ref):
    nq = q_ref.shape[1] // CA_TQ
    wb_ref[...] = wf_ref[...].astype(wb_ref.dtype)

    @pl.when(pl.program_id(1) == 0)
    def _():
        _build_bias_table(dist_ref, bias_ref)

    def key_blocks(blk):
        return [(j, blk - (CA_NKB - 1) + j) for j in range(CA_NKB) if blk - (CA_NKB - 1) + j >= 0]

    def scores(blk, slot):
        qq = _stack_halves(q_ref[0, blk * CA_TQ:(blk + 1) * CA_TQ, :])
        m = None
        for j, kb in key_blocks(blk):
            rows = slice(j * CA_TQ, (j + 1) * CA_TQ)
            s = _dot_nt(k_ref[0, kb * CA_TQ:(kb + 1) * CA_TQ, :], qq) + bias_ref[rows, :]
            s_ref[slot, rows, :] = s
            mj = jnp.max(s, axis=0, keepdims=True)
            m = mj if m is None else jnp.maximum(m, mj)
        m_ref[slot] = m

    def output(blk, slot):
        m = m_ref[slot]
        acc = None
        for j, kb in key_blocks(blk):
            p = jnp.exp2(s_ref[slot, j * CA_TQ:(j + 1) * CA_TQ, :] - m).astype(BF16)
            part = _dot(vt_ref[:, kb * CA_TQ:(kb + 1) * CA_TQ], p)
            acc = part if acc is None else acc + part
        l = acc[LANES:LANES + 1, :]
        o_t = jnp.concatenate([acc[:HEAD_DIM, :CA_TQ] / l[:, :CA_TQ],
                               acc[HEAD_DIM:LANES, CA_TQ:] / l[:, CA_TQ:]], axis=0)
        o_ref[0, blk * CA_TQ:(blk + 1) * CA_TQ, :] = o_t.T.astype(o_ref.dtype)

    _fill_vt_ones(vt_ref, v_ref[0])
    scores(0, 0)
    for blk in range(nq):
        if blk + 1 < nq:
            scores(blk + 1, (blk + 1) % 2)
        output(blk, blk % 2)


def _chunkattn_call(bias, h3, w_cast):
    B, S, _ = h3.shape
    npairs = N_HEADS_CHUNK // 2
    wrows, wcols = w_cast.shape
    wspec = pl.BlockSpec((wrows // (npairs * B), wcols), lambda p, b: (p * B + b, 0))
    return pl.pallas_call(
        _chunkattn_kernel,
        grid=(npairs, B),
        in_specs=[
            pl.BlockSpec((1, 2, (CA_NKB + 1) * CA_TQ), lambda p, b: (p, 0, 0)),
            pl.BlockSpec((1, S, LANES), lambda p, b: (b, 0, QB_BLK + p)),
            pl.BlockSpec((1, S, LANES), lambda p, b: (b, 0, KB_BLK + p)),
            pl.BlockSpec((1, S, LANES), lambda p, b: (b, 0, VB_BLK + p)),
            wspec,
        ],
        out_specs=[pl.BlockSpec((1, S, LANES), lambda p, b: (b, 0, p)), wspec],
        out_shape=[jax.ShapeDtypeStruct((B, S, N_HEADS_CHUNK * HEAD_DIM), BF16),
                   jax.ShapeDtypeStruct(w_cast.shape, BF16)],
        scratch_shapes=[pltpu.VMEM((CA_NKB * CA_TQ, 2 * CA_TQ), F32),
                        pltpu.VMEM((LANES + VT_ONES, S), BF16),
                        pltpu.VMEM((2, CA_NKB * CA_TQ, 2 * CA_TQ), F32),
                        pltpu.VMEM((2, 1, 2 * CA_TQ), F32)],
        compiler_params=_params(("arbitrary", "arbitrary")),
        name="chunkattn",
    )(bias, h3, h3, h3, w_cast)


def _outproj_kernel(x_ref, ya_ref, yb_ref, wof_ref, g_ref, b_ref, wqf_ref, x1_ref, q_ref, wo_ref, wq_ref):
    _cast_weights_once(pl.program_id(0) == 0, [(wof_ref, wo_ref), (wqf_ref, wq_ref)])
    half = ya_ref.shape[1]
    tiles = _sub_tiles(x_ref.shape[0])
    ys = [_dot(ya_ref[rows, :], wo_ref[0:half, :]) + _dot(yb_ref[rows, :], wo_ref[half:, :]) for rows in tiles]
    for rows, y in zip(tiles, ys):
        x1 = _layer_norm(DEEPNORM_ALPHA * x_ref[rows, :] + y, g_ref[...], b_ref[...])
        x1_ref[rows, :] = x1
        q = _dot(x1.astype(BF16), wq_ref[...]) * (MEM_HEAD_DIM ** -0.5 * LOG2E)
        q_ref[rows, :] = q.astype(BF16)


def _outproj_call(x2d, ya, yb, w_o, g, b, w_mq, tm):
    T = x2d.shape[0]
    half = ya.shape[1]
    row = lambda i: (i, 0)
    fixed = lambda i: (0, 0)
    return pl.pallas_call(
        _outproj_kernel,
        grid=(T // tm,),
        in_specs=[
            pl.BlockSpec((tm, D_MODEL), row),
            pl.BlockSpec((tm, half), row),
            pl.BlockSpec((tm, half), row),
            _resident((D_MODEL, D_MODEL)),
            pl.BlockSpec((1, D_MODEL), fixed),
            pl.BlockSpec((1, D_MODEL), fixed),
            _resident((D_MODEL, D_MODEL)),
        ],
        out_specs=[pl.BlockSpec((tm, D_MODEL), row), pl.BlockSpec((tm, D_MODEL), row)],
        out_shape=[jax.ShapeDtypeStruct((T, D_MODEL), F32), jax.ShapeDtypeStruct((T, D_MODEL), BF16)],
        scratch_shapes=[pltpu.VMEM((D_MODEL, D_MODEL), BF16)] * 2,
        compiler_params=_params(("arbitrary",)),
        name="outproj",
    )(x2d, ya, yb, w_o, g, b, w_mq)


def _memattn_kernel(x1_ref, q_ref, mem_ref, wkf_ref, wvf_ref, wof_ref, g_ref, b_ref, x2_ref,
                    wk_ref, wv_ref, wo_ref, k_ref, v_ref):
    first_step = jnp.logical_and(pl.program_id(0) == 0, pl.program_id(1) == 0)
    _cast_weights_once(first_step, [(wkf_ref, wk_ref), (wvf_ref, wv_ref), (wof_ref, wo_ref)])
    hd = MEM_HEAD_DIM

    @pl.when(pl.program_id(1) == 0)
    def _():
        mb = mem_ref[0].astype(BF16)
        k_ref[...] = _dot(mb, wk_ref[...]).astype(BF16)
        v_ref[...] = _dot(mb, wv_ref[...]).astype(BF16)

    tiles = _sub_tiles(x1_ref.shape[1])
    attn = []
    for rows in tiles:
        outs = []
        for h in range(N_HEADS_MEM):
            sl = slice(h * hd, (h + 1) * hd)
            s = _dot_nt(q_ref[0, rows, sl], k_ref[:, sl])
            p = jnp.exp2(s - jnp.max(s, axis=1, keepdims=True))
            l = jnp.sum(p, axis=1, keepdims=True)
            outs.append((_dot(p.astype(BF16), v_ref[:, sl]) / l).astype(BF16))
        attn.append(jnp.concatenate(outs, axis=1))
    ys = [_dot(o, wo_ref[...]) for o in attn]
    for rows, y in zip(tiles, ys):
        x2_ref[0, rows, :] = _layer_norm(DEEPNORM_ALPHA * x1_ref[0, rows, :] + y, g_ref[...], b_ref[...])


def _memattn_call(x1, q, mem, w_mk, w_mv, w_mo, g, b, tm):
    B, S, _ = x1.shape
    M = mem.shape[1]
    row = pl.BlockSpec((1, tm, D_MODEL), lambda bb, i: (bb, i, 0))
    memblk = pl.BlockSpec((1, M, D_MODEL), lambda bb, i: (bb, 0, 0))
    vec = pl.BlockSpec((1, D_MODEL), lambda bb, i: (0, 0))
    wspec = _resident((D_MODEL, D_MODEL))
    return pl.pallas_call(
        _memattn_kernel,
        grid=(B, S // tm),
        in_specs=[row, row, memblk, wspec, wspec, wspec, vec, vec],
        out_specs=row,
        out_shape=jax.ShapeDtypeStruct((B, S, D_MODEL), F32),
        scratch_shapes=[pltpu.VMEM((D_MODEL, D_MODEL), BF16)] * 3
        + [pltpu.VMEM((M, D_MODEL), BF16)] * 2,
        compiler_params=_params(("arbitrary", "arbitrary")),
        name="memattn",
    )(x1, q, mem, w_mk, w_mv, w_mo, g, b)


def _mlp_kernel(x_ref, wu_ref, wd_ref, g_ref, b_ref, o_ref, *, ff_chunk):
    def finish(rows, y):
        o_ref[rows, :] = _layer_norm(DEEPNORM_ALPHA * x_ref[rows, :] + y, g_ref[...], b_ref[...])

    pending = None
    for rows in _sub_tiles(x_ref.shape[0]):
        xb = x_ref[rows, :].astype(BF16)
        y = None
        for c in range(D_FF // ff_chunk):
            sl = slice(c * ff_chunk, (c + 1) * ff_chunk)
            h = jnp.maximum(_dot(xb, wu_ref[:, sl]), 0.0)
            part = _dot((h * h).astype(BF16), wd_ref[sl, :])
            y = part if y is None else y + part
            if c == 0 and pending is not None:
                finish(*pending)
                pending = None
        pending = (rows, y)
    finish(*pending)


def _mlp_call(x2d, w_up, w_down, g, b, tm, ff_chunk):
    T = x2d.shape[0]
    row = pl.BlockSpec((tm, D_MODEL), lambda i: (i, 0))
    vec = pl.BlockSpec((1, D_MODEL), lambda i: (0, 0))
    return pl.pallas_call(
        functools.partial(_mlp_kernel, ff_chunk=ff_chunk),
        grid=(T // tm,),
        in_specs=[row,
                  pl.BlockSpec((D_MODEL, D_FF), lambda i: (0, 0), pipeline_mode=pl.Buffered(1)),
                  pl.BlockSpec((D_FF, D_MODEL), lambda i: (0, 0), pipeline_mode=pl.Buffered(1)),
                  vec, vec],
        out_specs=row,
        out_shape=jax.ShapeDtypeStruct((T, D_MODEL), F32),
        compiler_params=_params(("arbitrary",)),
        name="mlp",
    )(x2d, w_up, w_down, g, b)


def kernel(x, mem, positions, w_in, diff_lambda, subln_g, rel_bias, w_o, ln1_g, ln1_b,
           w_mq, w_mk, w_mv, w_mo, ln2_g, ln2_b, w_up, w_down, ln3_g, ln3_b):
    B, S, D = x.shape
    T = B * S
    depth = w_in.shape[0]
    assert depth == DEPTH and D == D_MODEL and S % CA_TQ == 0
    inv_freq = 1.0 / (ROPE_THETA ** (jnp.arange(0, HEAD_DIM, 2, dtype=F32) / HEAD_DIM))
    invf = jnp.tile(inv_freq, LANES // (HEAD_DIM // 2)).reshape(1, LANES)
    pos = positions.reshape(T // IN_TM, ROPE_PACK, IN_TM // ROPE_PACK).transpose(0, 2, 1)
    pos = jnp.repeat(pos, HEAD_DIM // 2, axis=2).reshape(T // ROPE_PACK, LANES)
    vec = lambda a: a.reshape(1, -1)

    for l in range(depth):
        lambda_init = 0.8 - 0.6 * math.exp(-0.3 * l)
        lam = _lam_call(diff_lambda[l], lambda_init, DA_TQ)
        h = _inproj_call(pos, invf, x.reshape(T, D), w_in[l], tm=IN_TM)
        h3 = h.reshape(B, S, IN_WIDTH)
        ya, w_up_b = _diffattn_call(lam, subln_g[l].reshape(-1, 1), h3, w_up[l], tq=DA_TQ,
                                    lambda_init=lambda_init)
        yb, w_down_b = _chunkattn_call(_chunk_bias_table(rel_bias[l]), h3, w_down[l])
        x1, qm = _outproj_call(x.reshape(T, D), ya.reshape(T, -1), yb.reshape(T, -1),
                               w_o[l], vec(ln1_g[l]), vec(ln1_b[l]), w_mq[l], tm=1024)
        x2 = _memattn_call(x1.reshape(B, S, D), qm.reshape(B, S, D), mem, w_mk[l], w_mv[l],
                           w_mo[l], vec(ln2_g[l]), vec(ln2_b[l]), tm=1024)
        out = _mlp_call(x2.reshape(T, D), w_up_b, w_down_b,
                        vec(ln3_g[l]), vec(ln3_b[l]), tm=1024, ff_chunk=1024)
        x = out.reshape(B, S, D)
    return x
```

```python
import functools
import math

import jax
import jax.numpy as jnp
import numpy as np
from jax import lax
from jax.experimental import pallas as pl
from jax.experimental.pallas import tpu as pltpu

D_MODEL = 1024
CHUNK = 64
HEAD_DIM = 64
N_HEADS_DIFF = 4
DIFF_V_DIM = 128
N_HEADS_CHUNK = 8
LEFT_CHUNKS = 8
REL_CLIP = 128
N_HEADS_MEM = 4
MEM_HEAD_DIM = 256
D_FF = 4096
ROPE_THETA = 10000.0
LN_EPS = 1e-5
NEG_INF = -1e30
LOG2E = math.log2(math.e)
DEPTH = 1
DEEPNORM_ALPHA = (2.0 * DEPTH) ** 0.25
IN_WIDTH = 3072

LANES = 128
VMEM_LIMIT = 56 * 1024 * 1024

F32 = jnp.float32
BF16 = jnp.bfloat16

QA_BLK, KA_BLK, VA_BLK, QB_BLK, KB_BLK, VB_BLK = 0, 4, 8, 12, 16, 20


def _params(sem):
    return pltpu.CompilerParams(dimension_semantics=sem, vmem_limit_bytes=VMEM_LIMIT)


def _dot(a, b):
    return jnp.dot(a, b, preferred_element_type=F32)


def _dot_nt(a, b):
    return lax.dot_general(a, b, (((1,), (1,)), ((), ())), preferred_element_type=F32)


SUB_ROWS = 512


def _sub_tiles(n_rows):
    return [slice(r, r + SUB_ROWS) for r in range(0, n_rows, SUB_ROWS)]


def _resident(shape):
    return pl.BlockSpec(shape, lambda *_: (0,) * len(shape), pipeline_mode=pl.Buffered(1))


def _cast_weights_once(first_step, pairs):
    @pl.when(first_step)
    def _():
        for src_ref, dst_ref in pairs:
            dst_ref[...] = src_ref[...].astype(dst_ref.dtype)


def _layer_norm(z, g, b):
    mu = jnp.mean(z, axis=-1, keepdims=True)
    zc = z - mu
    var = jnp.mean(zc * zc, axis=-1, keepdims=True)
    return zc * lax.rsqrt(var + LN_EPS) * g + b


def _lam_kernel(dl_ref, o_ref, *, lambda_init):
    dl = dl_ref[...]
    s1 = jnp.sum(dl[0:1, :] * dl[1:2, :], axis=1, keepdims=True)
    s2 = jnp.sum(dl[2:3, :] * dl[3:4, :], axis=1, keepdims=True)
    lam = jnp.exp(s1) - jnp.exp(s2) + lambda_init
    o_ref[...] = jnp.broadcast_to(lam, o_ref.shape)


def _lam_call(diff_lambda, lambda_init, width):
    return pl.pallas_call(
        functools.partial(_lam_kernel, lambda_init=lambda_init),
        out_shape=jax.ShapeDtypeStruct((1, width), F32),
        name="lam",
    )(diff_lambda)


IN_TM = 512
ROPE_PACK = LANES // (HEAD_DIM // 2)


def _spread_token_groups(t):
    n = t.shape[0]
    grp = lax.broadcasted_iota(jnp.int32, t.shape, 1) // (HEAD_DIM // 2)
    out = []
    for a in range(ROPE_PACK):
        spread = t
        for g in range(ROPE_PACK):
            if g != a:
                moved = pltpu.roll(t, ((g - a) % ROPE_PACK) * (HEAD_DIM // 2), 1)
                spread = jnp.where(grp == g, moved, spread)
        out.append(spread)
    return jnp.concatenate(out, axis=0)


def _inproj_kernel(pos_ref, invf_ref, x_ref, wf_ref, *refs):
    n_cast = (len(refs) - 2) // 2
    cast_in, o_ref, cast_out, w_ref = refs[:n_cast], refs[n_cast], refs[n_cast + 1:-1], refs[-1]
    for src_ref, dst_ref in zip(cast_in, cast_out):
        dst_ref[...] = src_ref[...].astype(dst_ref.dtype)
    _cast_weights_once(pl.program_id(0) == 0, [(wf_ref, w_ref)])
    xb = x_ref[...].astype(BF16)
    ang = pos_ref[...].astype(F32) * invf_ref[...]
    cos = _spread_token_groups(jnp.cos(ang))
    sin = _spread_token_groups(jnp.sin(ang))
    lane = lax.broadcasted_iota(jnp.int32, cos.shape, 1)
    upper = (lane % HEAD_DIM) >= (HEAD_DIM // 2)
    sin_up = jnp.where(upper, sin, 0.0)
    sin_lo = jnp.where(upper, 0.0, -sin)
    n_chunks = IN_WIDTH // 512
    for c in range(n_chunks):
        h = _dot(xb, w_ref[:, c * 512:(c + 1) * 512])
        if c < 2:
            slabs = []
            for s in range(4):
                sl = h[:, s * LANES:(s + 1) * LANES]
                slabs.append(sl * cos
                             + pltpu.roll(sl, HEAD_DIM // 2, 1) * sin_up
                             + pltpu.roll(sl, LANES - HEAD_DIM // 2, 1) * sin_lo)
            h = jnp.concatenate(slabs, axis=1)
        if c == 0 or c == 3:
            h = h * (HEAD_DIM ** -0.5 * LOG2E)
        o_ref[:, c * 512:(c + 1) * 512] = h.astype(BF16)


def _inproj_call(pos, invf, x2d, w_in, w_cast, tm):
    T = x2d.shape[0]
    steps = T // tm
    cast_specs = [pl.BlockSpec((w.shape[0] // steps, w.shape[1]), lambda i: (i, 0)) for w in w_cast]
    outs = pl.pallas_call(
        _inproj_kernel,
        grid=(steps,),
        in_specs=[
            pl.BlockSpec((tm // ROPE_PACK, LANES), lambda i: (i, 0)),
            pl.BlockSpec((1, LANES), lambda i: (0, 0)),
            pl.BlockSpec((tm, D_MODEL), lambda i: (i, 0)),
            _resident((D_MODEL, IN_WIDTH)),
        ] + cast_specs,
        out_specs=[pl.BlockSpec((tm, IN_WIDTH), lambda i: (i, 0))] + cast_specs,
        out_shape=[jax.ShapeDtypeStruct((T, IN_WIDTH), BF16)]
        + [jax.ShapeDtypeStruct(w.shape, BF16) for w in w_cast],
        scratch_shapes=[pltpu.VMEM((D_MODEL, IN_WIDTH), BF16)],
        compiler_params=_params(("arbitrary",)),
        name="inproj",
    )(pos, invf, x2d, w_in, *w_cast)
    return outs[0], outs[1:]


def _stack_halves(q):
    lane = lax.broadcasted_iota(jnp.int32, q.shape, 1)
    zero = jnp.zeros_like(q)
    return jnp.concatenate([jnp.where(lane < HEAD_DIM, q, zero),
                            jnp.where(lane >= HEAD_DIM, q, zero)], axis=0)


DA_TQ = 256
VT_ONES = 16


def _fill_vt_ones(vt_ref, v):
    nv = v.shape[1]
    vt_ref[0:nv, :] = v.astype(F32).T.astype(vt_ref.dtype)
    vt_ref[nv:, :] = jnp.ones((vt_ref.shape[0] - nv, vt_ref.shape[1]), vt_ref.dtype)


def _diff_scores(nkb, q, k_ref, s_ref, m_ref, *, tq):
    qq = _stack_halves(q)
    m = None
    for j in range(nkb):
        s = _dot_nt(k_ref[0, j * tq:(j + 1) * tq, :], qq)
        if j == nkb - 1:
            row = lax.broadcasted_iota(jnp.int32, s.shape, 0)
            col = lax.broadcasted_iota(jnp.int32, s.shape, 1)
            s = jnp.where((row // CHUNK) <= ((col % tq) // CHUNK), s, NEG_INF)
        s_ref[j * tq:(j + 1) * tq, :] = s
        mj = jnp.max(s, axis=0, keepdims=True)
        m = mj if m is None else jnp.maximum(m, mj)
    m_ref[...] = m


def _diff_output(nkb, s_ref, m_ref, vt_ref, lam_ref, g_ref, o_ref, *, tq, lambda_init):
    m = m_ref[...]
    acc = None
    for j in range(nkb):
        rows = slice(j * tq, (j + 1) * tq)
        part = _dot(vt_ref[:, rows], jnp.exp2(s_ref[rows, :] - m).astype(BF16))
        acc = part if acc is None else acc + part
    o = acc[:DIFF_V_DIM, :] / acc[DIFF_V_DIM:DIFF_V_DIM + 1, :]
    o = o[:, :tq] - lam_ref[...] * o[:, tq:]
    y = o * lax.rsqrt(jnp.mean(o * o, axis=0, keepdims=True) + LN_EPS) * g_ref[...]
    o_ref[...] = (y * (1.0 - lambda_init)).T.astype(o_ref.dtype)


def _diffattn_kernel(lam_ref, g_ref, q_ref, k_ref, v_ref, wf_ref, o_ref, wb_ref, vt_ref, s_ref, m_ref,
                     *, tq, nq, lambda_init):
    wb_ref[...] = wf_ref[...].astype(wb_ref.dtype)
    _fill_vt_ones(vt_ref, v_ref[0])
    _diff_scores(1, q_ref[0, 0:tq, :], k_ref, s_ref.at[0], m_ref.at[0], tq=tq)
    for c in range(nq):
        if c + 1 < nq:
            nxt = (c + 1) % 2
            _diff_scores(c + 2, q_ref[0, (c + 1) * tq:(c + 2) * tq, :], k_ref, s_ref.at[nxt], m_ref.at[nxt],
                         tq=tq)
        _diff_output(c + 1, s_ref.at[c % 2], m_ref.at[c % 2], vt_ref, lam_ref, g_ref,
                     o_ref.at[0, c * tq:(c + 1) * tq, :], tq=tq, lambda_init=lambda_init)


def _diffattn_call(lam, g, h3, w_cast, tq, lambda_init):
    B, S, _ = h3.shape
    nq = S // tq
    wrows, wcols = w_cast.shape
    wspec = pl.BlockSpec((wrows // (B * N_HEADS_DIFF), wcols), lambda b, h: (b * N_HEADS_DIFF + h, 0))
    return pl.pallas_call(
        functools.partial(_diffattn_kernel, tq=tq, nq=nq, lambda_init=lambda_init),
        grid=(B, N_HEADS_DIFF),
        scratch_shapes=[pltpu.VMEM((DIFF_V_DIM + VT_ONES, S), BF16),
                        pltpu.VMEM((2, S, 2 * tq), F32),
                        pltpu.VMEM((2, 1, 2 * tq), F32)],
        in_specs=[
            pl.BlockSpec((1, tq), lambda b, h: (0, 0)),
            pl.BlockSpec((DIFF_V_DIM, 1), lambda b, h: (0, 0)),
            pl.BlockSpec((1, S, LANES), lambda b, h: (b, 0, QA_BLK + h)),
            pl.BlockSpec((1, S, LANES), lambda b, h: (b, 0, KA_BLK + h)),
            pl.BlockSpec((1, S, LANES), lambda b, h: (b, 0, VA_BLK + h)),
            wspec,
        ],
        out_specs=[pl.BlockSpec((1, S, LANES), lambda b, h: (b, 0, h)), wspec],
        out_shape=[jax.ShapeDtypeStruct((B, S, N_HEADS_DIFF * DIFF_V_DIM), BF16),
                   jax.ShapeDtypeStruct(w_cast.shape, BF16)],
        compiler_params=_params(("arbitrary", "arbitrary")),
        name="diffattn",
    )(lam, g, h3, h3, h3, w_cast)


CA_TQ = 256
CA_NKB = 3


def _chunk_bias_table(rel_bias):
    nh = rel_bias.shape[0]
    n_neg = CA_TQ - REL_CLIP + 1
    n_far = CA_NKB * CA_TQ - REL_CLIP
    by_dist = jnp.concatenate([
        jnp.broadcast_to(rel_bias[:, :1], (nh, n_neg)),
        rel_bias[:, 1:2 * REL_CLIP],
        jnp.broadcast_to(rel_bias[:, 2 * REL_CLIP:], (nh, n_far)),
    ], axis=1).astype(F32) * LOG2E
    assert by_dist.shape[1] == (CA_NKB + 1) * CA_TQ
    return by_dist.reshape(nh // 2, 2, (CA_NKB + 1) * CA_TQ)


def _build_bias_table(dist_ref, bias_ref):
    row = lax.broadcasted_iota(jnp.int32, (CA_TQ, CA_TQ), 0) // CHUNK
    col = lax.broadcasted_iota(jnp.int32, (CA_TQ, CA_TQ), 1) // CHUNK
    for t in range(2):
        for j in range(CA_NKB):
            d0 = (CA_NKB - 1 - j) * CA_TQ
            g = jnp.concatenate([dist_ref[0, t:t + 1, d0 + CA_TQ:d0 + 2 * CA_TQ],
                                 dist_ref[0, t:t + 1, d0:d0 + CA_TQ]], axis=1)
            rolled = pltpu.roll(jnp.broadcast_to(g, (CA_TQ, 2 * CA_TQ)), 0, 1, stride=1, stride_axis=0)
            blk = rolled[:, :CA_TQ]
            if j == 0:
                blk = jnp.where(row >= col, blk, NEG_INF)
            elif j == CA_NKB - 1:
                blk = jnp.where(row <= col, blk, NEG_INF)
            bias_ref[j * CA_TQ:(j + 1) * CA_TQ, t * CA_TQ:(t + 1) * CA_TQ] = blk


def _chunkattn_kernel(dist_ref, q_ref, k_ref, v_ref, wf_ref, o_ref, wb_ref, bias_ref, vt_ref, s_ref, m_ref):
    nq = q_ref.shape[1] // CA_TQ
    wb_ref[...] = wf_ref[...].astype(wb_ref.dtype)

    @pl.when(pl.program_id(1) == 0)
    def _():
        _build_bias_table(dist_ref, bias_ref)

    def key_blocks(blk):
        return [(j, blk - (CA_NKB - 1) + j) for j in range(CA_NKB) if blk - (CA_NKB - 1) + j >= 0]

    def scores(blk, slot):
        qq = _stack_halves(q_ref[0, blk * CA_TQ:(blk + 1) * CA_TQ, :])
        m = None
        for j, kb in key_blocks(blk):
            rows = slice(j * CA_TQ, (j + 1) * CA_TQ)
            s = _dot_nt(k_ref[0, kb * CA_TQ:(kb + 1) * CA_TQ, :], qq) + bias_ref[rows, :]
            s_ref[slot, rows, :] = s
            mj = jnp.max(s, axis=0, keepdims=True)
            m = mj if m is None else jnp.maximum(m, mj)
        m_ref[slot] = m

    def output(blk, slot):
        m = m_ref[slot]
        acc = None
        for j, kb in key_blocks(blk):
            p = jnp.exp2(s_ref[slot, j * CA_TQ:(j + 1) * CA_TQ, :] - m).astype(BF16)
            part = _dot(vt_ref[:, kb * CA_TQ:(kb + 1) * CA_TQ], p)
            acc = part if acc is None else acc + part
        l = acc[LANES:LANES + 1, :]
        o_t = jnp.concatenate([acc[:HEAD_DIM, :CA_TQ] / l[:, :CA_TQ],
                               acc[HEAD_DIM:LANES, CA_TQ:] / l[:, CA_TQ:]], axis=0)
        o_ref[0, blk * CA_TQ:(blk + 1) * CA_TQ, :] = o_t.T.astype(o_ref.dtype)

    _fill_vt_ones(vt_ref, v_ref[0])
    scores(0, 0)
    for blk in range(nq):
        if blk + 1 < nq:
            scores(blk + 1, (blk + 1) % 2)
        output(blk, blk % 2)


def _chunkattn_call(bias, h3, w_cast):
    B, S, _ = h3.shape
    npairs = N_HEADS_CHUNK // 2
    wrows, wcols = w_cast.shape
    wspec = pl.BlockSpec((wrows // (npairs * B), wcols), lambda p, b: (p * B + b, 0))
    return pl.pallas_call(
        _chunkattn_kernel,
        grid=(npairs, B),
        in_specs=[
            pl.BlockSpec((1, 2, (CA_NKB + 1) * CA_TQ), lambda p, b: (p, 0, 0)),
            pl.BlockSpec((1, S, LANES), lambda p, b: (b, 0, QB_BLK + p)),
            pl.BlockSpec((1, S, LANES), lambda p, b: (b, 0, KB_BLK + p)),
            pl.BlockSpec((1, S, LANES), lambda p, b: (b, 0, VB_BLK + p)),
            wspec,
        ],
        out_specs=[pl.BlockSpec((1, S, LANES), lambda p, b: (b, 0, p)), wspec],
        out_shape=[jax.ShapeDtypeStruct((B, S, N_HEADS_CHUNK * HEAD_DIM), BF16),
                   jax.ShapeDtypeStruct(w_cast.shape, BF16)],
        scratch_shapes=[pltpu.VMEM((CA_NKB * CA_TQ, 2 * CA_TQ), F32),
                        pltpu.VMEM((LANES + VT_ONES, S), BF16),
                        pltpu.VMEM((2, CA_NKB * CA_TQ, 2 * CA_TQ), F32),
                        pltpu.VMEM((2, 1, 2 * CA_TQ), F32)],
        compiler_params=_params(("arbitrary", "arbitrary")),
        name="chunkattn",
    )(bias, h3, h3, h3, w_cast)


def _mid_kernel(x_ref, ya_ref, yb_ref, mem_ref, wo_ref, wq_ref, wk_ref, wv_ref, wmo_ref,
                g1_ref, b1_ref, g2_ref, b2_ref, x2_ref, k_ref, v_ref, x1_ref):
    hd = MEM_HEAD_DIM
    half = ya_ref.shape[2]

    @pl.when(pl.program_id(1) == 0)
    def _():
        mb = mem_ref[0].astype(BF16)
        k_ref[...] = _dot(mb, wk_ref[...]).astype(BF16)
        v_ref[...] = _dot(mb, wv_ref[...]).astype(BF16)

    tiles = _sub_tiles(x_ref.shape[1])
    ys = [_dot(ya_ref[0, rows, :], wo_ref[0:half, :]) + _dot(yb_ref[0, rows, :], wo_ref[half:, :])
          for rows in tiles]
    qs = []
    for rows, y in zip(tiles, ys):
        x1 = _layer_norm(DEEPNORM_ALPHA * x_ref[0, rows, :] + y, g1_ref[...], b1_ref[...])
        x1_ref[rows, :] = x1
        q = _dot(x1.astype(BF16), wq_ref[...]) * (hd ** -0.5 * LOG2E)
        qs.append(q.astype(BF16))
    attn = []
    for q in qs:
        outs = []
        for h in range(N_HEADS_MEM):
            sl = slice(h * hd, (h + 1) * hd)
            s = _dot_nt(q[:, sl], k_ref[:, sl])
            p = jnp.exp2(s - jnp.max(s, axis=1, keepdims=True))
            l = jnp.sum(p, axis=1, keepdims=True)
            outs.append((_dot(p.astype(BF16), v_ref[:, sl]) / l).astype(BF16))
        attn.append(jnp.concatenate(outs, axis=1))
    y2s = [_dot(o, wmo_ref[...]) for o in attn]
    for rows, y2 in zip(tiles, y2s):
        x2_ref[0, rows, :] = _layer_norm(DEEPNORM_ALPHA * x1_ref[rows, :] + y2, g2_ref[...], b2_ref[...])


def _mid_call(x, ya, yb, mem, w_o, w_mq, w_mk, w_mv, w_mo, g1, b1, g2, b2, tm):
    B, S, _ = x.shape
    M = mem.shape[1]
    half = ya.shape[2]
    row = pl.BlockSpec((1, tm, D_MODEL), lambda bb, i: (bb, i, 0))
    hrow = pl.BlockSpec((1, tm, half), lambda bb, i: (bb, i, 0))
    memblk = pl.BlockSpec((1, M, D_MODEL), lambda bb, i: (bb, 0, 0))
    vec = pl.BlockSpec((1, D_MODEL), lambda bb, i: (0, 0))
    wspec = _resident((D_MODEL, D_MODEL))
    return pl.pallas_call(
        _mid_kernel,
        grid=(B, S // tm),
        in_specs=[row, hrow, hrow, memblk] + [wspec] * 5 + [vec] * 4,
        out_specs=row,
        out_shape=jax.ShapeDtypeStruct((B, S, D_MODEL), F32),
        scratch_shapes=[pltpu.VMEM((M, D_MODEL), BF16)] * 2 + [pltpu.VMEM((tm, D_MODEL), F32)],
        compiler_params=_params(("arbitrary", "arbitrary")),
        name="mid",
    )(x, ya, yb, mem, w_o, w_mq, w_mk, w_mv, w_mo, g1, b1, g2, b2)


def _mlp_kernel(x_ref, wu_ref, wd_ref, g_ref, b_ref, o_ref, *, ff_chunk):
    def finish(rows, y):
        o_ref[rows, :] = _layer_norm(DEEPNORM_ALPHA * x_ref[rows, :] + y, g_ref[...], b_ref[...])

    pending = None
    for rows in _sub_tiles(x_ref.shape[0]):
        xb = x_ref[rows, :].astype(BF16)
        y = None
        for c in range(D_FF // ff_chunk):
            sl = slice(c * ff_chunk, (c + 1) * ff_chunk)
            h = jnp.maximum(_dot(xb, wu_ref[:, sl]), 0.0)
            part = _dot((h * h).astype(BF16), wd_ref[sl, :])
            y = part if y is None else y + part
            if c == 0 and pending is not None:
                finish(*pending)
                pending = None
        pending = (rows, y)
    finish(*pending)


def _mlp_call(x2d, w_up, w_down, g, b, tm, ff_chunk):
    T = x2d.shape[0]
    row = pl.BlockSpec((tm, D_MODEL), lambda i: (i, 0))
    vec = pl.BlockSpec((1, D_MODEL), lambda i: (0, 0))
    return pl.pallas_call(
        functools.partial(_mlp_kernel, ff_chunk=ff_chunk),
        grid=(T // tm,),
        in_specs=[row,
                  pl.BlockSpec((D_MODEL, D_FF), lambda i: (0, 0), pipeline_mode=pl.Buffered(1)),
                  pl.BlockSpec((D_FF, D_MODEL), lambda i: (0, 0), pipeline_mode=pl.Buffered(1)),
                  vec, vec],
        out_specs=row,
        out_shape=jax.ShapeDtypeStruct((T, D_MODEL), F32),
        compiler_params=_params(("arbitrary",)),
        name="mlp",
    )(x2d, w_up, w_down, g, b)


def kernel(x, mem, positions, w_in, diff_lambda, subln_g, rel_bias, w_o, ln1_g, ln1_b,
           w_mq, w_mk, w_mv, w_mo, ln2_g, ln2_b, w_up, w_down, ln3_g, ln3_b):
    B, S, D = x.shape
    T = B * S
    depth = w_in.shape[0]
    assert depth == DEPTH and D == D_MODEL and S % CA_TQ == 0
    inv_freq = 1.0 / (ROPE_THETA ** (jnp.arange(0, HEAD_DIM, 2, dtype=F32) / HEAD_DIM))
    invf = jnp.tile(inv_freq, LANES // (HEAD_DIM // 2)).reshape(1, LANES)
    pos = positions.reshape(T // IN_TM, ROPE_PACK, IN_TM // ROPE_PACK).transpose(0, 2, 1)
    pos = jnp.repeat(pos, HEAD_DIM // 2, axis=2).reshape(T // ROPE_PACK, LANES)
    vec = lambda a: a.reshape(1, -1)

    for l in range(depth):
        lambda_init = 0.8 - 0.6 * math.exp(-0.3 * l)
        lam = _lam_call(diff_lambda[l], lambda_init, DA_TQ)
        h, mid_w = _inproj_call(pos, invf, x.reshape(T, D), w_in[l],
                                [w_o[l], w_mq[l], w_mk[l], w_mv[l], w_mo[l]], tm=IN_TM)
        h3 = h.reshape(B, S, IN_WIDTH)
        ya, w_up_b = _diffattn_call(lam, subln_g[l].reshape(-1, 1), h3, w_up[l], tq=DA_TQ,
                                    lambda_init=lambda_init)
        yb, w_down_b = _chunkattn_call(_chunk_bias_table(rel_bias[l]), h3, w_down[l])
        x2 = _mid_call(x, ya, yb, mem, *mid_w, vec(ln1_g[l]), vec(ln1_b[l]), vec(ln2_g[l]), vec(ln2_b[l]),
                       tm=1024)
        out = _mlp_call(x2.reshape(T, D), w_up_b, w_down_b,
                        vec(ln3_g[l]), vec(ln3_b[l]), tm=1024, ff_chunk=1024)
        x = out.reshape(B, S, D)
    return x
```

```python
import functools
import math

import jax
import jax.numpy as jnp
import numpy as np
from jax import lax
from jax.experimental import pallas as pl
from jax.experimental.pallas import tpu as pltpu

D_MODEL = 1024
CHUNK = 64
HEAD_DIM = 64
N_HEADS_DIFF = 4
DIFF_V_DIM = 128
N_HEADS_CHUNK = 8
LEFT_CHUNKS = 8
REL_CLIP = 128
N_HEADS_MEM = 4
MEM_HEAD_DIM = 256
D_FF = 4096
ROPE_THETA = 10000.0
LN_EPS = 1e-5
NEG_INF = -1e30
LOG2E = math.log2(math.e)
DEPTH = 1
DEEPNORM_ALPHA = (2.0 * DEPTH) ** 0.25
IN_WIDTH = 3072

LANES = 128
VMEM_LIMIT = 56 * 1024 * 1024

F32 = jnp.float32
BF16 = jnp.bfloat16

QA_BLK, KA_BLK, VA_BLK, QB_BLK, KB_BLK, VB_BLK = 0, 4, 8, 12, 16, 20


def _params(sem):
    return pltpu.CompilerParams(dimension_semantics=sem, vmem_limit_bytes=VMEM_LIMIT)


def _dot(a, b):
    return jnp.dot(a, b, preferred_element_type=F32)


def _dot_nt(a, b):
    return lax.dot_general(a, b, (((1,), (1,)), ((), ())), preferred_element_type=F32)


SUB_ROWS = 512


def _sub_tiles(n_rows):
    return [slice(r, r + SUB_ROWS) for r in range(0, n_rows, SUB_ROWS)]


def _resident(shape):
    return pl.BlockSpec(shape, lambda *_: (0,) * len(shape), pipeline_mode=pl.Buffered(1))


def _cast_weights_once(first_step, pairs):
    @pl.when(first_step)
    def _():
        for src_ref, dst_ref in pairs:
            dst_ref[...] = src_ref[...].astype(dst_ref.dtype)


def _layer_norm(z, g, b):
    mu = jnp.mean(z, axis=-1, keepdims=True)
    zc = z - mu
    var = jnp.mean(zc * zc, axis=-1, keepdims=True)
    return zc * lax.rsqrt(var + LN_EPS) * g + b


def _lam_kernel(dl_ref, o_ref, *, lambda_init):
    dl = dl_ref[...]
    s1 = jnp.sum(dl[0:1, :] * dl[1:2, :], axis=1, keepdims=True)
    s2 = jnp.sum(dl[2:3, :] * dl[3:4, :], axis=1, keepdims=True)
    lam = jnp.exp(s1) - jnp.exp(s2) + lambda_init
    o_ref[...] = jnp.broadcast_to(lam, o_ref.shape)


def _lam_call(diff_lambda, lambda_init, width):
    return pl.pallas_call(
        functools.partial(_lam_kernel, lambda_init=lambda_init),
        out_shape=jax.ShapeDtypeStruct((1, width), F32),
        name="lam",
    )(diff_lambda)


IN_TM = 512
ROPE_PACK = LANES // (HEAD_DIM // 2)


def _spread_token_groups(t):
    n = t.shape[0]
    grp = lax.broadcasted_iota(jnp.int32, t.shape, 1) // (HEAD_DIM // 2)
    out = []
    for a in range(ROPE_PACK):
        spread = t
        for g in range(ROPE_PACK):
            if g != a:
                moved = pltpu.roll(t, ((g - a) % ROPE_PACK) * (HEAD_DIM // 2), 1)
                spread = jnp.where(grp == g, moved, spread)
        out.append(spread)
    return jnp.concatenate(out, axis=0)


def _inproj_kernel(pos_ref, invf_ref, x_ref, wf_ref, *refs):
    n_cast = (len(refs) - 2) // 2
    cast_in, o_ref, cast_out, w_ref = refs[:n_cast], refs[n_cast], refs[n_cast + 1:-1], refs[-1]
    for src_ref, dst_ref in zip(cast_in, cast_out):
        dst_ref[...] = src_ref[...].astype(dst_ref.dtype)
    _cast_weights_once(pl.program_id(0) == 0, [(wf_ref, w_ref)])
    xb = x_ref[...].astype(BF16)
    ang = pos_ref[...].astype(F32) * invf_ref[...]
    cos = _spread_token_groups(jnp.cos(ang))
    sin = _spread_token_groups(jnp.sin(ang))
    lane = lax.broadcasted_iota(jnp.int32, cos.shape, 1)
    upper = (lane % HEAD_DIM) >= (HEAD_DIM // 2)
    sin_up = jnp.where(upper, sin, 0.0)
    sin_lo = jnp.where(upper, 0.0, -sin)
    n_chunks = IN_WIDTH // 512
    for c in range(n_chunks):
        h = _dot(xb, w_ref[:, c * 512:(c + 1) * 512])
        if c < 2:
            slabs = []
            for s in range(4):
                sl = h[:, s * LANES:(s + 1) * LANES]
                slabs.append(sl * cos
                             + pltpu.roll(sl, HEAD_DIM // 2, 1) * sin_up
                             + pltpu.roll(sl, LANES - HEAD_DIM // 2, 1) * sin_lo)
            h = jnp.concatenate(slabs, axis=1)
        if c == 0 or c == 3:
            h = h * (HEAD_DIM ** -0.5 * LOG2E)
        o_ref[:, c * 512:(c + 1) * 512] = h.astype(BF16)


def _inproj_call(pos, invf, x2d, w_in, w_cast, tm):
    T = x2d.shape[0]
    steps = T // tm
    cast_specs = [pl.BlockSpec((w.shape[0] // steps, w.shape[1]), lambda i: (i, 0)) for w in w_cast]
    outs = pl.pallas_call(
        _inproj_kernel,
        grid=(steps,),
        in_specs=[
            pl.BlockSpec((tm // ROPE_PACK, LANES), lambda i: (i, 0)),
            pl.BlockSpec((1, LANES), lambda i: (0, 0)),
            pl.BlockSpec((tm, D_MODEL), lambda i: (i, 0)),
            _resident((D_MODEL, IN_WIDTH)),
        ] + cast_specs,
        out_specs=[pl.BlockSpec((tm, IN_WIDTH), lambda i: (i, 0))] + cast_specs,
        out_shape=[jax.ShapeDtypeStruct((T, IN_WIDTH), BF16)]
        + [jax.ShapeDtypeStruct(w.shape, BF16) for w in w_cast],
        scratch_shapes=[pltpu.VMEM((D_MODEL, IN_WIDTH), BF16)],
        compiler_params=_params(("arbitrary",)),
        name="inproj",
    )(pos, invf, x2d, w_in, *w_cast)
    return outs[0], outs[1:]


def _stack_halves(q):
    lane = lax.broadcasted_iota(jnp.int32, q.shape, 1)
    zero = jnp.zeros_like(q)
    return jnp.concatenate([jnp.where(lane < HEAD_DIM, q, zero),
                            jnp.where(lane >= HEAD_DIM, q, zero)], axis=0)


DA_TQ = 256
DA_HEADS = 2
VT_ONES = 16


def _fill_vt_ones(vt_ref, v):
    nv = v.shape[1]
    vt_ref[0:nv, :] = v.astype(F32).T.astype(vt_ref.dtype)
    vt_ref[nv:, :] = jnp.ones((vt_ref.shape[0] - nv, vt_ref.shape[1]), vt_ref.dtype)


def _diff_scores(nkb, q, k_ref, mask_ref, s_ref, m_ref, *, tq):
    qq = _stack_halves(q)
    m = None
    for j in range(nkb):
        s = _dot_nt(k_ref[j * tq:(j + 1) * tq, :], qq)
        if j == nkb - 1:
            s = s + mask_ref[...]
        s_ref[j * tq:(j + 1) * tq, :] = s
        mj = jnp.max(s, axis=0, keepdims=True)
        m = mj if m is None else jnp.maximum(m, mj)
    m_ref[...] = m


def _diff_output(nkb, s_ref, m_ref, vt_ref, lam_ref, g_ref, o_ref, *, tq, lambda_init):
    m = m_ref[...]
    acc = None
    for j in range(nkb):
        rows = slice(j * tq, (j + 1) * tq)
        part = _dot(vt_ref[:, rows], jnp.exp2(s_ref[rows, :] - m).astype(BF16))
        acc = part if acc is None else acc + part
    o = acc[:DIFF_V_DIM, :] * (1.0 / acc[DIFF_V_DIM:DIFF_V_DIM + 1, :])
    o = o[:, :tq] - lam_ref[...] * o[:, tq:]
    y = o * lax.rsqrt(jnp.mean(o * o, axis=0, keepdims=True) + LN_EPS) * g_ref[...]
    o_ref[...] = (y * (1.0 - lambda_init)).T.astype(o_ref.dtype)


def _diffattn_kernel(lam_ref, g_ref, mask_ref, q_ref, k_ref, v_ref, wf_ref, o_ref, wb_ref, vt_ref, s_ref,
                     m_ref, *, tq, nq, lambda_init):
    wb_ref[...] = wf_ref[...].astype(wb_ref.dtype)
    for hh in range(DA_HEADS):
        _fill_vt_ones(vt_ref.at[hh], v_ref[0, :, hh * LANES:(hh + 1) * LANES])
    stages = [(hh, c) for hh in range(DA_HEADS) for c in range(nq)]

    def scores(i):
        hh, c = stages[i]
        cols = slice(hh * LANES, (hh + 1) * LANES)
        _diff_scores(c + 1, q_ref[0, c * tq:(c + 1) * tq, cols], k_ref.at[0, :, cols], mask_ref,
                     s_ref.at[i % 2], m_ref.at[i % 2], tq=tq)

    scores(0)
    for i, (hh, c) in enumerate(stages):
        if i + 1 < len(stages):
            scores(i + 1)
        _diff_output(c + 1, s_ref.at[i % 2], m_ref.at[i % 2], vt_ref.at[hh], lam_ref, g_ref,
                     o_ref.at[0, c * tq:(c + 1) * tq, hh * LANES:(hh + 1) * LANES], tq=tq,
                     lambda_init=lambda_init)


def _diffattn_call(lam, g, h3, w_cast, tq, lambda_init):
    B, S, _ = h3.shape
    nq = S // tq
    wrows, wcols = w_cast.shape
    ngroups = N_HEADS_DIFF // DA_HEADS
    width = DA_HEADS * LANES
    wspec = pl.BlockSpec((wrows // (B * ngroups), wcols), lambda b, h: (b * ngroups + h, 0))
    key_chunk = np.arange(tq)[:, None] // CHUNK
    query_chunk = (np.arange(2 * tq)[None, :] % tq) // CHUNK
    diag_mask = jnp.asarray(np.where(key_chunk <= query_chunk, 0.0, NEG_INF), F32)
    return pl.pallas_call(
        functools.partial(_diffattn_kernel, tq=tq, nq=nq, lambda_init=lambda_init),
        grid=(B, ngroups),
        scratch_shapes=[pltpu.VMEM((DA_HEADS, DIFF_V_DIM + VT_ONES, S), BF16),
                        pltpu.VMEM((2, S, 2 * tq), F32),
                        pltpu.VMEM((2, 1, 2 * tq), F32)],
        in_specs=[
            pl.BlockSpec((1, tq), lambda b, h: (0, 0)),
            pl.BlockSpec((DIFF_V_DIM, 1), lambda b, h: (0, 0)),
            pl.BlockSpec((tq, 2 * tq), lambda b, h: (0, 0)),
            pl.BlockSpec((1, S, width), lambda b, h: (b, 0, QA_BLK // DA_HEADS + h)),
            pl.BlockSpec((1, S, width), lambda b, h: (b, 0, KA_BLK // DA_HEADS + h)),
            pl.BlockSpec((1, S, width), lambda b, h: (b, 0, VA_BLK // DA_HEADS + h)),
            wspec,
        ],
        out_specs=[pl.BlockSpec((1, S, width), lambda b, h: (b, 0, h)), wspec],
        out_shape=[jax.ShapeDtypeStruct((B, S, N_HEADS_DIFF * DIFF_V_DIM), BF16),
                   jax.ShapeDtypeStruct(w_cast.shape, BF16)],
        compiler_params=_params(("arbitrary", "arbitrary")),
        name="diffattn",
    )(lam, g, diag_mask, h3, h3, h3, w_cast)


CA_TQ = 256
CA_NKB = 3


def _chunk_bias_table(rel_bias):
    nh = rel_bias.shape[0]
    n_neg = CA_TQ - REL_CLIP + 1
    n_far = CA_NKB * CA_TQ - REL_CLIP
    by_dist = jnp.concatenate([
        jnp.broadcast_to(rel_bias[:, :1], (nh, n_neg)),
        rel_bias[:, 1:2 * REL_CLIP],
        jnp.broadcast_to(rel_bias[:, 2 * REL_CLIP:], (nh, n_far)),
    ], axis=1).astype(F32) * LOG2E
    assert by_dist.shape[1] == (CA_NKB + 1) * CA_TQ
    return by_dist.reshape(nh // 2, 2, (CA_NKB + 1) * CA_TQ)


def _build_bias_table(dist_ref, bias_ref):
    row = lax.broadcasted_iota(jnp.int32, (CA_TQ, CA_TQ), 0) // CHUNK
    col = lax.broadcasted_iota(jnp.int32, (CA_TQ, CA_TQ), 1) // CHUNK
    for t in range(2):
        for j in range(CA_NKB):
            d0 = (CA_NKB - 1 - j) * CA_TQ
            g = jnp.concatenate([dist_ref[0, t:t + 1, d0 + CA_TQ:d0 + 2 * CA_TQ],
                                 dist_ref[0, t:t + 1, d0:d0 + CA_TQ]], axis=1)
            rolled = pltpu.roll(jnp.broadcast_to(g, (CA_TQ, 2 * CA_TQ)), 0, 1, stride=1, stride_axis=0)
            blk = rolled[:, :CA_TQ]
            if j == 0:
                blk = jnp.where(row >= col, blk, NEG_INF)
            elif j == CA_NKB - 1:
                blk = jnp.where(row <= col, blk, NEG_INF)
            bias_ref[j * CA_TQ:(j + 1) * CA_TQ, t * CA_TQ:(t + 1) * CA_TQ] = blk


def _chunkattn_kernel(dist_ref, q_ref, k_ref, v_ref, wf_ref, o_ref, wb_ref, bias_ref, vt_ref, s_ref, m_ref):
    nq = q_ref.shape[1] // CA_TQ
    wb_ref[...] = wf_ref[...].astype(wb_ref.dtype)

    @pl.when(pl.program_id(1) == 0)
    def _():
        _build_bias_table(dist_ref, bias_ref)

    def key_blocks(blk):
        return [(j, blk - (CA_NKB - 1) + j) for j in range(CA_NKB) if blk - (CA_NKB - 1) + j >= 0]

    def scores(blk, slot):
        qq = _stack_halves(q_ref[0, blk * CA_TQ:(blk + 1) * CA_TQ, :])
        m = None
        for j, kb in key_blocks(blk):
            rows = slice(j * CA_TQ, (j + 1) * CA_TQ)
            s = _dot_nt(k_ref[0, kb * CA_TQ:(kb + 1) * CA_TQ, :], qq) + bias_ref[rows, :]
            s_ref[slot, rows, :] = s
            mj = jnp.max(s, axis=0, keepdims=True)
            m = mj if m is None else jnp.maximum(m, mj)
        m_ref[slot] = m

    def output(blk, slot):
        m = m_ref[slot]
        acc = None
        for j, kb in key_blocks(blk):
            p = jnp.exp2(s_ref[slot, j * CA_TQ:(j + 1) * CA_TQ, :] - m).astype(BF16)
            part = _dot(vt_ref[:, kb * CA_TQ:(kb + 1) * CA_TQ], p)
            acc = part if acc is None else acc + part
        inv_l = 1.0 / acc[LANES:LANES + 1, :]
        o_t = jnp.concatenate([acc[:HEAD_DIM, :CA_TQ] * inv_l[:, :CA_TQ],
                               acc[HEAD_DIM:LANES, CA_TQ:] * inv_l[:, CA_TQ:]], axis=0)
        o_ref[0, blk * CA_TQ:(blk + 1) * CA_TQ, :] = o_t.T.astype(o_ref.dtype)

    _fill_vt_ones(vt_ref, v_ref[0])
    scores(0, 0)
    for blk in range(nq):
        if blk + 1 < nq:
            scores(blk + 1, (blk + 1) % 2)
        output(blk, blk % 2)


def _chunkattn_call(bias, h3, w_cast):
    B, S, _ = h3.shape
    npairs = N_HEADS_CHUNK // 2
    wrows, wcols = w_cast.shape
    wspec = pl.BlockSpec((wrows // (npairs * B), wcols), lambda p, b: (p * B + b, 0))
    return pl.pallas_call(
        _chunkattn_kernel,
        grid=(npairs, B),
        in_specs=[
            pl.BlockSpec((1, 2, (CA_NKB + 1) * CA_TQ), lambda p, b: (p, 0, 0)),
            pl.BlockSpec((1, S, LANES), lambda p, b: (b, 0, QB_BLK + p)),
            pl.BlockSpec((1, S, LANES), lambda p, b: (b, 0, KB_BLK + p)),
            pl.BlockSpec((1, S, LANES), lambda p, b: (b, 0, VB_BLK + p)),
            wspec,
        ],
        out_specs=[pl.BlockSpec((1, S, LANES), lambda p, b: (b, 0, p)), wspec],
        out_shape=[jax.ShapeDtypeStruct((B, S, N_HEADS_CHUNK * HEAD_DIM), BF16),
                   jax.ShapeDtypeStruct(w_cast.shape, BF16)],
        scratch_shapes=[pltpu.VMEM((CA_NKB * CA_TQ, 2 * CA_TQ), F32),
                        pltpu.VMEM((LANES + VT_ONES, S), BF16),
                        pltpu.VMEM((2, CA_NKB * CA_TQ, 2 * CA_TQ), F32),
                        pltpu.VMEM((2, 1, 2 * CA_TQ), F32)],
        compiler_params=_params(("arbitrary", "arbitrary")),
        name="chunkattn",
    )(bias, h3, h3, h3, w_cast)


def _mid_kernel(x_ref, ya_ref, yb_ref, mem_ref, wo_ref, wq_ref, wk_ref, wv_ref, wmo_ref,
                g1_ref, b1_ref, g2_ref, b2_ref, x2_ref, k_ref, v_ref, x1_ref):
    hd = MEM_HEAD_DIM
    half = ya_ref.shape[2]

    @pl.when(pl.program_id(1) == 0)
    def _():
        mb = mem_ref[0].astype(BF16)
        k_ref[...] = _dot(mb, wk_ref[...]).astype(BF16)
        v_ref[...] = _dot(mb, wv_ref[...]).astype(BF16)

    tiles = _sub_tiles(x_ref.shape[1])
    ys = [_dot(ya_ref[0, rows, :], wo_ref[0:half, :]) + _dot(yb_ref[0, rows, :], wo_ref[half:, :])
          for rows in tiles]
    qs = []
    for rows, y in zip(tiles, ys):
        x1 = _layer_norm(DEEPNORM_ALPHA * x_ref[0, rows, :] + y, g1_ref[...], b1_ref[...])
        x1_ref[rows, :] = x1
        q = _dot(x1.astype(BF16), wq_ref[...]) * (hd ** -0.5 * LOG2E)
        qs.append(q.astype(BF16))
    attn = []
    for q in qs:
        outs = []
        for h in range(N_HEADS_MEM):
            sl = slice(h * hd, (h + 1) * hd)
            s = _dot_nt(q[:, sl], k_ref[:, sl])
            p = jnp.exp2(s - jnp.max(s, axis=1, keepdims=True))
            l = jnp.sum(p, axis=1, keepdims=True)
            outs.append((_dot(p.astype(BF16), v_ref[:, sl]) / l).astype(BF16))
        attn.append(jnp.concatenate(outs, axis=1))
    y2s = [_dot(o, wmo_ref[...]) for o in attn]
    for rows, y2 in zip(tiles, y2s):
        x2_ref[0, rows, :] = _layer_norm(DEEPNORM_ALPHA * x1_ref[rows, :] + y2, g2_ref[...], b2_ref[...])


def _mid_call(x, ya, yb, mem, w_o, w_mq, w_mk, w_mv, w_mo, g1, b1, g2, b2, tm):
    B, S, _ = x.shape
    M = mem.shape[1]
    half = ya.shape[2]
    row = pl.BlockSpec((1, tm, D_MODEL), lambda bb, i: (bb, i, 0))
    hrow = pl.BlockSpec((1, tm, half), lambda bb, i: (bb, i, 0))
    memblk = pl.BlockSpec((1, M, D_MODEL), lambda bb, i: (bb, 0, 0))
    vec = pl.BlockSpec((1, D_MODEL), lambda bb, i: (0, 0))
    wspec = _resident((D_MODEL, D_MODEL))
    return pl.pallas_call(
        _mid_kernel,
        grid=(B, S // tm),
        in_specs=[row, hrow, hrow, memblk] + [wspec] * 5 + [vec] * 4,
        out_specs=row,
        out_shape=jax.ShapeDtypeStruct((B, S, D_MODEL), F32),
        scratch_shapes=[pltpu.VMEM((M, D_MODEL), BF16)] * 2 + [pltpu.VMEM((tm, D_MODEL), F32)],
        compiler_params=_params(("arbitrary", "arbitrary")),
        name="mid",
    )(x, ya, yb, mem, w_o, w_mq, w_mk, w_mv, w_mo, g1, b1, g2, b2)


def _mlp_kernel(x_ref, wu_ref, wd_ref, g_ref, b_ref, o_ref, *, ff_chunk):
    def finish(rows, y):
        o_ref[rows, :] = _layer_norm(DEEPNORM_ALPHA * x_ref[rows, :] + y, g_ref[...], b_ref[...])

    pending = None
    for rows in _sub_tiles(x_ref.shape[0]):
        xb = x_ref[rows, :].astype(BF16)
        y = None
        for c in range(D_FF // ff_chunk):
            sl = slice(c * ff_chunk, (c + 1) * ff_chunk)
            h = jnp.maximum(_dot(xb, wu_ref[:, sl]), 0.0)
            part = _dot((h * h).astype(BF16), wd_ref[sl, :])
            y = part if y is None else y + part
            if c == 0 and pending is not None:
                finish(*pending)
                pending = None
        pending = (rows, y)
    finish(*pending)


def _mlp_call(x2d, w_up, w_down, g, b, tm, ff_chunk):
    T = x2d.shape[0]
    row = pl.BlockSpec((tm, D_MODEL), lambda i: (i, 0))
    vec = pl.BlockSpec((1, D_MODEL), lambda i: (0, 0))
    return pl.pallas_call(
        functools.partial(_mlp_kernel, ff_chunk=ff_chunk),
        grid=(T // tm,),
        in_specs=[row,
                  pl.BlockSpec((D_MODEL, D_FF), lambda i: (0, 0), pipeline_mode=pl.Buffered(1)),
                  pl.BlockSpec((D_FF, D_MODEL), lambda i: (0, 0), pipeline_mode=pl.Buffered(1)),
                  vec, vec],
        out_specs=row,
        out_shape=jax.ShapeDtypeStruct((T, D_MODEL), F32),
        compiler_params=_params(("arbitrary",)),
        name="mlp",
    )(x2d, w_up, w_down, g, b)


def kernel(x, mem, positions, w_in, diff_lambda, subln_g, rel_bias, w_o, ln1_g, ln1_b,
           w_mq, w_mk, w_mv, w_mo, ln2_g, ln2_b, w_up, w_down, ln3_g, ln3_b):
    B, S, D = x.shape
    T = B * S
    depth = w_in.shape[0]
    assert depth == DEPTH and D == D_MODEL and S % CA_TQ == 0
    inv_freq = 1.0 / (ROPE_THETA ** (jnp.arange(0, HEAD_DIM, 2, dtype=F32) / HEAD_DIM))
    invf = jnp.tile(inv_freq, LANES // (HEAD_DIM // 2)).reshape(1, LANES)
    pos = positions.reshape(T // IN_TM, ROPE_PACK, IN_TM // ROPE_PACK).transpose(0, 2, 1)
    pos = jnp.repeat(pos, HEAD_DIM // 2, axis=2).reshape(T // ROPE_PACK, LANES)
    vec = lambda a: a.reshape(1, -1)

    for l in range(depth):
        lambda_init = 0.8 - 0.6 * math.exp(-0.3 * l)
        lam = _lam_call(diff_lambda[l], lambda_init, DA_TQ)
        h, mid_w = _inproj_call(pos, invf, x.reshape(T, D), w_in[l],
                                [w_o[l], w_mq[l], w_mk[l], w_mv[l], w_mo[l]], tm=IN_TM)
        h3 = h.reshape(B, S, IN_WIDTH)
        ya, w_up_b = _diffattn_call(lam, subln_g[l].reshape(-1, 1), h3, w_up[l], tq=DA_TQ,
                                    lambda_init=lambda_init)
        yb, w_down_b = _chunkattn_call(_chunk_bias_table(rel_bias[l]), h3, w_down[l])
        x2 = _mid_call(x, ya, yb, mem, *mid_w, vec(ln1_g[l]), vec(ln1_b[l]), vec(ln2_g[l]), vec(ln2_b[l]),
                       tm=1024)
        out = _mlp_call(x2.reshape(T, D), w_up_b, w_down_b,
                        vec(ln3_g[l]), vec(ln3_b[l]), tm=1024, ff_chunk=1024)
        x = out.reshape(B, S, D)
    return x
```

```python
import functools
import math

import jax
import jax.numpy as jnp
import numpy as np
from jax import lax
from jax.experimental import pallas as pl
from jax.experimental.pallas import tpu as pltpu

D_MODEL = 1024
CHUNK = 64
HEAD_DIM = 64
N_HEADS_DIFF = 4
DIFF_V_DIM = 128
N_HEADS_CHUNK = 8
LEFT_CHUNKS = 8
REL_CLIP = 128
N_HEADS_MEM = 4
MEM_HEAD_DIM = 256
D_FF = 4096
ROPE_THETA = 10000.0
LN_EPS = 1e-5
NEG_INF = -1e30
LOG2E = math.log2(math.e)
DEPTH = 1
DEEPNORM_ALPHA = (2.0 * DEPTH) ** 0.25
IN_WIDTH = 3072

LANES = 128
VMEM_LIMIT = 56 * 1024 * 1024

F32 = jnp.float32
BF16 = jnp.bfloat16

QA_BLK, KA_BLK, VA_BLK, QB_BLK, KB_BLK, VB_BLK = 0, 4, 8, 12, 16, 20


def _params(sem):
    return pltpu.CompilerParams(dimension_semantics=sem, vmem_limit_bytes=VMEM_LIMIT)


def _dot(a, b):
    return jnp.dot(a, b, preferred_element_type=F32)


def _dot_nt(a, b):
    return lax.dot_general(a, b, (((1,), (1,)), ((), ())), preferred_element_type=F32)


SUB_ROWS = 512


def _sub_tiles(n_rows):
    return [slice(r, r + SUB_ROWS) for r in range(0, n_rows, SUB_ROWS)]


def _resident(shape):
    return pl.BlockSpec(shape, lambda *_: (0,) * len(shape), pipeline_mode=pl.Buffered(1))


def _cast_weights_once(first_step, pairs):
    @pl.when(first_step)
    def _():
        for src_ref, dst_ref in pairs:
            dst_ref[...] = src_ref[...].astype(dst_ref.dtype)


def _layer_norm(z, g, b):
    mu = jnp.mean(z, axis=-1, keepdims=True)
    zc = z - mu
    var = jnp.mean(zc * zc, axis=-1, keepdims=True)
    return zc * lax.rsqrt(var + LN_EPS) * g + b


def _lam_kernel(dl_ref, o_ref, *, lambda_init):
    dl = dl_ref[...]
    s1 = jnp.sum(dl[0:1, :] * dl[1:2, :], axis=1, keepdims=True)
    s2 = jnp.sum(dl[2:3, :] * dl[3:4, :], axis=1, keepdims=True)
    lam = jnp.exp(s1) - jnp.exp(s2) + lambda_init
    o_ref[...] = jnp.broadcast_to(lam, o_ref.shape)


def _lam_call(diff_lambda, lambda_init, width):
    return pl.pallas_call(
        functools.partial(_lam_kernel, lambda_init=lambda_init),
        out_shape=jax.ShapeDtypeStruct((1, width), F32),
        name="lam",
    )(diff_lambda)


IN_TM = 512
ROPE_PACK = LANES // (HEAD_DIM // 2)


def _spread_token_groups(t):
    n = t.shape[0]
    grp = lax.broadcasted_iota(jnp.int32, t.shape, 1) // (HEAD_DIM // 2)
    out = []
    for a in range(ROPE_PACK):
        spread = t
        for g in range(ROPE_PACK):
            if g != a:
                moved = pltpu.roll(t, ((g - a) % ROPE_PACK) * (HEAD_DIM // 2), 1)
                spread = jnp.where(grp == g, moved, spread)
        out.append(spread)
    return jnp.concatenate(out, axis=0)


def _inproj_kernel(pos_ref, invf_ref, x_ref, wf_ref, *refs):
    n_cast = (len(refs) - 2) // 2
    cast_in, o_ref, cast_out, w_ref = refs[:n_cast], refs[n_cast], refs[n_cast + 1:-1], refs[-1]
    for src_ref, dst_ref in zip(cast_in, cast_out):
        dst_ref[...] = src_ref[...].astype(dst_ref.dtype)
    _cast_weights_once(pl.program_id(0) == 0, [(wf_ref, w_ref)])
    xb = x_ref[...].astype(BF16)
    ang = pos_ref[...].astype(F32) * invf_ref[...]
    cos = _spread_token_groups(jnp.cos(ang))
    sin = _spread_token_groups(jnp.sin(ang))
    lane = lax.broadcasted_iota(jnp.int32, cos.shape, 1)
    upper = (lane % HEAD_DIM) >= (HEAD_DIM // 2)
    sin_up = jnp.where(upper, sin, 0.0)
    sin_lo = jnp.where(upper, 0.0, -sin)
    n_chunks = IN_WIDTH // 512
    for c in range(n_chunks):
        h = _dot(xb, w_ref[:, c * 512:(c + 1) * 512])
        if c < 2:
            slabs = []
            for s in range(4):
                sl = h[:, s * LANES:(s + 1) * LANES]
                slabs.append(sl * cos
                             + pltpu.roll(sl, HEAD_DIM // 2, 1) * sin_up
                             + pltpu.roll(sl, LANES - HEAD_DIM // 2, 1) * sin_lo)
            h = jnp.concatenate(slabs, axis=1)
        if c == 0 or c == 3:
            h = h * (HEAD_DIM ** -0.5 * LOG2E)
        o_ref[:, c * 512:(c + 1) * 512] = h.astype(BF16)


def _inproj_call(pos, invf, x2d, w_in, w_cast, tm):
    T = x2d.shape[0]
    steps = T // tm
    cast_specs = [pl.BlockSpec((w.shape[0] // steps, w.shape[1]), lambda i: (i, 0)) for w in w_cast]
    outs = pl.pallas_call(
        _inproj_kernel,
        grid=(steps,),
        in_specs=[
            pl.BlockSpec((tm // ROPE_PACK, LANES), lambda i: (i, 0)),
            pl.BlockSpec((1, LANES), lambda i: (0, 0)),
            pl.BlockSpec((tm, D_MODEL), lambda i: (i, 0)),
            _resident((D_MODEL, IN_WIDTH)),
        ] + cast_specs,
        out_specs=[pl.BlockSpec((tm, IN_WIDTH), lambda i: (i, 0))] + cast_specs,
        out_shape=[jax.ShapeDtypeStruct((T, IN_WIDTH), BF16)]
        + [jax.ShapeDtypeStruct(w.shape, BF16) for w in w_cast],
        scratch_shapes=[pltpu.VMEM((D_MODEL, IN_WIDTH), BF16)],
        compiler_params=_params(("arbitrary",)),
        name="inproj",
    )(pos, invf, x2d, w_in, *w_cast)
    return outs[0], outs[1:]


def _stack_halves(q):
    lane = lax.broadcasted_iota(jnp.int32, q.shape, 1)
    zero = jnp.zeros_like(q)
    return jnp.concatenate([jnp.where(lane < HEAD_DIM, q, zero),
                            jnp.where(lane >= HEAD_DIM, q, zero)], axis=0)


DA_TQ = 256
DA_HEADS = 2
VT_ONES = 16


def _fill_vt_ones(vt_ref, v):
    nv = v.shape[1]
    vt_ref[0:nv, :] = v.astype(F32).T.astype(vt_ref.dtype)
    vt_ref[nv:, :] = jnp.ones((vt_ref.shape[0] - nv, vt_ref.shape[1]), vt_ref.dtype)


def _diff_scores(nkb, q, k_ref, mask_ref, s_ref, m_ref, *, tq):
    qq = _stack_halves(q)
    m = None
    for j in range(nkb):
        s = _dot_nt(k_ref[j * tq:(j + 1) * tq, :], qq)
        if j == nkb - 1:
            s = s + mask_ref[...]
        s_ref[j * tq:(j + 1) * tq, :] = s
        mj = jnp.max(s, axis=0, keepdims=True)
        m = mj if m is None else jnp.maximum(m, mj)
    m_ref[...] = m


def _diff_output(nkb, s_ref, m_ref, vt_ref, lam_ref, g_ref, o_ref, *, tq, lambda_init):
    m = m_ref[...]
    acc = None
    for j in range(nkb):
        rows = slice(j * tq, (j + 1) * tq)
        part = _dot(vt_ref[:, rows], jnp.exp2(s_ref[rows, :] - m).astype(BF16))
        acc = part if acc is None else acc + part
    o = acc[:DIFF_V_DIM, :] * (1.0 / acc[DIFF_V_DIM:DIFF_V_DIM + 1, :])
    o = o[:, :tq] - lam_ref[...] * o[:, tq:]
    y = o * lax.rsqrt(jnp.mean(o * o, axis=0, keepdims=True) + LN_EPS) * g_ref[...]
    o_ref[...] = (y * (1.0 - lambda_init)).T.astype(o_ref.dtype)


def _diffattn_kernel(lam_ref, g_ref, mask_ref, q_ref, k_ref, v_ref, wf_ref, o_ref, wb_ref, vt_ref, s_ref,
                     m_ref, *, tq, nq, lambda_init):
    wb_ref[...] = wf_ref[...].astype(wb_ref.dtype)
    for hh in range(DA_HEADS):
        _fill_vt_ones(vt_ref.at[hh], v_ref[0, :, hh * LANES:(hh + 1) * LANES])
    stages = [(hh, c) for hh in range(DA_HEADS) for c in range(nq)]

    def scores(i):
        hh, c = stages[i]
        cols = slice(hh * LANES, (hh + 1) * LANES)
        _diff_scores(c + 1, q_ref[0, c * tq:(c + 1) * tq, cols], k_ref.at[0, :, cols], mask_ref,
                     s_ref.at[i % 2], m_ref.at[i % 2], tq=tq)

    scores(0)
    for i, (hh, c) in enumerate(stages):
        if i + 1 < len(stages):
            scores(i + 1)
        _diff_output(c + 1, s_ref.at[i % 2], m_ref.at[i % 2], vt_ref.at[hh], lam_ref, g_ref,
                     o_ref.at[0, c * tq:(c + 1) * tq, hh * LANES:(hh + 1) * LANES], tq=tq,
                     lambda_init=lambda_init)


def _diffattn_call(lam, g, h3, w_cast, tq, lambda_init):
    B, S, _ = h3.shape
    nq = S // tq
    wrows, wcols = w_cast.shape
    ngroups = N_HEADS_DIFF // DA_HEADS
    width = DA_HEADS * LANES
    wspec = pl.BlockSpec((wrows // (B * ngroups), wcols), lambda b, h: (b * ngroups + h, 0))
    key_chunk = np.arange(tq)[:, None] // CHUNK
    query_chunk = (np.arange(2 * tq)[None, :] % tq) // CHUNK
    diag_mask = jnp.asarray(np.where(key_chunk <= query_chunk, 0.0, NEG_INF), F32)
    return pl.pallas_call(
        functools.partial(_diffattn_kernel, tq=tq, nq=nq, lambda_init=lambda_init),
        grid=(B, ngroups),
        scratch_shapes=[pltpu.VMEM((DA_HEADS, DIFF_V_DIM + VT_ONES, S), BF16),
                        pltpu.VMEM((2, S, 2 * tq), F32),
                        pltpu.VMEM((2, 1, 2 * tq), F32)],
        in_specs=[
            pl.BlockSpec((1, tq), lambda b, h: (0, 0)),
            pl.BlockSpec((DIFF_V_DIM, 1), lambda b, h: (0, 0)),
            pl.BlockSpec((tq, 2 * tq), lambda b, h: (0, 0)),
            pl.BlockSpec((1, S, width), lambda b, h: (b, 0, QA_BLK // DA_HEADS + h)),
            pl.BlockSpec((1, S, width), lambda b, h: (b, 0, KA_BLK // DA_HEADS + h)),
            pl.BlockSpec((1, S, width), lambda b, h: (b, 0, VA_BLK // DA_HEADS + h)),
            wspec,
        ],
        out_specs=[pl.BlockSpec((1, S, width), lambda b, h: (b, 0, h)), wspec],
        out_shape=[jax.ShapeDtypeStruct((B, S, N_HEADS_DIFF * DIFF_V_DIM), BF16),
                   jax.ShapeDtypeStruct(w_cast.shape, BF16)],
        compiler_params=_params(("arbitrary", "arbitrary")),
        name="diffattn",
    )(lam, g, diag_mask, h3, h3, h3, w_cast)


CA_TQ = 256
CA_NKB = 3
CA_PAIRS = 2


def _chunk_bias_table(rel_bias):
    nh = rel_bias.shape[0]
    n_neg = CA_TQ - REL_CLIP + 1
    n_far = CA_NKB * CA_TQ - REL_CLIP
    by_dist = jnp.concatenate([
        jnp.broadcast_to(rel_bias[:, :1], (nh, n_neg)),
        rel_bias[:, 1:2 * REL_CLIP],
        jnp.broadcast_to(rel_bias[:, 2 * REL_CLIP:], (nh, n_far)),
    ], axis=1).astype(F32) * LOG2E
    assert by_dist.shape[1] == (CA_NKB + 1) * CA_TQ
    return by_dist.reshape(nh // 2, 2, (CA_NKB + 1) * CA_TQ)


def _build_bias_table(dist_ref, bias_ref):
    row = lax.broadcasted_iota(jnp.int32, (CA_TQ, CA_TQ), 0) // CHUNK
    col = lax.broadcasted_iota(jnp.int32, (CA_TQ, CA_TQ), 1) // CHUNK
    for t in range(2):
        for j in range(CA_NKB):
            d0 = (CA_NKB - 1 - j) * CA_TQ
            g = jnp.concatenate([dist_ref[t:t + 1, d0 + CA_TQ:d0 + 2 * CA_TQ],
                                 dist_ref[t:t + 1, d0:d0 + CA_TQ]], axis=1)
            rolled = pltpu.roll(jnp.broadcast_to(g, (CA_TQ, 2 * CA_TQ)), 0, 1, stride=1, stride_axis=0)
            blk = rolled[:, :CA_TQ]
            if j == 0:
                blk = jnp.where(row >= col, blk, NEG_INF)
            elif j == CA_NKB - 1:
                blk = jnp.where(row <= col, blk, NEG_INF)
            bias_ref[j * CA_TQ:(j + 1) * CA_TQ, t * CA_TQ:(t + 1) * CA_TQ] = blk


def _chunkattn_kernel(dist_ref, q_ref, k_ref, v_ref, wf_ref, o_ref, wb_ref, bias_ref, vt_ref, s_ref, m_ref):
    nq = q_ref.shape[1] // CA_TQ
    wb_ref[...] = wf_ref[...].astype(wb_ref.dtype)

    @pl.when(pl.program_id(1) == 0)
    def _():
        for pp in range(CA_PAIRS):
            _build_bias_table(dist_ref.at[pp], bias_ref.at[pp])

    def key_blocks(blk):
        return [(j, blk - (CA_NKB - 1) + j) for j in range(CA_NKB) if blk - (CA_NKB - 1) + j >= 0]

    def scores(pp, blk, slot):
        cols = slice(pp * LANES, (pp + 1) * LANES)
        qq = _stack_halves(q_ref[0, blk * CA_TQ:(blk + 1) * CA_TQ, cols])
        m = None
        for j, kb in key_blocks(blk):
            rows = slice(j * CA_TQ, (j + 1) * CA_TQ)
            s = _dot_nt(k_ref[0, kb * CA_TQ:(kb + 1) * CA_TQ, cols], qq) + bias_ref[pp, rows, :]
            s_ref[slot, rows, :] = s
            mj = jnp.max(s, axis=0, keepdims=True)
            m = mj if m is None else jnp.maximum(m, mj)
        m_ref[slot] = m

    def output(pp, blk, slot):
        m = m_ref[slot]
        acc = None
        for j, kb in key_blocks(blk):
            p = jnp.exp2(s_ref[slot, j * CA_TQ:(j + 1) * CA_TQ, :] - m).astype(BF16)
            part = _dot(vt_ref[pp, :, kb * CA_TQ:(kb + 1) * CA_TQ], p)
            acc = part if acc is None else acc + part
        inv_l = 1.0 / acc[LANES:LANES + 1, :]
        o_t = jnp.concatenate([acc[:HEAD_DIM, :CA_TQ] * inv_l[:, :CA_TQ],
                               acc[HEAD_DIM:LANES, CA_TQ:] * inv_l[:, CA_TQ:]], axis=0)
        o_ref[0, blk * CA_TQ:(blk + 1) * CA_TQ, pp * LANES:(pp + 1) * LANES] = o_t.T.astype(o_ref.dtype)

    for pp in range(CA_PAIRS):
        _fill_vt_ones(vt_ref.at[pp], v_ref[0, :, pp * LANES:(pp + 1) * LANES])
    stages = [(pp, blk) for pp in range(CA_PAIRS) for blk in range(nq)]
    scores(*stages[0], 0)
    for i, stage in enumerate(stages):
        if i + 1 < len(stages):
            scores(*stages[i + 1], (i + 1) % 2)
        output(*stage, i % 2)


def _chunkattn_call(bias, h3, w_cast):
    B, S, _ = h3.shape
    ngroups = N_HEADS_CHUNK // 2 // CA_PAIRS
    width = CA_PAIRS * LANES
    wrows, wcols = w_cast.shape
    wspec = pl.BlockSpec((wrows // (ngroups * B), wcols), lambda p, b: (p * B + b, 0))
    return pl.pallas_call(
        _chunkattn_kernel,
        grid=(ngroups, B),
        in_specs=[
            pl.BlockSpec((CA_PAIRS, 2, (CA_NKB + 1) * CA_TQ), lambda p, b: (p, 0, 0)),
            pl.BlockSpec((1, S, width), lambda p, b: (b, 0, QB_BLK // CA_PAIRS + p)),
            pl.BlockSpec((1, S, width), lambda p, b: (b, 0, KB_BLK // CA_PAIRS + p)),
            pl.BlockSpec((1, S, width), lambda p, b: (b, 0, VB_BLK // CA_PAIRS + p)),
            wspec,
        ],
        out_specs=[pl.BlockSpec((1, S, width), lambda p, b: (b, 0, p)), wspec],
        out_shape=[jax.ShapeDtypeStruct((B, S, N_HEADS_CHUNK * HEAD_DIM), BF16),
                   jax.ShapeDtypeStruct(w_cast.shape, BF16)],
        scratch_shapes=[pltpu.VMEM((CA_PAIRS, CA_NKB * CA_TQ, 2 * CA_TQ), F32),
                        pltpu.VMEM((CA_PAIRS, LANES + VT_ONES, S), BF16),
                        pltpu.VMEM((2, CA_NKB * CA_TQ, 2 * CA_TQ), F32),
                        pltpu.VMEM((2, 1, 2 * CA_TQ), F32)],
        compiler_params=_params(("arbitrary", "arbitrary")),
        name="chunkattn",
    )(bias, h3, h3, h3, w_cast)


def _mid_kernel(x_ref, ya_ref, yb_ref, mem_ref, wo_ref, wq_ref, wk_ref, wv_ref, wmo_ref,
                g1_ref, b1_ref, g2_ref, b2_ref, x2_ref, k_ref, v_ref, x1_ref):
    hd = MEM_HEAD_DIM
    half = ya_ref.shape[2]

    @pl.when(pl.program_id(1) == 0)
    def _():
        mb = mem_ref[0].astype(BF16)
        k_ref[...] = _dot(mb, wk_ref[...]).astype(BF16)
        v_ref[...] = _dot(mb, wv_ref[...]).astype(BF16)

    tiles = _sub_tiles(x_ref.shape[1])
    ys = [_dot(ya_ref[0, rows, :], wo_ref[0:half, :]) + _dot(yb_ref[0, rows, :], wo_ref[half:, :])
          for rows in tiles]
    qs = []
    for rows, y in zip(tiles, ys):
        x1 = _layer_norm(DEEPNORM_ALPHA * x_ref[0, rows, :] + y, g1_ref[...], b1_ref[...])
        x1_ref[rows, :] = x1
        q = _dot(x1.astype(BF16), wq_ref[...]) * (hd ** -0.5 * LOG2E)
        qs.append(q.astype(BF16))
    attn = []
    for q in qs:
        outs = []
        for h in range(N_HEADS_MEM):
            sl = slice(h * hd, (h + 1) * hd)
            s = _dot_nt(q[:, sl], k_ref[:, sl])
            p = jnp.exp2(s - jnp.max(s, axis=1, keepdims=True))
            l = jnp.sum(p, axis=1, keepdims=True)
            outs.append((_dot(p.astype(BF16), v_ref[:, sl]) / l).astype(BF16))
        attn.append(jnp.concatenate(outs, axis=1))
    y2s = [_dot(o, wmo_ref[...]) for o in attn]
    for rows, y2 in zip(tiles, y2s):
        x2_ref[0, rows, :] = _layer_norm(DEEPNORM_ALPHA * x1_ref[rows, :] + y2, g2_ref[...], b2_ref[...])


def _mid_call(x, ya, yb, mem, w_o, w_mq, w_mk, w_mv, w_mo, g1, b1, g2, b2, tm):
    B, S, _ = x.shape
    M = mem.shape[1]
    half = ya.shape[2]
    row = pl.BlockSpec((1, tm, D_MODEL), lambda bb, i: (bb, i, 0))
    hrow = pl.BlockSpec((1, tm, half), lambda bb, i: (bb, i, 0))
    memblk = pl.BlockSpec((1, M, D_MODEL), lambda bb, i: (bb, 0, 0))
    vec = pl.BlockSpec((1, D_MODEL), lambda bb, i: (0, 0))
    wspec = _resident((D_MODEL, D_MODEL))
    return pl.pallas_call(
        _mid_kernel,
        grid=(B, S // tm),
        in_specs=[row, hrow, hrow, memblk] + [wspec] * 5 + [vec] * 4,
        out_specs=row,
        out_shape=jax.ShapeDtypeStruct((B, S, D_MODEL), F32),
        scratch_shapes=[pltpu.VMEM((M, D_MODEL), BF16)] * 2 + [pltpu.VMEM((tm, D_MODEL), F32)],
        compiler_params=_params(("arbitrary", "arbitrary")),
        name="mid",
    )(x, ya, yb, mem, w_o, w_mq, w_mk, w_mv, w_mo, g1, b1, g2, b2)


def _mlp_kernel(x_ref, wu_ref, wd_ref, g_ref, b_ref, o_ref, *, ff_chunk):
    def finish(rows, y):
        o_ref[rows, :] = _layer_norm(DEEPNORM_ALPHA * x_ref[rows, :] + y, g_ref[...], b_ref[...])

    pending = None
    for rows in _sub_tiles(x_ref.shape[0]):
        xb = x_ref[rows, :].astype(BF16)
        y = None
        for c in range(D_FF // ff_chunk):
            sl = slice(c * ff_chunk, (c + 1) * ff_chunk)
            h = jnp.maximum(_dot(xb, wu_ref[:, sl]), 0.0)
            part = _dot((h * h).astype(BF16), wd_ref[sl, :])
            y = part if y is None else y + part
            if c == 0 and pending is not None:
                finish(*pending)
                pending = None
        pending = (rows, y)
    finish(*pending)


def _mlp_call(x2d, w_up, w_down, g, b, tm, ff_chunk):
    T = x2d.shape[0]
    row = pl.BlockSpec((tm, D_MODEL), lambda i: (i, 0))
    vec = pl.BlockSpec((1, D_MODEL), lambda i: (0, 0))
    return pl.pallas_call(
        functools.partial(_mlp_kernel, ff_chunk=ff_chunk),
        grid=(T // tm,),
        in_specs=[row,
                  pl.BlockSpec((D_MODEL, D_FF), lambda i: (0, 0), pipeline_mode=pl.Buffered(1)),
                  pl.BlockSpec((D_FF, D_MODEL), lambda i: (0, 0), pipeline_mode=pl.Buffered(1)),
                  vec, vec],
        out_specs=row,
        out_shape=jax.ShapeDtypeStruct((T, D_MODEL), F32),
        compiler_params=_params(("arbitrary",)),
        name="mlp",
    )(x2d, w_up, w_down, g, b)


def kernel(x, mem, positions, w_in, diff_lambda, subln_g, rel_bias, w_o, ln1_g, ln1_b,
           w_mq, w_mk, w_mv, w_mo, ln2_g, ln2_b, w_up, w_down, ln3_g, ln3_b):
    B, S, D = x.shape
    T = B * S
    depth = w_in.shape[0]
    assert depth == DEPTH and D == D_MODEL and S % CA_TQ == 0
    inv_freq = 1.0 / (ROPE_THETA ** (jnp.arange(0, HEAD_DIM, 2, dtype=F32) / HEAD_DIM))
    invf = jnp.tile(inv_freq, LANES // (HEAD_DIM // 2)).reshape(1, LANES)
    pos = positions.reshape(T // IN_TM, ROPE_PACK, IN_TM // ROPE_PACK).transpose(0, 2, 1)
    pos = jnp.repeat(pos, HEAD_DIM // 2, axis=2).reshape(T // ROPE_PACK, LANES)
    vec = lambda a: a.reshape(1, -1)

    for l in range(depth):
        lambda_init = 0.8 - 0.6 * math.exp(-0.3 * l)
        lam = _lam_call(diff_lambda[l], lambda_init, DA_TQ)
        h, mid_w = _inproj_call(pos, invf, x.reshape(T, D), w_in[l],
                                [w_o[l], w_mq[l], w_mk[l], w_mv[l], w_mo[l]], tm=IN_TM)
        h3 = h.reshape(B, S, IN_WIDTH)
        ya, w_up_b = _diffattn_call(lam, subln_g[l].reshape(-1, 1), h3, w_up[l], tq=DA_TQ,
                                    lambda_init=lambda_init)
        yb, w_down_b = _chunkattn_call(_chunk_bias_table(rel_bias[l]), h3, w_down[l])
        x2 = _mid_call(x, ya, yb, mem, *mid_w, vec(ln1_g[l]), vec(ln1_b[l]), vec(ln2_g[l]), vec(ln2_b[l]),
                       tm=1024)
        out = _mlp_call(x2.reshape(T, D), w_up_b, w_down_b,
                        vec(ln3_g[l]), vec(ln3_b[l]), tm=1024, ff_chunk=1024)
        x = out.reshape(B, S, D)
    return x
```

```python
import functools
import math

import jax
import jax.numpy as jnp
import numpy as np
from jax import lax
from jax.experimental import pallas as pl
from jax.experimental.pallas import tpu as pltpu

D_MODEL = 1024
CHUNK = 64
HEAD_DIM = 64
N_HEADS_DIFF = 4
DIFF_V_DIM = 128
N_HEADS_CHUNK = 8
LEFT_CHUNKS = 8
REL_CLIP = 128
N_HEADS_MEM = 4
MEM_HEAD_DIM = 256
D_FF = 4096
ROPE_THETA = 10000.0
LN_EPS = 1e-5
NEG_INF = -1e30
LOG2E = math.log2(math.e)
DEPTH = 1
DEEPNORM_ALPHA = (2.0 * DEPTH) ** 0.25
IN_WIDTH = 3072

LANES = 128
VMEM_LIMIT = 56 * 1024 * 1024

F32 = jnp.float32
BF16 = jnp.bfloat16

QA_BLK, KA_BLK, VA_BLK, QB_BLK, KB_BLK, VB_BLK = 0, 4, 8, 12, 16, 20


def _params(sem):
    return pltpu.CompilerParams(dimension_semantics=sem, vmem_limit_bytes=VMEM_LIMIT)


def _dot(a, b):
    return jnp.dot(a, b, preferred_element_type=F32)


def _dot_nt(a, b):
    return lax.dot_general(a, b, (((1,), (1,)), ((), ())), preferred_element_type=F32)


SUB_ROWS = 512


def _sub_tiles(n_rows):
    return [slice(r, r + SUB_ROWS) for r in range(0, n_rows, SUB_ROWS)]


def _resident(shape):
    return pl.BlockSpec(shape, lambda *_: (0,) * len(shape), pipeline_mode=pl.Buffered(1))


def _cast_weights_once(first_step, pairs):
    @pl.when(first_step)
    def _():
        for src_ref, dst_ref in pairs:
            dst_ref[...] = src_ref[...].astype(dst_ref.dtype)


def _layer_norm(z, g, b):
    mu = jnp.mean(z, axis=-1, keepdims=True)
    zc = z - mu
    var = jnp.mean(zc * zc, axis=-1, keepdims=True)
    return zc * lax.rsqrt(var + LN_EPS) * g + b


def _lam_kernel(dl_ref, o_ref, *, lambda_init):
    dl = dl_ref[...]
    s1 = jnp.sum(dl[0:1, :] * dl[1:2, :], axis=1, keepdims=True)
    s2 = jnp.sum(dl[2:3, :] * dl[3:4, :], axis=1, keepdims=True)
    lam = jnp.exp(s1) - jnp.exp(s2) + lambda_init
    o_ref[...] = jnp.broadcast_to(lam, o_ref.shape)


def _lam_call(diff_lambda, lambda_init, width):
    return pl.pallas_call(
        functools.partial(_lam_kernel, lambda_init=lambda_init),
        out_shape=jax.ShapeDtypeStruct((1, width), F32),
        name="lam",
    )(diff_lambda)


IN_TM = 512
ROPE_PACK = LANES // (HEAD_DIM // 2)


def _spread_token_groups(t):
    n = t.shape[0]
    grp = lax.broadcasted_iota(jnp.int32, t.shape, 1) // (HEAD_DIM // 2)
    out = []
    for a in range(ROPE_PACK):
        spread = t
        for g in range(ROPE_PACK):
            if g != a:
                moved = pltpu.roll(t, ((g - a) % ROPE_PACK) * (HEAD_DIM // 2), 1)
                spread = jnp.where(grp == g, moved, spread)
        out.append(spread)
    return jnp.concatenate(out, axis=0)


def _inproj_kernel(pos_ref, invf_ref, x_ref, wf_ref, *refs):
    n_cast = (len(refs) - 2) // 2
    cast_in, o_ref, cast_out, w_ref = refs[:n_cast], refs[n_cast], refs[n_cast + 1:-1], refs[-1]
    for src_ref, dst_ref in zip(cast_in, cast_out):
        dst_ref[...] = src_ref[...].astype(dst_ref.dtype)
    _cast_weights_once(pl.program_id(0) == 0, [(wf_ref, w_ref)])
    xb = x_ref[...].astype(BF16)
    ang = pos_ref[...].astype(F32) * invf_ref[...]
    cos = _spread_token_groups(jnp.cos(ang))
    sin = _spread_token_groups(jnp.sin(ang))
    lane = lax.broadcasted_iota(jnp.int32, cos.shape, 1)
    upper = (lane % HEAD_DIM) >= (HEAD_DIM // 2)
    sin_up = jnp.where(upper, sin, 0.0)
    sin_lo = jnp.where(upper, 0.0, -sin)
    n_chunks = IN_WIDTH // 512
    for c in range(n_chunks):
        h = _dot(xb, w_ref[:, c * 512:(c + 1) * 512])
        if c < 2:
            slabs = []
            for s in range(4):
                sl = h[:, s * LANES:(s + 1) * LANES]
                slabs.append(sl * cos
                             + pltpu.roll(sl, HEAD_DIM // 2, 1) * sin_up
                             + pltpu.roll(sl, LANES - HEAD_DIM // 2, 1) * sin_lo)
            h = jnp.concatenate(slabs, axis=1)
        if c == 0 or c == 3:
            h = h * (HEAD_DIM ** -0.5 * LOG2E)
        o_ref[:, c * 512:(c + 1) * 512] = h.astype(BF16)


def _inproj_call(pos, invf, x2d, w_in, w_cast, tm):
    T = x2d.shape[0]
    steps = T // tm
    cast_specs = [pl.BlockSpec((w.shape[0] // steps, w.shape[1]), lambda i: (i, 0)) for w in w_cast]
    outs = pl.pallas_call(
        _inproj_kernel,
        grid=(steps,),
        in_specs=[
            pl.BlockSpec((tm // ROPE_PACK, LANES), lambda i: (i, 0)),
            pl.BlockSpec((1, LANES), lambda i: (0, 0)),
            pl.BlockSpec((tm, D_MODEL), lambda i: (i, 0)),
            _resident((D_MODEL, IN_WIDTH)),
        ] + cast_specs,
        out_specs=[pl.BlockSpec((tm, IN_WIDTH), lambda i: (i, 0))] + cast_specs,
        out_shape=[jax.ShapeDtypeStruct((T, IN_WIDTH), BF16)]
        + [jax.ShapeDtypeStruct(w.shape, BF16) for w in w_cast],
        scratch_shapes=[pltpu.VMEM((D_MODEL, IN_WIDTH), BF16)],
        compiler_params=_params(("arbitrary",)),
        name="inproj",
    )(pos, invf, x2d, w_in, *w_cast)
    return outs[0], outs[1:]


def _interleave(*generators):
    pending = list(generators)
    while pending:
        for g in list(pending):
            try:
                next(g)
            except StopIteration:
                pending.remove(g)


def _stack_halves(q):
    lane = lax.broadcasted_iota(jnp.int32, q.shape, 1)
    zero = jnp.zeros_like(q)
    return jnp.concatenate([jnp.where(lane < HEAD_DIM, q, zero),
                            jnp.where(lane >= HEAD_DIM, q, zero)], axis=0)


DA_TQ = 256
DA_HEADS = 2
VT_ONES = 16


def _fill_vt_ones(vt_ref, v):
    nv = v.shape[1]
    vt_ref[0:nv, :] = v.astype(F32).T.astype(vt_ref.dtype)
    vt_ref[nv:, :] = jnp.ones((vt_ref.shape[0] - nv, vt_ref.shape[1]), vt_ref.dtype)


def _diff_scores(nkb, q, k_ref, mask_ref, s_ref, m_ref, *, tq):
    qq = _stack_halves(q)
    m = None
    for j in range(nkb):
        s = _dot_nt(k_ref[j * tq:(j + 1) * tq, :], qq)
        if j == nkb - 1:
            s = s + mask_ref[...]
        s_ref[j * tq:(j + 1) * tq, :] = s
        mj = jnp.max(s, axis=0, keepdims=True)
        m = mj if m is None else jnp.maximum(m, mj)
        yield
    m_ref[...] = m


def _diff_output(nkb, s_ref, m_ref, vt_ref, lam_ref, g_ref, o_ref, *, tq, lambda_init):
    m = m_ref[...]
    acc = None
    for j in range(nkb):
        rows = slice(j * tq, (j + 1) * tq)
        part = _dot(vt_ref[:, rows], jnp.exp2(s_ref[rows, :] - m).astype(BF16))
        acc = part if acc is None else acc + part
        yield
    o = acc[:DIFF_V_DIM, :] * (1.0 / acc[DIFF_V_DIM:DIFF_V_DIM + 1, :])
    o = o[:, :tq] - lam_ref[...] * o[:, tq:]
    y = o * lax.rsqrt(jnp.mean(o * o, axis=0, keepdims=True) + LN_EPS) * g_ref[...]
    o_ref[...] = (y * (1.0 - lambda_init)).T.astype(o_ref.dtype)


def _diffattn_kernel(lam_ref, g_ref, mask_ref, q_ref, k_ref, v_ref, wf_ref, o_ref, wb_ref, vt_ref, s_ref,
                     m_ref, *, tq, nq, lambda_init):
    wb_ref[...] = wf_ref[...].astype(wb_ref.dtype)
    for hh in range(DA_HEADS):
        _fill_vt_ones(vt_ref.at[hh], v_ref[0, :, hh * LANES:(hh + 1) * LANES])
    stages = [(hh, c) for hh in range(DA_HEADS) for c in range(nq)]

    def scores(i):
        hh, c = stages[i]
        cols = slice(hh * LANES, (hh + 1) * LANES)
        return _diff_scores(c + 1, q_ref[0, c * tq:(c + 1) * tq, cols], k_ref.at[0, :, cols], mask_ref,
                            s_ref.at[i % 2], m_ref.at[i % 2], tq=tq)

    _interleave(scores(0))
    for i, (hh, c) in enumerate(stages):
        if i + 1 < len(stages):
            _interleave(scores(i + 1))
        _interleave(_diff_output(c + 1, s_ref.at[i % 2], m_ref.at[i % 2], vt_ref.at[hh], lam_ref, g_ref,
                                 o_ref.at[0, c * tq:(c + 1) * tq, hh * LANES:(hh + 1) * LANES], tq=tq,
                                 lambda_init=lambda_init))


def _diffattn_call(lam, g, h3, w_cast, tq, lambda_init):
    B, S, _ = h3.shape
    nq = S // tq
    wrows, wcols = w_cast.shape
    ngroups = N_HEADS_DIFF // DA_HEADS
    width = DA_HEADS * LANES
    wspec = pl.BlockSpec((wrows // (B * ngroups), wcols), lambda b, h: (b * ngroups + h, 0))
    key_chunk = np.arange(tq)[:, None] // CHUNK
    query_chunk = (np.arange(2 * tq)[None, :] % tq) // CHUNK
    diag_mask = jnp.asarray(np.where(key_chunk <= query_chunk, 0.0, NEG_INF), F32)
    return pl.pallas_call(
        functools.partial(_diffattn_kernel, tq=tq, nq=nq, lambda_init=lambda_init),
        grid=(B, ngroups),
        scratch_shapes=[pltpu.VMEM((DA_HEADS, DIFF_V_DIM + VT_ONES, S), BF16),
                        pltpu.VMEM((2, S, 2 * tq), F32),
                        pltpu.VMEM((2, 1, 2 * tq), F32)],
        in_specs=[
            pl.BlockSpec((1, tq), lambda b, h: (0, 0)),
            pl.BlockSpec((DIFF_V_DIM, 1), lambda b, h: (0, 0)),
            pl.BlockSpec((tq, 2 * tq), lambda b, h: (0, 0)),
            pl.BlockSpec((1, S, width), lambda b, h: (b, 0, QA_BLK // DA_HEADS + h)),
            pl.BlockSpec((1, S, width), lambda b, h: (b, 0, KA_BLK // DA_HEADS + h)),
            pl.BlockSpec((1, S, width), lambda b, h: (b, 0, VA_BLK // DA_HEADS + h)),
            wspec,
        ],
        out_specs=[pl.BlockSpec((1, S, width), lambda b, h: (b, 0, h)), wspec],
        out_shape=[jax.ShapeDtypeStruct((B, S, N_HEADS_DIFF * DIFF_V_DIM), BF16),
                   jax.ShapeDtypeStruct(w_cast.shape, BF16)],
        compiler_params=_params(("arbitrary", "arbitrary")),
        name="diffattn",
    )(lam, g, diag_mask, h3, h3, h3, w_cast)


CA_TQ = 256
CA_NKB = 3
CA_PAIRS = 2


def _chunk_bias_table(rel_bias):
    nh = rel_bias.shape[0]
    n_neg = CA_TQ - REL_CLIP + 1
    n_far = CA_NKB * CA_TQ - REL_CLIP
    by_dist = jnp.concatenate([
        jnp.broadcast_to(rel_bias[:, :1], (nh, n_neg)),
        rel_bias[:, 1:2 * REL_CLIP],
        jnp.broadcast_to(rel_bias[:, 2 * REL_CLIP:], (nh, n_far)),
    ], axis=1).astype(F32) * LOG2E
    assert by_dist.shape[1] == (CA_NKB + 1) * CA_TQ
    return by_dist.reshape(nh // 2, 2, (CA_NKB + 1) * CA_TQ)


def _build_bias_table(dist_ref, bias_ref):
    row = lax.broadcasted_iota(jnp.int32, (CA_TQ, CA_TQ), 0) // CHUNK
    col = lax.broadcasted_iota(jnp.int32, (CA_TQ, CA_TQ), 1) // CHUNK
    for t in range(2):
        for j in range(CA_NKB):
            d0 = (CA_NKB - 1 - j) * CA_TQ
            g = jnp.concatenate([dist_ref[t:t + 1, d0 + CA_TQ:d0 + 2 * CA_TQ],
                                 dist_ref[t:t + 1, d0:d0 + CA_TQ]], axis=1)
            rolled = pltpu.roll(jnp.broadcast_to(g, (CA_TQ, 2 * CA_TQ)), 0, 1, stride=1, stride_axis=0)
            blk = rolled[:, :CA_TQ]
            if j == 0:
                blk = jnp.where(row >= col, blk, NEG_INF)
            elif j == CA_NKB - 1:
                blk = jnp.where(row <= col, blk, NEG_INF)
            bias_ref[j * CA_TQ:(j + 1) * CA_TQ, t * CA_TQ:(t + 1) * CA_TQ] = blk


def _chunkattn_kernel(dist_ref, q_ref, k_ref, v_ref, wf_ref, o_ref, wb_ref, bias_ref, vt_ref, s_ref, m_ref):
    nq = q_ref.shape[1] // CA_TQ
    wb_ref[...] = wf_ref[...].astype(wb_ref.dtype)

    @pl.when(pl.program_id(1) == 0)
    def _():
        for pp in range(CA_PAIRS):
            _build_bias_table(dist_ref.at[pp], bias_ref.at[pp])

    def key_blocks(blk):
        return [(j, blk - (CA_NKB - 1) + j) for j in range(CA_NKB) if blk - (CA_NKB - 1) + j >= 0]

    def scores(pp, blk, slot):
        cols = slice(pp * LANES, (pp + 1) * LANES)
        qq = _stack_halves(q_ref[0, blk * CA_TQ:(blk + 1) * CA_TQ, cols])
        m = None
        for j, kb in key_blocks(blk):
            rows = slice(j * CA_TQ, (j + 1) * CA_TQ)
            s = _dot_nt(k_ref[0, kb * CA_TQ:(kb + 1) * CA_TQ, cols], qq) + bias_ref[pp, rows, :]
            s_ref[slot, rows, :] = s
            mj = jnp.max(s, axis=0, keepdims=True)
            m = mj if m is None else jnp.maximum(m, mj)
            yield
        m_ref[slot] = m

    def output(pp, blk, slot):
        m = m_ref[slot]
        acc = None
        for j, kb in key_blocks(blk):
            p = jnp.exp2(s_ref[slot, j * CA_TQ:(j + 1) * CA_TQ, :] - m).astype(BF16)
            part = _dot(vt_ref[pp, :, kb * CA_TQ:(kb + 1) * CA_TQ], p)
            acc = part if acc is None else acc + part
            yield
        inv_l = 1.0 / acc[LANES:LANES + 1, :]
        o_t = jnp.concatenate([acc[:HEAD_DIM, :CA_TQ] * inv_l[:, :CA_TQ],
                               acc[HEAD_DIM:LANES, CA_TQ:] * inv_l[:, CA_TQ:]], axis=0)
        o_ref[0, blk * CA_TQ:(blk + 1) * CA_TQ, pp * LANES:(pp + 1) * LANES] = o_t.T.astype(o_ref.dtype)

    for pp in range(CA_PAIRS):
        _fill_vt_ones(vt_ref.at[pp], v_ref[0, :, pp * LANES:(pp + 1) * LANES])
    stages = [(pp, blk) for pp in range(CA_PAIRS) for blk in range(nq)]
    _interleave(scores(*stages[0], 0))
    for i, stage in enumerate(stages):
        nxt = [scores(*stages[i + 1], (i + 1) % 2)] if i + 1 < len(stages) else []
        _interleave(*nxt, output(*stage, i % 2))


def _chunkattn_call(bias, h3, w_cast):
    B, S, _ = h3.shape
    ngroups = N_HEADS_CHUNK // 2 // CA_PAIRS
    width = CA_PAIRS * LANES
    wrows, wcols = w_cast.shape
    wspec = pl.BlockSpec((wrows // (ngroups * B), wcols), lambda p, b: (p * B + b, 0))
    return pl.pallas_call(
        _chunkattn_kernel,
        grid=(ngroups, B),
        in_specs=[
            pl.BlockSpec((CA_PAIRS, 2, (CA_NKB + 1) * CA_TQ), lambda p, b: (p, 0, 0)),
            pl.BlockSpec((1, S, width), lambda p, b: (b, 0, QB_BLK // CA_PAIRS + p)),
            pl.BlockSpec((1, S, width), lambda p, b: (b, 0, KB_BLK // CA_PAIRS + p)),
            pl.BlockSpec((1, S, width), lambda p, b: (b, 0, VB_BLK // CA_PAIRS + p)),
            wspec,
        ],
        out_specs=[pl.BlockSpec((1, S, width), lambda p, b: (b, 0, p)), wspec],
        out_shape=[jax.ShapeDtypeStruct((B, S, N_HEADS_CHUNK * HEAD_DIM), BF16),
                   jax.ShapeDtypeStruct(w_cast.shape, BF16)],
        scratch_shapes=[pltpu.VMEM((CA_PAIRS, CA_NKB * CA_TQ, 2 * CA_TQ), F32),
                        pltpu.VMEM((CA_PAIRS, LANES + VT_ONES, S), BF16),
                        pltpu.VMEM((2, CA_NKB * CA_TQ, 2 * CA_TQ), F32),
                        pltpu.VMEM((2, 1, 2 * CA_TQ), F32)],
        compiler_params=_params(("arbitrary", "arbitrary")),
        name="chunkattn",
    )(bias, h3, h3, h3, w_cast)


def _mid_kernel(x_ref, ya_ref, yb_ref, mem_ref, wo_ref, wq_ref, wk_ref, wv_ref, wmo_ref,
                g1_ref, b1_ref, g2_ref, b2_ref, x2_ref, k_ref, v_ref, x1_ref):
    hd = MEM_HEAD_DIM
    half = ya_ref.shape[2]

    @pl.when(pl.program_id(1) == 0)
    def _():
        mb = mem_ref[0].astype(BF16)
        k_ref[...] = _dot(mb, wk_ref[...]).astype(BF16)
        v_ref[...] = _dot(mb, wv_ref[...]).astype(BF16)

    tiles = _sub_tiles(x_ref.shape[1])
    ys = [_dot(ya_ref[0, rows, :], wo_ref[0:half, :]) + _dot(yb_ref[0, rows, :], wo_ref[half:, :])
          for rows in tiles]
    qs = []
    for rows, y in zip(tiles, ys):
        x1 = _layer_norm(DEEPNORM_ALPHA * x_ref[0, rows, :] + y, g1_ref[...], b1_ref[...])
        x1_ref[rows, :] = x1
        q = _dot(x1.astype(BF16), wq_ref[...]) * (hd ** -0.5 * LOG2E)
        qs.append(q.astype(BF16))
    attn = []
    for q in qs:
        outs = []
        for h in range(N_HEADS_MEM):
            sl = slice(h * hd, (h + 1) * hd)
            s = _dot_nt(q[:, sl], k_ref[:, sl])
            p = jnp.exp2(s - jnp.max(s, axis=1, keepdims=True))
            l = jnp.sum(p, axis=1, keepdims=True)
            outs.append((_dot(p.astype(BF16), v_ref[:, sl]) / l).astype(BF16))
        attn.append(jnp.concatenate(outs, axis=1))
    y2s = [_dot(o, wmo_ref[...]) for o in attn]
    for rows, y2 in zip(tiles, y2s):
        x2_ref[0, rows, :] = _layer_norm(DEEPNORM_ALPHA * x1_ref[rows, :] + y2, g2_ref[...], b2_ref[...])


def _mid_call(x, ya, yb, mem, w_o, w_mq, w_mk, w_mv, w_mo, g1, b1, g2, b2, tm):
    B, S, _ = x.shape
    M = mem.shape[1]
    half = ya.shape[2]
    row = pl.BlockSpec((1, tm, D_MODEL), lambda bb, i: (bb, i, 0))
    hrow = pl.BlockSpec((1, tm, half), lambda bb, i: (bb, i, 0))
    memblk = pl.BlockSpec((1, M, D_MODEL), lambda bb, i: (bb, 0, 0))
    vec = pl.BlockSpec((1, D_MODEL), lambda bb, i: (0, 0))
    wspec = _resident((D_MODEL, D_MODEL))
    return pl.pallas_call(
        _mid_kernel,
        grid=(B, S // tm),
        in_specs=[row, hrow, hrow, memblk] + [wspec] * 5 + [vec] * 4,
        out_specs=row,
        out_shape=jax.ShapeDtypeStruct((B, S, D_MODEL), F32),
        scratch_shapes=[pltpu.VMEM((M, D_MODEL), BF16)] * 2 + [pltpu.VMEM((tm, D_MODEL), F32)],
        compiler_params=_params(("arbitrary", "arbitrary")),
        name="mid",
    )(x, ya, yb, mem, w_o, w_mq, w_mk, w_mv, w_mo, g1, b1, g2, b2)


def _mlp_kernel(x_ref, wu_ref, wd_ref, g_ref, b_ref, o_ref, *, ff_chunk):
    def finish(rows, y):
        o_ref[rows, :] = _layer_norm(DEEPNORM_ALPHA * x_ref[rows, :] + y, g_ref[...], b_ref[...])

    pending = None
    for rows in _sub_tiles(x_ref.shape[0]):
        xb = x_ref[rows, :].astype(BF16)
        y = None
        for c in range(D_FF // ff_chunk):
            sl = slice(c * ff_chunk, (c + 1) * ff_chunk)
            h = jnp.maximum(_dot(xb, wu_ref[:, sl]), 0.0)
            part = _dot((h * h).astype(BF16), wd_ref[sl, :])
            y = part if y is None else y + part
            if c == 0 and pending is not None:
                finish(*pending)
                pending = None
        pending = (rows, y)
    finish(*pending)


def _mlp_call(x2d, w_up, w_down, g, b, tm, ff_chunk):
    T = x2d.shape[0]
    row = pl.BlockSpec((tm, D_MODEL), lambda i: (i, 0))
    vec = pl.BlockSpec((1, D_MODEL), lambda i: (0, 0))
    return pl.pallas_call(
        functools.partial(_mlp_kernel, ff_chunk=ff_chunk),
        grid=(T // tm,),
        in_specs=[row,
                  pl.BlockSpec((D_MODEL, D_FF), lambda i: (0, 0), pipeline_mode=pl.Buffered(1)),
                  pl.BlockSpec((D_FF, D_MODEL), lambda i: (0, 0), pipeline_mode=pl.Buffered(1)),
                  vec, vec],
        out_specs=row,
        out_shape=jax.ShapeDtypeStruct((T, D_MODEL), F32),
        compiler_params=_params(("arbitrary",)),
        name="mlp",
    )(x2d, w_up, w_down, g, b)


def kernel(x, mem, positions, w_in, diff_lambda, subln_g, rel_bias, w_o, ln1_g, ln1_b,
           w_mq, w_mk, w_mv, w_mo, ln2_g, ln2_b, w_up, w_down, ln3_g, ln3_b):
    B, S, D = x.shape
    T = B * S
    depth = w_in.shape[0]
    assert depth == DEPTH and D == D_MODEL and S % CA_TQ == 0
    inv_freq = 1.0 / (ROPE_THETA ** (jnp.arange(0, HEAD_DIM, 2, dtype=F32) / HEAD_DIM))
    invf = jnp.tile(inv_freq, LANES // (HEAD_DIM // 2)).reshape(1, LANES)
    pos = positions.reshape(T // IN_TM, ROPE_PACK, IN_TM // ROPE_PACK).transpose(0, 2, 1)
    pos = jnp.repeat(pos, HEAD_DIM // 2, axis=2).reshape(T // ROPE_PACK, LANES)
    vec = lambda a: a.reshape(1, -1)

    for l in range(depth):
        lambda_init = 0.8 - 0.6 * math.exp(-0.3 * l)
        lam = _lam_call(diff_lambda[l], lambda_init, DA_TQ)
        h, mid_w = _inproj_call(pos, invf, x.reshape(T, D), w_in[l],
                                [w_o[l], w_mq[l], w_mk[l], w_mv[l], w_mo[l]], tm=IN_TM)
        h3 = h.reshape(B, S, IN_WIDTH)
        ya, w_up_b = _diffattn_call(lam, subln_g[l].reshape(-1, 1), h3, w_up[l], tq=DA_TQ,
                                    lambda_init=lambda_init)
        yb, w_down_b = _chunkattn_call(_chunk_bias_table(rel_bias[l]), h3, w_down[l])
        x2 = _mid_call(x, ya, yb, mem, *mid_w, vec(ln1_g[l]), vec(ln1_b[l]), vec(ln2_g[l]), vec(ln2_b[l]),
                       tm=1024)
        out = _mlp_call(x2.reshape(T, D), w_up_b, w_down_b,
                        vec(ln3_g[l]), vec(ln3_b[l]), tm=1024, ff_chunk=1024)
        x = out.reshape(B, S, D)
    return x
```

```python
import functools
import math

import jax
import jax.numpy as jnp
import numpy as np
from jax import lax
from jax.experimental import pallas as pl
from jax.experimental.pallas import tpu as pltpu

D_MODEL = 1024
CHUNK = 64
HEAD_DIM = 64
N_HEADS_DIFF = 4
DIFF_V_DIM = 128
N_HEADS_CHUNK = 8
LEFT_CHUNKS = 8
REL_CLIP = 128
N_HEADS_MEM = 4
MEM_HEAD_DIM = 256
D_FF = 4096
ROPE_THETA = 10000.0
LN_EPS = 1e-5
NEG_INF = -1e30
LOG2E = math.log2(math.e)
DEPTH = 1
DEEPNORM_ALPHA = (2.0 * DEPTH) ** 0.25
IN_WIDTH = 3072

LANES = 128
VMEM_LIMIT = 56 * 1024 * 1024

F32 = jnp.float32
BF16 = jnp.bfloat16

QA_BLK, KA_BLK, VA_BLK, QB_BLK, KB_BLK, VB_BLK = 0, 4, 8, 12, 16, 20


def _params(sem):
    return pltpu.CompilerParams(dimension_semantics=sem, vmem_limit_bytes=VMEM_LIMIT)


def _dot(a, b):
    return jnp.dot(a, b, preferred_element_type=F32)


def _dot_nt(a, b):
    return lax.dot_general(a, b, (((1,), (1,)), ((), ())), preferred_element_type=F32)


SUB_ROWS = 512


def _sub_tiles(n_rows, sub_rows=SUB_ROWS):
    return [slice(r, r + sub_rows) for r in range(0, n_rows, sub_rows)]


def _resident(shape):
    return pl.BlockSpec(shape, lambda *_: (0,) * len(shape), pipeline_mode=pl.Buffered(1))


def _cast_weights_once(first_step, pairs):
    @pl.when(first_step)
    def _():
        for src_ref, dst_ref in pairs:
            dst_ref[...] = src_ref[...].astype(dst_ref.dtype)


def _layer_norm(z, g, b):
    mu = jnp.mean(z, axis=-1, keepdims=True)
    zc = z - mu
    var = jnp.mean(zc * zc, axis=-1, keepdims=True)
    return zc * lax.rsqrt(var + LN_EPS) * g + b


def _lam_kernel(dl_ref, o_ref, *, lambda_init):
    dl = dl_ref[...]
    s1 = jnp.sum(dl[0:1, :] * dl[1:2, :], axis=1, keepdims=True)
    s2 = jnp.sum(dl[2:3, :] * dl[3:4, :], axis=1, keepdims=True)
    lam = jnp.exp(s1) - jnp.exp(s2) + lambda_init
    o_ref[...] = jnp.broadcast_to(lam, o_ref.shape)


def _lam_call(diff_lambda, lambda_init, width):
    return pl.pallas_call(
        functools.partial(_lam_kernel, lambda_init=lambda_init),
        out_shape=jax.ShapeDtypeStruct((1, width), F32),
        name="lam",
    )(diff_lambda)


IN_TM = 512
ROPE_PACK = LANES // (HEAD_DIM // 2)


def _spread_token_groups(t):
    n = t.shape[0]
    grp = lax.broadcasted_iota(jnp.int32, t.shape, 1) // (HEAD_DIM // 2)
    out = []
    for a in range(ROPE_PACK):
        spread = t
        for g in range(ROPE_PACK):
            if g != a:
                moved = pltpu.roll(t, ((g - a) % ROPE_PACK) * (HEAD_DIM // 2), 1)
                spread = jnp.where(grp == g, moved, spread)
        out.append(spread)
    return jnp.concatenate(out, axis=0)


def _inproj_kernel(pos_ref, invf_ref, x_ref, wf_ref, *refs):
    n_cast = (len(refs) - 2) // 2
    cast_in, o_ref, cast_out, w_ref = refs[:n_cast], refs[n_cast], refs[n_cast + 1:-1], refs[-1]
    for src_ref, dst_ref in zip(cast_in, cast_out):
        dst_ref[...] = src_ref[...].astype(dst_ref.dtype)
    _cast_weights_once(pl.program_id(0) == 0, [(wf_ref, w_ref)])
    xb = x_ref[...].astype(BF16)
    ang = pos_ref[...].astype(F32) * invf_ref[...]
    cos = _spread_token_groups(jnp.cos(ang))
    sin = _spread_token_groups(jnp.sin(ang))
    lane = lax.broadcasted_iota(jnp.int32, cos.shape, 1)
    upper = (lane % HEAD_DIM) >= (HEAD_DIM // 2)
    sin_up = jnp.where(upper, sin, 0.0)
    sin_lo = jnp.where(upper, 0.0, -sin)
    n_chunks = IN_WIDTH // 512
    for c in range(n_chunks):
        h = _dot(xb, w_ref[:, c * 512:(c + 1) * 512])
        if c < 2:
            slabs = []
            for s in range(4):
                sl = h[:, s * LANES:(s + 1) * LANES]
                slabs.append(sl * cos
                             + pltpu.roll(sl, HEAD_DIM // 2, 1) * sin_up
                             + pltpu.roll(sl, LANES - HEAD_DIM // 2, 1) * sin_lo)
            h = jnp.concatenate(slabs, axis=1)
        if c == 0 or c == 3:
            h = h * (HEAD_DIM ** -0.5 * LOG2E)
        o_ref[:, c * 512:(c + 1) * 512] = h.astype(BF16)


def _inproj_call(pos, invf, x2d, w_in, w_cast, tm):
    T = x2d.shape[0]
    steps = T // tm
    cast_specs = [pl.BlockSpec((w.shape[0] // steps, w.shape[1]), lambda i: (i, 0)) for w in w_cast]
    outs = pl.pallas_call(
        _inproj_kernel,
        grid=(steps,),
        in_specs=[
            pl.BlockSpec((tm // ROPE_PACK, LANES), lambda i: (i, 0)),
            pl.BlockSpec((1, LANES), lambda i: (0, 0)),
            pl.BlockSpec((tm, D_MODEL), lambda i: (i, 0)),
            _resident((D_MODEL, IN_WIDTH)),
        ] + cast_specs,
        out_specs=[pl.BlockSpec((tm, IN_WIDTH), lambda i: (i, 0))] + cast_specs,
        out_shape=[jax.ShapeDtypeStruct((T, IN_WIDTH), BF16)]
        + [jax.ShapeDtypeStruct(w.shape, BF16) for w in w_cast],
        scratch_shapes=[pltpu.VMEM((D_MODEL, IN_WIDTH), BF16)],
        compiler_params=_params(("arbitrary",)),
        name="inproj",
    )(pos, invf, x2d, w_in, *w_cast)
    return outs[0], outs[1:]


def _interleave(*generators):
    pending = list(generators)
    while pending:
        for g in list(pending):
            try:
                next(g)
            except StopIteration:
                pending.remove(g)


def _stack_halves(q):
    lane = lax.broadcasted_iota(jnp.int32, q.shape, 1)
    zero = jnp.zeros_like(q)
    return jnp.concatenate([jnp.where(lane < HEAD_DIM, q, zero),
                            jnp.where(lane >= HEAD_DIM, q, zero)], axis=0)


DA_TQ = 256
DA_HEADS = 2
VT_ONES = 16


def _fill_vt_ones(vt_ref, v):
    nv = v.shape[1]
    vt_ref[0:nv, :] = v.astype(F32).T.astype(vt_ref.dtype)
    vt_ref[nv:, :] = jnp.ones((vt_ref.shape[0] - nv, vt_ref.shape[1]), vt_ref.dtype)


def _diff_scores(nkb, q, k_ref, mask_ref, s_ref, m_ref, *, tq):
    qq = _stack_halves(q)
    m = None
    for j in range(nkb):
        s = _dot_nt(k_ref[j * tq:(j + 1) * tq, :], qq)
        if j == nkb - 1:
            s = s + mask_ref[...]
        s_ref[j * tq:(j + 1) * tq, :] = s
        mj = jnp.max(s, axis=0, keepdims=True)
        m = mj if m is None else jnp.maximum(m, mj)
        yield
    m_ref[...] = m


def _diff_output(nkb, s_ref, m_ref, vt_ref, lam_ref, g_ref, o_ref, *, tq, lambda_init):
    m = m_ref[...]
    acc = None
    for j in range(nkb):
        rows = slice(j * tq, (j + 1) * tq)
        part = _dot(vt_ref[:, rows], jnp.exp2(s_ref[rows, :] - m).astype(BF16))
        acc = part if acc is None else acc + part
        yield
    o = acc[:DIFF_V_DIM, :] * (1.0 / acc[DIFF_V_DIM:DIFF_V_DIM + 1, :])
    o = o[:, :tq] - lam_ref[...] * o[:, tq:]
    y = o * lax.rsqrt(jnp.mean(o * o, axis=0, keepdims=True) + LN_EPS) * g_ref[...]
    o_ref[...] = (y * (1.0 - lambda_init)).T.astype(o_ref.dtype)


def _diffattn_kernel(lam_ref, g_ref, mask_ref, q_ref, k_ref, v_ref, wf_ref, o_ref, wb_ref, vt_ref, s_ref,
                     m_ref, *, tq, nq, lambda_init):
    wb_ref[...] = wf_ref[...].astype(wb_ref.dtype)
    for hh in range(DA_HEADS):
        _fill_vt_ones(vt_ref.at[hh], v_ref[0, :, hh * LANES:(hh + 1) * LANES])
    stages = [(hh, c) for hh in range(DA_HEADS) for c in range(nq)]

    def scores(i):
        hh, c = stages[i]
        cols = slice(hh * LANES, (hh + 1) * LANES)
        return _diff_scores(c + 1, q_ref[0, c * tq:(c + 1) * tq, cols], k_ref.at[0, :, cols], mask_ref,
                            s_ref.at[i % 2], m_ref.at[i % 2], tq=tq)

    _interleave(scores(0))
    for i, (hh, c) in enumerate(stages):
        if i + 1 < len(stages):
            _interleave(scores(i + 1))
        _interleave(_diff_output(c + 1, s_ref.at[i % 2], m_ref.at[i % 2], vt_ref.at[hh], lam_ref, g_ref,
                                 o_ref.at[0, c * tq:(c + 1) * tq, hh * LANES:(hh + 1) * LANES], tq=tq,
                                 lambda_init=lambda_init))


def _diffattn_call(lam, g, h3, w_cast, tq, lambda_init):
    B, S, _ = h3.shape
    nq = S // tq
    wrows, wcols = w_cast.shape
    ngroups = N_HEADS_DIFF // DA_HEADS
    width = DA_HEADS * LANES
    wspec = pl.BlockSpec((wrows // (B * ngroups), wcols), lambda b, h: (b * ngroups + h, 0))
    key_chunk = np.arange(tq)[:, None] // CHUNK
    query_chunk = (np.arange(2 * tq)[None, :] % tq) // CHUNK
    diag_mask = jnp.asarray(np.where(key_chunk <= query_chunk, 0.0, NEG_INF), F32)
    return pl.pallas_call(
        functools.partial(_diffattn_kernel, tq=tq, nq=nq, lambda_init=lambda_init),
        grid=(B, ngroups),
        scratch_shapes=[pltpu.VMEM((DA_HEADS, DIFF_V_DIM + VT_ONES, S), BF16),
                        pltpu.VMEM((2, S, 2 * tq), F32),
                        pltpu.VMEM((2, 1, 2 * tq), F32)],
        in_specs=[
            pl.BlockSpec((1, tq), lambda b, h: (0, 0)),
            pl.BlockSpec((DIFF_V_DIM, 1), lambda b, h: (0, 0)),
            pl.BlockSpec((tq, 2 * tq), lambda b, h: (0, 0)),
            pl.BlockSpec((1, S, width), lambda b, h: (b, 0, QA_BLK // DA_HEADS + h)),
            pl.BlockSpec((1, S, width), lambda b, h: (b, 0, KA_BLK // DA_HEADS + h)),
            pl.BlockSpec((1, S, width), lambda b, h: (b, 0, VA_BLK // DA_HEADS + h)),
            wspec,
        ],
        out_specs=[pl.BlockSpec((1, S, width), lambda b, h: (b, 0, h)), wspec],
        out_shape=[jax.ShapeDtypeStruct((B, S, N_HEADS_DIFF * DIFF_V_DIM), BF16),
                   jax.ShapeDtypeStruct(w_cast.shape, BF16)],
        compiler_params=_params(("arbitrary", "arbitrary")),
        name="diffattn",
    )(lam, g, diag_mask, h3, h3, h3, w_cast)


CA_TQ = 256
CA_NKB = 3
CA_PAIRS = 2


def _chunk_bias_table(rel_bias):
    nh = rel_bias.shape[0]
    n_neg = CA_TQ - REL_CLIP + 1
    n_far = CA_NKB * CA_TQ - REL_CLIP
    by_dist = jnp.concatenate([
        jnp.broadcast_to(rel_bias[:, :1], (nh, n_neg)),
        rel_bias[:, 1:2 * REL_CLIP],
        jnp.broadcast_to(rel_bias[:, 2 * REL_CLIP:], (nh, n_far)),
    ], axis=1).astype(F32) * LOG2E
    assert by_dist.shape[1] == (CA_NKB + 1) * CA_TQ
    return by_dist.reshape(nh // 2, 2, (CA_NKB + 1) * CA_TQ)


def _build_bias_table(dist_ref, bias_ref):
    row = lax.broadcasted_iota(jnp.int32, (CA_TQ, CA_TQ), 0) // CHUNK
    col = lax.broadcasted_iota(jnp.int32, (CA_TQ, CA_TQ), 1) // CHUNK
    for t in range(2):
        for j in range(CA_NKB):
            d0 = (CA_NKB - 1 - j) * CA_TQ
            g = jnp.concatenate([dist_ref[t:t + 1, d0 + CA_TQ:d0 + 2 * CA_TQ],
                                 dist_ref[t:t + 1, d0:d0 + CA_TQ]], axis=1)
            rolled = pltpu.roll(jnp.broadcast_to(g, (CA_TQ, 2 * CA_TQ)), 0, 1, stride=1, stride_axis=0)
            blk = rolled[:, :CA_TQ]
            if j == 0:
                blk = jnp.where(row >= col, blk, NEG_INF)
            elif j == CA_NKB - 1:
                blk = jnp.where(row <= col, blk, NEG_INF)
            bias_ref[j * CA_TQ:(j + 1) * CA_TQ, t * CA_TQ:(t + 1) * CA_TQ] = blk


def _chunkattn_kernel(dist_ref, q_ref, k_ref, v_ref, wf_ref, o_ref, wb_ref, bias_ref, vt_ref, s_ref, m_ref):
    nq = q_ref.shape[1] // CA_TQ
    wb_ref[...] = wf_ref[...].astype(wb_ref.dtype)

    @pl.when(pl.program_id(1) == 0)
    def _():
        for pp in range(CA_PAIRS):
            _build_bias_table(dist_ref.at[pp], bias_ref.at[pp])

    def key_blocks(blk):
        return [(j, blk - (CA_NKB - 1) + j) for j in range(CA_NKB) if blk - (CA_NKB - 1) + j >= 0]

    def scores(pp, blk, slot):
        cols = slice(pp * LANES, (pp + 1) * LANES)
        qq = _stack_halves(q_ref[0, blk * CA_TQ:(blk + 1) * CA_TQ, cols])
        m = None
        for j, kb in key_blocks(blk):
            rows = slice(j * CA_TQ, (j + 1) * CA_TQ)
            s = _dot_nt(k_ref[0, kb * CA_TQ:(kb + 1) * CA_TQ, cols], qq) + bias_ref[pp, rows, :]
            s_ref[slot, rows, :] = s
            mj = jnp.max(s, axis=0, keepdims=True)
            m = mj if m is None else jnp.maximum(m, mj)
            yield
        m_ref[slot] = m

    def output(pp, blk, slot):
        m = m_ref[slot]
        acc = None
        for j, kb in key_blocks(blk):
            p = jnp.exp2(s_ref[slot, j * CA_TQ:(j + 1) * CA_TQ, :] - m).astype(BF16)
            part = _dot(vt_ref[pp, :, kb * CA_TQ:(kb + 1) * CA_TQ], p)
            acc = part if acc is None else acc + part
            yield
        inv_l = 1.0 / acc[LANES:LANES + 1, :]
        o_t = jnp.concatenate([acc[:HEAD_DIM, :CA_TQ] * inv_l[:, :CA_TQ],
                               acc[HEAD_DIM:LANES, CA_TQ:] * inv_l[:, CA_TQ:]], axis=0)
        o_ref[0, blk * CA_TQ:(blk + 1) * CA_TQ, pp * LANES:(pp + 1) * LANES] = o_t.T.astype(o_ref.dtype)

    for pp in range(CA_PAIRS):
        _fill_vt_ones(vt_ref.at[pp], v_ref[0, :, pp * LANES:(pp + 1) * LANES])
    stages = [(pp, blk) for pp in range(CA_PAIRS) for blk in range(nq)]
    _interleave(scores(*stages[0], 0))
    for i, stage in enumerate(stages):
        nxt = [scores(*stages[i + 1], (i + 1) % 2)] if i + 1 < len(stages) else []
        _interleave(*nxt, output(*stage, i % 2))


def _chunkattn_call(bias, h3, w_cast):
    B, S, _ = h3.shape
    ngroups = N_HEADS_CHUNK // 2 // CA_PAIRS
    width = CA_PAIRS * LANES
    wrows, wcols = w_cast.shape
    wspec = pl.BlockSpec((wrows // (ngroups * B), wcols), lambda p, b: (p * B + b, 0))
    return pl.pallas_call(
        _chunkattn_kernel,
        grid=(ngroups, B),
        in_specs=[
            pl.BlockSpec((CA_PAIRS, 2, (CA_NKB + 1) * CA_TQ), lambda p, b: (p, 0, 0)),
            pl.BlockSpec((1, S, width), lambda p, b: (b, 0, QB_BLK // CA_PAIRS + p)),
            pl.BlockSpec((1, S, width), lambda p, b: (b, 0, KB_BLK // CA_PAIRS + p)),
            pl.BlockSpec((1, S, width), lambda p, b: (b, 0, VB_BLK // CA_PAIRS + p)),
            wspec,
        ],
        out_specs=[pl.BlockSpec((1, S, width), lambda p, b: (b, 0, p)), wspec],
        out_shape=[jax.ShapeDtypeStruct((B, S, N_HEADS_CHUNK * HEAD_DIM), BF16),
                   jax.ShapeDtypeStruct(w_cast.shape, BF16)],
        scratch_shapes=[pltpu.VMEM((CA_PAIRS, CA_NKB * CA_TQ, 2 * CA_TQ), F32),
                        pltpu.VMEM((CA_PAIRS, LANES + VT_ONES, S), BF16),
                        pltpu.VMEM((2, CA_NKB * CA_TQ, 2 * CA_TQ), F32),
                        pltpu.VMEM((2, 1, 2 * CA_TQ), F32)],
        compiler_params=_params(("arbitrary", "arbitrary")),
        name="chunkattn",
    )(bias, h3, h3, h3, w_cast)


def _mid_kernel(x_ref, ya_ref, yb_ref, mem_ref, wo_ref, wq_ref, wk_ref, wv_ref, wmo_ref,
                g1_ref, b1_ref, g2_ref, b2_ref, x2_ref, k_ref, v_ref, x1_ref):
    hd = MEM_HEAD_DIM
    half = ya_ref.shape[2]

    @pl.when(pl.program_id(1) == 0)
    def _():
        mb = mem_ref[0].astype(BF16)
        k_ref[...] = _dot(mb, wk_ref[...]).astype(BF16)
        v_ref[...] = _dot(mb, wv_ref[...]).astype(BF16)

    tiles = _sub_tiles(x_ref.shape[1], 256)
    ys = [_dot(ya_ref[0, rows, :], wo_ref[0:half, :]) + _dot(yb_ref[0, rows, :], wo_ref[half:, :])
          for rows in tiles]
    qs = []
    for rows, y in zip(tiles, ys):
        x1 = _layer_norm(DEEPNORM_ALPHA * x_ref[0, rows, :] + y, g1_ref[...], b1_ref[...])
        x1_ref[rows, :] = x1
        q = _dot(x1.astype(BF16), wq_ref[...]) * (hd ** -0.5 * LOG2E)
        qs.append(q.astype(BF16))
    attn = []
    for q in qs:
        outs = []
        for h in range(N_HEADS_MEM):
            sl = slice(h * hd, (h + 1) * hd)
            s = _dot_nt(q[:, sl], k_ref[:, sl])
            p = jnp.exp2(s - jnp.max(s, axis=1, keepdims=True))
            l = jnp.sum(p, axis=1, keepdims=True)
            outs.append((_dot(p.astype(BF16), v_ref[:, sl]) / l).astype(BF16))
        attn.append(jnp.concatenate(outs, axis=1))
    y2s = [_dot(o, wmo_ref[...]) for o in attn]
    for rows, y2 in zip(tiles, y2s):
        x2_ref[0, rows, :] = _layer_norm(DEEPNORM_ALPHA * x1_ref[rows, :] + y2, g2_ref[...], b2_ref[...])


def _mid_call(x, ya, yb, mem, w_o, w_mq, w_mk, w_mv, w_mo, g1, b1, g2, b2, tm):
    B, S, _ = x.shape
    M = mem.shape[1]
    half = ya.shape[2]
    row = pl.BlockSpec((1, tm, D_MODEL), lambda bb, i: (bb, i, 0))
    hrow = pl.BlockSpec((1, tm, half), lambda bb, i: (bb, i, 0))
    memblk = pl.BlockSpec((1, M, D_MODEL), lambda bb, i: (bb, 0, 0))
    vec = pl.BlockSpec((1, D_MODEL), lambda bb, i: (0, 0))
    wspec = _resident((D_MODEL, D_MODEL))
    return pl.pallas_call(
        _mid_kernel,
        grid=(B, S // tm),
        in_specs=[row, hrow, hrow, memblk] + [wspec] * 5 + [vec] * 4,
        out_specs=row,
        out_shape=jax.ShapeDtypeStruct((B, S, D_MODEL), F32),
        scratch_shapes=[pltpu.VMEM((M, D_MODEL), BF16)] * 2 + [pltpu.VMEM((tm, D_MODEL), F32)],
        compiler_params=_params(("arbitrary", "arbitrary")),
        name="mid",
    )(x, ya, yb, mem, w_o, w_mq, w_mk, w_mv, w_mo, g1, b1, g2, b2)


def _mlp_kernel(x_ref, wu_ref, wd_ref, g_ref, b_ref, o_ref, *, ff_chunk):
    def finish(rows, y):
        o_ref[rows, :] = _layer_norm(DEEPNORM_ALPHA * x_ref[rows, :] + y, g_ref[...], b_ref[...])

    pending = None
    for rows in _sub_tiles(x_ref.shape[0], 256):
        xb = x_ref[rows, :].astype(BF16)
        y = None
        for c in range(D_FF // ff_chunk):
            sl = slice(c * ff_chunk, (c + 1) * ff_chunk)
            h = jnp.maximum(_dot(xb, wu_ref[:, sl]), 0.0)
            part = _dot((h * h).astype(BF16), wd_ref[sl, :])
            y = part if y is None else y + part
            if c == 0 and pending is not None:
                finish(*pending)
                pending = None
        pending = (rows, y)
    finish(*pending)


def _mlp_call(x2d, w_up, w_down, g, b, tm, ff_chunk):
    T = x2d.shape[0]
    row = pl.BlockSpec((tm, D_MODEL), lambda i: (i, 0))
    vec = pl.BlockSpec((1, D_MODEL), lambda i: (0, 0))
    return pl.pallas_call(
        functools.partial(_mlp_kernel, ff_chunk=ff_chunk),
        grid=(T // tm,),
        in_specs=[row,
                  pl.BlockSpec((D_MODEL, D_FF), lambda i: (0, 0), pipeline_mode=pl.Buffered(1)),
                  pl.BlockSpec((D_FF, D_MODEL), lambda i: (0, 0), pipeline_mode=pl.Buffered(1)),
                  vec, vec],
        out_specs=row,
        out_shape=jax.ShapeDtypeStruct((T, D_MODEL), F32),
        compiler_params=_params(("arbitrary",)),
        name="mlp",
    )(x2d, w_up, w_down, g, b)


def kernel(x, mem, positions, w_in, diff_lambda, subln_g, rel_bias, w_o, ln1_g, ln1_b,
           w_mq, w_mk, w_mv, w_mo, ln2_g, ln2_b, w_up, w_down, ln3_g, ln3_b):
    B, S, D = x.shape
    T = B * S
    depth = w_in.shape[0]
    assert depth == DEPTH and D == D_MODEL and S % CA_TQ == 0
    inv_freq = 1.0 / (ROPE_THETA ** (jnp.arange(0, HEAD_DIM, 2, dtype=F32) / HEAD_DIM))
    invf = jnp.tile(inv_freq, LANES // (HEAD_DIM // 2)).reshape(1, LANES)
    pos = positions.reshape(T // IN_TM, ROPE_PACK, IN_TM // ROPE_PACK).transpose(0, 2, 1)
    pos = jnp.repeat(pos, HEAD_DIM // 2, axis=2).reshape(T // ROPE_PACK, LANES)
    vec = lambda a: a.reshape(1, -1)

    for l in range(depth):
        lambda_init = 0.8 - 0.6 * math.exp(-0.3 * l)
        lam = _lam_call(diff_lambda[l], lambda_init, DA_TQ)
        h, mid_w = _inproj_call(pos, invf, x.reshape(T, D), w_in[l],
                                [w_o[l], w_mq[l], w_mk[l], w_mv[l], w_mo[l]], tm=IN_TM)
        h3 = h.reshape(B, S, IN_WIDTH)
        ya, w_up_b = _diffattn_call(lam, subln_g[l].reshape(-1, 1), h3, w_up[l], tq=DA_TQ,
                                    lambda_init=lambda_init)
        yb, w_down_b = _chunkattn_call(_chunk_bias_table(rel_bias[l]), h3, w_down[l])
        x2 = _mid_call(x, ya, yb, mem, *mid_w, vec(ln1_g[l]), vec(ln1_b[l]), vec(ln2_g[l]), vec(ln2_b[l]),
                       tm=1024)
        out = _mlp_call(x2.reshape(T, D), w_up_b, w_down_b,
                        vec(ln3_g[l]), vec(ln3_b[l]), tm=1024, ff_chunk=1024)
        x = out.reshape(B, S, D)
    return x
```

```python
import functools
import math

import jax
import jax.numpy as jnp
import numpy as np
from jax import lax
from jax.experimental import pallas as pl
from jax.experimental.pallas import tpu as pltpu

D_MODEL = 1024
CHUNK = 64
HEAD_DIM = 64
N_HEADS_DIFF = 4
DIFF_V_DIM = 128
N_HEADS_CHUNK = 8
LEFT_CHUNKS = 8
REL_CLIP = 128
N_HEADS_MEM = 4
MEM_HEAD_DIM = 256
D_FF = 4096
ROPE_THETA = 10000.0
LN_EPS = 1e-5
NEG_INF = -1e30
LOG2E = math.log2(math.e)
DEPTH = 1
DEEPNORM_ALPHA = (2.0 * DEPTH) ** 0.25
IN_WIDTH = 3072

LANES = 128
VMEM_LIMIT = 56 * 1024 * 1024

F32 = jnp.float32
BF16 = jnp.bfloat16

QA_BLK, KA_BLK, VA_BLK, QB_BLK, KB_BLK, VB_BLK = 0, 4, 8, 12, 16, 20


def _params(sem):
    return pltpu.CompilerParams(dimension_semantics=sem, vmem_limit_bytes=VMEM_LIMIT)


def _dot(a, b):
    return jnp.dot(a, b, preferred_element_type=F32)


def _dot_nt(a, b):
    return lax.dot_general(a, b, (((1,), (1,)), ((), ())), preferred_element_type=F32)


SUB_ROWS = 512


def _sub_tiles(n_rows, sub_rows=SUB_ROWS):
    return [slice(r, r + sub_rows) for r in range(0, n_rows, sub_rows)]


def _resident(shape):
    return pl.BlockSpec(shape, lambda *_: (0,) * len(shape), pipeline_mode=pl.Buffered(1))


def _cast_weights_once(first_step, pairs):
    @pl.when(first_step)
    def _():
        for src_ref, dst_ref in pairs:
            dst_ref[...] = src_ref[...].astype(dst_ref.dtype)


def _layer_norm(z, g, b):
    mu = jnp.mean(z, axis=-1, keepdims=True)
    zc = z - mu
    var = jnp.mean(zc * zc, axis=-1, keepdims=True)
    return zc * lax.rsqrt(var + LN_EPS) * g + b


def _lam_kernel(dl_ref, o_ref, *, lambda_init):
    dl = dl_ref[...]
    s1 = jnp.sum(dl[0:1, :] * dl[1:2, :], axis=1, keepdims=True)
    s2 = jnp.sum(dl[2:3, :] * dl[3:4, :], axis=1, keepdims=True)
    lam = jnp.exp(s1) - jnp.exp(s2) + lambda_init
    o_ref[...] = jnp.broadcast_to(lam, o_ref.shape)


def _lam_call(diff_lambda, lambda_init, width):
    return pl.pallas_call(
        functools.partial(_lam_kernel, lambda_init=lambda_init),
        out_shape=jax.ShapeDtypeStruct((1, width), F32),
        name="lam",
    )(diff_lambda)


IN_TM = 1024
ROPE_PACK = LANES // (HEAD_DIM // 2)


def _spread_token_groups(t):
    n = t.shape[0]
    grp = lax.broadcasted_iota(jnp.int32, t.shape, 1) // (HEAD_DIM // 2)
    out = []
    for a in range(ROPE_PACK):
        spread = t
        for g in range(ROPE_PACK):
            if g != a:
                moved = pltpu.roll(t, ((g - a) % ROPE_PACK) * (HEAD_DIM // 2), 1)
                spread = jnp.where(grp == g, moved, spread)
        out.append(spread)
    return jnp.concatenate(out, axis=0)


def _inproj_kernel(pos_ref, invf_ref, x_ref, wf_ref, *refs):
    n_cast = (len(refs) - 2) // 2
    cast_in, o_ref, cast_out, w_ref = refs[:n_cast], refs[n_cast], refs[n_cast + 1:-1], refs[-1]
    for src_ref, dst_ref in zip(cast_in, cast_out):
        dst_ref[...] = src_ref[...].astype(dst_ref.dtype)
    _cast_weights_once(pl.program_id(0) == 0, [(wf_ref, w_ref)])
    xb = x_ref[...].astype(BF16)
    ang = pos_ref[...].astype(F32) * invf_ref[...]
    cos = _spread_token_groups(jnp.cos(ang))
    sin = _spread_token_groups(jnp.sin(ang))
    lane = lax.broadcasted_iota(jnp.int32, cos.shape, 1)
    upper = (lane % HEAD_DIM) >= (HEAD_DIM // 2)
    sin_up = jnp.where(upper, sin, 0.0)
    sin_lo = jnp.where(upper, 0.0, -sin)
    n_chunks = IN_WIDTH // 512
    for c in range(n_chunks):
        h = _dot(xb, w_ref[:, c * 512:(c + 1) * 512])
        if c < 2:
            slabs = []
            for s in range(4):
                sl = h[:, s * LANES:(s + 1) * LANES]
                slabs.append(sl * cos
                             + pltpu.roll(sl, HEAD_DIM // 2, 1) * sin_up
                             + pltpu.roll(sl, LANES - HEAD_DIM // 2, 1) * sin_lo)
            h = jnp.concatenate(slabs, axis=1)
        if c == 0 or c == 3:
            h = h * (HEAD_DIM ** -0.5 * LOG2E)
        o_ref[:, c * 512:(c + 1) * 512] = h.astype(BF16)


def _inproj_call(pos, invf, x2d, w_in, w_cast, tm):
    T = x2d.shape[0]
    steps = T // tm
    cast_specs = [pl.BlockSpec((w.shape[0] // steps, w.shape[1]), lambda i: (i, 0)) for w in w_cast]
    outs = pl.pallas_call(
        _inproj_kernel,
        grid=(steps,),
        in_specs=[
            pl.BlockSpec((tm // ROPE_PACK, LANES), lambda i: (i, 0)),
            pl.BlockSpec((1, LANES), lambda i: (0, 0)),
            pl.BlockSpec((tm, D_MODEL), lambda i: (i, 0)),
            _resident((D_MODEL, IN_WIDTH)),
        ] + cast_specs,
        out_specs=[pl.BlockSpec((tm, IN_WIDTH), lambda i: (i, 0))] + cast_specs,
        out_shape=[jax.ShapeDtypeStruct((T, IN_WIDTH), BF16)]
        + [jax.ShapeDtypeStruct(w.shape, BF16) for w in w_cast],
        scratch_shapes=[pltpu.VMEM((D_MODEL, IN_WIDTH), BF16)],
        compiler_params=_params(("arbitrary",)),
        name="inproj",
    )(pos, invf, x2d, w_in, *w_cast)
    return outs[0], outs[1:]


def _interleave(*generators):
    pending = list(generators)
    while pending:
        for g in list(pending):
            try:
                next(g)
            except StopIteration:
                pending.remove(g)


def _stack_halves(q):
    lane = lax.broadcasted_iota(jnp.int32, q.shape, 1)
    zero = jnp.zeros_like(q)
    return jnp.concatenate([jnp.where(lane < HEAD_DIM, q, zero),
                            jnp.where(lane >= HEAD_DIM, q, zero)], axis=0)


DA_TQ = 256
DA_HEADS = 4
VT_ONES = 16


def _fill_vt_ones(vt_ref, v):
    nv = v.shape[1]
    vt_ref[0:nv, :] = v.astype(F32).T.astype(vt_ref.dtype)
    vt_ref[nv:, :] = jnp.ones((vt_ref.shape[0] - nv, vt_ref.shape[1]), vt_ref.dtype)


def _diff_scores(nkb, q, k_ref, mask_ref, s_ref, m_ref, *, tq):
    qq = _stack_halves(q)
    m = None
    for j in range(nkb):
        s = _dot_nt(k_ref[j * tq:(j + 1) * tq, :], qq)
        if j == nkb - 1:
            s = s + mask_ref[...]
        s_ref[j * tq:(j + 1) * tq, :] = s
        mj = jnp.max(s, axis=0, keepdims=True)
        m = mj if m is None else jnp.maximum(m, mj)
        yield
    m_ref[...] = m


def _diff_output(nkb, s_ref, m_ref, vt_ref, lam_ref, g_ref, o_ref, *, tq, lambda_init):
    m = m_ref[...]
    acc = None
    for j in range(nkb):
        rows = slice(j * tq, (j + 1) * tq)
        part = _dot(vt_ref[:, rows], jnp.exp2(s_ref[rows, :] - m).astype(BF16))
        acc = part if acc is None else acc + part
        yield
    o = acc[:DIFF_V_DIM, :] * (1.0 / acc[DIFF_V_DIM:DIFF_V_DIM + 1, :])
    o = o[:, :tq] - lam_ref[...] * o[:, tq:]
    y = o * lax.rsqrt(jnp.mean(o * o, axis=0, keepdims=True) + LN_EPS) * g_ref[...]
    o_ref[...] = (y * (1.0 - lambda_init)).T.astype(o_ref.dtype)


def _diffattn_kernel(lam_ref, g_ref, mask_ref, q_ref, k_ref, v_ref, wf_ref, o_ref, wb_ref, vt_ref, s_ref,
                     m_ref, *, tq, nq, lambda_init):
    wb_ref[...] = wf_ref[...].astype(wb_ref.dtype)
    for hh in range(DA_HEADS):
        _fill_vt_ones(vt_ref.at[hh], v_ref[0, :, hh * LANES:(hh + 1) * LANES])
    stages = [(hh, c) for hh in range(DA_HEADS) for c in range(nq)]

    def scores(i):
        hh, c = stages[i]
        cols = slice(hh * LANES, (hh + 1) * LANES)
        return _diff_scores(c + 1, q_ref[0, c * tq:(c + 1) * tq, cols], k_ref.at[0, :, cols], mask_ref,
                            s_ref.at[i % 2], m_ref.at[i % 2], tq=tq)

    _interleave(scores(0))
    for i, (hh, c) in enumerate(stages):
        if i + 1 < len(stages):
            _interleave(scores(i + 1))
        _interleave(_diff_output(c + 1, s_ref.at[i % 2], m_ref.at[i % 2], vt_ref.at[hh], lam_ref, g_ref,
                                 o_ref.at[0, c * tq:(c + 1) * tq, hh * LANES:(hh + 1) * LANES], tq=tq,
                                 lambda_init=lambda_init))


def _diffattn_call(lam, g, h3, w_cast, tq, lambda_init):
    B, S, _ = h3.shape
    nq = S // tq
    wrows, wcols = w_cast.shape
    ngroups = N_HEADS_DIFF // DA_HEADS
    width = DA_HEADS * LANES
    wspec = pl.BlockSpec((wrows // (B * ngroups), wcols), lambda b, h: (b * ngroups + h, 0))
    key_chunk = np.arange(tq)[:, None] // CHUNK
    query_chunk = (np.arange(2 * tq)[None, :] % tq) // CHUNK
    diag_mask = jnp.asarray(np.where(key_chunk <= query_chunk, 0.0, NEG_INF), F32)
    return pl.pallas_call(
        functools.partial(_diffattn_kernel, tq=tq, nq=nq, lambda_init=lambda_init),
        grid=(B, ngroups),
        scratch_shapes=[pltpu.VMEM((DA_HEADS, DIFF_V_DIM + VT_ONES, S), BF16),
                        pltpu.VMEM((2, S, 2 * tq), F32),
                        pltpu.VMEM((2, 1, 2 * tq), F32)],
        in_specs=[
            pl.BlockSpec((1, tq), lambda b, h: (0, 0)),
            pl.BlockSpec((DIFF_V_DIM, 1), lambda b, h: (0, 0)),
            pl.BlockSpec((tq, 2 * tq), lambda b, h: (0, 0)),
            pl.BlockSpec((1, S, width), lambda b, h: (b, 0, QA_BLK // DA_HEADS + h)),
            pl.BlockSpec((1, S, width), lambda b, h: (b, 0, KA_BLK // DA_HEADS + h)),
            pl.BlockSpec((1, S, width), lambda b, h: (b, 0, VA_BLK // DA_HEADS + h)),
            wspec,
        ],
        out_specs=[pl.BlockSpec((1, S, width), lambda b, h: (b, 0, h)), wspec],
        out_shape=[jax.ShapeDtypeStruct((B, S, N_HEADS_DIFF * DIFF_V_DIM), BF16),
                   jax.ShapeDtypeStruct(w_cast.shape, BF16)],
        compiler_params=_params(("arbitrary", "arbitrary")),
        name="diffattn",
    )(lam, g, diag_mask, h3, h3, h3, w_cast)


CA_TQ = 256
CA_NKB = 3
CA_PAIRS = 4


def _chunk_bias_table(rel_bias):
    nh = rel_bias.shape[0]
    n_neg = CA_TQ - REL_CLIP + 1
    n_far = CA_NKB * CA_TQ - REL_CLIP
    by_dist = jnp.concatenate([
        jnp.broadcast_to(rel_bias[:, :1], (nh, n_neg)),
        rel_bias[:, 1:2 * REL_CLIP],
        jnp.broadcast_to(rel_bias[:, 2 * REL_CLIP:], (nh, n_far)),
    ], axis=1).astype(F32) * LOG2E
    assert by_dist.shape[1] == (CA_NKB + 1) * CA_TQ
    return by_dist.reshape(nh // 2, 2, (CA_NKB + 1) * CA_TQ)


def _build_bias_table(dist_ref, bias_ref):
    row = lax.broadcasted_iota(jnp.int32, (CA_TQ, CA_TQ), 0) // CHUNK
    col = lax.broadcasted_iota(jnp.int32, (CA_TQ, CA_TQ), 1) // CHUNK
    for t in range(2):
        for j in range(CA_NKB):
            d0 = (CA_NKB - 1 - j) * CA_TQ
            g = jnp.concatenate([dist_ref[t:t + 1, d0 + CA_TQ:d0 + 2 * CA_TQ],
                                 dist_ref[t:t + 1, d0:d0 + CA_TQ]], axis=1)
            rolled = pltpu.roll(jnp.broadcast_to(g, (CA_TQ, 2 * CA_TQ)), 0, 1, stride=1, stride_axis=0)
            blk = rolled[:, :CA_TQ]
            if j == 0:
                blk = jnp.where(row >= col, blk, NEG_INF)
            elif j == CA_NKB - 1:
                blk = jnp.where(row <= col, blk, NEG_INF)
            bias_ref[j * CA_TQ:(j + 1) * CA_TQ, t * CA_TQ:(t + 1) * CA_TQ] = blk


def _chunkattn_kernel(dist_ref, q_ref, k_ref, v_ref, wf_ref, o_ref, wb_ref, bias_ref, vt_ref, s_ref, m_ref):
    nq = q_ref.shape[1] // CA_TQ
    wb_ref[...] = wf_ref[...].astype(wb_ref.dtype)

    @pl.when(pl.program_id(1) == 0)
    def _():
        for pp in range(CA_PAIRS):
            _build_bias_table(dist_ref.at[pp], bias_ref.at[pp])

    def key_blocks(blk):
        return [(j, blk - (CA_NKB - 1) + j) for j in range(CA_NKB) if blk - (CA_NKB - 1) + j >= 0]

    def scores(pp, blk, slot):
        cols = slice(pp * LANES, (pp + 1) * LANES)
        qq = _stack_halves(q_ref[0, blk * CA_TQ:(blk + 1) * CA_TQ, cols])
        m = None
        for j, kb in key_blocks(blk):
            rows = slice(j * CA_TQ, (j + 1) * CA_TQ)
            s = _dot_nt(k_ref[0, kb * CA_TQ:(kb + 1) * CA_TQ, cols], qq) + bias_ref[pp, rows, :]
            s_ref[slot, rows, :] = s
            mj = jnp.max(s, axis=0, keepdims=True)
            m = mj if m is None else jnp.maximum(m, mj)
            yield
        m_ref[slot] = m

    def output(pp, blk, slot):
        m = m_ref[slot]
        acc = None
        for j, kb in key_blocks(blk):
            p = jnp.exp2(s_ref[slot, j * CA_TQ:(j + 1) * CA_TQ, :] - m).astype(BF16)
            part = _dot(vt_ref[pp, :, kb * CA_TQ:(kb + 1) * CA_TQ], p)
            acc = part if acc is None else acc + part
            yield
        inv_l = 1.0 / acc[LANES:LANES + 1, :]
        o_t = jnp.concatenate([acc[:HEAD_DIM, :CA_TQ] * inv_l[:, :CA_TQ],
                               acc[HEAD_DIM:LANES, CA_TQ:] * inv_l[:, CA_TQ:]], axis=0)
        o_ref[0, blk * CA_TQ:(blk + 1) * CA_TQ, pp * LANES:(pp + 1) * LANES] = o_t.T.astype(o_ref.dtype)

    for pp in range(CA_PAIRS):
        _fill_vt_ones(vt_ref.at[pp], v_ref[0, :, pp * LANES:(pp + 1) * LANES])
    stages = [(pp, blk) for pp in range(CA_PAIRS) for blk in range(nq)]
    _interleave(scores(*stages[0], 0))
    for i, stage in enumerate(stages):
        nxt = [scores(*stages[i + 1], (i + 1) % 2)] if i + 1 < len(stages) else []
        _interleave(*nxt, output(*stage, i % 2))


def _chunkattn_call(bias, h3, w_cast):
    B, S, _ = h3.shape
    ngroups = N_HEADS_CHUNK // 2 // CA_PAIRS
    width = CA_PAIRS * LANES
    wrows, wcols = w_cast.shape
    wspec = pl.BlockSpec((wrows // (ngroups * B), wcols), lambda p, b: (p * B + b, 0))
    return pl.pallas_call(
        _chunkattn_kernel,
        grid=(ngroups, B),
        in_specs=[
            pl.BlockSpec((CA_PAIRS, 2, (CA_NKB + 1) * CA_TQ), lambda p, b: (p, 0, 0)),
            pl.BlockSpec((1, S, width), lambda p, b: (b, 0, QB_BLK // CA_PAIRS + p)),
            pl.BlockSpec((1, S, width), lambda p, b: (b, 0, KB_BLK // CA_PAIRS + p)),
            pl.BlockSpec((1, S, width), lambda p, b: (b, 0, VB_BLK // CA_PAIRS + p)),
            wspec,
        ],
        out_specs=[pl.BlockSpec((1, S, width), lambda p, b: (b, 0, p)), wspec],
        out_shape=[jax.ShapeDtypeStruct((B, S, N_HEADS_CHUNK * HEAD_DIM), BF16),
                   jax.ShapeDtypeStruct(w_cast.shape, BF16)],
        scratch_shapes=[pltpu.VMEM((CA_PAIRS, CA_NKB * CA_TQ, 2 * CA_TQ), F32),
                        pltpu.VMEM((CA_PAIRS, LANES + VT_ONES, S), BF16),
                        pltpu.VMEM((2, CA_NKB * CA_TQ, 2 * CA_TQ), F32),
                        pltpu.VMEM((2, 1, 2 * CA_TQ), F32)],
        compiler_params=_params(("arbitrary", "arbitrary")),
        name="chunkattn",
    )(bias, h3, h3, h3, w_cast)


def _mid_kernel(x_ref, ya_ref, yb_ref, mem_ref, wo_ref, wq_ref, wk_ref, wv_ref, wmo_ref,
                g1_ref, b1_ref, g2_ref, b2_ref, x2_ref, k_ref, v_ref, x1_ref):
    hd = MEM_HEAD_DIM
    half = ya_ref.shape[2]

    @pl.when(pl.program_id(1) == 0)
    def _():
        mb = mem_ref[0].astype(BF16)
        k_ref[...] = _dot(mb, wk_ref[...]).astype(BF16)
        v_ref[...] = _dot(mb, wv_ref[...]).astype(BF16)

    tiles = _sub_tiles(x_ref.shape[1], 256)
    ys = [_dot(ya_ref[0, rows, :], wo_ref[0:half, :]) + _dot(yb_ref[0, rows, :], wo_ref[half:, :])
          for rows in tiles]
    qs = []
    for rows, y in zip(tiles, ys):
        x1 = _layer_norm(DEEPNORM_ALPHA * x_ref[0, rows, :] + y, g1_ref[...], b1_ref[...])
        x1_ref[rows, :] = x1
        q = _dot(x1.astype(BF16), wq_ref[...]) * (hd ** -0.5 * LOG2E)
        qs.append(q.astype(BF16))
    attn = []
    for q in qs:
        outs = []
        for h in range(N_HEADS_MEM):
            sl = slice(h * hd, (h + 1) * hd)
            s = _dot_nt(q[:, sl], k_ref[:, sl])
            p = jnp.exp2(s - jnp.max(s, axis=1, keepdims=True))
            l = jnp.sum(p, axis=1, keepdims=True)
            outs.append((_dot(p.astype(BF16), v_ref[:, sl]) / l).astype(BF16))
        attn.append(jnp.concatenate(outs, axis=1))
    y2s = [_dot(o, wmo_ref[...]) for o in attn]
    for rows, y2 in zip(tiles, y2s):
        x2_ref[0, rows, :] = _layer_norm(DEEPNORM_ALPHA * x1_ref[rows, :] + y2, g2_ref[...], b2_ref[...])


def _mid_call(x, ya, yb, mem, w_o, w_mq, w_mk, w_mv, w_mo, g1, b1, g2, b2, tm):
    B, S, _ = x.shape
    M = mem.shape[1]
    half = ya.shape[2]
    row = pl.BlockSpec((1, tm, D_MODEL), lambda bb, i: (bb, i, 0))
    hrow = pl.BlockSpec((1, tm, half), lambda bb, i: (bb, i, 0))
    memblk = pl.BlockSpec((1, M, D_MODEL), lambda bb, i: (bb, 0, 0))
    vec = pl.BlockSpec((1, D_MODEL), lambda bb, i: (0, 0))
    wspec = _resident((D_MODEL, D_MODEL))
    return pl.pallas_call(
        _mid_kernel,
        grid=(B, S // tm),
        in_specs=[row, hrow, hrow, memblk] + [wspec] * 5 + [vec] * 4,
        out_specs=row,
        out_shape=jax.ShapeDtypeStruct((B, S, D_MODEL), F32),
        scratch_shapes=[pltpu.VMEM((M, D_MODEL), BF16)] * 2 + [pltpu.VMEM((tm, D_MODEL), F32)],
        compiler_params=_params(("arbitrary", "arbitrary")),
        name="mid",
    )(x, ya, yb, mem, w_o, w_mq, w_mk, w_mv, w_mo, g1, b1, g2, b2)


def _mlp_kernel(x_ref, wu_ref, wd_ref, g_ref, b_ref, o_ref, *, ff_chunk):
    def finish(rows, y):
        o_ref[rows, :] = _layer_norm(DEEPNORM_ALPHA * x_ref[rows, :] + y, g_ref[...], b_ref[...])

    pending = None
    for rows in _sub_tiles(x_ref.shape[0], 256):
        xb = x_ref[rows, :].astype(BF16)
        y = None
        for c in range(D_FF // ff_chunk):
            sl = slice(c * ff_chunk, (c + 1) * ff_chunk)
            h = jnp.maximum(_dot(xb, wu_ref[:, sl]), 0.0)
            part = _dot((h * h).astype(BF16), wd_ref[sl, :])
            y = part if y is None else y + part
            if c == 0 and pending is not None:
                finish(*pending)
                pending = None
        pending = (rows, y)
    finish(*pending)


def _mlp_call(x2d, w_up, w_down, g, b, tm, ff_chunk):
    T = x2d.shape[0]
    row = pl.BlockSpec((tm, D_MODEL), lambda i: (i, 0))
    vec = pl.BlockSpec((1, D_MODEL), lambda i: (0, 0))
    return pl.pallas_call(
        functools.partial(_mlp_kernel, ff_chunk=ff_chunk),
        grid=(T // tm,),
        in_specs=[row,
                  pl.BlockSpec((D_MODEL, D_FF), lambda i: (0, 0), pipeline_mode=pl.Buffered(1)),
                  pl.BlockSpec((D_FF, D_MODEL), lambda i: (0, 0), pipeline_mode=pl.Buffered(1)),
                  vec, vec],
        out_specs=row,
        out_shape=jax.ShapeDtypeStruct((T, D_MODEL), F32),
        compiler_params=_params(("arbitrary",)),
        name="mlp",
    )(x2d, w_up, w_down, g, b)


def kernel(x, mem, positions, w_in, diff_lambda, subln_g, rel_bias, w_o, ln1_g, ln1_b,
           w_mq, w_mk, w_mv, w_mo, ln2_g, ln2_b, w_up, w_down, ln3_g, ln3_b):
    B, S, D = x.shape
    T = B * S
    depth = w_in.shape[0]
    assert depth == DEPTH and D == D_MODEL and S % CA_TQ == 0
    inv_freq = 1.0 / (ROPE_THETA ** (jnp.arange(0, HEAD_DIM, 2, dtype=F32) / HEAD_DIM))
    invf = jnp.tile(inv_freq, LANES // (HEAD_DIM // 2)).reshape(1, LANES)
    pos = positions.reshape(T // IN_TM, ROPE_PACK, IN_TM // ROPE_PACK).transpose(0, 2, 1)
    pos = jnp.repeat(pos, HEAD_DIM // 2, axis=2).reshape(T // ROPE_PACK, LANES)
    vec = lambda a: a.reshape(1, -1)

    for l in range(depth):
        lambda_init = 0.8 - 0.6 * math.exp(-0.3 * l)
        lam = _lam_call(diff_lambda[l], lambda_init, DA_TQ)
        h, mid_w = _inproj_call(pos, invf, x.reshape(T, D), w_in[l],
                                [w_o[l], w_mq[l], w_mk[l], w_mv[l], w_mo[l]], tm=IN_TM)
        h3 = h.reshape(B, S, IN_WIDTH)
        ya, w_up_b = _diffattn_call(lam, subln_g[l].reshape(-1, 1), h3, w_up[l], tq=DA_TQ,
                                    lambda_init=lambda_init)
        yb, w_down_b = _chunkattn_call(_chunk_bias_table(rel_bias[l]), h3, w_down[l])
        x2 = _mid_call(x, ya, yb, mem, *mid_w, vec(ln1_g[l]), vec(ln1_b[l]), vec(ln2_g[l]), vec(ln2_b[l]),
                       tm=1024)
        out = _mlp_call(x2.reshape(T, D), w_up_b, w_down_b,
                        vec(ln3_g[l]), vec(ln3_b[l]), tm=1024, ff_chunk=1024)
        x = out.reshape(B, S, D)
    return x
```

```python
import functools
import math

import jax
import jax.numpy as jnp
import numpy as np
from jax import lax
from jax.experimental import pallas as pl
from jax.experimental.pallas import tpu as pltpu

D_MODEL = 1024
CHUNK = 64
HEAD_DIM = 64
N_HEADS_DIFF = 4
DIFF_V_DIM = 128
N_HEADS_CHUNK = 8
LEFT_CHUNKS = 8
REL_CLIP = 128
N_HEADS_MEM = 4
MEM_HEAD_DIM = 256
D_FF = 4096
ROPE_THETA = 10000.0
LN_EPS = 1e-5
NEG_INF = -1e30
LOG2E = math.log2(math.e)
DEPTH = 1
DEEPNORM_ALPHA = (2.0 * DEPTH) ** 0.25
IN_WIDTH = 3072

LANES = 128
VMEM_LIMIT = 56 * 1024 * 1024

F32 = jnp.float32
BF16 = jnp.bfloat16

QA_BLK, KA_BLK, VA_BLK, QB_BLK, KB_BLK, VB_BLK = 0, 4, 8, 12, 16, 20


def _params(sem):
    return pltpu.CompilerParams(dimension_semantics=sem, vmem_limit_bytes=VMEM_LIMIT)


def _dot(a, b):
    return jnp.dot(a, b, preferred_element_type=F32)


def _dot_nt(a, b):
    return lax.dot_general(a, b, (((1,), (1,)), ((), ())), preferred_element_type=F32)


SUB_ROWS = 512


def _sub_tiles(n_rows, sub_rows=SUB_ROWS):
    return [slice(r, r + sub_rows) for r in range(0, n_rows, sub_rows)]


def _resident(shape):
    return pl.BlockSpec(shape, lambda *_: (0,) * len(shape), pipeline_mode=pl.Buffered(1))


def _cast_weights_once(first_step, pairs):
    @pl.when(first_step)
    def _():
        for src_ref, dst_ref in pairs:
            dst_ref[...] = src_ref[...].astype(dst_ref.dtype)


def _layer_norm(z, g, b):
    mu = jnp.mean(z, axis=-1, keepdims=True)
    zc = z - mu
    var = jnp.mean(zc * zc, axis=-1, keepdims=True)
    return zc * lax.rsqrt(var + LN_EPS) * g + b


def _lam_kernel(dl_ref, o_ref, *, lambda_init):
    dl = dl_ref[...]
    s1 = jnp.sum(dl[0:1, :] * dl[1:2, :], axis=1, keepdims=True)
    s2 = jnp.sum(dl[2:3, :] * dl[3:4, :], axis=1, keepdims=True)
    lam = jnp.exp(s1) - jnp.exp(s2) + lambda_init
    o_ref[...] = jnp.broadcast_to(lam, o_ref.shape)


def _lam_call(diff_lambda, lambda_init, width):
    return pl.pallas_call(
        functools.partial(_lam_kernel, lambda_init=lambda_init),
        out_shape=jax.ShapeDtypeStruct((1, width), F32),
        name="lam",
    )(diff_lambda)


IN_TM = 512
ROPE_PACK = LANES // (HEAD_DIM // 2)


def _spread_token_groups(t):
    n = t.shape[0]
    grp = lax.broadcasted_iota(jnp.int32, t.shape, 1) // (HEAD_DIM // 2)
    out = []
    for a in range(ROPE_PACK):
        spread = t
        for g in range(ROPE_PACK):
            if g != a:
                moved = pltpu.roll(t, ((g - a) % ROPE_PACK) * (HEAD_DIM // 2), 1)
                spread = jnp.where(grp == g, moved, spread)
        out.append(spread)
    return jnp.concatenate(out, axis=0)


def _inproj_kernel(pos_ref, invf_ref, x_ref, wf_ref, *refs):
    n_cast = (len(refs) - 2) // 2
    cast_in, o_ref, cast_out, w_ref = refs[:n_cast], refs[n_cast], refs[n_cast + 1:-1], refs[-1]
    for src_ref, dst_ref in zip(cast_in, cast_out):
        dst_ref[...] = src_ref[...].astype(dst_ref.dtype)
    _cast_weights_once(pl.program_id(0) == 0, [(wf_ref, w_ref)])
    xb = x_ref[...].astype(BF16)
    ang = pos_ref[...].astype(F32) * invf_ref[...]
    cos = _spread_token_groups(jnp.cos(ang))
    sin = _spread_token_groups(jnp.sin(ang))
    lane = lax.broadcasted_iota(jnp.int32, cos.shape, 1)
    upper = (lane % HEAD_DIM) >= (HEAD_DIM // 2)
    sin_up = jnp.where(upper, sin, 0.0)
    sin_lo = jnp.where(upper, 0.0, -sin)
    n_chunks = IN_WIDTH // 512
    for c in range(n_chunks):
        h = _dot(xb, w_ref[:, c * 512:(c + 1) * 512])
        if c < 2:
            slabs = []
            for s in range(4):
                sl = h[:, s * LANES:(s + 1) * LANES]
                slabs.append(sl * cos
                             + pltpu.roll(sl, HEAD_DIM // 2, 1) * sin_up
                             + pltpu.roll(sl, LANES - HEAD_DIM // 2, 1) * sin_lo)
            h = jnp.concatenate(slabs, axis=1)
        if c == 0 or c == 3:
            h = h * (HEAD_DIM ** -0.5 * LOG2E)
        o_ref[:, c * 512:(c + 1) * 512] = h.astype(BF16)


def _inproj_call(pos, invf, x2d, w_in, w_cast, tm):
    T = x2d.shape[0]
    steps = T // tm
    cast_specs = [pl.BlockSpec((w.shape[0] // steps, w.shape[1]), lambda i: (i, 0)) for w in w_cast]
    outs = pl.pallas_call(
        _inproj_kernel,
        grid=(steps,),
        in_specs=[
            pl.BlockSpec((tm // ROPE_PACK, LANES), lambda i: (i, 0)),
            pl.BlockSpec((1, LANES), lambda i: (0, 0)),
            pl.BlockSpec((tm, D_MODEL), lambda i: (i, 0)),
            _resident((D_MODEL, IN_WIDTH)),
        ] + cast_specs,
        out_specs=[pl.BlockSpec((tm, IN_WIDTH), lambda i: (i, 0))] + cast_specs,
        out_shape=[jax.ShapeDtypeStruct((T, IN_WIDTH), BF16)]
        + [jax.ShapeDtypeStruct(w.shape, BF16) for w in w_cast],
        scratch_shapes=[pltpu.VMEM((D_MODEL, IN_WIDTH), BF16)],
        compiler_params=_params(("arbitrary",)),
        name="inproj",
    )(pos, invf, x2d, w_in, *w_cast)
    return outs[0], outs[1:]


def _interleave(*generators):
    pending = list(generators)
    while pending:
        for g in list(pending):
            try:
                next(g)
            except StopIteration:
                pending.remove(g)


def _stack_halves(q):
    lane = lax.broadcasted_iota(jnp.int32, q.shape, 1)
    zero = jnp.zeros_like(q)
    return jnp.concatenate([jnp.where(lane < HEAD_DIM, q, zero),
                            jnp.where(lane >= HEAD_DIM, q, zero)], axis=0)


DA_TQ = 256
DA_SLOTS = 3
DA_HEADS = 2
VT_ONES = 16


def _fill_vt_ones(vt_ref, v):
    nv = v.shape[1]
    vt_ref[0:nv, :] = v.astype(F32).T.astype(vt_ref.dtype)
    vt_ref[nv:, :] = jnp.ones((vt_ref.shape[0] - nv, vt_ref.shape[1]), vt_ref.dtype)


def _diff_scores(nkb, q, k_ref, mask_ref, s_ref, m_ref, *, tq):
    qq = _stack_halves(q)
    m = None
    for j in range(nkb):
        s = _dot_nt(k_ref[j * tq:(j + 1) * tq, :], qq)
        if j == nkb - 1:
            s = s + mask_ref[...]
        s_ref[j * tq:(j + 1) * tq, :] = s
        mj = jnp.max(s, axis=0, keepdims=True)
        m = mj if m is None else jnp.maximum(m, mj)
        yield
    m_ref[...] = m


def _diff_output(nkb, s_ref, m_ref, vt_ref, lam_ref, g_ref, o_ref, *, tq, lambda_init):
    m = m_ref[...]
    acc = None
    for j in range(nkb):
        rows = slice(j * tq, (j + 1) * tq)
        part = _dot(vt_ref[:, rows], jnp.exp2(s_ref[rows, :] - m).astype(BF16))
        acc = part if acc is None else acc + part
        yield
    o = acc[:DIFF_V_DIM, :] * (1.0 / acc[DIFF_V_DIM:DIFF_V_DIM + 1, :])
    o = o[:, :tq] - lam_ref[...] * o[:, tq:]
    y = o * lax.rsqrt(jnp.mean(o * o, axis=0, keepdims=True) + LN_EPS) * g_ref[...]
    o_ref[...] = (y * (1.0 - lambda_init)).T.astype(o_ref.dtype)


def _diffattn_kernel(lam_ref, g_ref, mask_ref, q_ref, k_ref, v_ref, wf_ref, o_ref, wb_ref, vt_ref, s_ref,
                     m_ref, *, tq, nq, lambda_init):
    wb_ref[...] = wf_ref[...].astype(wb_ref.dtype)
    for hh in range(DA_HEADS):
        _fill_vt_ones(vt_ref.at[hh], v_ref[0, :, hh * LANES:(hh + 1) * LANES])
    stages = [(hh, c) for hh in range(DA_HEADS) for c in range(nq)]

    def scores(i):
        hh, c = stages[i]
        cols = slice(hh * LANES, (hh + 1) * LANES)
        return _diff_scores(c + 1, q_ref[0, c * tq:(c + 1) * tq, cols], k_ref.at[0, :, cols], mask_ref,
                            s_ref.at[i % DA_SLOTS], m_ref.at[i % DA_SLOTS], tq=tq)

    for i in range(DA_SLOTS - 1):
        _interleave(scores(i))
    for i, (hh, c) in enumerate(stages):
        if i + DA_SLOTS - 1 < len(stages):
            _interleave(scores(i + DA_SLOTS - 1))
        _interleave(_diff_output(c + 1, s_ref.at[i % DA_SLOTS], m_ref.at[i % DA_SLOTS], vt_ref.at[hh], lam_ref, g_ref,
                                 o_ref.at[0, c * tq:(c + 1) * tq, hh * LANES:(hh + 1) * LANES], tq=tq,
                                 lambda_init=lambda_init))


def _diffattn_call(lam, g, h3, w_cast, tq, lambda_init):
    B, S, _ = h3.shape
    nq = S // tq
    wrows, wcols = w_cast.shape
    ngroups = N_HEADS_DIFF // DA_HEADS
    width = DA_HEADS * LANES
    wspec = pl.BlockSpec((wrows // (B * ngroups), wcols), lambda b, h: (b * ngroups + h, 0))
    key_chunk = np.arange(tq)[:, None] // CHUNK
    query_chunk = (np.arange(2 * tq)[None, :] % tq) // CHUNK
    diag_mask = jnp.asarray(np.where(key_chunk <= query_chunk, 0.0, NEG_INF), F32)
    return pl.pallas_call(
        functools.partial(_diffattn_kernel, tq=tq, nq=nq, lambda_init=lambda_init),
        grid=(B, ngroups),
        scratch_shapes=[pltpu.VMEM((DA_HEADS, DIFF_V_DIM + VT_ONES, S), BF16),
                        pltpu.VMEM((DA_SLOTS, S, 2 * tq), F32),
                        pltpu.VMEM((DA_SLOTS, 1, 2 * tq), F32)],
        in_specs=[
            pl.BlockSpec((1, tq), lambda b, h: (0, 0)),
            pl.BlockSpec((DIFF_V_DIM, 1), lambda b, h: (0, 0)),
            pl.BlockSpec((tq, 2 * tq), lambda b, h: (0, 0)),
            pl.BlockSpec((1, S, width), lambda b, h: (b, 0, QA_BLK // DA_HEADS + h)),
            pl.BlockSpec((1, S, width), lambda b, h: (b, 0, KA_BLK // DA_HEADS + h)),
            pl.BlockSpec((1, S, width), lambda b, h: (b, 0, VA_BLK // DA_HEADS + h)),
            wspec,
        ],
        out_specs=[pl.BlockSpec((1, S, width), lambda b, h: (b, 0, h)), wspec],
        out_shape=[jax.ShapeDtypeStruct((B, S, N_HEADS_DIFF * DIFF_V_DIM), BF16),
                   jax.ShapeDtypeStruct(w_cast.shape, BF16)],
        compiler_params=_params(("arbitrary", "arbitrary")),
        name="diffattn",
    )(lam, g, diag_mask, h3, h3, h3, w_cast)


CA_TQ = 256
CA_NKB = 3
CA_SLOTS = 3
CA_PAIRS = 2


def _chunk_bias_table(rel_bias):
    nh = rel_bias.shape[0]
    n_neg = CA_TQ - REL_CLIP + 1
    n_far = CA_NKB * CA_TQ - REL_CLIP
    by_dist = jnp.concatenate([
        jnp.broadcast_to(rel_bias[:, :1], (nh, n_neg)),
        rel_bias[:, 1:2 * REL_CLIP],
        jnp.broadcast_to(rel_bias[:, 2 * REL_CLIP:], (nh, n_far)),
    ], axis=1).astype(F32) * LOG2E
    assert by_dist.shape[1] == (CA_NKB + 1) * CA_TQ
    return by_dist.reshape(nh // 2, 2, (CA_NKB + 1) * CA_TQ)


def _build_bias_table(dist_ref, bias_ref):
    row = lax.broadcasted_iota(jnp.int32, (CA_TQ, CA_TQ), 0) // CHUNK
    col = lax.broadcasted_iota(jnp.int32, (CA_TQ, CA_TQ), 1) // CHUNK
    for t in range(2):
        for j in range(CA_NKB):
            d0 = (CA_NKB - 1 - j) * CA_TQ
            g = jnp.concatenate([dist_ref[t:t + 1, d0 + CA_TQ:d0 + 2 * CA_TQ],
                                 dist_ref[t:t + 1, d0:d0 + CA_TQ]], axis=1)
            rolled = pltpu.roll(jnp.broadcast_to(g, (CA_TQ, 2 * CA_TQ)), 0, 1, stride=1, stride_axis=0)
            blk = rolled[:, :CA_TQ]
            if j == 0:
                blk = jnp.where(row >= col, blk, NEG_INF)
            elif j == CA_NKB - 1:
                blk = jnp.where(row <= col, blk, NEG_INF)
            bias_ref[j * CA_TQ:(j + 1) * CA_TQ, t * CA_TQ:(t + 1) * CA_TQ] = blk


def _chunkattn_kernel(dist_ref, q_ref, k_ref, v_ref, wf_ref, o_ref, wb_ref, bias_ref, vt_ref, s_ref, m_ref):
    nq = q_ref.shape[1] // CA_TQ
    wb_ref[...] = wf_ref[...].astype(wb_ref.dtype)

    @pl.when(pl.program_id(1) == 0)
    def _():
        for pp in range(CA_PAIRS):
            _build_bias_table(dist_ref.at[pp], bias_ref.at[pp])

    def key_blocks(blk):
        return [(j, blk - (CA_NKB - 1) + j) for j in range(CA_NKB) if blk - (CA_NKB - 1) + j >= 0]

    def scores(pp, blk, slot):
        cols = slice(pp * LANES, (pp + 1) * LANES)
        qq = _stack_halves(q_ref[0, blk * CA_TQ:(blk + 1) * CA_TQ, cols])
        m = None
        for j, kb in key_blocks(blk):
            rows = slice(j * CA_TQ, (j + 1) * CA_TQ)
            s = _dot_nt(k_ref[0, kb * CA_TQ:(kb + 1) * CA_TQ, cols], qq) + bias_ref[pp, rows, :]
            s_ref[slot, rows, :] = s
            mj = jnp.max(s, axis=0, keepdims=True)
            m = mj if m is None else jnp.maximum(m, mj)
            yield
        m_ref[slot] = m

    def output(pp, blk, slot):
        m = m_ref[slot]
        acc = None
        for j, kb in key_blocks(blk):
            p = jnp.exp2(s_ref[slot, j * CA_TQ:(j + 1) * CA_TQ, :] - m).astype(BF16)
            part = _dot(vt_ref[pp, :, kb * CA_TQ:(kb + 1) * CA_TQ], p)
            acc = part if acc is None else acc + part
            yield
        inv_l = 1.0 / acc[LANES:LANES + 1, :]
        o_t = jnp.concatenate([acc[:HEAD_DIM, :CA_TQ] * inv_l[:, :CA_TQ],
                               acc[HEAD_DIM:LANES, CA_TQ:] * inv_l[:, CA_TQ:]], axis=0)
        o_ref[0, blk * CA_TQ:(blk + 1) * CA_TQ, pp * LANES:(pp + 1) * LANES] = o_t.T.astype(o_ref.dtype)

    for pp in range(CA_PAIRS):
        _fill_vt_ones(vt_ref.at[pp], v_ref[0, :, pp * LANES:(pp + 1) * LANES])
    stages = [(pp, blk) for pp in range(CA_PAIRS) for blk in range(nq)]
    ahead = CA_SLOTS - 1
    for i in range(ahead):
        _interleave(scores(*stages[i], i % CA_SLOTS))
    for i, stage in enumerate(stages):
        nxt = [scores(*stages[i + ahead], (i + ahead) % CA_SLOTS)] if i + ahead < len(stages) else []
        _interleave(*nxt, output(*stage, i % CA_SLOTS))


def _chunkattn_call(bias, h3, w_cast):
    B, S, _ = h3.shape
    ngroups = N_HEADS_CHUNK // 2 // CA_PAIRS
    width = CA_PAIRS * LANES
    wrows, wcols = w_cast.shape
    wspec = pl.BlockSpec((wrows // (ngroups * B), wcols), lambda p, b: (p * B + b, 0))
    return pl.pallas_call(
        _chunkattn_kernel,
        grid=(ngroups, B),
        in_specs=[
            pl.BlockSpec((CA_PAIRS, 2, (CA_NKB + 1) * CA_TQ), lambda p, b: (p, 0, 0)),
            pl.BlockSpec((1, S, width), lambda p, b: (b, 0, QB_BLK // CA_PAIRS + p)),
            pl.BlockSpec((1, S, width), lambda p, b: (b, 0, KB_BLK // CA_PAIRS + p)),
            pl.BlockSpec((1, S, width), lambda p, b: (b, 0, VB_BLK // CA_PAIRS + p)),
            wspec,
        ],
        out_specs=[pl.BlockSpec((1, S, width), lambda p, b: (b, 0, p)), wspec],
        out_shape=[jax.ShapeDtypeStruct((B, S, N_HEADS_CHUNK * HEAD_DIM), BF16),
                   jax.ShapeDtypeStruct(w_cast.shape, BF16)],
        scratch_shapes=[pltpu.VMEM((CA_PAIRS, CA_NKB * CA_TQ, 2 * CA_TQ), F32),
                        pltpu.VMEM((CA_PAIRS, LANES + VT_ONES, S), BF16),
                        pltpu.VMEM((CA_SLOTS, CA_NKB * CA_TQ, 2 * CA_TQ), F32),
                        pltpu.VMEM((CA_SLOTS, 1, 2 * CA_TQ), F32)],
        compiler_params=_params(("arbitrary", "arbitrary")),
        name="chunkattn",
    )(bias, h3, h3, h3, w_cast)


def _mid_kernel(x_ref, ya_ref, yb_ref, mem_ref, wo_ref, wq_ref, wk_ref, wv_ref, wmo_ref,
                g1_ref, b1_ref, g2_ref, b2_ref, x2_ref, k_ref, v_ref, x1_ref):
    hd = MEM_HEAD_DIM
    half = ya_ref.shape[2]

    @pl.when(pl.program_id(1) == 0)
    def _():
        mb = mem_ref[0].astype(BF16)
        k_ref[...] = _dot(mb, wk_ref[...]).astype(BF16)
        v_ref[...] = _dot(mb, wv_ref[...]).astype(BF16)

    tiles = _sub_tiles(x_ref.shape[1], 256)
    ys = [_dot(ya_ref[0, rows, :], wo_ref[0:half, :]) + _dot(yb_ref[0, rows, :], wo_ref[half:, :])
          for rows in tiles]
    qs = []
    for rows, y in zip(tiles, ys):
        x1 = _layer_norm(DEEPNORM_ALPHA * x_ref[0, rows, :] + y, g1_ref[...], b1_ref[...])
        x1_ref[rows, :] = x1
        q = _dot(x1.astype(BF16), wq_ref[...]) * (hd ** -0.5 * LOG2E)
        qs.append(q.astype(BF16))
    attn = []
    for q in qs:
        outs = []
        for h in range(N_HEADS_MEM):
            sl = slice(h * hd, (h + 1) * hd)
            s = _dot_nt(q[:, sl], k_ref[:, sl])
            p = jnp.exp2(s - jnp.max(s, axis=1, keepdims=True))
            l = jnp.sum(p, axis=1, keepdims=True)
            outs.append((_dot(p.astype(BF16), v_ref[:, sl]) / l).astype(BF16))
        attn.append(jnp.concatenate(outs, axis=1))
    y2s = [_dot(o, wmo_ref[...]) for o in attn]
    for rows, y2 in zip(tiles, y2s):
        x2_ref[0, rows, :] = _layer_norm(DEEPNORM_ALPHA * x1_ref[rows, :] + y2, g2_ref[...], b2_ref[...])


def _mid_call(x, ya, yb, mem, w_o, w_mq, w_mk, w_mv, w_mo, g1, b1, g2, b2, tm):
    B, S, _ = x.shape
    M = mem.shape[1]
    half = ya.shape[2]
    row = pl.BlockSpec((1, tm, D_MODEL), lambda bb, i: (bb, i, 0))
    hrow = pl.BlockSpec((1, tm, half), lambda bb, i: (bb, i, 0))
    memblk = pl.BlockSpec((1, M, D_MODEL), lambda bb, i: (bb, 0, 0))
    vec = pl.BlockSpec((1, D_MODEL), lambda bb, i: (0, 0))
    wspec = _resident((D_MODEL, D_MODEL))
    return pl.pallas_call(
        _mid_kernel,
        grid=(B, S // tm),
        in_specs=[row, hrow, hrow, memblk] + [wspec] * 5 + [vec] * 4,
        out_specs=row,
        out_shape=jax.ShapeDtypeStruct((B, S, D_MODEL), F32),
        scratch_shapes=[pltpu.VMEM((M, D_MODEL), BF16)] * 2 + [pltpu.VMEM((tm, D_MODEL), F32)],
        compiler_params=_params(("arbitrary", "arbitrary")),
        name="mid",
    )(x, ya, yb, mem, w_o, w_mq, w_mk, w_mv, w_mo, g1, b1, g2, b2)


def _mlp_kernel(x_ref, wu_ref, wd_ref, g_ref, b_ref, o_ref, *, ff_chunk):
    def finish(rows, y):
        o_ref[rows, :] = _layer_norm(DEEPNORM_ALPHA * x_ref[rows, :] + y, g_ref[...], b_ref[...])

    pending = None
    for rows in _sub_tiles(x_ref.shape[0], 256):
        xb = x_ref[rows, :].astype(BF16)
        y = None
        for c in range(D_FF // ff_chunk):
            sl = slice(c * ff_chunk, (c + 1) * ff_chunk)
            h = jnp.maximum(_dot(xb, wu_ref[:, sl]), 0.0)
            part = _dot((h * h).astype(BF16), wd_ref[sl, :])
            y = part if y is None else y + part
            if c == 0 and pending is not None:
                finish(*pending)
                pending = None
        pending = (rows, y)
    finish(*pending)


def _mlp_call(x2d, w_up, w_down, g, b, tm, ff_chunk):
    T = x2d.shape[0]
    row = pl.BlockSpec((tm, D_MODEL), lambda i: (i, 0))
    vec = pl.BlockSpec((1, D_MODEL), lambda i: (0, 0))
    return pl.pallas_call(
        functools.partial(_mlp_kernel, ff_chunk=ff_chunk),
        grid=(T // tm,),
        in_specs=[row,
                  pl.BlockSpec((D_MODEL, D_FF), lambda i: (0, 0), pipeline_mode=pl.Buffered(1)),
                  pl.BlockSpec((D_FF, D_MODEL), lambda i: (0, 0), pipeline_mode=pl.Buffered(1)),
                  vec, vec],
        out_specs=row,
        out_shape=jax.ShapeDtypeStruct((T, D_MODEL), F32),
        compiler_params=_params(("arbitrary",)),
        name="mlp",
    )(x2d, w_up, w_down, g, b)


def kernel(x, mem, positions, w_in, diff_lambda, subln_g, rel_bias, w_o, ln1_g, ln1_b,
           w_mq, w_mk, w_mv, w_mo, ln2_g, ln2_b, w_up, w_down, ln3_g, ln3_b):
    B, S, D = x.shape
    T = B * S
    depth = w_in.shape[0]
    assert depth == DEPTH and D == D_MODEL and S % CA_TQ == 0
    inv_freq = 1.0 / (ROPE_THETA ** (jnp.arange(0, HEAD_DIM, 2, dtype=F32) / HEAD_DIM))
    invf = jnp.tile(inv_freq, LANES // (HEAD_DIM // 2)).reshape(1, LANES)
    pos = positions.reshape(T // IN_TM, ROPE_PACK, IN_TM // ROPE_PACK).transpose(0, 2, 1)
    pos = jnp.repeat(pos, HEAD_DIM // 2, axis=2).reshape(T // ROPE_PACK, LANES)
    vec = lambda a: a.reshape(1, -1)

    for l in range(depth):
        lambda_init = 0.8 - 0.6 * math.exp(-0.3 * l)
        lam = _lam_call(diff_lambda[l], lambda_init, DA_TQ)
        h, mid_w = _inproj_call(pos, invf, x.reshape(T, D), w_in[l],
                                [w_o[l], w_mq[l], w_mk[l], w_mv[l], w_mo[l]], tm=IN_TM)
        h3 = h.reshape(B, S, IN_WIDTH)
        ya, w_up_b = _diffattn_call(lam, subln_g[l].reshape(-1, 1), h3, w_up[l], tq=DA_TQ,
                                    lambda_init=lambda_init)
        yb, w_down_b = _chunkattn_call(_chunk_bias_table(rel_bias[l]), h3, w_down[l])
        x2 = _mid_call(x, ya, yb, mem, *mid_w, vec(ln1_g[l]), vec(ln1_b[l]), vec(ln2_g[l]), vec(ln2_b[l]),
                       tm=1024)
        out = _mlp_call(x2.reshape(T, D), w_up_b, w_down_b,
                        vec(ln3_g[l]), vec(ln3_b[l]), tm=1024, ff_chunk=1024)
        x = out.reshape(B, S, D)
    return x
```

```python
import functools
import math

import jax
import jax.numpy as jnp
import numpy as np
from jax import lax
from jax.experimental import pallas as pl
from jax.experimental.pallas import tpu as pltpu

D_MODEL = 1024
CHUNK = 64
HEAD_DIM = 64
N_HEADS_DIFF = 4
DIFF_V_DIM = 128
N_HEADS_CHUNK = 8
LEFT_CHUNKS = 8
REL_CLIP = 128
N_HEADS_MEM = 4
MEM_HEAD_DIM = 256
D_FF = 4096
ROPE_THETA = 10000.0
LN_EPS = 1e-5
NEG_INF = -1e30
LOG2E = math.log2(math.e)
DEPTH = 1
DEEPNORM_ALPHA = (2.0 * DEPTH) ** 0.25
IN_WIDTH = 3072

LANES = 128
VMEM_LIMIT = 56 * 1024 * 1024

F32 = jnp.float32
BF16 = jnp.bfloat16

QA_BLK, KA_BLK, VA_BLK, QB_BLK, KB_BLK, VB_BLK = 0, 4, 8, 12, 16, 20


def _params(sem):
    return pltpu.CompilerParams(dimension_semantics=sem, vmem_limit_bytes=VMEM_LIMIT)


def _dot(a, b):
    return jnp.dot(a, b, preferred_element_type=F32)


def _dot_nt(a, b):
    return lax.dot_general(a, b, (((1,), (1,)), ((), ())), preferred_element_type=F32)


SUB_ROWS = 512


def _sub_tiles(n_rows, sub_rows=SUB_ROWS):
    return [slice(r, r + sub_rows) for r in range(0, n_rows, sub_rows)]


def _resident(shape):
    return pl.BlockSpec(shape, lambda *_: (0,) * len(shape), pipeline_mode=pl.Buffered(1))


def _cast_weights_once(first_step, pairs):
    @pl.when(first_step)
    def _():
        for src_ref, dst_ref in pairs:
            dst_ref[...] = src_ref[...].astype(dst_ref.dtype)


def _layer_norm(z, g, b):
    mu = jnp.mean(z, axis=-1, keepdims=True)
    zc = z - mu
    var = jnp.mean(zc * zc, axis=-1, keepdims=True)
    return zc * lax.rsqrt(var + LN_EPS) * g + b


def _lam_kernel(dl_ref, o_ref, *, lambda_init):
    dl = dl_ref[...]
    s1 = jnp.sum(dl[0:1, :] * dl[1:2, :], axis=1, keepdims=True)
    s2 = jnp.sum(dl[2:3, :] * dl[3:4, :], axis=1, keepdims=True)
    lam = jnp.exp(s1) - jnp.exp(s2) + lambda_init
    o_ref[...] = jnp.broadcast_to(lam, o_ref.shape)


def _lam_call(diff_lambda, lambda_init, width):
    return pl.pallas_call(
        functools.partial(_lam_kernel, lambda_init=lambda_init),
        out_shape=jax.ShapeDtypeStruct((1, width), F32),
        name="lam",
    )(diff_lambda)


IN_TM = 512
ROPE_PACK = LANES // (HEAD_DIM // 2)


def _spread_token_groups(t):
    n = t.shape[0]
    grp = lax.broadcasted_iota(jnp.int32, t.shape, 1) // (HEAD_DIM // 2)
    out = []
    for a in range(ROPE_PACK):
        spread = t
        for g in range(ROPE_PACK):
            if g != a:
                moved = pltpu.roll(t, ((g - a) % ROPE_PACK) * (HEAD_DIM // 2), 1)
                spread = jnp.where(grp == g, moved, spread)
        out.append(spread)
    return jnp.concatenate(out, axis=0)


def _inproj_kernel(pos_ref, invf_ref, x_ref, wf_ref, *refs):
    n_cast = (len(refs) - 2) // 2
    cast_in, o_ref, cast_out, w_ref = refs[:n_cast], refs[n_cast], refs[n_cast + 1:-1], refs[-1]
    for src_ref, dst_ref in zip(cast_in, cast_out):
        dst_ref[...] = src_ref[...].astype(dst_ref.dtype)
    _cast_weights_once(pl.program_id(0) == 0, [(wf_ref, w_ref)])
    xb = x_ref[...].astype(BF16)
    ang = pos_ref[...].astype(F32) * invf_ref[...]
    cos = _spread_token_groups(jnp.cos(ang))
    sin = _spread_token_groups(jnp.sin(ang))
    lane = lax.broadcasted_iota(jnp.int32, cos.shape, 1)
    upper = (lane % HEAD_DIM) >= (HEAD_DIM // 2)
    sin_up = jnp.where(upper, sin, 0.0)
    sin_lo = jnp.where(upper, 0.0, -sin)
    n_chunks = IN_WIDTH // 512
    for c in range(n_chunks):
        h = _dot(xb, w_ref[:, c * 512:(c + 1) * 512])
        if c < 2:
            slabs = []
            for s in range(4):
                sl = h[:, s * LANES:(s + 1) * LANES]
                slabs.append(sl * cos
                             + pltpu.roll(sl, HEAD_DIM // 2, 1) * sin_up
                             + pltpu.roll(sl, LANES - HEAD_DIM // 2, 1) * sin_lo)
            h = jnp.concatenate(slabs, axis=1)
        if c == 0 or c == 3:
            h = h * (HEAD_DIM ** -0.5 * LOG2E)
        o_ref[:, c * 512:(c + 1) * 512] = h.astype(BF16)


def _inproj_call(pos, invf, x2d, w_in, w_cast, tm):
    T = x2d.shape[0]
    steps = T // tm
    cast_specs = [pl.BlockSpec((w.shape[0] // steps, w.shape[1]), lambda i: (i, 0)) for w in w_cast]
    outs = pl.pallas_call(
        _inproj_kernel,
        grid=(steps,),
        in_specs=[
            pl.BlockSpec((tm // ROPE_PACK, LANES), lambda i: (i, 0)),
            pl.BlockSpec((1, LANES), lambda i: (0, 0)),
            pl.BlockSpec((tm, D_MODEL), lambda i: (i, 0)),
            _resident((D_MODEL, IN_WIDTH)),
        ] + cast_specs,
        out_specs=[pl.BlockSpec((tm, IN_WIDTH), lambda i: (i, 0))] + cast_specs,
        out_shape=[jax.ShapeDtypeStruct((T, IN_WIDTH), BF16)]
        + [jax.ShapeDtypeStruct(w.shape, BF16) for w in w_cast],
        scratch_shapes=[pltpu.VMEM((D_MODEL, IN_WIDTH), BF16)],
        compiler_params=_params(("arbitrary",)),
        name="inproj",
    )(pos, invf, x2d, w_in, *w_cast)
    return outs[0], outs[1:]


def _interleave(*generators):
    pending = list(generators)
    while pending:
        for g in list(pending):
            try:
                next(g)
            except StopIteration:
                pending.remove(g)


def _stack_halves(q):
    lane = lax.broadcasted_iota(jnp.int32, q.shape, 1)
    zero = jnp.zeros_like(q)
    return jnp.concatenate([jnp.where(lane < HEAD_DIM, q, zero),
                            jnp.where(lane >= HEAD_DIM, q, zero)], axis=0)


DA_TQ = 256
DA_SLOTS = 3
DA_HEADS = 2
VT_ONES = 16


def _fill_vt_ones(vt_ref, v):
    nv = v.shape[1]
    vt_ref[0:nv, :] = v.astype(F32).T.astype(vt_ref.dtype)
    vt_ref[nv:, :] = jnp.ones((vt_ref.shape[0] - nv, vt_ref.shape[1]), vt_ref.dtype)


def _diff_scores(nkb, q, k_ref, mask_ref, s_ref, m_ref, *, tq):
    qq = _stack_halves(q)
    m = None
    for j in range(nkb):
        s = _dot_nt(k_ref[j * tq:(j + 1) * tq, :], qq)
        if j == nkb - 1:
            s = s + mask_ref[...]
        s_ref[j * tq:(j + 1) * tq, :] = s
        mj = jnp.max(s, axis=0, keepdims=True)
        m = mj if m is None else jnp.maximum(m, mj)
        yield
    m_ref[...] = m


def _diff_output(nkb, s_ref, m_ref, vt_ref, lam_ref, g_ref, o_ref, *, tq, lambda_init):
    m = m_ref[...]
    acc = None
    for j in range(nkb):
        rows = slice(j * tq, (j + 1) * tq)
        part = _dot(vt_ref[:, rows], jnp.exp2(s_ref[rows, :] - m).astype(BF16))
        acc = part if acc is None else acc + part
        yield
    o = acc[:DIFF_V_DIM, :] * (1.0 / acc[DIFF_V_DIM:DIFF_V_DIM + 1, :])
    o = o[:, :tq] - lam_ref[...] * o[:, tq:]
    y = o * lax.rsqrt(jnp.mean(o * o, axis=0, keepdims=True) + LN_EPS) * g_ref[...]
    o_ref[...] = (y * (1.0 - lambda_init)).T.astype(o_ref.dtype)


def _diffattn_kernel(lam_ref, g_ref, mask_ref, q_ref, k_ref, v_ref, wf_ref, o_ref, wb_ref, vt_ref, s_ref,
                     m_ref, *, tq, nq, lambda_init):
    wb_ref[...] = wf_ref[...].astype(wb_ref.dtype)
    for hh in range(DA_HEADS):
        _fill_vt_ones(vt_ref.at[hh], v_ref[0, :, hh * LANES:(hh + 1) * LANES])
    stages = [(hh, c) for hh in range(DA_HEADS) for c in range(nq)]

    def scores(i):
        hh, c = stages[i]
        cols = slice(hh * LANES, (hh + 1) * LANES)
        return _diff_scores(c + 1, q_ref[0, c * tq:(c + 1) * tq, cols], k_ref.at[0, :, cols], mask_ref,
                            s_ref.at[i % DA_SLOTS], m_ref.at[i % DA_SLOTS], tq=tq)

    for i in range(DA_SLOTS - 1):
        _interleave(scores(i))
    for i, (hh, c) in enumerate(stages):
        nxt = [scores(i + DA_SLOTS - 1)] if i + DA_SLOTS - 1 < len(stages) else []
        _interleave(*nxt, _diff_output(c + 1, s_ref.at[i % DA_SLOTS], m_ref.at[i % DA_SLOTS], vt_ref.at[hh],
                                       lam_ref, g_ref,
                                       o_ref.at[0, c * tq:(c + 1) * tq, hh * LANES:(hh + 1) * LANES], tq=tq,
                                       lambda_init=lambda_init))


def _diffattn_call(lam, g, h3, w_cast, tq, lambda_init):
    B, S, _ = h3.shape
    nq = S // tq
    wrows, wcols = w_cast.shape
    ngroups = N_HEADS_DIFF // DA_HEADS
    width = DA_HEADS * LANES
    wspec = pl.BlockSpec((wrows // (B * ngroups), wcols), lambda b, h: (b * ngroups + h, 0))
    key_chunk = np.arange(tq)[:, None] // CHUNK
    query_chunk = (np.arange(2 * tq)[None, :] % tq) // CHUNK
    diag_mask = jnp.asarray(np.where(key_chunk <= query_chunk, 0.0, NEG_INF), F32)
    return pl.pallas_call(
        functools.partial(_diffattn_kernel, tq=tq, nq=nq, lambda_init=lambda_init),
        grid=(B, ngroups),
        scratch_shapes=[pltpu.VMEM((DA_HEADS, DIFF_V_DIM + VT_ONES, S), BF16),
                        pltpu.VMEM((DA_SLOTS, S, 2 * tq), F32),
                        pltpu.VMEM((DA_SLOTS, 1, 2 * tq), F32)],
        in_specs=[
            pl.BlockSpec((1, tq), lambda b, h: (0, 0)),
            pl.BlockSpec((DIFF_V_DIM, 1), lambda b, h: (0, 0)),
            pl.BlockSpec((tq, 2 * tq), lambda b, h: (0, 0)),
            pl.BlockSpec((1, S, width), lambda b, h: (b, 0, QA_BLK // DA_HEADS + h)),
            pl.BlockSpec((1, S, width), lambda b, h: (b, 0, KA_BLK // DA_HEADS + h)),
            pl.BlockSpec((1, S, width), lambda b, h: (b, 0, VA_BLK // DA_HEADS + h)),
            wspec,
        ],
        out_specs=[pl.BlockSpec((1, S, width), lambda b, h: (b, 0, h)), wspec],
        out_shape=[jax.ShapeDtypeStruct((B, S, N_HEADS_DIFF * DIFF_V_DIM), BF16),
                   jax.ShapeDtypeStruct(w_cast.shape, BF16)],
        compiler_params=_params(("arbitrary", "arbitrary")),
        name="diffattn",
    )(lam, g, diag_mask, h3, h3, h3, w_cast)


CA_TQ = 256
CA_NKB = 3
CA_SLOTS = 3
CA_PAIRS = 2


def _chunk_bias_table(rel_bias):
    nh = rel_bias.shape[0]
    n_neg = CA_TQ - REL_CLIP + 1
    n_far = CA_NKB * CA_TQ - REL_CLIP
    by_dist = jnp.concatenate([
        jnp.broadcast_to(rel_bias[:, :1], (nh, n_neg)),
        rel_bias[:, 1:2 * REL_CLIP],
        jnp.broadcast_to(rel_bias[:, 2 * REL_CLIP:], (nh, n_far)),
    ], axis=1).astype(F32) * LOG2E
    assert by_dist.shape[1] == (CA_NKB + 1) * CA_TQ
    return by_dist.reshape(nh // 2, 2, (CA_NKB + 1) * CA_TQ)


def _build_bias_table(dist_ref, bias_ref):
    row = lax.broadcasted_iota(jnp.int32, (CA_TQ, CA_TQ), 0) // CHUNK
    col = lax.broadcasted_iota(jnp.int32, (CA_TQ, CA_TQ), 1) // CHUNK
    for t in range(2):
        for j in range(CA_NKB):
            d0 = (CA_NKB - 1 - j) * CA_TQ
            g = jnp.concatenate([dist_ref[t:t + 1, d0 + CA_TQ:d0 + 2 * CA_TQ],
                                 dist_ref[t:t + 1, d0:d0 + CA_TQ]], axis=1)
            rolled = pltpu.roll(jnp.broadcast_to(g, (CA_TQ, 2 * CA_TQ)), 0, 1, stride=1, stride_axis=0)
            blk = rolled[:, :CA_TQ]
            if j == 0:
                blk = jnp.where(row >= col, blk, NEG_INF)
            elif j == CA_NKB - 1:
                blk = jnp.where(row <= col, blk, NEG_INF)
            bias_ref[j * CA_TQ:(j + 1) * CA_TQ, t * CA_TQ:(t + 1) * CA_TQ] = blk


def _chunkattn_kernel(dist_ref, q_ref, k_ref, v_ref, wf_ref, o_ref, wb_ref, bias_ref, vt_ref, s_ref, m_ref):
    nq = q_ref.shape[1] // CA_TQ
    wb_ref[...] = wf_ref[...].astype(wb_ref.dtype)

    @pl.when(pl.program_id(1) == 0)
    def _():
        for pp in range(CA_PAIRS):
            _build_bias_table(dist_ref.at[pp], bias_ref.at[pp])

    def key_blocks(blk):
        return [(j, blk - (CA_NKB - 1) + j) for j in range(CA_NKB) if blk - (CA_NKB - 1) + j >= 0]

    def scores(pp, blk, slot):
        cols = slice(pp * LANES, (pp + 1) * LANES)
        qq = _stack_halves(q_ref[0, blk * CA_TQ:(blk + 1) * CA_TQ, cols])
        m = None
        for j, kb in key_blocks(blk):
            rows = slice(j * CA_TQ, (j + 1) * CA_TQ)
            s = _dot_nt(k_ref[0, kb * CA_TQ:(kb + 1) * CA_TQ, cols], qq) + bias_ref[pp, rows, :]
            s_ref[slot, rows, :] = s
            mj = jnp.max(s, axis=0, keepdims=True)
            m = mj if m is None else jnp.maximum(m, mj)
            yield
        m_ref[slot] = m

    def output(pp, blk, slot):
        m = m_ref[slot]
        acc = None
        for j, kb in key_blocks(blk):
            p = jnp.exp2(s_ref[slot, j * CA_TQ:(j + 1) * CA_TQ, :] - m).astype(BF16)
            part = _dot(vt_ref[pp, :, kb * CA_TQ:(kb + 1) * CA_TQ], p)
            acc = part if acc is None else acc + part
            yield
        inv_l = 1.0 / acc[LANES:LANES + 1, :]
        o_t = jnp.concatenate([acc[:HEAD_DIM, :CA_TQ] * inv_l[:, :CA_TQ],
                               acc[HEAD_DIM:LANES, CA_TQ:] * inv_l[:, CA_TQ:]], axis=0)
        o_ref[0, blk * CA_TQ:(blk + 1) * CA_TQ, pp * LANES:(pp + 1) * LANES] = o_t.T.astype(o_ref.dtype)

    for pp in range(CA_PAIRS):
        _fill_vt_ones(vt_ref.at[pp], v_ref[0, :, pp * LANES:(pp + 1) * LANES])
    stages = [(pp, blk) for pp in range(CA_PAIRS) for blk in range(nq)]
    ahead = CA_SLOTS - 1
    for i in range(ahead):
        _interleave(scores(*stages[i], i % CA_SLOTS))
    for i, stage in enumerate(stages):
        nxt = [scores(*stages[i + ahead], (i + ahead) % CA_SLOTS)] if i + ahead < len(stages) else []
        _interleave(*nxt, output(*stage, i % CA_SLOTS))


def _chunkattn_call(bias, h3, w_cast):
    B, S, _ = h3.shape
    ngroups = N_HEADS_CHUNK // 2 // CA_PAIRS
    width = CA_PAIRS * LANES
    wrows, wcols = w_cast.shape
    wspec = pl.BlockSpec((wrows // (ngroups * B), wcols), lambda p, b: (p * B + b, 0))
    return pl.pallas_call(
        _chunkattn_kernel,
        grid=(ngroups, B),
        in_specs=[
            pl.BlockSpec((CA_PAIRS, 2, (CA_NKB + 1) * CA_TQ), lambda p, b: (p, 0, 0)),
            pl.BlockSpec((1, S, width), lambda p, b: (b, 0, QB_BLK // CA_PAIRS + p)),
            pl.BlockSpec((1, S, width), lambda p, b: (b, 0, KB_BLK // CA_PAIRS + p)),
            pl.BlockSpec((1, S, width), lambda p, b: (b, 0, VB_BLK // CA_PAIRS + p)),
            wspec,
        ],
        out_specs=[pl.BlockSpec((1, S, width), lambda p, b: (b, 0, p)), wspec],
        out_shape=[jax.ShapeDtypeStruct((B, S, N_HEADS_CHUNK * HEAD_DIM), BF16),
                   jax.ShapeDtypeStruct(w_cast.shape, BF16)],
        scratch_shapes=[pltpu.VMEM((CA_PAIRS, CA_NKB * CA_TQ, 2 * CA_TQ), F32),
                        pltpu.VMEM((CA_PAIRS, LANES + VT_ONES, S), BF16),
                        pltpu.VMEM((CA_SLOTS, CA_NKB * CA_TQ, 2 * CA_TQ), F32),
                        pltpu.VMEM((CA_SLOTS, 1, 2 * CA_TQ), F32)],
        compiler_params=_params(("arbitrary", "arbitrary")),
        name="chunkattn",
    )(bias, h3, h3, h3, w_cast)


def _mid_kernel(x_ref, ya_ref, yb_ref, mem_ref, wo_ref, wq_ref, wk_ref, wv_ref, wmo_ref,
                g1_ref, b1_ref, g2_ref, b2_ref, x2_ref, k_ref, v_ref, x1_ref):
    hd = MEM_HEAD_DIM
    half = ya_ref.shape[2]

    @pl.when(pl.program_id(1) == 0)
    def _():
        mb = mem_ref[0].astype(BF16)
        k_ref[...] = _dot(mb, wk_ref[...]).astype(BF16)
        v_ref[...] = _dot(mb, wv_ref[...]).astype(BF16)

    tiles = _sub_tiles(x_ref.shape[1], 256)
    ys = [_dot(ya_ref[0, rows, :], wo_ref[0:half, :]) + _dot(yb_ref[0, rows, :], wo_ref[half:, :])
          for rows in tiles]
    qs = []
    for rows, y in zip(tiles, ys):
        x1 = _layer_norm(DEEPNORM_ALPHA * x_ref[0, rows, :] + y, g1_ref[...], b1_ref[...])
        x1_ref[rows, :] = x1
        q = _dot(x1.astype(BF16), wq_ref[...]) * (hd ** -0.5 * LOG2E)
        qs.append(q.astype(BF16))
    attn = []
    for q in qs:
        outs = []
        for h in range(N_HEADS_MEM):
            sl = slice(h * hd, (h + 1) * hd)
            s = _dot_nt(q[:, sl], k_ref[:, sl])
            p = jnp.exp2(s - jnp.max(s, axis=1, keepdims=True))
            l = jnp.sum(p, axis=1, keepdims=True)
            outs.append((_dot(p.astype(BF16), v_ref[:, sl]) / l).astype(BF16))
        attn.append(jnp.concatenate(outs, axis=1))
    y2s = [_dot(o, wmo_ref[...]) for o in attn]
    for rows, y2 in zip(tiles, y2s):
        x2_ref[0, rows, :] = _layer_norm(DEEPNORM_ALPHA * x1_ref[rows, :] + y2, g2_ref[...], b2_ref[...])


def _mid_call(x, ya, yb, mem, w_o, w_mq, w_mk, w_mv, w_mo, g1, b1, g2, b2, tm):
    B, S, _ = x.shape
    M = mem.shape[1]
    half = ya.shape[2]
    row = pl.BlockSpec((1, tm, D_MODEL), lambda bb, i: (bb, i, 0))
    hrow = pl.BlockSpec((1, tm, half), lambda bb, i: (bb, i, 0))
    memblk = pl.BlockSpec((1, M, D_MODEL), lambda bb, i: (bb, 0, 0))
    vec = pl.BlockSpec((1, D_MODEL), lambda bb, i: (0, 0))
    wspec = _resident((D_MODEL, D_MODEL))
    return pl.pallas_call(
        _mid_kernel,
        grid=(B, S // tm),
        in_specs=[row, hrow, hrow, memblk] + [wspec] * 5 + [vec] * 4,
        out_specs=row,
        out_shape=jax.ShapeDtypeStruct((B, S, D_MODEL), F32),
        scratch_shapes=[pltpu.VMEM((M, D_MODEL), BF16)] * 2 + [pltpu.VMEM((tm, D_MODEL), F32)],
        compiler_params=_params(("arbitrary", "arbitrary")),
        name="mid",
    )(x, ya, yb, mem, w_o, w_mq, w_mk, w_mv, w_mo, g1, b1, g2, b2)


def _mlp_kernel(x_ref, wu_ref, wd_ref, g_ref, b_ref, o_ref, *, ff_chunk):
    def finish(rows, y):
        o_ref[rows, :] = _layer_norm(DEEPNORM_ALPHA * x_ref[rows, :] + y, g_ref[...], b_ref[...])

    pending = None
    for rows in _sub_tiles(x_ref.shape[0], 256):
        xb = x_ref[rows, :].astype(BF16)
        y = None
        for c in range(D_FF // ff_chunk):
            sl = slice(c * ff_chunk, (c + 1) * ff_chunk)
            h = jnp.maximum(_dot(xb, wu_ref[:, sl]), 0.0)
            part = _dot((h * h).astype(BF16), wd_ref[sl, :])
            y = part if y is None else y + part
            if c == 0 and pending is not None:
                finish(*pending)
                pending = None
        pending = (rows, y)
    finish(*pending)


def _mlp_call(x2d, w_up, w_down, g, b, tm, ff_chunk):
    T = x2d.shape[0]
    row = pl.BlockSpec((tm, D_MODEL), lambda i: (i, 0))
    vec = pl.BlockSpec((1, D_MODEL), lambda i: (0, 0))
    return pl.pallas_call(
        functools.partial(_mlp_kernel, ff_chunk=ff_chunk),
        grid=(T // tm,),
        in_specs=[row,
                  pl.BlockSpec((D_MODEL, D_FF), lambda i: (0, 0), pipeline_mode=pl.Buffered(1)),
                  pl.BlockSpec((D_FF, D_MODEL), lambda i: (0, 0), pipeline_mode=pl.Buffered(1)),
                  vec, vec],
        out_specs=row,
        out_shape=jax.ShapeDtypeStruct((T, D_MODEL), F32),
        compiler_params=_params(("arbitrary",)),
        name="mlp",
    )(x2d, w_up, w_down, g, b)


def kernel(x, mem, positions, w_in, diff_lambda, subln_g, rel_bias, w_o, ln1_g, ln1_b,
           w_mq, w_mk, w_mv, w_mo, ln2_g, ln2_b, w_up, w_down, ln3_g, ln3_b):
    B, S, D = x.shape
    T = B * S
    depth = w_in.shape[0]
    assert depth == DEPTH and D == D_MODEL and S % CA_TQ == 0
    inv_freq = 1.0 / (ROPE_THETA ** (jnp.arange(0, HEAD_DIM, 2, dtype=F32) / HEAD_DIM))
    invf = jnp.tile(inv_freq, LANES // (HEAD_DIM // 2)).reshape(1, LANES)
    pos = positions.reshape(T // IN_TM, ROPE_PACK, IN_TM // ROPE_PACK).transpose(0, 2, 1)
    pos = jnp.repeat(pos, HEAD_DIM // 2, axis=2).reshape(T // ROPE_PACK, LANES)
    vec = lambda a: a.reshape(1, -1)

    for l in range(depth):
        lambda_init = 0.8 - 0.6 * math.exp(-0.3 * l)
        lam = _lam_call(diff_lambda[l], lambda_init, DA_TQ)
        h, mid_w = _inproj_call(pos, invf, x.reshape(T, D), w_in[l],
                                [w_o[l], w_mq[l], w_mk[l], w_mv[l], w_mo[l]], tm=IN_TM)
        h3 = h.reshape(B, S, IN_WIDTH)
        ya, w_up_b = _diffattn_call(lam, subln_g[l].reshape(-1, 1), h3, w_up[l], tq=DA_TQ,
                                    lambda_init=lambda_init)
        yb, w_down_b = _chunkattn_call(_chunk_bias_table(rel_bias[l]), h3, w_down[l])
        x2 = _mid_call(x, ya, yb, mem, *mid_w, vec(ln1_g[l]), vec(ln1_b[l]), vec(ln2_g[l]), vec(ln2_b[l]),
                       tm=1024)
        out = _mlp_call(x2.reshape(T, D), w_up_b, w_down_b,
                        vec(ln3_g[l]), vec(ln3_b[l]), tm=1024, ff_chunk=1024)
        x = out.reshape(B, S, D)
    return x
```

```python
import functools
import math

import jax
import jax.numpy as jnp
import numpy as np
from jax import lax
from jax.experimental import pallas as pl
from jax.experimental.pallas import tpu as pltpu

D_MODEL = 1024
CHUNK = 64
HEAD_DIM = 64
N_HEADS_DIFF = 4
DIFF_V_DIM = 128
N_HEADS_CHUNK = 8
LEFT_CHUNKS = 8
REL_CLIP = 128
N_HEADS_MEM = 4
MEM_HEAD_DIM = 256
D_FF = 4096
ROPE_THETA = 10000.0
LN_EPS = 1e-5
NEG_INF = -1e30
LOG2E = math.log2(math.e)
DEPTH = 1
DEEPNORM_ALPHA = (2.0 * DEPTH) ** 0.25
IN_WIDTH = 3072

LANES = 128
VMEM_LIMIT = 56 * 1024 * 1024

F32 = jnp.float32
BF16 = jnp.bfloat16

QA_BLK, KA_BLK, VA_BLK, QB_BLK, KB_BLK, VB_BLK = 0, 4, 8, 12, 16, 20


def _params(sem):
    return pltpu.CompilerParams(dimension_semantics=sem, vmem_limit_bytes=VMEM_LIMIT)


def _dot(a, b):
    return jnp.dot(a, b, preferred_element_type=F32)


def _dot_nt(a, b):
    return lax.dot_general(a, b, (((1,), (1,)), ((), ())), preferred_element_type=F32)


SUB_ROWS = 512


def _sub_tiles(n_rows, sub_rows=SUB_ROWS):
    return [slice(r, r + sub_rows) for r in range(0, n_rows, sub_rows)]


def _resident(shape):
    return pl.BlockSpec(shape, lambda *_: (0,) * len(shape), pipeline_mode=pl.Buffered(1))


def _cast_weights_once(first_step, pairs):
    @pl.when(first_step)
    def _():
        for src_ref, dst_ref in pairs:
            dst_ref[...] = src_ref[...].astype(dst_ref.dtype)


def _layer_norm(z, g, b):
    mu = jnp.mean(z, axis=-1, keepdims=True)
    zc = z - mu
    var = jnp.mean(zc * zc, axis=-1, keepdims=True)
    return zc * lax.rsqrt(var + LN_EPS) * g + b


def _lam_kernel(dl_ref, o_ref, *, lambda_init):
    dl = dl_ref[...]
    s1 = jnp.sum(dl[0:1, :] * dl[1:2, :], axis=1, keepdims=True)
    s2 = jnp.sum(dl[2:3, :] * dl[3:4, :], axis=1, keepdims=True)
    lam = jnp.exp(s1) - jnp.exp(s2) + lambda_init
    o_ref[...] = jnp.broadcast_to(lam, o_ref.shape)


def _lam_call(diff_lambda, lambda_init, width):
    return pl.pallas_call(
        functools.partial(_lam_kernel, lambda_init=lambda_init),
        out_shape=jax.ShapeDtypeStruct((1, width), F32),
        name="lam",
    )(diff_lambda)


IN_TM = 512
ROPE_PACK = LANES // (HEAD_DIM // 2)


def _spread_token_groups(t):
    n = t.shape[0]
    grp = lax.broadcasted_iota(jnp.int32, t.shape, 1) // (HEAD_DIM // 2)
    out = []
    for a in range(ROPE_PACK):
        spread = t
        for g in range(ROPE_PACK):
            if g != a:
                moved = pltpu.roll(t, ((g - a) % ROPE_PACK) * (HEAD_DIM // 2), 1)
                spread = jnp.where(grp == g, moved, spread)
        out.append(spread)
    return jnp.concatenate(out, axis=0)


def _inproj_kernel(pos_ref, invf_ref, x_ref, wf_ref, *refs):
    n_cast = (len(refs) - 2) // 2
    cast_in, o_ref, cast_out, w_ref = refs[:n_cast], refs[n_cast], refs[n_cast + 1:-1], refs[-1]
    for src_ref, dst_ref in zip(cast_in, cast_out):
        dst_ref[...] = src_ref[...].astype(dst_ref.dtype)
    _cast_weights_once(pl.program_id(0) == 0, [(wf_ref, w_ref)])
    xb = x_ref[...].astype(BF16)
    ang = pos_ref[...].astype(F32) * invf_ref[...]
    cos = _spread_token_groups(jnp.cos(ang))
    sin = _spread_token_groups(jnp.sin(ang))
    lane = lax.broadcasted_iota(jnp.int32, cos.shape, 1)
    upper = (lane % HEAD_DIM) >= (HEAD_DIM // 2)
    sin_up = jnp.where(upper, sin, 0.0)
    sin_lo = jnp.where(upper, 0.0, -sin)
    n_chunks = IN_WIDTH // 512
    for c in range(n_chunks):
        h = _dot(xb, w_ref[:, c * 512:(c + 1) * 512])
        if c < 2:
            slabs = []
            for s in range(4):
                sl = h[:, s * LANES:(s + 1) * LANES]
                slabs.append(sl * cos
                             + pltpu.roll(sl, HEAD_DIM // 2, 1) * sin_up
                             + pltpu.roll(sl, LANES - HEAD_DIM // 2, 1) * sin_lo)
            h = jnp.concatenate(slabs, axis=1)
        if c == 0 or c == 3:
            h = h * (HEAD_DIM ** -0.5 * LOG2E)
        o_ref[:, c * 512:(c + 1) * 512] = h.astype(BF16)


def _inproj_call(pos, invf, x2d, w_in, w_cast, tm):
    T = x2d.shape[0]
    steps = T // tm
    cast_specs = [pl.BlockSpec((w.shape[0] // steps, w.shape[1]), lambda i: (i, 0)) for w in w_cast]
    outs = pl.pallas_call(
        _inproj_kernel,
        grid=(steps,),
        in_specs=[
            pl.BlockSpec((tm // ROPE_PACK, LANES), lambda i: (i, 0)),
            pl.BlockSpec((1, LANES), lambda i: (0, 0)),
            pl.BlockSpec((tm, D_MODEL), lambda i: (i, 0)),
            _resident((D_MODEL, IN_WIDTH)),
        ] + cast_specs,
        out_specs=[pl.BlockSpec((tm, IN_WIDTH), lambda i: (i, 0))] + cast_specs,
        out_shape=[jax.ShapeDtypeStruct((T, IN_WIDTH), BF16)]
        + [jax.ShapeDtypeStruct(w.shape, BF16) for w in w_cast],
        scratch_shapes=[pltpu.VMEM((D_MODEL, IN_WIDTH), BF16)],
        compiler_params=_params(("arbitrary",)),
        name="inproj",
    )(pos, invf, x2d, w_in, *w_cast)
    return outs[0], outs[1:]


def _interleave(*generators):
    pending = list(generators)
    while pending:
        for g in list(pending):
            try:
                next(g)
            except StopIteration:
                pending.remove(g)


def _wavefront(generators, lag=1):
    pending = dict(enumerate(generators))
    wave = 0
    while pending:
        for t in sorted(pending, reverse=True):
            if wave >= t * lag:
                try:
                    next(pending[t])
                except StopIteration:
                    del pending[t]
        wave += 1


def _stack_halves(q):
    lane = lax.broadcasted_iota(jnp.int32, q.shape, 1)
    zero = jnp.zeros_like(q)
    return jnp.concatenate([jnp.where(lane < HEAD_DIM, q, zero),
                            jnp.where(lane >= HEAD_DIM, q, zero)], axis=0)


DA_TQ = 256
DA_SLOTS = 3
DA_HEADS = 2
VT_ONES = 16


def _fill_vt_ones(vt_ref, v):
    nv = v.shape[1]
    vt_ref[0:nv, :] = v.astype(F32).T.astype(vt_ref.dtype)
    vt_ref[nv:, :] = jnp.ones((vt_ref.shape[0] - nv, vt_ref.shape[1]), vt_ref.dtype)


def _diag_tiles(tq):
    assert tq == 2 * LANES
    return [(rh, ct) for rh in range(tq // LANES) for ct in range(2 * tq // LANES)
            if not (rh == 1 and ct % 2 == 0)]


def _diff_scores(nkb, q, k_ref, mask_ref, s_ref, m_ref, *, tq):
    qq = _stack_halves(q)
    m = None
    for j in range(nkb):
        s = _dot_nt(k_ref[j * tq:(j + 1) * tq, :], qq)
        if j < nkb - 1:
            s_ref[j * tq:(j + 1) * tq, :] = s
            mj = jnp.max(s, axis=0, keepdims=True)
        else:
            m_tiles = [None] * (2 * tq // LANES)
            for rh, ct in _diag_tiles(tq):
                rows = slice(rh * LANES, (rh + 1) * LANES)
                lanes = slice(ct * LANES, (ct + 1) * LANES)
                st = s[rows, lanes] + mask_ref[rows, lanes]
                s_ref[j * tq + rh * LANES:j * tq + (rh + 1) * LANES, lanes] = st
                mt = jnp.max(st, axis=0, keepdims=True)
                m_tiles[ct] = mt if m_tiles[ct] is None else jnp.maximum(m_tiles[ct], mt)
            mj = jnp.concatenate(m_tiles, axis=1)
        m = mj if m is None else jnp.maximum(m, mj)
        yield
    m_ref[...] = m


def _diff_output(nkb, s_ref, m_ref, vt_ref, lam_ref, g_ref, o_ref, *, tq, lambda_init):
    m = m_ref[...]
    acc = None
    for j in range(nkb):
        rows = slice(j * tq, (j + 1) * tq)
        if j < nkb - 1:
            p = jnp.exp2(s_ref[rows, :] - m).astype(BF16)
        else:
            vis = _diag_tiles(tq)
            p = jnp.concatenate([jnp.concatenate([
                jnp.exp2(s_ref[j * tq + rh * LANES:j * tq + (rh + 1) * LANES, ct * LANES:(ct + 1) * LANES]
                         - m[:, ct * LANES:(ct + 1) * LANES]).astype(BF16)
                if (rh, ct) in vis else jnp.zeros((LANES, LANES), BF16)
                for ct in range(2 * tq // LANES)], axis=1) for rh in range(tq // LANES)], axis=0)
        part = _dot(vt_ref[:, rows], p)
        acc = part if acc is None else acc + part
        yield
    o = acc[:DIFF_V_DIM, :] * (1.0 / acc[DIFF_V_DIM:DIFF_V_DIM + 1, :])
    o = o[:, :tq] - lam_ref[...] * o[:, tq:]
    y = o * lax.rsqrt(jnp.mean(o * o, axis=0, keepdims=True) + LN_EPS) * g_ref[...]
    o_ref[...] = (y * (1.0 - lambda_init)).T.astype(o_ref.dtype)


def _diffattn_kernel(lam_ref, g_ref, mask_ref, q_ref, k_ref, v_ref, wf_ref, o_ref, wb_ref, vt_ref, s_ref,
                     m_ref, *, tq, nq, lambda_init):
    wb_ref[...] = wf_ref[...].astype(wb_ref.dtype)
    for hh in range(DA_HEADS):
        _fill_vt_ones(vt_ref.at[hh], v_ref[0, :, hh * LANES:(hh + 1) * LANES])
    stages = [(hh, c) for hh in range(DA_HEADS) for c in range(nq)]

    def scores(i):
        hh, c = stages[i]
        cols = slice(hh * LANES, (hh + 1) * LANES)
        return _diff_scores(c + 1, q_ref[0, c * tq:(c + 1) * tq, cols], k_ref.at[0, :, cols], mask_ref,
                            s_ref.at[i % DA_SLOTS], m_ref.at[i % DA_SLOTS], tq=tq)

    for i in range(DA_SLOTS - 1):
        _interleave(scores(i))
    for i, (hh, c) in enumerate(stages):
        nxt = [scores(i + DA_SLOTS - 1)] if i + DA_SLOTS - 1 < len(stages) else []
        _interleave(*nxt, _diff_output(c + 1, s_ref.at[i % DA_SLOTS], m_ref.at[i % DA_SLOTS], vt_ref.at[hh],
                                       lam_ref, g_ref,
                                       o_ref.at[0, c * tq:(c + 1) * tq, hh * LANES:(hh + 1) * LANES], tq=tq,
                                       lambda_init=lambda_init))


def _diffattn_call(lam, g, h3, w_cast, tq, lambda_init):
    B, S, _ = h3.shape
    nq = S // tq
    wrows, wcols = w_cast.shape
    ngroups = N_HEADS_DIFF // DA_HEADS
    width = DA_HEADS * LANES
    wspec = pl.BlockSpec((wrows // (B * ngroups), wcols), lambda b, h: (b * ngroups + h, 0))
    key_chunk = np.arange(tq)[:, None] // CHUNK
    query_chunk = (np.arange(2 * tq)[None, :] % tq) // CHUNK
    diag_mask = jnp.asarray(np.where(key_chunk <= query_chunk, 0.0, NEG_INF), F32)
    return pl.pallas_call(
        functools.partial(_diffattn_kernel, tq=tq, nq=nq, lambda_init=lambda_init),
        grid=(B, ngroups),
        scratch_shapes=[pltpu.VMEM((DA_HEADS, DIFF_V_DIM + VT_ONES, S), BF16),
                        pltpu.VMEM((DA_SLOTS, S, 2 * tq), F32),
                        pltpu.VMEM((DA_SLOTS, 1, 2 * tq), F32)],
        in_specs=[
            pl.BlockSpec((1, tq), lambda b, h: (0, 0)),
            pl.BlockSpec((DIFF_V_DIM, 1), lambda b, h: (0, 0)),
            pl.BlockSpec((tq, 2 * tq), lambda b, h: (0, 0)),
            pl.BlockSpec((1, S, width), lambda b, h: (b, 0, QA_BLK // DA_HEADS + h)),
            pl.BlockSpec((1, S, width), lambda b, h: (b, 0, KA_BLK // DA_HEADS + h)),
            pl.BlockSpec((1, S, width), lambda b, h: (b, 0, VA_BLK // DA_HEADS + h)),
            wspec,
        ],
        out_specs=[pl.BlockSpec((1, S, width), lambda b, h: (b, 0, h)), wspec],
        out_shape=[jax.ShapeDtypeStruct((B, S, N_HEADS_DIFF * DIFF_V_DIM), BF16),
                   jax.ShapeDtypeStruct(w_cast.shape, BF16)],
        compiler_params=_params(("arbitrary", "arbitrary")),
        name="diffattn",
    )(lam, g, diag_mask, h3, h3, h3, w_cast)


CA_TQ = 256
CA_NKB = 3
CA_SLOTS = 3
CA_PAIRS = 2


def _chunk_bias_table(rel_bias):
    nh = rel_bias.shape[0]
    n_neg = CA_TQ - REL_CLIP + 1
    n_far = CA_NKB * CA_TQ - REL_CLIP
    by_dist = jnp.concatenate([
        jnp.broadcast_to(rel_bias[:, :1], (nh, n_neg)),
        rel_bias[:, 1:2 * REL_CLIP],
        jnp.broadcast_to(rel_bias[:, 2 * REL_CLIP:], (nh, n_far)),
    ], axis=1).astype(F32) * LOG2E
    assert by_dist.shape[1] == (CA_NKB + 1) * CA_TQ
    return by_dist.reshape(nh // 2, 2, (CA_NKB + 1) * CA_TQ)


def _build_bias_table(dist_ref, bias_ref):
    row = lax.broadcasted_iota(jnp.int32, (CA_TQ, CA_TQ), 0) // CHUNK
    col = lax.broadcasted_iota(jnp.int32, (CA_TQ, CA_TQ), 1) // CHUNK
    for t in range(2):
        for j in range(CA_NKB):
            d0 = (CA_NKB - 1 - j) * CA_TQ
            g = jnp.concatenate([dist_ref[t:t + 1, d0 + CA_TQ:d0 + 2 * CA_TQ],
                                 dist_ref[t:t + 1, d0:d0 + CA_TQ]], axis=1)
            rolled = pltpu.roll(jnp.broadcast_to(g, (CA_TQ, 2 * CA_TQ)), 0, 1, stride=1, stride_axis=0)
            blk = rolled[:, :CA_TQ]
            if j == 0:
                blk = jnp.where(row >= col, blk, NEG_INF)
            elif j == CA_NKB - 1:
                blk = jnp.where(row <= col, blk, NEG_INF)
            bias_ref[j * CA_TQ:(j + 1) * CA_TQ, t * CA_TQ:(t + 1) * CA_TQ] = blk


def _chunkattn_kernel(dist_ref, q_ref, k_ref, v_ref, wf_ref, o_ref, wb_ref, bias_ref, vt_ref, s_ref, m_ref):
    nq = q_ref.shape[1] // CA_TQ
    wb_ref[...] = wf_ref[...].astype(wb_ref.dtype)

    @pl.when(pl.program_id(1) == 0)
    def _():
        for pp in range(CA_PAIRS):
            _build_bias_table(dist_ref.at[pp], bias_ref.at[pp])

    def key_blocks(blk):
        return [(j, blk - (CA_NKB - 1) + j) for j in range(CA_NKB) if blk - (CA_NKB - 1) + j >= 0]

    def visible_tiles(j):
        tiles = [(rh, ct) for rh in range(CA_TQ // LANES) for ct in range(2 * CA_TQ // LANES)]
        if j == 0:
            return [(rh, ct) for rh, ct in tiles if not (rh == 0 and ct % 2 == 1)]
        if j == CA_NKB - 1:
            return [(rh, ct) for rh, ct in tiles if not (rh == 1 and ct % 2 == 0)]
        return tiles

    def scores(pp, blk, slot):
        cols = slice(pp * LANES, (pp + 1) * LANES)
        qq = _stack_halves(q_ref[0, blk * CA_TQ:(blk + 1) * CA_TQ, cols])
        m = [None] * (2 * CA_TQ // LANES)
        for j, kb in key_blocks(blk):
            s = _dot_nt(k_ref[0, kb * CA_TQ:(kb + 1) * CA_TQ, cols], qq)
            for rh, ct in visible_tiles(j):
                rows = slice(j * CA_TQ + rh * LANES, j * CA_TQ + (rh + 1) * LANES)
                lanes = slice(ct * LANES, (ct + 1) * LANES)
                st = s[rh * LANES:(rh + 1) * LANES, lanes] + bias_ref[pp, rows, lanes]
                s_ref[slot, rows, lanes] = st
                mt = jnp.max(st, axis=0, keepdims=True)
                m[ct] = mt if m[ct] is None else jnp.maximum(m[ct], mt)
            yield
        m_ref[slot] = jnp.concatenate(m, axis=1)

    def output(pp, blk, slot):
        m = m_ref[slot]
        acc = None
        for j, kb in key_blocks(blk):
            vis = visible_tiles(j)
            p_rows = []
            for rh in range(CA_TQ // LANES):
                p_tiles = []
                for ct in range(2 * CA_TQ // LANES):
                    if (rh, ct) in vis:
                        rows = slice(j * CA_TQ + rh * LANES, j * CA_TQ + (rh + 1) * LANES)
                        lanes = slice(ct * LANES, (ct + 1) * LANES)
                        p_tiles.append(jnp.exp2(s_ref[slot, rows, lanes] - m[:, lanes]).astype(BF16))
                    else:
                        p_tiles.append(jnp.zeros((LANES, LANES), BF16))
                p_rows.append(jnp.concatenate(p_tiles, axis=1))
            p = jnp.concatenate(p_rows, axis=0)
            part = _dot(vt_ref[pp, :, kb * CA_TQ:(kb + 1) * CA_TQ], p)
            acc = part if acc is None else acc + part
            yield
        inv_l = 1.0 / acc[LANES:LANES + 1, :]
        o_t = jnp.concatenate([acc[:HEAD_DIM, :CA_TQ] * inv_l[:, :CA_TQ],
                               acc[HEAD_DIM:LANES, CA_TQ:] * inv_l[:, CA_TQ:]], axis=0)
        o_ref[0, blk * CA_TQ:(blk + 1) * CA_TQ, pp * LANES:(pp + 1) * LANES] = o_t.T.astype(o_ref.dtype)

    for pp in range(CA_PAIRS):
        _fill_vt_ones(vt_ref.at[pp], v_ref[0, :, pp * LANES:(pp + 1) * LANES])
    stages = [(pp, blk) for pp in range(CA_PAIRS) for blk in range(nq)]
    ahead = CA_SLOTS - 1
    for i in range(ahead):
        _interleave(scores(*stages[i], i % CA_SLOTS))
    for i, stage in enumerate(stages):
        nxt = [scores(*stages[i + ahead], (i + ahead) % CA_SLOTS)] if i + ahead < len(stages) else []
        _interleave(*nxt, output(*stage, i % CA_SLOTS))


def _chunkattn_call(bias, h3, w_cast):
    B, S, _ = h3.shape
    ngroups = N_HEADS_CHUNK // 2 // CA_PAIRS
    width = CA_PAIRS * LANES
    wrows, wcols = w_cast.shape
    wspec = pl.BlockSpec((wrows // (ngroups * B), wcols), lambda p, b: (p * B + b, 0))
    return pl.pallas_call(
        _chunkattn_kernel,
        grid=(ngroups, B),
        in_specs=[
            pl.BlockSpec((CA_PAIRS, 2, (CA_NKB + 1) * CA_TQ), lambda p, b: (p, 0, 0)),
            pl.BlockSpec((1, S, width), lambda p, b: (b, 0, QB_BLK // CA_PAIRS + p)),
            pl.BlockSpec((1, S, width), lambda p, b: (b, 0, KB_BLK // CA_PAIRS + p)),
            pl.BlockSpec((1, S, width), lambda p, b: (b, 0, VB_BLK // CA_PAIRS + p)),
            wspec,
        ],
        out_specs=[pl.BlockSpec((1, S, width), lambda p, b: (b, 0, p)), wspec],
        out_shape=[jax.ShapeDtypeStruct((B, S, N_HEADS_CHUNK * HEAD_DIM), BF16),
                   jax.ShapeDtypeStruct(w_cast.shape, BF16)],
        scratch_shapes=[pltpu.VMEM((CA_PAIRS, CA_NKB * CA_TQ, 2 * CA_TQ), F32),
                        pltpu.VMEM((CA_PAIRS, LANES + VT_ONES, S), BF16),
                        pltpu.VMEM((CA_SLOTS, CA_NKB * CA_TQ, 2 * CA_TQ), F32),
                        pltpu.VMEM((CA_SLOTS, 1, 2 * CA_TQ), F32)],
        compiler_params=_params(("arbitrary", "arbitrary")),
        name="chunkattn",
    )(bias, h3, h3, h3, w_cast)


def _mid_kernel(x_ref, ya_ref, yb_ref, mem_ref, wo_ref, wq_ref, wk_ref, wv_ref, wmo_ref,
                g1_ref, b1_ref, g2_ref, b2_ref, x2_ref, k_ref, v_ref, x1_ref):
    hd = MEM_HEAD_DIM
    half = ya_ref.shape[2]

    @pl.when(pl.program_id(1) == 0)
    def _():
        mb = mem_ref[0].astype(BF16)
        k_ref[...] = _dot(mb, wk_ref[...]).astype(BF16)
        v_ref[...] = _dot(mb, wv_ref[...]).astype(BF16)

    def sub_tile(rows):
        y = _dot(ya_ref[0, rows, :], wo_ref[0:half, :]) + _dot(yb_ref[0, rows, :], wo_ref[half:, :])
        yield
        x1 = _layer_norm(DEEPNORM_ALPHA * x_ref[0, rows, :] + y, g1_ref[...], b1_ref[...])
        x1_ref[rows, :] = x1
        q = (_dot(x1.astype(BF16), wq_ref[...]) * (hd ** -0.5 * LOG2E)).astype(BF16)
        yield
        outs = []
        for h in range(N_HEADS_MEM):
            sl = slice(h * hd, (h + 1) * hd)
            s = _dot_nt(q[:, sl], k_ref[:, sl])
            p = jnp.exp2(s - jnp.max(s, axis=1, keepdims=True))
            l = jnp.sum(p, axis=1, keepdims=True)
            outs.append((_dot(p.astype(BF16), v_ref[:, sl]) / l).astype(BF16))
            yield
        o = jnp.concatenate(outs, axis=1)
        y2 = _dot(o, wmo_ref[...])
        yield
        x2_ref[0, rows, :] = _layer_norm(DEEPNORM_ALPHA * x1_ref[rows, :] + y2, g2_ref[...], b2_ref[...])

    _wavefront([sub_tile(rows) for rows in _sub_tiles(x_ref.shape[1], 256)], lag=0)


def _mid_call(x, ya, yb, mem, w_o, w_mq, w_mk, w_mv, w_mo, g1, b1, g2, b2, tm):
    B, S, _ = x.shape
    M = mem.shape[1]
    half = ya.shape[2]
    row = pl.BlockSpec((1, tm, D_MODEL), lambda bb, i: (bb, i, 0))
    hrow = pl.BlockSpec((1, tm, half), lambda bb, i: (bb, i, 0))
    memblk = pl.BlockSpec((1, M, D_MODEL), lambda bb, i: (bb, 0, 0))
    vec = pl.BlockSpec((1, D_MODEL), lambda bb, i: (0, 0))
    wspec = _resident((D_MODEL, D_MODEL))
    return pl.pallas_call(
        _mid_kernel,
        grid=(B, S // tm),
        in_specs=[row, hrow, hrow, memblk] + [wspec] * 5 + [vec] * 4,
        out_specs=row,
        out_shape=jax.ShapeDtypeStruct((B, S, D_MODEL), F32),
        scratch_shapes=[pltpu.VMEM((M, D_MODEL), BF16)] * 2 + [pltpu.VMEM((tm, D_MODEL), F32)],
        compiler_params=_params(("arbitrary", "arbitrary")),
        name="mid",
    )(x, ya, yb, mem, w_o, w_mq, w_mk, w_mv, w_mo, g1, b1, g2, b2)


def _mlp_kernel(x_ref, wu_ref, wd_ref, g_ref, b_ref, o_ref, *, ff_chunk):
    def finish(rows, y):
        o_ref[rows, :] = _layer_norm(DEEPNORM_ALPHA * x_ref[rows, :] + y, g_ref[...], b_ref[...])

    pending = None
    for rows in _sub_tiles(x_ref.shape[0], 256):
        xb = x_ref[rows, :].astype(BF16)
        y = None
        for c in range(D_FF // ff_chunk):
            sl = slice(c * ff_chunk, (c + 1) * ff_chunk)
            h = jnp.maximum(_dot(xb, wu_ref[:, sl]), 0.0)
            part = _dot((h * h).astype(BF16), wd_ref[sl, :])
            y = part if y is None else y + part
            if c == 0 and pending is not None:
                finish(*pending)
                pending = None
        pending = (rows, y)
    finish(*pending)


def _mlp_call(x2d, w_up, w_down, g, b, tm, ff_chunk):
    T = x2d.shape[0]
    row = pl.BlockSpec((tm, D_MODEL), lambda i: (i, 0))
    vec = pl.BlockSpec((1, D_MODEL), lambda i: (0, 0))
    return pl.pallas_call(
        functools.partial(_mlp_kernel, ff_chunk=ff_chunk),
        grid=(T // tm,),
        in_specs=[row,
                  pl.BlockSpec((D_MODEL, D_FF), lambda i: (0, 0), pipeline_mode=pl.Buffered(1)),
                  pl.BlockSpec((D_FF, D_MODEL), lambda i: (0, 0), pipeline_mode=pl.Buffered(1)),
                  vec, vec],
        out_specs=row,
        out_shape=jax.ShapeDtypeStruct((T, D_MODEL), F32),
        compiler_params=_params(("arbitrary",)),
        name="mlp",
    )(x2d, w_up, w_down, g, b)


def kernel(x, mem, positions, w_in, diff_lambda, subln_g, rel_bias, w_o, ln1_g, ln1_b,
           w_mq, w_mk, w_mv, w_mo, ln2_g, ln2_b, w_up, w_down, ln3_g, ln3_b):
    B, S, D = x.shape
    T = B * S
    depth = w_in.shape[0]
    assert depth == DEPTH and D == D_MODEL and S % CA_TQ == 0
    inv_freq = 1.0 / (ROPE_THETA ** (jnp.arange(0, HEAD_DIM, 2, dtype=F32) / HEAD_DIM))
    invf = jnp.tile(inv_freq, LANES // (HEAD_DIM // 2)).reshape(1, LANES)
    pos = positions.reshape(T // IN_TM, ROPE_PACK, IN_TM // ROPE_PACK).transpose(0, 2, 1)
    pos = jnp.repeat(pos, HEAD_DIM // 2, axis=2).reshape(T // ROPE_PACK, LANES)
    vec = lambda a: a.reshape(1, -1)

    for l in range(depth):
        lambda_init = 0.8 - 0.6 * math.exp(-0.3 * l)
        lam = _lam_call(diff_lambda[l], lambda_init, DA_TQ)
        h, mid_w = _inproj_call(pos, invf, x.reshape(T, D), w_in[l],
                                [w_o[l], w_mq[l], w_mk[l], w_mv[l], w_mo[l]], tm=IN_TM)
        h3 = h.reshape(B, S, IN_WIDTH)
        ya, w_up_b = _diffattn_call(lam, subln_g[l].reshape(-1, 1), h3, w_up[l], tq=DA_TQ,
                                    lambda_init=lambda_init)
        yb, w_down_b = _chunkattn_call(_chunk_bias_table(rel_bias[l]), h3, w_down[l])
        x2 = _mid_call(x, ya, yb, mem, *mid_w, vec(ln1_g[l]), vec(ln1_b[l]), vec(ln2_g[l]), vec(ln2_b[l]),
                       tm=1024)
        out = _mlp_call(x2.reshape(T, D), w_up_b, w_down_b,
                        vec(ln3_g[l]), vec(ln3_b[l]), tm=1024, ff_chunk=1024)
        x = out.reshape(B, S, D)
    return x
```

```python
import functools
import math

import jax
import jax.numpy as jnp
import numpy as np
from jax import lax
from jax.experimental import pallas as pl
from jax.experimental.pallas import tpu as pltpu

D_MODEL = 1024
CHUNK = 64
HEAD_DIM = 64
N_HEADS_DIFF = 4
DIFF_V_DIM = 128
N_HEADS_CHUNK = 8
LEFT_CHUNKS = 8
REL_CLIP = 128
N_HEADS_MEM = 4
MEM_HEAD_DIM = 256
D_FF = 4096
ROPE_THETA = 10000.0
LN_EPS = 1e-5
NEG_INF = -1e30
LOG2E = math.log2(math.e)
DEPTH = 1
DEEPNORM_ALPHA = (2.0 * DEPTH) ** 0.25
IN_WIDTH = 3072

LANES = 128
VMEM_LIMIT = 56 * 1024 * 1024

F32 = jnp.float32
BF16 = jnp.bfloat16

QA_BLK, KA_BLK, VA_BLK, QB_BLK, KB_BLK, VB_BLK = 0, 4, 8, 12, 16, 20


def _params(sem):
    return pltpu.CompilerParams(dimension_semantics=sem, vmem_limit_bytes=VMEM_LIMIT)


def _dot(a, b):
    return jnp.dot(a, b, preferred_element_type=F32)


def _dot_nt(a, b):
    return lax.dot_general(a, b, (((1,), (1,)), ((), ())), preferred_element_type=F32)


SUB_ROWS = 512


def _sub_tiles(n_rows, sub_rows=SUB_ROWS):
    return [slice(r, r + sub_rows) for r in range(0, n_rows, sub_rows)]


def _resident(shape):
    return pl.BlockSpec(shape, lambda *_: (0,) * len(shape), pipeline_mode=pl.Buffered(1))


def _cast_weights_once(first_step, pairs):
    @pl.when(first_step)
    def _():
        for src_ref, dst_ref in pairs:
            dst_ref[...] = src_ref[...].astype(dst_ref.dtype)


def _layer_norm(z, g, b):
    mu = jnp.mean(z, axis=-1, keepdims=True)
    zc = z - mu
    var = jnp.mean(zc * zc, axis=-1, keepdims=True)
    return zc * lax.rsqrt(var + LN_EPS) * g + b


IN_TM = 1024
ROPE_PACK = LANES // (HEAD_DIM // 2)


def _spread_token_groups(t):
    n = t.shape[0]
    grp = lax.broadcasted_iota(jnp.int32, t.shape, 1) // (HEAD_DIM // 2)
    out = []
    for a in range(ROPE_PACK):
        spread = t
        for g in range(ROPE_PACK):
            if g != a:
                moved = pltpu.roll(t, ((g - a) % ROPE_PACK) * (HEAD_DIM // 2), 1)
                spread = jnp.where(grp == g, moved, spread)
        out.append(spread)
    return jnp.concatenate(out, axis=0)


def _inproj_kernel(pos_ref, invf_ref, x_ref, wf_ref, *refs):
    n_cast = (len(refs) - 2) // 2
    cast_in, o_ref, cast_out, w_ref = refs[:n_cast], refs[n_cast], refs[n_cast + 1:-1], refs[-1]
    for src_ref, dst_ref in zip(cast_in, cast_out):
        dst_ref[...] = src_ref[...].astype(dst_ref.dtype)
    _cast_weights_once(pl.program_id(0) == 0, [(wf_ref, w_ref)])
    xb = x_ref[...].astype(BF16)
    ang = pos_ref[...].astype(F32) * invf_ref[...]
    cos = _spread_token_groups(jnp.cos(ang))
    sin = _spread_token_groups(jnp.sin(ang))
    lane = lax.broadcasted_iota(jnp.int32, cos.shape, 1)
    upper = (lane % HEAD_DIM) >= (HEAD_DIM // 2)
    sin_up = jnp.where(upper, sin, 0.0)
    sin_lo = jnp.where(upper, 0.0, -sin)
    n_chunks = IN_WIDTH // 512
    for c in range(n_chunks):
        h = _dot(xb, w_ref[:, c * 512:(c + 1) * 512])
        if c < 2:
            slabs = []
            for s in range(4):
                sl = h[:, s * LANES:(s + 1) * LANES]
                slabs.append(sl * cos
                             + pltpu.roll(sl, HEAD_DIM // 2, 1) * sin_up
                             + pltpu.roll(sl, LANES - HEAD_DIM // 2, 1) * sin_lo)
            h = jnp.concatenate(slabs, axis=1)
        if c == 0 or c == 3:
            h = h * (HEAD_DIM ** -0.5 * LOG2E)
        o_ref[:, c * 512:(c + 1) * 512] = h.astype(BF16)


def _inproj_call(pos, invf, x2d, w_in, w_cast, tm):
    T = x2d.shape[0]
    steps = T // tm
    cast_specs = [pl.BlockSpec((w.shape[0] // steps, w.shape[1]), lambda i: (i, 0)) for w in w_cast]
    outs = pl.pallas_call(
        _inproj_kernel,
        grid=(steps,),
        in_specs=[
            pl.BlockSpec((tm // ROPE_PACK, LANES), lambda i: (i, 0)),
            pl.BlockSpec((1, LANES), lambda i: (0, 0)),
            pl.BlockSpec((tm, D_MODEL), lambda i: (i, 0)),
            _resident((D_MODEL, IN_WIDTH)),
        ] + cast_specs,
        out_specs=[pl.BlockSpec((tm, IN_WIDTH), lambda i: (i, 0))] + cast_specs,
        out_shape=[jax.ShapeDtypeStruct((T, IN_WIDTH), BF16)]
        + [jax.ShapeDtypeStruct(w.shape, BF16) for w in w_cast],
        scratch_shapes=[pltpu.VMEM((D_MODEL, IN_WIDTH), BF16)],
        compiler_params=_params(("arbitrary",)),
        name="inproj",
    )(pos, invf, x2d, w_in, *w_cast)
    return outs[0], outs[1:]


def _interleave(*generators):
    pending = list(generators)
    while pending:
        for g in list(pending):
            try:
                next(g)
            except StopIteration:
                pending.remove(g)


def _wavefront(generators, lag=1):
    pending = dict(enumerate(generators))
    wave = 0
    while pending:
        for t in sorted(pending, reverse=True):
            if wave >= t * lag:
                try:
                    next(pending[t])
                except StopIteration:
                    del pending[t]
        wave += 1


def _stack_halves(q):
    lane = lax.broadcasted_iota(jnp.int32, q.shape, 1)
    zero = jnp.zeros_like(q)
    return jnp.concatenate([jnp.where(lane < HEAD_DIM, q, zero),
                            jnp.where(lane >= HEAD_DIM, q, zero)], axis=0)


DA_TQ = 256
DA_SLOTS = 3
DA_HEADS = 2
VT_ONES = 16


def _fill_vt_ones(vt_ref, v):
    nv = v.shape[1]
    vt_ref[0:nv, :] = v.astype(F32).T.astype(vt_ref.dtype)
    vt_ref[nv:, :] = jnp.ones((vt_ref.shape[0] - nv, vt_ref.shape[1]), vt_ref.dtype)


def _diag_tiles(tq):
    assert tq == 2 * LANES
    return [(rh, ct) for rh in range(tq // LANES) for ct in range(2 * tq // LANES)
            if not (rh == 1 and ct % 2 == 0)]


def _diff_scores(nkb, q, k_ref, mask_ref, s_ref, m_ref, *, tq):
    qq = _stack_halves(q)
    m = None
    for j in range(nkb):
        s = _dot_nt(k_ref[j * tq:(j + 1) * tq, :], qq)
        if j < nkb - 1:
            s_ref[j * tq:(j + 1) * tq, :] = s
            mj = jnp.max(s, axis=0, keepdims=True)
        else:
            m_tiles = [None] * (2 * tq // LANES)
            for rh, ct in _diag_tiles(tq):
                rows = slice(rh * LANES, (rh + 1) * LANES)
                lanes = slice(ct * LANES, (ct + 1) * LANES)
                st = s[rows, lanes] + mask_ref[rows, lanes]
                s_ref[j * tq + rh * LANES:j * tq + (rh + 1) * LANES, lanes] = st
                mt = jnp.max(st, axis=0, keepdims=True)
                m_tiles[ct] = mt if m_tiles[ct] is None else jnp.maximum(m_tiles[ct], mt)
            mj = jnp.concatenate(m_tiles, axis=1)
        m = mj if m is None else jnp.maximum(m, mj)
        yield
    m_ref[...] = m


def _diff_output(nkb, s_ref, m_ref, vt_ref, lam, g_ref, o_ref, *, tq, lambda_init):
    m = m_ref[...]
    acc = None
    for j in range(nkb):
        rows = slice(j * tq, (j + 1) * tq)
        if j < nkb - 1:
            p = jnp.exp2(s_ref[rows, :] - m).astype(BF16)
        else:
            vis = _diag_tiles(tq)
            p = jnp.concatenate([jnp.concatenate([
                jnp.exp2(s_ref[j * tq + rh * LANES:j * tq + (rh + 1) * LANES, ct * LANES:(ct + 1) * LANES]
                         - m[:, ct * LANES:(ct + 1) * LANES]).astype(BF16)
                if (rh, ct) in vis else jnp.zeros((LANES, LANES), BF16)
                for ct in range(2 * tq // LANES)], axis=1) for rh in range(tq // LANES)], axis=0)
        part = _dot(vt_ref[:, rows], p)
        acc = part if acc is None else acc + part
        yield
    o = acc[:DIFF_V_DIM, :] * (1.0 / acc[DIFF_V_DIM:DIFF_V_DIM + 1, :])
    o = o[:, :tq] - lam * o[:, tq:]
    y = (o * lax.rsqrt(jnp.mean(o * o, axis=0, keepdims=True) + LN_EPS)).T * g_ref[...]
    o_ref[...] = (y * (1.0 - lambda_init)).astype(o_ref.dtype)


def _diffattn_kernel(dl_ref, g_ref, mask_ref, q_ref, k_ref, v_ref, wf_ref, o_ref, wb_ref, vt_ref, s_ref,
                     m_ref, *, tq, nq, lambda_init):
    wb_ref[...] = wf_ref[...].astype(wb_ref.dtype)
    dl = dl_ref[...]
    lam = (jnp.exp(jnp.sum(dl[0:1, :] * dl[1:2, :], axis=1, keepdims=True))
           - jnp.exp(jnp.sum(dl[2:3, :] * dl[3:4, :], axis=1, keepdims=True)) + lambda_init)
    for hh in range(DA_HEADS):
        _fill_vt_ones(vt_ref.at[hh], v_ref[0, :, hh * LANES:(hh + 1) * LANES])
    stages = [(hh, c) for hh in range(DA_HEADS) for c in range(nq)]

    def scores(i):
        hh, c = stages[i]
        cols = slice(hh * LANES, (hh + 1) * LANES)
        return _diff_scores(c + 1, q_ref[0, c * tq:(c + 1) * tq, cols], k_ref.at[0, :, cols], mask_ref,
                            s_ref.at[i % DA_SLOTS], m_ref.at[i % DA_SLOTS], tq=tq)

    for i in range(DA_SLOTS - 1):
        _interleave(scores(i))
    for i, (hh, c) in enumerate(stages):
        nxt = [scores(i + DA_SLOTS - 1)] if i + DA_SLOTS - 1 < len(stages) else []
        _interleave(*nxt, _diff_output(c + 1, s_ref.at[i % DA_SLOTS], m_ref.at[i % DA_SLOTS], vt_ref.at[hh],
                                       lam, g_ref,
                                       o_ref.at[0, c * tq:(c + 1) * tq, hh * LANES:(hh + 1) * LANES], tq=tq,
                                       lambda_init=lambda_init))


def _diffattn_call(diff_lambda, g, h3, w_cast, tq, lambda_init):
    B, S, _ = h3.shape
    nq = S // tq
    wrows, wcols = w_cast.shape
    ngroups = N_HEADS_DIFF // DA_HEADS
    width = DA_HEADS * LANES
    wspec = pl.BlockSpec((wrows // (B * ngroups), wcols), lambda b, h: (b * ngroups + h, 0))
    key_chunk = np.arange(tq)[:, None] // CHUNK
    query_chunk = (np.arange(2 * tq)[None, :] % tq) // CHUNK
    diag_mask = jnp.asarray(np.where(key_chunk <= query_chunk, 0.0, NEG_INF), F32)
    return pl.pallas_call(
        functools.partial(_diffattn_kernel, tq=tq, nq=nq, lambda_init=lambda_init),
        grid=(B, ngroups),
        scratch_shapes=[pltpu.VMEM((DA_HEADS, DIFF_V_DIM + VT_ONES, S), BF16),
                        pltpu.VMEM((DA_SLOTS, S, 2 * tq), F32),
                        pltpu.VMEM((DA_SLOTS, 1, 2 * tq), F32)],
        in_specs=[
            pl.BlockSpec(diff_lambda.shape, lambda b, h: (0, 0)),
            pl.BlockSpec((1, DIFF_V_DIM), lambda b, h: (0, 0)),
            pl.BlockSpec((tq, 2 * tq), lambda b, h: (0, 0)),
            pl.BlockSpec((1, S, width), lambda b, h: (b, 0, QA_BLK // DA_HEADS + h)),
            pl.BlockSpec((1, S, width), lambda b, h: (b, 0, KA_BLK // DA_HEADS + h)),
            pl.BlockSpec((1, S, width), lambda b, h: (b, 0, VA_BLK // DA_HEADS + h)),
            wspec,
        ],
        out_specs=[pl.BlockSpec((1, S, width), lambda b, h: (b, 0, h)), wspec],
        out_shape=[jax.ShapeDtypeStruct((B, S, N_HEADS_DIFF * DIFF_V_DIM), BF16),
                   jax.ShapeDtypeStruct(w_cast.shape, BF16)],
        compiler_params=_params(("arbitrary", "arbitrary")),
        name="diffattn",
    )(diff_lambda, g, diag_mask, h3, h3, h3, w_cast)


CA_TQ = 256
CA_NKB = 3
CA_SLOTS = 3
CA_PAIRS = 2


def _chunk_bias_table(rel_bias):
    nh = rel_bias.shape[0]
    n_neg = CA_TQ - REL_CLIP + 1
    n_far = CA_NKB * CA_TQ - REL_CLIP
    by_dist = jnp.concatenate([
        jnp.broadcast_to(rel_bias[:, :1], (nh, n_neg)),
        rel_bias[:, 1:2 * REL_CLIP],
        jnp.broadcast_to(rel_bias[:, 2 * REL_CLIP:], (nh, n_far)),
    ], axis=1).astype(F32) * LOG2E
    assert by_dist.shape[1] == (CA_NKB + 1) * CA_TQ
    return by_dist.reshape(nh // 2, 2, (CA_NKB + 1) * CA_TQ)


def _build_bias_table(dist_ref, bias_ref):
    row = lax.broadcasted_iota(jnp.int32, (CA_TQ, CA_TQ), 0) // CHUNK
    col = lax.broadcasted_iota(jnp.int32, (CA_TQ, CA_TQ), 1) // CHUNK
    for t in range(2):
        for j in range(CA_NKB):
            d0 = (CA_NKB - 1 - j) * CA_TQ
            g = jnp.concatenate([dist_ref[t:t + 1, d0 + CA_TQ:d0 + 2 * CA_TQ],
                                 dist_ref[t:t + 1, d0:d0 + CA_TQ]], axis=1)
            rolled = pltpu.roll(jnp.broadcast_to(g, (CA_TQ, 2 * CA_TQ)), 0, 1, stride=1, stride_axis=0)
            blk = rolled[:, :CA_TQ]
            if j == 0:
                blk = jnp.where(row >= col, blk, NEG_INF)
            elif j == CA_NKB - 1:
                blk = jnp.where(row <= col, blk, NEG_INF)
            bias_ref[j * CA_TQ:(j + 1) * CA_TQ, t * CA_TQ:(t + 1) * CA_TQ] = blk


def _chunkattn_kernel(dist_ref, q_ref, k_ref, v_ref, wf_ref, o_ref, wb_ref, bias_ref, vt_ref, s_ref, m_ref):
    nq = q_ref.shape[1] // CA_TQ
    wb_ref[...] = wf_ref[...].astype(wb_ref.dtype)

    @pl.when(pl.program_id(1) == 0)
    def _():
        for pp in range(CA_PAIRS):
            _build_bias_table(dist_ref.at[pp], bias_ref.at[pp])

    def key_blocks(blk):
        return [(j, blk - (CA_NKB - 1) + j) for j in range(CA_NKB) if blk - (CA_NKB - 1) + j >= 0]

    def visible_tiles(j):
        tiles = [(rh, ct) for rh in range(CA_TQ // LANES) for ct in range(2 * CA_TQ // LANES)]
        if j == 0:
            return [(rh, ct) for rh, ct in tiles if not (rh == 0 and ct % 2 == 1)]
        if j == CA_NKB - 1:
            return [(rh, ct) for rh, ct in tiles if not (rh == 1 and ct % 2 == 0)]
        return tiles

    def scores(pp, blk, slot):
        cols = slice(pp * LANES, (pp + 1) * LANES)
        qq = _stack_halves(q_ref[0, blk * CA_TQ:(blk + 1) * CA_TQ, cols])
        m = [None] * (2 * CA_TQ // LANES)
        for j, kb in key_blocks(blk):
            s = _dot_nt(k_ref[0, kb * CA_TQ:(kb + 1) * CA_TQ, cols], qq)
            for rh, ct in visible_tiles(j):
                rows = slice(j * CA_TQ + rh * LANES, j * CA_TQ + (rh + 1) * LANES)
                lanes = slice(ct * LANES, (ct + 1) * LANES)
                st = s[rh * LANES:(rh + 1) * LANES, lanes] + bias_ref[pp, rows, lanes]
                s_ref[slot, rows, lanes] = st
                mt = jnp.max(st, axis=0, keepdims=True)
                m[ct] = mt if m[ct] is None else jnp.maximum(m[ct], mt)
            yield
        m_ref[slot] = jnp.concatenate(m, axis=1)

    def output(pp, blk, slot):
        m = m_ref[slot]
        acc = None
        for j, kb in key_blocks(blk):
            vis = visible_tiles(j)
            p_rows = []
            for rh in range(CA_TQ // LANES):
                p_tiles = []
                for ct in range(2 * CA_TQ // LANES):
                    if (rh, ct) in vis:
                        rows = slice(j * CA_TQ + rh * LANES, j * CA_TQ + (rh + 1) * LANES)
                        lanes = slice(ct * LANES, (ct + 1) * LANES)
                        p_tiles.append(jnp.exp2(s_ref[slot, rows, lanes] - m[:, lanes]).astype(BF16))
                    else:
                        p_tiles.append(jnp.zeros((LANES, LANES), BF16))
                p_rows.append(jnp.concatenate(p_tiles, axis=1))
            p = jnp.concatenate(p_rows, axis=0)
            part = _dot(vt_ref[pp, :, kb * CA_TQ:(kb + 1) * CA_TQ], p)
            acc = part if acc is None else acc + part
            yield
        inv_l = 1.0 / acc[LANES:LANES + 1, :]
        o_t = jnp.concatenate([acc[:HEAD_DIM, :CA_TQ] * inv_l[:, :CA_TQ],
                               acc[HEAD_DIM:LANES, CA_TQ:] * inv_l[:, CA_TQ:]], axis=0)
        o_ref[0, blk * CA_TQ:(blk + 1) * CA_TQ, pp * LANES:(pp + 1) * LANES] = o_t.T.astype(o_ref.dtype)

    for pp in range(CA_PAIRS):
        _fill_vt_ones(vt_ref.at[pp], v_ref[0, :, pp * LANES:(pp + 1) * LANES])
    stages = [(pp, blk) for pp in range(CA_PAIRS) for blk in range(nq)]
    ahead = CA_SLOTS - 1
    for i in range(ahead):
        _interleave(scores(*stages[i], i % CA_SLOTS))
    for i, stage in enumerate(stages):
        nxt = [scores(*stages[i + ahead], (i + ahead) % CA_SLOTS)] if i + ahead < len(stages) else []
        _interleave(*nxt, output(*stage, i % CA_SLOTS))


def _chunkattn_call(bias, h3, w_cast):
    B, S, _ = h3.shape
    ngroups = N_HEADS_CHUNK // 2 // CA_PAIRS
    width = CA_PAIRS * LANES
    wrows, wcols = w_cast.shape
    wspec = pl.BlockSpec((wrows // (ngroups * B), wcols), lambda p, b: (p * B + b, 0))
    return pl.pallas_call(
        _chunkattn_kernel,
        grid=(ngroups, B),
        in_specs=[
            pl.BlockSpec((CA_PAIRS, 2, (CA_NKB + 1) * CA_TQ), lambda p, b: (p, 0, 0)),
            pl.BlockSpec((1, S, width), lambda p, b: (b, 0, QB_BLK // CA_PAIRS + p)),
            pl.BlockSpec((1, S, width), lambda p, b: (b, 0, KB_BLK // CA_PAIRS + p)),
            pl.BlockSpec((1, S, width), lambda p, b: (b, 0, VB_BLK // CA_PAIRS + p)),
            wspec,
        ],
        out_specs=[pl.BlockSpec((1, S, width), lambda p, b: (b, 0, p)), wspec],
        out_shape=[jax.ShapeDtypeStruct((B, S, N_HEADS_CHUNK * HEAD_DIM), BF16),
                   jax.ShapeDtypeStruct(w_cast.shape, BF16)],
        scratch_shapes=[pltpu.VMEM((CA_PAIRS, CA_NKB * CA_TQ, 2 * CA_TQ), F32),
                        pltpu.VMEM((CA_PAIRS, LANES + VT_ONES, S), BF16),
                        pltpu.VMEM((CA_SLOTS, CA_NKB * CA_TQ, 2 * CA_TQ), F32),
                        pltpu.VMEM((CA_SLOTS, 1, 2 * CA_TQ), F32)],
        compiler_params=_params(("arbitrary", "arbitrary")),
        name="chunkattn",
    )(bias, h3, h3, h3, w_cast)


def _mid_kernel(x_ref, ya_ref, yb_ref, mem_ref, wo_ref, wq_ref, wk_ref, wv_ref, wmo_ref,
                g1_ref, b1_ref, g2_ref, b2_ref, x2_ref, k_ref, v_ref, x1_ref):
    hd = MEM_HEAD_DIM
    half = ya_ref.shape[2]

    @pl.when(pl.program_id(1) == 0)
    def _():
        mb = mem_ref[0].astype(BF16)
        k_ref[...] = _dot(mb, wk_ref[...]).astype(BF16)
        v_ref[...] = _dot(mb, wv_ref[...]).astype(BF16)

    def sub_tile(rows):
        y = _dot(ya_ref[0, rows, :], wo_ref[0:half, :]) + _dot(yb_ref[0, rows, :], wo_ref[half:, :])
        yield
        x1 = _layer_norm(DEEPNORM_ALPHA * x_ref[0, rows, :] + y, g1_ref[...], b1_ref[...])
        x1_ref[rows, :] = x1
        q = (_dot(x1.astype(BF16), wq_ref[...]) * (hd ** -0.5 * LOG2E)).astype(BF16)
        yield
        outs = []
        for h in range(N_HEADS_MEM):
            sl = slice(h * hd, (h + 1) * hd)
            s = _dot_nt(q[:, sl], k_ref[:, sl])
            p = jnp.exp2(s - jnp.max(s, axis=1, keepdims=True))
            l = jnp.sum(p, axis=1, keepdims=True)
            outs.append((_dot(p.astype(BF16), v_ref[:, sl]) / l).astype(BF16))
            yield
        o = jnp.concatenate(outs, axis=1)
        y2 = _dot(o, wmo_ref[...])
        yield
        x2_ref[0, rows, :] = _layer_norm(DEEPNORM_ALPHA * x1_ref[rows, :] + y2, g2_ref[...], b2_ref[...])

    _wavefront([sub_tile(rows) for rows in _sub_tiles(x_ref.shape[1], 256)], lag=0)


def _mid_call(x, ya, yb, mem, w_o, w_mq, w_mk, w_mv, w_mo, g1, b1, g2, b2, tm):
    B, S, _ = x.shape
    M = mem.shape[1]
    half = ya.shape[2]
    row = pl.BlockSpec((1, tm, D_MODEL), lambda bb, i: (bb, i, 0))
    hrow = pl.BlockSpec((1, tm, half), lambda bb, i: (bb, i, 0))
    memblk = pl.BlockSpec((1, M, D_MODEL), lambda bb, i: (bb, 0, 0))
    vec = pl.BlockSpec((1, D_MODEL), lambda bb, i: (0, 0))
    wspec = _resident((D_MODEL, D_MODEL))
    return pl.pallas_call(
        _mid_kernel,
        grid=(B, S // tm),
        in_specs=[row, hrow, hrow, memblk] + [wspec] * 5 + [vec] * 4,
        out_specs=row,
        out_shape=jax.ShapeDtypeStruct((B, S, D_MODEL), F32),
        scratch_shapes=[pltpu.VMEM((M, D_MODEL), BF16)] * 2 + [pltpu.VMEM((tm, D_MODEL), F32)],
        compiler_params=_params(("arbitrary", "arbitrary")),
        name="mid",
    )(x, ya, yb, mem, w_o, w_mq, w_mk, w_mv, w_mo, g1, b1, g2, b2)


def _mlp_kernel(x_ref, wu_ref, wd_ref, g_ref, b_ref, o_ref, *, ff_chunk):
    def finish(rows, y):
        o_ref[rows, :] = _layer_norm(DEEPNORM_ALPHA * x_ref[rows, :] + y, g_ref[...], b_ref[...])

    pending = None
    for rows in _sub_tiles(x_ref.shape[0], 256):
        xb = x_ref[rows, :].astype(BF16)
        y = None
        for c in range(D_FF // ff_chunk):
            sl = slice(c * ff_chunk, (c + 1) * ff_chunk)
            h = jnp.maximum(_dot(xb, wu_ref[:, sl]), 0.0)
            part = _dot((h * h).astype(BF16), wd_ref[sl, :])
            y = part if y is None else y + part
            if c == 0 and pending is not None:
                finish(*pending)
                pending = None
        pending = (rows, y)
    finish(*pending)


def _mlp_call(x2d, w_up, w_down, g, b, tm, ff_chunk):
    T = x2d.shape[0]
    row = pl.BlockSpec((tm, D_MODEL), lambda i: (i, 0))
    vec = pl.BlockSpec((1, D_MODEL), lambda i: (0, 0))
    return pl.pallas_call(
        functools.partial(_mlp_kernel, ff_chunk=ff_chunk),
        grid=(T // tm,),
        in_specs=[row,
                  pl.BlockSpec((D_MODEL, D_FF), lambda i: (0, 0), pipeline_mode=pl.Buffered(1)),
                  pl.BlockSpec((D_FF, D_MODEL), lambda i: (0, 0), pipeline_mode=pl.Buffered(1)),
                  vec, vec],
        out_specs=row,
        out_shape=jax.ShapeDtypeStruct((T, D_MODEL), F32),
        compiler_params=_params(("arbitrary",)),
        name="mlp",
    )(x2d, w_up, w_down, g, b)


def kernel(x, mem, positions, w_in, diff_lambda, subln_g, rel_bias, w_o, ln1_g, ln1_b,
           w_mq, w_mk, w_mv, w_mo, ln2_g, ln2_b, w_up, w_down, ln3_g, ln3_b):
    B, S, D = x.shape
    T = B * S
    depth = w_in.shape[0]
    assert depth == DEPTH and D == D_MODEL and S % CA_TQ == 0
    inv_freq = 1.0 / (ROPE_THETA ** (jnp.arange(0, HEAD_DIM, 2, dtype=F32) / HEAD_DIM))
    invf = jnp.tile(inv_freq, LANES // (HEAD_DIM // 2)).reshape(1, LANES)
    pos = positions.reshape(T // IN_TM, ROPE_PACK, IN_TM // ROPE_PACK).transpose(0, 2, 1)
    pos = jnp.repeat(pos, HEAD_DIM // 2, axis=2).reshape(T // ROPE_PACK, LANES)
    vec = lambda a: a.reshape(1, -1)

    for l in range(depth):
        lambda_init = 0.8 - 0.6 * math.exp(-0.3 * l)
        h, mid_w = _inproj_call(pos, invf, x.reshape(T, D), w_in[l],
                                [w_o[l], w_mq[l], w_mk[l], w_mv[l], w_mo[l]], tm=IN_TM)
        h3 = h.reshape(B, S, IN_WIDTH)
        ya, w_up_b = _diffattn_call(diff_lambda[l], vec(subln_g[l]), h3, w_up[l], tq=DA_TQ,
                                    lambda_init=lambda_init)
        yb, w_down_b = _chunkattn_call(_chunk_bias_table(rel_bias[l]), h3, w_down[l])
        x2 = _mid_call(x, ya, yb, mem, *mid_w, vec(ln1_g[l]), vec(ln1_b[l]), vec(ln2_g[l]), vec(ln2_b[l]),
                       tm=1024)
        out = _mlp_call(x2.reshape(T, D), w_up_b, w_down_b,
                        vec(ln3_g[l]), vec(ln3_b[l]), tm=1024, ff_chunk=1024)
        x = out.reshape(B, S, D)
    return x
```

```python
import functools
import math

import jax
import jax.numpy as jnp
import numpy as np
from jax import lax
from jax.experimental import pallas as pl
from jax.experimental.pallas import tpu as pltpu

D_MODEL = 1024
CHUNK = 64
HEAD_DIM = 64
N_HEADS_DIFF = 4
DIFF_V_DIM = 128
N_HEADS_CHUNK = 8
LEFT_CHUNKS = 8
REL_CLIP = 128
N_HEADS_MEM = 4
MEM_HEAD_DIM = 256
D_FF = 4096
ROPE_THETA = 10000.0
LN_EPS = 1e-5
NEG_INF = -1e30
LOG2E = math.log2(math.e)
DEPTH = 1
DEEPNORM_ALPHA = (2.0 * DEPTH) ** 0.25
IN_WIDTH = 3072

LANES = 128
VMEM_LIMIT = 56 * 1024 * 1024

F32 = jnp.float32
BF16 = jnp.bfloat16

QA_BLK, KA_BLK, VA_BLK, QB_BLK, KB_BLK, VB_BLK = 0, 4, 8, 12, 16, 20


def _params(sem):
    return pltpu.CompilerParams(dimension_semantics=sem, vmem_limit_bytes=VMEM_LIMIT)


def _dot(a, b):
    return jnp.dot(a, b, preferred_element_type=F32)


def _dot_nt(a, b):
    return lax.dot_general(a, b, (((1,), (1,)), ((), ())), preferred_element_type=F32)


SUB_ROWS = 512


def _sub_tiles(n_rows, sub_rows=SUB_ROWS):
    return [slice(r, r + sub_rows) for r in range(0, n_rows, sub_rows)]


def _resident(shape):
    return pl.BlockSpec(shape, lambda *_: (0,) * len(shape), pipeline_mode=pl.Buffered(1))


def _cast_weights_once(first_step, pairs):
    @pl.when(first_step)
    def _():
        for src_ref, dst_ref in pairs:
            dst_ref[...] = src_ref[...].astype(dst_ref.dtype)


def _layer_norm(z, g, b):
    mu = jnp.mean(z, axis=-1, keepdims=True)
    zc = z - mu
    var = jnp.mean(zc * zc, axis=-1, keepdims=True)
    return zc * lax.rsqrt(var + LN_EPS) * g + b


IN_TM = 1024
ROPE_PACK = LANES // (HEAD_DIM // 2)


def _spread_token_groups(t):
    n = t.shape[0]
    grp = lax.broadcasted_iota(jnp.int32, t.shape, 1) // (HEAD_DIM // 2)
    out = []
    for a in range(ROPE_PACK):
        spread = t
        for g in range(ROPE_PACK):
            if g != a:
                moved = pltpu.roll(t, ((g - a) % ROPE_PACK) * (HEAD_DIM // 2), 1)
                spread = jnp.where(grp == g, moved, spread)
        out.append(spread)
    return jnp.concatenate(out, axis=0)


def _inproj_kernel(pos_ref, invf_ref, x_ref, wf_ref, *refs):
    n_cast = (len(refs) - 2) // 2
    cast_in, o_ref, cast_out, w_ref = refs[:n_cast], refs[n_cast], refs[n_cast + 1:-1], refs[-1]
    for src_ref, dst_ref in zip(cast_in, cast_out):
        dst_ref[...] = src_ref[...].astype(dst_ref.dtype)
    _cast_weights_once(pl.program_id(0) == 0, [(wf_ref, w_ref)])
    xb = x_ref[...].astype(BF16)
    ang = pos_ref[...].astype(F32) * invf_ref[...]
    cos = _spread_token_groups(jnp.cos(ang))
    sin = _spread_token_groups(jnp.sin(ang))
    lane = lax.broadcasted_iota(jnp.int32, cos.shape, 1)
    upper = (lane % HEAD_DIM) >= (HEAD_DIM // 2)
    sin_up = jnp.where(upper, sin, 0.0)
    sin_lo = jnp.where(upper, 0.0, -sin)
    n_chunks = IN_WIDTH // 512
    for c in range(n_chunks):
        h = _dot(xb, w_ref[:, c * 512:(c + 1) * 512])
        if c < 2:
            slabs = []
            for s in range(4):
                sl = h[:, s * LANES:(s + 1) * LANES]
                slabs.append(sl * cos
                             + pltpu.roll(sl, HEAD_DIM // 2, 1) * sin_up
                             + pltpu.roll(sl, LANES - HEAD_DIM // 2, 1) * sin_lo)
            h = jnp.concatenate(slabs, axis=1)
        if c == 0 or c == 3:
            h = h * (HEAD_DIM ** -0.5 * LOG2E)
        o_ref[:, c * 512:(c + 1) * 512] = h.astype(BF16)


def _inproj_call(pos, invf, x2d, w_in, w_cast, tm):
    T = x2d.shape[0]
    steps = T // tm
    cast_specs = [pl.BlockSpec((w.shape[0] // steps, w.shape[1]), lambda i: (i, 0)) for w in w_cast]
    outs = pl.pallas_call(
        _inproj_kernel,
        grid=(steps,),
        in_specs=[
            pl.BlockSpec((tm // ROPE_PACK, LANES), lambda i: (i, 0)),
            pl.BlockSpec((1, LANES), lambda i: (0, 0)),
            pl.BlockSpec((tm, D_MODEL), lambda i: (i, 0)),
            _resident((D_MODEL, IN_WIDTH)),
        ] + cast_specs,
        out_specs=[pl.BlockSpec((tm, IN_WIDTH), lambda i: (i, 0))] + cast_specs,
        out_shape=[jax.ShapeDtypeStruct((T, IN_WIDTH), BF16)]
        + [jax.ShapeDtypeStruct(w.shape, BF16) for w in w_cast],
        scratch_shapes=[pltpu.VMEM((D_MODEL, IN_WIDTH), BF16)],
        compiler_params=_params(("arbitrary",)),
        name="inproj",
    )(pos, invf, x2d, w_in, *w_cast)
    return outs[0], outs[1:]


def _interleave(*generators):
    pending = list(generators)
    while pending:
        for g in list(pending):
            try:
                next(g)
            except StopIteration:
                pending.remove(g)


def _wavefront(generators, lag=1):
    pending = dict(enumerate(generators))
    wave = 0
    while pending:
        for t in sorted(pending, reverse=True):
            if wave >= t * lag:
                try:
                    next(pending[t])
                except StopIteration:
                    del pending[t]
        wave += 1


def _stack_halves(q):
    lane = lax.broadcasted_iota(jnp.int32, q.shape, 1)
    zero = jnp.zeros_like(q)
    return jnp.concatenate([jnp.where(lane < HEAD_DIM, q, zero),
                            jnp.where(lane >= HEAD_DIM, q, zero)], axis=0)


DA_TQ = 256
DA_SLOTS = 3
DA_HEADS = 2
VT_ONES = 16


def _fill_vt_ones(vt_ref, v):
    nv = v.shape[1]
    vt_ref[0:nv, :] = v.astype(F32).T.astype(vt_ref.dtype)
    vt_ref[nv:, :] = jnp.ones((vt_ref.shape[0] - nv, vt_ref.shape[1]), vt_ref.dtype)


def _diag_tiles(tq):
    assert tq == 2 * LANES
    return [(rh, ct) for rh in range(tq // LANES) for ct in range(2 * tq // LANES)
            if not (rh == 1 and ct % 2 == 0)]


def _diff_scores(nkb, q, k_ref, mask_ref, s_ref, m_ref, *, tq):
    qq = _stack_halves(q)
    m = None
    for j in range(nkb):
        s = _dot_nt(k_ref[j * tq:(j + 1) * tq, :], qq)
        if j < nkb - 1:
            s_ref[j * tq:(j + 1) * tq, :] = s
            mj = jnp.max(s, axis=0, keepdims=True)
        else:
            m_tiles = [None] * (2 * tq // LANES)
            for rh, ct in _diag_tiles(tq):
                rows = slice(rh * LANES, (rh + 1) * LANES)
                lanes = slice(ct * LANES, (ct + 1) * LANES)
                st = s[rows, lanes] + mask_ref[rows, lanes]
                s_ref[j * tq + rh * LANES:j * tq + (rh + 1) * LANES, lanes] = st
                mt = jnp.max(st, axis=0, keepdims=True)
                m_tiles[ct] = mt if m_tiles[ct] is None else jnp.maximum(m_tiles[ct], mt)
            mj = jnp.concatenate(m_tiles, axis=1)
        m = mj if m is None else jnp.maximum(m, mj)
        yield
    m_ref[...] = m


def _diff_output(nkb, s_ref, m_ref, vt_ref, lam, g_ref, o_ref, *, tq, lambda_init):
    m = m_ref[...]
    acc = None
    for j in range(nkb):
        rows = slice(j * tq, (j + 1) * tq)
        if j < nkb - 1:
            p = jnp.exp2(s_ref[rows, :] - m).astype(BF16)
        else:
            vis = _diag_tiles(tq)
            p = jnp.concatenate([jnp.concatenate([
                jnp.exp2(s_ref[j * tq + rh * LANES:j * tq + (rh + 1) * LANES, ct * LANES:(ct + 1) * LANES]
                         - m[:, ct * LANES:(ct + 1) * LANES]).astype(BF16)
                if (rh, ct) in vis else jnp.zeros((LANES, LANES), BF16)
                for ct in range(2 * tq // LANES)], axis=1) for rh in range(tq // LANES)], axis=0)
        part = _dot(vt_ref[:, rows], p)
        acc = part if acc is None else acc + part
        yield
    o = acc[:DIFF_V_DIM, :] * (1.0 / acc[DIFF_V_DIM:DIFF_V_DIM + 1, :])
    o = o[:, :tq] - lam * o[:, tq:]
    y = (o * lax.rsqrt(jnp.mean(o * o, axis=0, keepdims=True) + LN_EPS)).T * g_ref[...]
    o_ref[...] = (y * (1.0 - lambda_init)).astype(o_ref.dtype)


def _diffattn_kernel(dl_ref, g_ref, mask_ref, q_ref, k_ref, v_ref, wf_ref, o_ref, wb_ref, vt_ref, s_ref,
                     m_ref, *, tq, nq, lambda_init):
    wb_ref[...] = wf_ref[...].astype(wb_ref.dtype)
    dl = dl_ref[...]
    lam = (jnp.exp(jnp.sum(dl[0:1, :] * dl[1:2, :], axis=1, keepdims=True))
           - jnp.exp(jnp.sum(dl[2:3, :] * dl[3:4, :], axis=1, keepdims=True)) + lambda_init)
    for hh in range(DA_HEADS):
        _fill_vt_ones(vt_ref.at[hh], v_ref[0, :, hh * LANES:(hh + 1) * LANES])
    stages = [(hh, c) for hh in range(DA_HEADS) for c in range(nq)]

    def scores(i):
        hh, c = stages[i]
        cols = slice(hh * LANES, (hh + 1) * LANES)
        return _diff_scores(c + 1, q_ref[0, c * tq:(c + 1) * tq, cols], k_ref.at[0, :, cols], mask_ref,
                            s_ref.at[i % DA_SLOTS], m_ref.at[i % DA_SLOTS], tq=tq)

    for i in range(DA_SLOTS - 1):
        _interleave(scores(i))
    for i, (hh, c) in enumerate(stages):
        nxt = [scores(i + DA_SLOTS - 1)] if i + DA_SLOTS - 1 < len(stages) else []
        _interleave(*nxt, _diff_output(c + 1, s_ref.at[i % DA_SLOTS], m_ref.at[i % DA_SLOTS], vt_ref.at[hh],
                                       lam, g_ref,
                                       o_ref.at[0, c * tq:(c + 1) * tq, hh * LANES:(hh + 1) * LANES], tq=tq,
                                       lambda_init=lambda_init))


def _diffattn_call(diff_lambda, g, h3, w_cast, tq, lambda_init):
    B, S, _ = h3.shape
    nq = S // tq
    wrows, wcols = w_cast.shape
    ngroups = N_HEADS_DIFF // DA_HEADS
    width = DA_HEADS * LANES
    wspec = pl.BlockSpec((wrows // (B * ngroups), wcols), lambda b, h: (b * ngroups + h, 0))
    key_chunk = np.arange(tq)[:, None] // CHUNK
    query_chunk = (np.arange(2 * tq)[None, :] % tq) // CHUNK
    diag_mask = jnp.asarray(np.where(key_chunk <= query_chunk, 0.0, NEG_INF), F32)
    return pl.pallas_call(
        functools.partial(_diffattn_kernel, tq=tq, nq=nq, lambda_init=lambda_init),
        grid=(B, ngroups),
        scratch_shapes=[pltpu.VMEM((DA_HEADS, DIFF_V_DIM + VT_ONES, S), BF16),
                        pltpu.VMEM((DA_SLOTS, S, 2 * tq), F32),
                        pltpu.VMEM((DA_SLOTS, 1, 2 * tq), F32)],
        in_specs=[
            pl.BlockSpec(diff_lambda.shape, lambda b, h: (0, 0)),
            pl.BlockSpec((1, DIFF_V_DIM), lambda b, h: (0, 0)),
            pl.BlockSpec((tq, 2 * tq), lambda b, h: (0, 0)),
            pl.BlockSpec((1, S, width), lambda b, h: (b, 0, QA_BLK // DA_HEADS + h)),
            pl.BlockSpec((1, S, width), lambda b, h: (b, 0, KA_BLK // DA_HEADS + h)),
            pl.BlockSpec((1, S, width), lambda b, h: (b, 0, VA_BLK // DA_HEADS + h)),
            wspec,
        ],
        out_specs=[pl.BlockSpec((1, S, width), lambda b, h: (b, 0, h)), wspec],
        out_shape=[jax.ShapeDtypeStruct((B, S, N_HEADS_DIFF * DIFF_V_DIM), BF16),
                   jax.ShapeDtypeStruct(w_cast.shape, BF16)],
        compiler_params=_params(("arbitrary", "arbitrary")),
        name="diffattn",
    )(diff_lambda, g, diag_mask, h3, h3, h3, w_cast)


CA_TQ = 256
CA_NKB = LEFT_CHUNKS * CHUNK // CA_TQ + 1
CA_SLOTS = 3
CA_PAIRS = 4


def _chunk_bias_table(rel_bias):
    nh = rel_bias.shape[0]
    n_neg = CA_TQ - REL_CLIP + 1
    n_far = CA_NKB * CA_TQ - REL_CLIP
    by_dist = jnp.concatenate([
        jnp.broadcast_to(rel_bias[:, :1], (nh, n_neg)),
        rel_bias[:, 1:2 * REL_CLIP],
        jnp.broadcast_to(rel_bias[:, 2 * REL_CLIP:], (nh, n_far)),
    ], axis=1).astype(F32) * LOG2E
    assert by_dist.shape[1] == (CA_NKB + 1) * CA_TQ
    return by_dist.reshape(nh // 2, 2, (CA_NKB + 1) * CA_TQ)


def _build_bias_table(dist_ref, bias_ref):
    row = lax.broadcasted_iota(jnp.int32, (CA_TQ, CA_TQ), 0) // CHUNK
    col = lax.broadcasted_iota(jnp.int32, (CA_TQ, CA_TQ), 1) // CHUNK
    for t in range(2):
        for j in range(CA_NKB):
            d0 = (CA_NKB - 1 - j) * CA_TQ
            g = jnp.concatenate([dist_ref[t:t + 1, d0 + CA_TQ:d0 + 2 * CA_TQ],
                                 dist_ref[t:t + 1, d0:d0 + CA_TQ]], axis=1)
            rolled = pltpu.roll(jnp.broadcast_to(g, (CA_TQ, 2 * CA_TQ)), 0, 1, stride=1, stride_axis=0)
            blk = rolled[:, :CA_TQ]
            if j == 0:
                blk = jnp.where(row >= col, blk, NEG_INF)
            elif j == CA_NKB - 1:
                blk = jnp.where(row <= col, blk, NEG_INF)
            bias_ref[j * CA_TQ:(j + 1) * CA_TQ, t * CA_TQ:(t + 1) * CA_TQ] = blk


def _chunkattn_kernel(dist_ref, q_ref, k_ref, v_ref, wf_ref, o_ref, wb_ref, bias_ref, vt_ref, s_ref, m_ref):
    nq = q_ref.shape[1] // CA_TQ
    wb_ref[...] = wf_ref[...].astype(wb_ref.dtype)

    @pl.when(pl.program_id(1) == 0)
    def _():
        for pp in range(CA_PAIRS):
            _build_bias_table(dist_ref.at[pp], bias_ref.at[pp])

    def key_blocks(blk):
        return [(j, blk - (CA_NKB - 1) + j) for j in range(CA_NKB) if blk - (CA_NKB - 1) + j >= 0]

    def visible_tiles(j):
        tiles = [(rh, ct) for rh in range(CA_TQ // LANES) for ct in range(2 * CA_TQ // LANES)]
        if j == 0:
            return [(rh, ct) for rh, ct in tiles if not (rh == 0 and ct % 2 == 1)]
        if j == CA_NKB - 1:
            return [(rh, ct) for rh, ct in tiles if not (rh == 1 and ct % 2 == 0)]
        return tiles

    def scores(pp, blk, slot):
        cols = slice(pp * LANES, (pp + 1) * LANES)
        qq = _stack_halves(q_ref[0, blk * CA_TQ:(blk + 1) * CA_TQ, cols])
        m = [None] * (2 * CA_TQ // LANES)
        for j, kb in key_blocks(blk):
            s = _dot_nt(k_ref[0, kb * CA_TQ:(kb + 1) * CA_TQ, cols], qq)
            for rh, ct in visible_tiles(j):
                rows = slice(j * CA_TQ + rh * LANES, j * CA_TQ + (rh + 1) * LANES)
                lanes = slice(ct * LANES, (ct + 1) * LANES)
                st = s[rh * LANES:(rh + 1) * LANES, lanes] + bias_ref[pp, rows, lanes]
                s_ref[slot, rows, lanes] = st
                mt = jnp.max(st, axis=0, keepdims=True)
                m[ct] = mt if m[ct] is None else jnp.maximum(m[ct], mt)
            yield
        m_ref[slot] = jnp.concatenate(m, axis=1)

    def output(pp, blk, slot):
        m = m_ref[slot]
        acc = None
        for j, kb in key_blocks(blk):
            vis = visible_tiles(j)
            p_rows = []
            for rh in range(CA_TQ // LANES):
                p_tiles = []
                for ct in range(2 * CA_TQ // LANES):
                    if (rh, ct) in vis:
                        rows = slice(j * CA_TQ + rh * LANES, j * CA_TQ + (rh + 1) * LANES)
                        lanes = slice(ct * LANES, (ct + 1) * LANES)
                        p_tiles.append(jnp.exp2(s_ref[slot, rows, lanes] - m[:, lanes]).astype(BF16))
                    else:
                        p_tiles.append(jnp.zeros((LANES, LANES), BF16))
                p_rows.append(jnp.concatenate(p_tiles, axis=1))
            p = jnp.concatenate(p_rows, axis=0)
            part = _dot(vt_ref[pp, :, kb * CA_TQ:(kb + 1) * CA_TQ], p)
            acc = part if acc is None else acc + part
            yield
        inv_l = 1.0 / acc[LANES:LANES + 1, :]
        o_t = jnp.concatenate([acc[:HEAD_DIM, :CA_TQ] * inv_l[:, :CA_TQ],
                               acc[HEAD_DIM:LANES, CA_TQ:] * inv_l[:, CA_TQ:]], axis=0)
        o_ref[0, blk * CA_TQ:(blk + 1) * CA_TQ, pp * LANES:(pp + 1) * LANES] = o_t.T.astype(o_ref.dtype)

    for pp in range(CA_PAIRS):
        _fill_vt_ones(vt_ref.at[pp], v_ref[0, :, pp * LANES:(pp + 1) * LANES])
    stages = [(pp, blk) for pp in range(CA_PAIRS) for blk in range(nq)]
    ahead = CA_SLOTS - 1
    for i in range(ahead):
        _interleave(scores(*stages[i], i % CA_SLOTS))
    for i, stage in enumerate(stages):
        nxt = [scores(*stages[i + ahead], (i + ahead) % CA_SLOTS)] if i + ahead < len(stages) else []
        _interleave(*nxt, output(*stage, i % CA_SLOTS))


def _chunkattn_call(bias, h3, w_cast):
    B, S, _ = h3.shape
    ngroups = N_HEADS_CHUNK // 2 // CA_PAIRS
    width = CA_PAIRS * LANES
    wrows, wcols = w_cast.shape
    wspec = pl.BlockSpec((wrows // (ngroups * B), wcols), lambda p, b: (p * B + b, 0))
    return pl.pallas_call(
        _chunkattn_kernel,
        grid=(ngroups, B),
        in_specs=[
            pl.BlockSpec((CA_PAIRS, 2, (CA_NKB + 1) * CA_TQ), lambda p, b: (p, 0, 0)),
            pl.BlockSpec((1, S, width), lambda p, b: (b, 0, QB_BLK // CA_PAIRS + p)),
            pl.BlockSpec((1, S, width), lambda p, b: (b, 0, KB_BLK // CA_PAIRS + p)),
            pl.BlockSpec((1, S, width), lambda p, b: (b, 0, VB_BLK // CA_PAIRS + p)),
            wspec,
        ],
        out_specs=[pl.BlockSpec((1, S, width), lambda p, b: (b, 0, p)), wspec],
        out_shape=[jax.ShapeDtypeStruct((B, S, N_HEADS_CHUNK * HEAD_DIM), BF16),
                   jax.ShapeDtypeStruct(w_cast.shape, BF16)],
        scratch_shapes=[pltpu.VMEM((CA_PAIRS, CA_NKB * CA_TQ, 2 * CA_TQ), F32),
                        pltpu.VMEM((CA_PAIRS, LANES + VT_ONES, S), BF16),
                        pltpu.VMEM((CA_SLOTS, CA_NKB * CA_TQ, 2 * CA_TQ), F32),
                        pltpu.VMEM((CA_SLOTS, 1, 2 * CA_TQ), F32)],
        compiler_params=_params(("arbitrary", "arbitrary")),
        name="chunkattn",
    )(bias, h3, h3, h3, w_cast)


def _mid_kernel(x_ref, ya_ref, yb_ref, mem_ref, wo_ref, wq_ref, wk_ref, wv_ref, wmo_ref,
                g1_ref, b1_ref, g2_ref, b2_ref, x2_ref, k_ref, v_ref, x1_ref):
    hd = MEM_HEAD_DIM
    half = ya_ref.shape[2]

    @pl.when(pl.program_id(1) == 0)
    def _():
        mb = mem_ref[0].astype(BF16)
        k_ref[...] = _dot(mb, wk_ref[...]).astype(BF16)
        v_ref[...] = _dot(mb, wv_ref[...]).astype(BF16)

    def sub_tile(rows):
        y = _dot(ya_ref[0, rows, :], wo_ref[0:half, :]) + _dot(yb_ref[0, rows, :], wo_ref[half:, :])
        yield
        x1 = _layer_norm(DEEPNORM_ALPHA * x_ref[0, rows, :] + y, g1_ref[...], b1_ref[...])
        x1_ref[rows, :] = x1
        q = (_dot(x1.astype(BF16), wq_ref[...]) * (hd ** -0.5 * LOG2E)).astype(BF16)
        yield
        outs = []
        for h in range(N_HEADS_MEM):
            sl = slice(h * hd, (h + 1) * hd)
            s = _dot_nt(q[:, sl], k_ref[:, sl])
            p = jnp.exp2(s - jnp.max(s, axis=1, keepdims=True))
            l = jnp.sum(p, axis=1, keepdims=True)
            outs.append((_dot(p.astype(BF16), v_ref[:, sl]) / l).astype(BF16))
            yield
        o = jnp.concatenate(outs, axis=1)
        y2 = _dot(o, wmo_ref[...])
        yield
        x2_ref[0, rows, :] = _layer_norm(DEEPNORM_ALPHA * x1_ref[rows, :] + y2, g2_ref[...], b2_ref[...])

    _wavefront([sub_tile(rows) for rows in _sub_tiles(x_ref.shape[1], 256)], lag=0)


def _mid_call(x, ya, yb, mem, w_o, w_mq, w_mk, w_mv, w_mo, g1, b1, g2, b2, tm):
    B, S, _ = x.shape
    M = mem.shape[1]
    half = ya.shape[2]
    row = pl.BlockSpec((1, tm, D_MODEL), lambda bb, i: (bb, i, 0))
    hrow = pl.BlockSpec((1, tm, half), lambda bb, i: (bb, i, 0))
    memblk = pl.BlockSpec((1, M, D_MODEL), lambda bb, i: (bb, 0, 0))
    vec = pl.BlockSpec((1, D_MODEL), lambda bb, i: (0, 0))
    wspec = _resident((D_MODEL, D_MODEL))
    return pl.pallas_call(
        _mid_kernel,
        grid=(B, S // tm),
        in_specs=[row, hrow, hrow, memblk] + [wspec] * 5 + [vec] * 4,
        out_specs=row,
        out_shape=jax.ShapeDtypeStruct((B, S, D_MODEL), F32),
        scratch_shapes=[pltpu.VMEM((M, D_MODEL), BF16)] * 2 + [pltpu.VMEM((tm, D_MODEL), F32)],
        compiler_params=_params(("arbitrary", "arbitrary")),
        name="mid",
    )(x, ya, yb, mem, w_o, w_mq, w_mk, w_mv, w_mo, g1, b1, g2, b2)


def _mlp_kernel(x_ref, wu_ref, wd_ref, g_ref, b_ref, o_ref, *, ff_chunk):
    def finish(rows, y):
        o_ref[rows, :] = _layer_norm(DEEPNORM_ALPHA * x_ref[rows, :] + y, g_ref[...], b_ref[...])

    pending = None
    for rows in _sub_tiles(x_ref.shape[0], 256):
        xb = x_ref[rows, :].astype(BF16)
        y = None
        for c in range(D_FF // ff_chunk):
            sl = slice(c * ff_chunk, (c + 1) * ff_chunk)
            h = jnp.maximum(_dot(xb, wu_ref[:, sl]), 0.0)
            part = _dot((h * h).astype(BF16), wd_ref[sl, :])
            y = part if y is None else y + part
            if c == 0 and pending is not None:
                finish(*pending)
                pending = None
        pending = (rows, y)
    finish(*pending)


def _mlp_call(x2d, w_up, w_down, g, b, tm, ff_chunk):
    T = x2d.shape[0]
    row = pl.BlockSpec((tm, D_MODEL), lambda i: (i, 0))
    vec = pl.BlockSpec((1, D_MODEL), lambda i: (0, 0))
    return pl.pallas_call(
        functools.partial(_mlp_kernel, ff_chunk=ff_chunk),
        grid=(T // tm,),
        in_specs=[row,
                  pl.BlockSpec((D_MODEL, D_FF), lambda i: (0, 0), pipeline_mode=pl.Buffered(1)),
                  pl.BlockSpec((D_FF, D_MODEL), lambda i: (0, 0), pipeline_mode=pl.Buffered(1)),
                  vec, vec],
        out_specs=row,
        out_shape=jax.ShapeDtypeStruct((T, D_MODEL), F32),
        compiler_params=_params(("arbitrary",)),
        name="mlp",
    )(x2d, w_up, w_down, g, b)


def kernel(x, mem, positions, w_in, diff_lambda, subln_g, rel_bias, w_o, ln1_g, ln1_b,
           w_mq, w_mk, w_mv, w_mo, ln2_g, ln2_b, w_up, w_down, ln3_g, ln3_b):
    B, S, D = x.shape
    T = B * S
    depth = w_in.shape[0]
    assert depth == DEPTH and D == D_MODEL and S % CA_TQ == 0
    inv_freq = 1.0 / (ROPE_THETA ** (jnp.arange(0, HEAD_DIM, 2, dtype=F32) / HEAD_DIM))
    invf = jnp.tile(inv_freq, LANES // (HEAD_DIM // 2)).reshape(1, LANES)
    pos = positions.reshape(T // IN_TM, ROPE_PACK, IN_TM // ROPE_PACK).transpose(0, 2, 1)
    pos = jnp.repeat(pos, HEAD_DIM // 2, axis=2).reshape(T // ROPE_PACK, LANES)
    vec = lambda a: a.reshape(1, -1)

    for l in range(depth):
        lambda_init = 0.8 - 0.6 * math.exp(-0.3 * l)
        h, mid_w = _inproj_call(pos, invf, x.reshape(T, D), w_in[l],
                                [w_o[l], w_mq[l], w_mk[l], w_mv[l], w_mo[l]], tm=IN_TM)
        h3 = h.reshape(B, S, IN_WIDTH)
        ya, w_up_b = _diffattn_call(diff_lambda[l], vec(subln_g[l]), h3, w_up[l], tq=DA_TQ,
                                    lambda_init=lambda_init)
        yb, w_down_b = _chunkattn_call(_chunk_bias_table(rel_bias[l]), h3, w_down[l])
        x2 = _mid_call(x, ya, yb, mem, *mid_w, vec(ln1_g[l]), vec(ln1_b[l]), vec(ln2_g[l]), vec(ln2_b[l]),
                       tm=1024)
        out = _mlp_call(x2.reshape(T, D), w_up_b, w_down_b,
                        vec(ln3_g[l]), vec(ln3_b[l]), tm=1024, ff_chunk=1024)
        x = out.reshape(B, S, D)
    return x
```

```python
import functools
import math

import jax
import jax.numpy as jnp
import numpy as np
from jax import lax
from jax.experimental import pallas as pl
from jax.experimental.pallas import tpu as pltpu

D_MODEL = 1024
CHUNK = 64
HEAD_DIM = 64
N_HEADS_DIFF = 4
DIFF_V_DIM = 128
N_HEADS_CHUNK = 8
LEFT_CHUNKS = 8
REL_CLIP = 128
N_HEADS_MEM = 4
MEM_HEAD_DIM = 256
D_FF = 4096
ROPE_THETA = 10000.0
LN_EPS = 1e-5
NEG_INF = -1e30
LOG2E = math.log2(math.e)
DEPTH = 1
DEEPNORM_ALPHA = (2.0 * DEPTH) ** 0.25
IN_WIDTH = 3072

LANES = 128
VMEM_LIMIT = 56 * 1024 * 1024

F32 = jnp.float32
BF16 = jnp.bfloat16

QA_BLK, KA_BLK, VA_BLK, QB_BLK, KB_BLK, VB_BLK = 0, 4, 8, 12, 16, 20


def _params(sem):
    return pltpu.CompilerParams(dimension_semantics=sem, vmem_limit_bytes=VMEM_LIMIT)


def _dot(a, b):
    return jnp.dot(a, b, preferred_element_type=F32)


def _dot_nt(a, b):
    return lax.dot_general(a, b, (((1,), (1,)), ((), ())), preferred_element_type=F32)


SUB_ROWS = 512


def _sub_tiles(n_rows, sub_rows=SUB_ROWS):
    return [slice(r, r + sub_rows) for r in range(0, n_rows, sub_rows)]


def _resident(shape):
    return pl.BlockSpec(shape, lambda *_: (0,) * len(shape), pipeline_mode=pl.Buffered(1))


def _cast_weights_once(first_step, pairs):
    @pl.when(first_step)
    def _():
        for src_ref, dst_ref in pairs:
            dst_ref[...] = src_ref[...].astype(dst_ref.dtype)


def _layer_norm(z, g, b):
    mu = jnp.mean(z, axis=-1, keepdims=True)
    zc = z - mu
    var = jnp.mean(zc * zc, axis=-1, keepdims=True)
    return zc * lax.rsqrt(var + LN_EPS) * g + b


IN_TM = 1024
ROPE_PACK = LANES // (HEAD_DIM // 2)


def _spread_token_groups(t):
    n = t.shape[0]
    grp = lax.broadcasted_iota(jnp.int32, t.shape, 1) // (HEAD_DIM // 2)
    out = []
    for a in range(ROPE_PACK):
        spread = t
        for g in range(ROPE_PACK):
            if g != a:
                moved = pltpu.roll(t, ((g - a) % ROPE_PACK) * (HEAD_DIM // 2), 1)
                spread = jnp.where(grp == g, moved, spread)
        out.append(spread)
    return jnp.concatenate(out, axis=0)


def _inproj_kernel(pos_ref, invf_ref, x_ref, wf_ref, *refs):
    n_cast = (len(refs) - 2) // 2
    cast_in, o_ref, cast_out, w_ref = refs[:n_cast], refs[n_cast], refs[n_cast + 1:-1], refs[-1]
    for src_ref, dst_ref in zip(cast_in, cast_out):
        dst_ref[...] = src_ref[...].astype(dst_ref.dtype)
    _cast_weights_once(pl.program_id(0) == 0, [(wf_ref, w_ref)])
    xb = x_ref[...].astype(BF16)
    ang = pos_ref[...].astype(F32) * invf_ref[...]
    cos = _spread_token_groups(jnp.cos(ang))
    sin = _spread_token_groups(jnp.sin(ang))
    lane = lax.broadcasted_iota(jnp.int32, cos.shape, 1)
    upper = (lane % HEAD_DIM) >= (HEAD_DIM // 2)
    sin_up = jnp.where(upper, sin, 0.0)
    sin_lo = jnp.where(upper, 0.0, -sin)
    n_chunks = IN_WIDTH // 512
    for c in range(n_chunks):
        h = _dot(xb, w_ref[:, c * 512:(c + 1) * 512])
        if c < 2:
            slabs = []
            for s in range(4):
                sl = h[:, s * LANES:(s + 1) * LANES]
                slabs.append(sl * cos
                             + pltpu.roll(sl, HEAD_DIM // 2, 1) * sin_up
                             + pltpu.roll(sl, LANES - HEAD_DIM // 2, 1) * sin_lo)
            h = jnp.concatenate(slabs, axis=1)
        if c == 0 or c == 3:
            h = h * (HEAD_DIM ** -0.5 * LOG2E)
        o_ref[:, c * 512:(c + 1) * 512] = h.astype(BF16)


def _inproj_call(pos, invf, x2d, w_in, w_cast, tm):
    T = x2d.shape[0]
    steps = T // tm
    cast_specs = [pl.BlockSpec((w.shape[0] // steps, w.shape[1]), lambda i: (i, 0)) for w in w_cast]
    outs = pl.pallas_call(
        _inproj_kernel,
        grid=(steps,),
        in_specs=[
            pl.BlockSpec((tm // ROPE_PACK, LANES), lambda i: (i, 0)),
            pl.BlockSpec((1, LANES), lambda i: (0, 0)),
            pl.BlockSpec((tm, D_MODEL), lambda i: (i, 0)),
            _resident((D_MODEL, IN_WIDTH)),
        ] + cast_specs,
        out_specs=[pl.BlockSpec((tm, IN_WIDTH), lambda i: (i, 0))] + cast_specs,
        out_shape=[jax.ShapeDtypeStruct((T, IN_WIDTH), BF16)]
        + [jax.ShapeDtypeStruct(w.shape, BF16) for w in w_cast],
        scratch_shapes=[pltpu.VMEM((D_MODEL, IN_WIDTH), BF16)],
        compiler_params=_params(("arbitrary",)),
        name="inproj",
    )(pos, invf, x2d, w_in, *w_cast)
    return outs[0], outs[1:]


def _interleave(*generators):
    pending = list(generators)
    while pending:
        for g in list(pending):
            try:
                next(g)
            except StopIteration:
                pending.remove(g)


def _wavefront(generators, lag=1):
    pending = dict(enumerate(generators))
    wave = 0
    while pending:
        for t in sorted(pending, reverse=True):
            if wave >= t * lag:
                try:
                    next(pending[t])
                except StopIteration:
                    del pending[t]
        wave += 1


def _stack_halves(q):
    lane = lax.broadcasted_iota(jnp.int32, q.shape, 1)
    zero = jnp.zeros_like(q)
    return jnp.concatenate([jnp.where(lane < HEAD_DIM, q, zero),
                            jnp.where(lane >= HEAD_DIM, q, zero)], axis=0)


DA_TQ = 256
DA_SLOTS = 3
DA_HEADS = 2
VT_ONES = 16


def _fill_vt_ones(vt_ref, v):
    nv = v.shape[1]
    vt_ref[0:nv, :] = v.astype(F32).T.astype(vt_ref.dtype)
    vt_ref[nv:, :] = jnp.ones((vt_ref.shape[0] - nv, vt_ref.shape[1]), vt_ref.dtype)


def _diag_tiles(tq):
    assert tq == 2 * LANES
    return [(rh, ct) for rh in range(tq // LANES) for ct in range(2 * tq // LANES)
            if not (rh == 1 and ct % 2 == 0)]


def _zero_after(x):
    bits = lax.bitcast_convert_type(x, jnp.uint32)
    bits = lax.shift_right_logical(lax.shift_right_logical(bits, jnp.uint32(16)), jnp.uint32(16))
    return lax.bitcast_convert_type(bits, F32)


def _diff_scores(nkb, q, k_ref, mask_ref, s_ref, m_ref, pace, *, tq):
    qq = _stack_halves(q)
    m = None
    for j in range(nkb):
        s = _dot_nt(k_ref[j * tq:(j + 1) * tq, :], qq)
        pace[:] = pace[1:] + [s[0:1, :]]
        if j < nkb - 1:
            s_ref[j * tq:(j + 1) * tq, :] = s
            mj = jnp.max(s, axis=0, keepdims=True)
        else:
            m_tiles = [None] * (2 * tq // LANES)
            for rh, ct in _diag_tiles(tq):
                rows = slice(rh * LANES, (rh + 1) * LANES)
                lanes = slice(ct * LANES, (ct + 1) * LANES)
                st = s[rows, lanes] + mask_ref[rows, lanes]
                s_ref[j * tq + rh * LANES:j * tq + (rh + 1) * LANES, lanes] = st
                mt = jnp.max(st, axis=0, keepdims=True)
                m_tiles[ct] = mt if m_tiles[ct] is None else jnp.maximum(m_tiles[ct], mt)
            mj = jnp.concatenate(m_tiles, axis=1)
        m = mj if m is None else jnp.maximum(m, mj)
        yield
    m_ref[...] = m


def _diff_output(nkb, s_ref, m_ref, vt_ref, lam, g_ref, o_ref, pace, *, tq, lambda_init):
    m0 = m_ref[...]
    acc = None
    for j in range(nkb):
        rows = slice(j * tq, (j + 1) * tq)
        m = m0 if pace[0] is None else m0 + _zero_after(pace[0])
        if j < nkb - 1:
            p = jnp.exp2(s_ref[rows, :] - m).astype(BF16)
        else:
            vis = _diag_tiles(tq)
            p = jnp.concatenate([jnp.concatenate([
                jnp.exp2(s_ref[j * tq + rh * LANES:j * tq + (rh + 1) * LANES, ct * LANES:(ct + 1) * LANES]
                         - m[:, ct * LANES:(ct + 1) * LANES]).astype(BF16)
                if (rh, ct) in vis else jnp.zeros((LANES, LANES), BF16)
                for ct in range(2 * tq // LANES)], axis=1) for rh in range(tq // LANES)], axis=0)
        part = _dot(vt_ref[:, rows], p)
        acc = part if acc is None else acc + part
        yield
    o = acc[:DIFF_V_DIM, :] * (1.0 / acc[DIFF_V_DIM:DIFF_V_DIM + 1, :])
    o = o[:, :tq] - lam * o[:, tq:]
    y = (o * lax.rsqrt(jnp.mean(o * o, axis=0, keepdims=True) + LN_EPS)).T * g_ref[...]
    o_ref[...] = (y * (1.0 - lambda_init)).astype(o_ref.dtype)


def _diffattn_kernel(dl_ref, g_ref, mask_ref, q_ref, k_ref, v_ref, wf_ref, o_ref, wb_ref, vt_ref, s_ref,
                     m_ref, *, tq, nq, lambda_init):
    wb_ref[...] = wf_ref[...].astype(wb_ref.dtype)
    dl = dl_ref[...]
    lam = (jnp.exp(jnp.sum(dl[0:1, :] * dl[1:2, :], axis=1, keepdims=True))
           - jnp.exp(jnp.sum(dl[2:3, :] * dl[3:4, :], axis=1, keepdims=True)) + lambda_init)
    for hh in range(DA_HEADS):
        _fill_vt_ones(vt_ref.at[hh], v_ref[0, :, hh * LANES:(hh + 1) * LANES])
    stages = [(hh, c) for hh in range(DA_HEADS) for c in range(nq)]
    pace = [None] * 2

    def scores(i):
        hh, c = stages[i]
        cols = slice(hh * LANES, (hh + 1) * LANES)
        return _diff_scores(c + 1, q_ref[0, c * tq:(c + 1) * tq, cols], k_ref.at[0, :, cols], mask_ref,
                            s_ref.at[i % DA_SLOTS], m_ref.at[i % DA_SLOTS], pace, tq=tq)

    for i in range(DA_SLOTS - 1):
        _interleave(scores(i))
    for i, (hh, c) in enumerate(stages):
        nxt = [scores(i + DA_SLOTS - 1)] if i + DA_SLOTS - 1 < len(stages) else []
        _interleave(*nxt, _diff_output(c + 1, s_ref.at[i % DA_SLOTS], m_ref.at[i % DA_SLOTS], vt_ref.at[hh],
                                       lam, g_ref,
                                       o_ref.at[0, c * tq:(c + 1) * tq, hh * LANES:(hh + 1) * LANES],
                                       pace if nxt else [None], tq=tq,
                                       lambda_init=lambda_init))


def _diffattn_call(diff_lambda, g, h3, w_cast, tq, lambda_init):
    B, S, _ = h3.shape
    nq = S // tq
    wrows, wcols = w_cast.shape
    ngroups = N_HEADS_DIFF // DA_HEADS
    width = DA_HEADS * LANES
    wspec = pl.BlockSpec((wrows // (B * ngroups), wcols), lambda b, h: (b * ngroups + h, 0))
    key_chunk = np.arange(tq)[:, None] // CHUNK
    query_chunk = (np.arange(2 * tq)[None, :] % tq) // CHUNK
    diag_mask = jnp.asarray(np.where(key_chunk <= query_chunk, 0.0, NEG_INF), F32)
    return pl.pallas_call(
        functools.partial(_diffattn_kernel, tq=tq, nq=nq, lambda_init=lambda_init),
        grid=(B, ngroups),
        scratch_shapes=[pltpu.VMEM((DA_HEADS, DIFF_V_DIM + VT_ONES, S), BF16),
                        pltpu.VMEM((DA_SLOTS, S, 2 * tq), F32),
                        pltpu.VMEM((DA_SLOTS, 1, 2 * tq), F32)],
        in_specs=[
            pl.BlockSpec(diff_lambda.shape, lambda b, h: (0, 0)),
            pl.BlockSpec((1, DIFF_V_DIM), lambda b, h: (0, 0)),
            pl.BlockSpec((tq, 2 * tq), lambda b, h: (0, 0)),
            pl.BlockSpec((1, S, width), lambda b, h: (b, 0, QA_BLK // DA_HEADS + h)),
            pl.BlockSpec((1, S, width), lambda b, h: (b, 0, KA_BLK // DA_HEADS + h)),
            pl.BlockSpec((1, S, width), lambda b, h: (b, 0, VA_BLK // DA_HEADS + h)),
            wspec,
        ],
        out_specs=[pl.BlockSpec((1, S, width), lambda b, h: (b, 0, h)), wspec],
        out_shape=[jax.ShapeDtypeStruct((B, S, N_HEADS_DIFF * DIFF_V_DIM), BF16),
                   jax.ShapeDtypeStruct(w_cast.shape, BF16)],
        compiler_params=_params(("arbitrary", "arbitrary")),
        name="diffattn",
    )(diff_lambda, g, diag_mask, h3, h3, h3, w_cast)


CA_TQ = 256
CA_NKB = 3
CA_SLOTS = 3
CA_PAIRS = 2


def _chunk_bias_table(rel_bias):
    nh = rel_bias.shape[0]
    n_neg = CA_TQ - REL_CLIP + 1
    n_far = CA_NKB * CA_TQ - REL_CLIP
    by_dist = jnp.concatenate([
        jnp.broadcast_to(rel_bias[:, :1], (nh, n_neg)),
        rel_bias[:, 1:2 * REL_CLIP],
        jnp.broadcast_to(rel_bias[:, 2 * REL_CLIP:], (nh, n_far)),
    ], axis=1).astype(F32) * LOG2E
    assert by_dist.shape[1] == (CA_NKB + 1) * CA_TQ
    return by_dist.reshape(nh // 2, 2, (CA_NKB + 1) * CA_TQ)


def _build_bias_table(dist_ref, bias_ref):
    row = lax.broadcasted_iota(jnp.int32, (CA_TQ, CA_TQ), 0) // CHUNK
    col = lax.broadcasted_iota(jnp.int32, (CA_TQ, CA_TQ), 1) // CHUNK
    for t in range(2):
        for j in range(CA_NKB):
            d0 = (CA_NKB - 1 - j) * CA_TQ
            g = jnp.concatenate([dist_ref[t:t + 1, d0 + CA_TQ:d0 + 2 * CA_TQ],
                                 dist_ref[t:t + 1, d0:d0 + CA_TQ]], axis=1)
            rolled = pltpu.roll(jnp.broadcast_to(g, (CA_TQ, 2 * CA_TQ)), 0, 1, stride=1, stride_axis=0)
            blk = rolled[:, :CA_TQ]
            if j == 0:
                blk = jnp.where(row >= col, blk, NEG_INF)
            elif j == CA_NKB - 1:
                blk = jnp.where(row <= col, blk, NEG_INF)
            bias_ref[j * CA_TQ:(j + 1) * CA_TQ, t * CA_TQ:(t + 1) * CA_TQ] = blk


def _chunkattn_kernel(dist_ref, q_ref, k_ref, v_ref, wf_ref, o_ref, wb_ref, bias_ref, vt_ref, s_ref, m_ref):
    nq = q_ref.shape[1] // CA_TQ
    wb_ref[...] = wf_ref[...].astype(wb_ref.dtype)

    @pl.when(pl.program_id(1) == 0)
    def _():
        for pp in range(CA_PAIRS):
            _build_bias_table(dist_ref.at[pp], bias_ref.at[pp])

    def key_blocks(blk):
        return [(j, blk - (CA_NKB - 1) + j) for j in range(CA_NKB) if blk - (CA_NKB - 1) + j >= 0]

    def visible_tiles(j):
        tiles = [(rh, ct) for rh in range(CA_TQ // LANES) for ct in range(2 * CA_TQ // LANES)]
        if j == 0:
            return [(rh, ct) for rh, ct in tiles if not (rh == 0 and ct % 2 == 1)]
        if j == CA_NKB - 1:
            return [(rh, ct) for rh, ct in tiles if not (rh == 1 and ct % 2 == 0)]
        return tiles

    def scores(pp, blk, slot):
        cols = slice(pp * LANES, (pp + 1) * LANES)
        qq = _stack_halves(q_ref[0, blk * CA_TQ:(blk + 1) * CA_TQ, cols])
        m = [None] * (2 * CA_TQ // LANES)
        for j, kb in key_blocks(blk):
            s = _dot_nt(k_ref[0, kb * CA_TQ:(kb + 1) * CA_TQ, cols], qq)
            for rh, ct in visible_tiles(j):
                rows = slice(j * CA_TQ + rh * LANES, j * CA_TQ + (rh + 1) * LANES)
                lanes = slice(ct * LANES, (ct + 1) * LANES)
                st = s[rh * LANES:(rh + 1) * LANES, lanes] + bias_ref[pp, rows, lanes]
                s_ref[slot, rows, lanes] = st
                mt = jnp.max(st, axis=0, keepdims=True)
                m[ct] = mt if m[ct] is None else jnp.maximum(m[ct], mt)
            yield
        m_ref[slot] = jnp.concatenate(m, axis=1)

    def output(pp, blk, slot):
        m = m_ref[slot]
        acc = None
        for j, kb in key_blocks(blk):
            vis = visible_tiles(j)
            p_rows = []
            for rh in range(CA_TQ // LANES):
                p_tiles = []
                for ct in range(2 * CA_TQ // LANES):
                    if (rh, ct) in vis:
                        rows = slice(j * CA_TQ + rh * LANES, j * CA_TQ + (rh + 1) * LANES)
                        lanes = slice(ct * LANES, (ct + 1) * LANES)
                        p_tiles.append(jnp.exp2(s_ref[slot, rows, lanes] - m[:, lanes]).astype(BF16))
                    else:
                        p_tiles.append(jnp.zeros((LANES, LANES), BF16))
                p_rows.append(jnp.concatenate(p_tiles, axis=1))
            p = jnp.concatenate(p_rows, axis=0)
            part = _dot(vt_ref[pp, :, kb * CA_TQ:(kb + 1) * CA_TQ], p)
            acc = part if acc is None else acc + part
            yield
        inv_l = 1.0 / acc[LANES:LANES + 1, :]
        o_t = jnp.concatenate([acc[:HEAD_DIM, :CA_TQ] * inv_l[:, :CA_TQ],
                               acc[HEAD_DIM:LANES, CA_TQ:] * inv_l[:, CA_TQ:]], axis=0)
        o_ref[0, blk * CA_TQ:(blk + 1) * CA_TQ, pp * LANES:(pp + 1) * LANES] = o_t.T.astype(o_ref.dtype)

    for pp in range(CA_PAIRS):
        _fill_vt_ones(vt_ref.at[pp], v_ref[0, :, pp * LANES:(pp + 1) * LANES])
    stages = [(pp, blk) for pp in range(CA_PAIRS) for blk in range(nq)]
    ahead = CA_SLOTS - 1
    for i in range(ahead):
        _interleave(scores(*stages[i], i % CA_SLOTS))
    for i, stage in enumerate(stages):
        nxt = [scores(*stages[i + ahead], (i + ahead) % CA_SLOTS)] if i + ahead < len(stages) else []
        _interleave(*nxt, output(*stage, i % CA_SLOTS))


def _chunkattn_call(bias, h3, w_cast):
    B, S, _ = h3.shape
    ngroups = N_HEADS_CHUNK // 2 // CA_PAIRS
    width = CA_PAIRS * LANES
    wrows, wcols = w_cast.shape
    wspec = pl.BlockSpec((wrows // (ngroups * B), wcols), lambda p, b: (p * B + b, 0))
    return pl.pallas_call(
        _chunkattn_kernel,
        grid=(ngroups, B),
        in_specs=[
            pl.BlockSpec((CA_PAIRS, 2, (CA_NKB + 1) * CA_TQ), lambda p, b: (p, 0, 0)),
            pl.BlockSpec((1, S, width), lambda p, b: (b, 0, QB_BLK // CA_PAIRS + p)),
            pl.BlockSpec((1, S, width), lambda p, b: (b, 0, KB_BLK // CA_PAIRS + p)),
            pl.BlockSpec((1, S, width), lambda p, b: (b, 0, VB_BLK // CA_PAIRS + p)),
            wspec,
        ],
        out_specs=[pl.BlockSpec((1, S, width), lambda p, b: (b, 0, p)), wspec],
        out_shape=[jax.ShapeDtypeStruct((B, S, N_HEADS_CHUNK * HEAD_DIM), BF16),
                   jax.ShapeDtypeStruct(w_cast.shape, BF16)],
        scratch_shapes=[pltpu.VMEM((CA_PAIRS, CA_NKB * CA_TQ, 2 * CA_TQ), F32),
                        pltpu.VMEM((CA_PAIRS, LANES + VT_ONES, S), BF16),
                        pltpu.VMEM((CA_SLOTS, CA_NKB * CA_TQ, 2 * CA_TQ), F32),
                        pltpu.VMEM((CA_SLOTS, 1, 2 * CA_TQ), F32)],
        compiler_params=_params(("arbitrary", "arbitrary")),
        name="chunkattn",
    )(bias, h3, h3, h3, w_cast)


def _mid_kernel(x_ref, ya_ref, yb_ref, mem_ref, wo_ref, wq_ref, wk_ref, wv_ref, wmo_ref,
                g1_ref, b1_ref, g2_ref, b2_ref, x2_ref, k_ref, v_ref, x1_ref):
    hd = MEM_HEAD_DIM
    half = ya_ref.shape[2]

    @pl.when(pl.program_id(1) == 0)
    def _():
        mb = mem_ref[0].astype(BF16)
        k_ref[...] = _dot(mb, wk_ref[...]).astype(BF16)
        v_ref[...] = _dot(mb, wv_ref[...]).astype(BF16)

    def sub_tile(rows):
        y = _dot(ya_ref[0, rows, :], wo_ref[0:half, :]) + _dot(yb_ref[0, rows, :], wo_ref[half:, :])
        yield
        x1 = _layer_norm(DEEPNORM_ALPHA * x_ref[0, rows, :] + y, g1_ref[...], b1_ref[...])
        x1_ref[rows, :] = x1
        q = (_dot(x1.astype(BF16), wq_ref[...]) * (hd ** -0.5 * LOG2E)).astype(BF16)
        yield
        outs = []
        for h in range(N_HEADS_MEM):
            sl = slice(h * hd, (h + 1) * hd)
            s = _dot_nt(q[:, sl], k_ref[:, sl])
            p = jnp.exp2(s - jnp.max(s, axis=1, keepdims=True))
            l = jnp.sum(p, axis=1, keepdims=True)
            outs.append((_dot(p.astype(BF16), v_ref[:, sl]) / l).astype(BF16))
            yield
        o = jnp.concatenate(outs, axis=1)
        y2 = _dot(o, wmo_ref[...])
        yield
        x2_ref[0, rows, :] = _layer_norm(DEEPNORM_ALPHA * x1_ref[rows, :] + y2, g2_ref[...], b2_ref[...])

    _wavefront([sub_tile(rows) for rows in _sub_tiles(x_ref.shape[1], 256)], lag=0)


def _mid_call(x, ya, yb, mem, w_o, w_mq, w_mk, w_mv, w_mo, g1, b1, g2, b2, tm):
    B, S, _ = x.shape
    M = mem.shape[1]
    half = ya.shape[2]
    row = pl.BlockSpec((1, tm, D_MODEL), lambda bb, i: (bb, i, 0))
    hrow = pl.BlockSpec((1, tm, half), lambda bb, i: (bb, i, 0))
    memblk = pl.BlockSpec((1, M, D_MODEL), lambda bb, i: (bb, 0, 0))
    vec = pl.BlockSpec((1, D_MODEL), lambda bb, i: (0, 0))
    wspec = _resident((D_MODEL, D_MODEL))
    return pl.pallas_call(
        _mid_kernel,
        grid=(B, S // tm),
        in_specs=[row, hrow, hrow, memblk] + [wspec] * 5 + [vec] * 4,
        out_specs=row,
        out_shape=jax.ShapeDtypeStruct((B, S, D_MODEL), F32),
        scratch_shapes=[pltpu.VMEM((M, D_MODEL), BF16)] * 2 + [pltpu.VMEM((tm, D_MODEL), F32)],
        compiler_params=_params(("arbitrary", "arbitrary")),
        name="mid",
    )(x, ya, yb, mem, w_o, w_mq, w_mk, w_mv, w_mo, g1, b1, g2, b2)


def _mlp_kernel(x_ref, wu_ref, wd_ref, g_ref, b_ref, o_ref, *, ff_chunk):
    def finish(rows, y):
        o_ref[rows, :] = _layer_norm(DEEPNORM_ALPHA * x_ref[rows, :] + y, g_ref[...], b_ref[...])

    pending = None
    for rows in _sub_tiles(x_ref.shape[0], 256):
        xb = x_ref[rows, :].astype(BF16)
        y = None
        for c in range(D_FF // ff_chunk):
            sl = slice(c * ff_chunk, (c + 1) * ff_chunk)
            h = jnp.maximum(_dot(xb, wu_ref[:, sl]), 0.0)
            part = _dot((h * h).astype(BF16), wd_ref[sl, :])
            y = part if y is None else y + part
            if c == 0 and pending is not None:
                finish(*pending)
                pending = None
        pending = (rows, y)
    finish(*pending)


def _mlp_call(x2d, w_up, w_down, g, b, tm, ff_chunk):
    T = x2d.shape[0]
    row = pl.BlockSpec((tm, D_MODEL), lambda i: (i, 0))
    vec = pl.BlockSpec((1, D_MODEL), lambda i: (0, 0))
    return pl.pallas_call(
        functools.partial(_mlp_kernel, ff_chunk=ff_chunk),
        grid=(T // tm,),
        in_specs=[row,
                  pl.BlockSpec((D_MODEL, D_FF), lambda i: (0, 0), pipeline_mode=pl.Buffered(1)),
                  pl.BlockSpec((D_FF, D_MODEL), lambda i: (0, 0), pipeline_mode=pl.Buffered(1)),
                  vec, vec],
        out_specs=row,
        out_shape=jax.ShapeDtypeStruct((T, D_MODEL), F32),
        compiler_params=_params(("arbitrary",)),
        name="mlp",
    )(x2d, w_up, w_down, g, b)


def kernel(x, mem, positions, w_in, diff_lambda, subln_g, rel_bias, w_o, ln1_g, ln1_b,
           w_mq, w_mk, w_mv, w_mo, ln2_g, ln2_b, w_up, w_down, ln3_g, ln3_b):
    B, S, D = x.shape
    T = B * S
    depth = w_in.shape[0]
    assert depth == DEPTH and D == D_MODEL and S % CA_TQ == 0
    inv_freq = 1.0 / (ROPE_THETA ** (jnp.arange(0, HEAD_DIM, 2, dtype=F32) / HEAD_DIM))
    invf = jnp.tile(inv_freq, LANES // (HEAD_DIM // 2)).reshape(1, LANES)
    pos = positions.reshape(T // IN_TM, ROPE_PACK, IN_TM // ROPE_PACK).transpose(0, 2, 1)
    pos = jnp.repeat(pos, HEAD_DIM // 2, axis=2).reshape(T // ROPE_PACK, LANES)
    vec = lambda a: a.reshape(1, -1)

    for l in range(depth):
        lambda_init = 0.8 - 0.6 * math.exp(-0.3 * l)
        h, mid_w = _inproj_call(pos, invf, x.reshape(T, D), w_in[l],
                                [w_o[l], w_mq[l], w_mk[l], w_mv[l], w_mo[l]], tm=IN_TM)
        h3 = h.reshape(B, S, IN_WIDTH)
        ya, w_up_b = _diffattn_call(diff_lambda[l], vec(subln_g[l]), h3, w_up[l], tq=DA_TQ,
                                    lambda_init=lambda_init)
        yb, w_down_b = _chunkattn_call(_chunk_bias_table(rel_bias[l]), h3, w_down[l])
        x2 = _mid_call(x, ya, yb, mem, *mid_w, vec(ln1_g[l]), vec(ln1_b[l]), vec(ln2_g[l]), vec(ln2_b[l]),
                       tm=1024)
        out = _mlp_call(x2.reshape(T, D), w_up_b, w_down_b,
                        vec(ln3_g[l]), vec(ln3_b[l]), tm=1024, ff_chunk=1024)
        x = out.reshape(B, S, D)
    return x
```

```python
import functools
import math

import jax
import jax.numpy as jnp
import numpy as np
from jax import lax
from jax.experimental import pallas as pl
from jax.experimental.pallas import tpu as pltpu

D_MODEL = 1024
CHUNK = 64
HEAD_DIM = 64
N_HEADS_DIFF = 4
DIFF_V_DIM = 128
N_HEADS_CHUNK = 8
LEFT_CHUNKS = 8
REL_CLIP = 128
N_HEADS_MEM = 4
MEM_HEAD_DIM = 256
D_FF = 4096
ROPE_THETA = 10000.0
LN_EPS = 1e-5
NEG_INF = -1e30
LOG2E = math.log2(math.e)
DEPTH = 1
DEEPNORM_ALPHA = (2.0 * DEPTH) ** 0.25
IN_WIDTH = 3072

LANES = 128
VMEM_LIMIT = 56 * 1024 * 1024

F32 = jnp.float32
BF16 = jnp.bfloat16

QA_BLK, KA_BLK, VA_BLK, QB_BLK, KB_BLK, VB_BLK = 0, 4, 8, 12, 16, 20


def _params(sem):
    return pltpu.CompilerParams(dimension_semantics=sem, vmem_limit_bytes=VMEM_LIMIT)


def _dot(a, b):
    return jnp.dot(a, b, preferred_element_type=F32)


def _dot_nt(a, b):
    return lax.dot_general(a, b, (((1,), (1,)), ((), ())), preferred_element_type=F32)


SUB_ROWS = 512


def _sub_tiles(n_rows, sub_rows=SUB_ROWS):
    return [slice(r, r + sub_rows) for r in range(0, n_rows, sub_rows)]


def _resident(shape):
    return pl.BlockSpec(shape, lambda *_: (0,) * len(shape), pipeline_mode=pl.Buffered(1))


def _cast_weights_once(first_step, pairs):
    @pl.when(first_step)
    def _():
        for src_ref, dst_ref in pairs:
            dst_ref[...] = src_ref[...].astype(dst_ref.dtype)


def _layer_norm(z, g, b):
    mu = jnp.mean(z, axis=-1, keepdims=True)
    zc = z - mu
    var = jnp.mean(zc * zc, axis=-1, keepdims=True)
    return zc * lax.rsqrt(var + LN_EPS) * g + b


IN_TM = 1024
ROPE_PACK = LANES // (HEAD_DIM // 2)


def _spread_token_groups(t):
    n = t.shape[0]
    grp = lax.broadcasted_iota(jnp.int32, t.shape, 1) // (HEAD_DIM // 2)
    out = []
    for a in range(ROPE_PACK):
        spread = t
        for g in range(ROPE_PACK):
            if g != a:
                moved = pltpu.roll(t, ((g - a) % ROPE_PACK) * (HEAD_DIM // 2), 1)
                spread = jnp.where(grp == g, moved, spread)
        out.append(spread)
    return jnp.concatenate(out, axis=0)


def _inproj_kernel(pos_ref, invf_ref, x_ref, wf_ref, *refs):
    n_cast = (len(refs) - 2) // 2
    cast_in, o_ref, cast_out, w_ref = refs[:n_cast], refs[n_cast], refs[n_cast + 1:-1], refs[-1]
    for src_ref, dst_ref in zip(cast_in, cast_out):
        dst_ref[...] = src_ref[...].astype(dst_ref.dtype)
    _cast_weights_once(pl.program_id(0) == 0, [(wf_ref, w_ref)])
    xb = x_ref[...].astype(BF16)
    ang = pos_ref[...].astype(F32) * invf_ref[...]
    cos = _spread_token_groups(jnp.cos(ang))
    sin = _spread_token_groups(jnp.sin(ang))
    lane = lax.broadcasted_iota(jnp.int32, cos.shape, 1)
    upper = (lane % HEAD_DIM) >= (HEAD_DIM // 2)
    sin_up = jnp.where(upper, sin, 0.0)
    sin_lo = jnp.where(upper, 0.0, -sin)
    n_chunks = IN_WIDTH // 512
    for c in range(n_chunks):
        h = _dot(xb, w_ref[:, c * 512:(c + 1) * 512])
        if c < 2:
            slabs = []
            for s in range(4):
                sl = h[:, s * LANES:(s + 1) * LANES]
                slabs.append(sl * cos
                             + pltpu.roll(sl, HEAD_DIM // 2, 1) * sin_up
                             + pltpu.roll(sl, LANES - HEAD_DIM // 2, 1) * sin_lo)
            h = jnp.concatenate(slabs, axis=1)
        if c == 0 or c == 3:
            h = h * (HEAD_DIM ** -0.5 * LOG2E)
        o_ref[:, c * 512:(c + 1) * 512] = h.astype(BF16)


def _inproj_call(pos, invf, x2d, w_in, w_cast, tm):
    T = x2d.shape[0]
    steps = T // tm
    cast_specs = [pl.BlockSpec((w.shape[0] // steps, w.shape[1]), lambda i: (i, 0)) for w in w_cast]
    outs = pl.pallas_call(
        _inproj_kernel,
        grid=(steps,),
        in_specs=[
            pl.BlockSpec((tm // ROPE_PACK, LANES), lambda i: (i, 0)),
            pl.BlockSpec((1, LANES), lambda i: (0, 0)),
            pl.BlockSpec((tm, D_MODEL), lambda i: (i, 0)),
            _resident((D_MODEL, IN_WIDTH)),
        ] + cast_specs,
        out_specs=[pl.BlockSpec((tm, IN_WIDTH), lambda i: (i, 0))] + cast_specs,
        out_shape=[jax.ShapeDtypeStruct((T, IN_WIDTH), BF16)]
        + [jax.ShapeDtypeStruct(w.shape, BF16) for w in w_cast],
        scratch_shapes=[pltpu.VMEM((D_MODEL, IN_WIDTH), BF16)],
        compiler_params=_params(("arbitrary",)),
        name="inproj",
    )(pos, invf, x2d, w_in, *w_cast)
    return outs[0], outs[1:]


def _interleave(*generators):
    pending = list(generators)
    while pending:
        for g in list(pending):
            try:
                next(g)
            except StopIteration:
                pending.remove(g)


def _wavefront(generators, lag=1):
    pending = dict(enumerate(generators))
    wave = 0
    while pending:
        for t in sorted(pending, reverse=True):
            if wave >= t * lag:
                try:
                    next(pending[t])
                except StopIteration:
                    del pending[t]
        wave += 1


def _stack_halves(q):
    lane = lax.broadcasted_iota(jnp.int32, q.shape, 1)
    zero = jnp.zeros_like(q)
    return jnp.concatenate([jnp.where(lane < HEAD_DIM, q, zero),
                            jnp.where(lane >= HEAD_DIM, q, zero)], axis=0)


DA_TQ = 256
DA_SLOTS = 3
DA_HEADS = 2
VT_ONES = 16


def _fill_vt_ones(vt_ref, v):
    nv = v.shape[1]
    vt_ref[0:nv, :] = v.astype(F32).T.astype(vt_ref.dtype)
    vt_ref[nv:, :] = jnp.ones((vt_ref.shape[0] - nv, vt_ref.shape[1]), vt_ref.dtype)


def _diag_tiles(tq):
    assert tq == 2 * LANES
    return [(rh, ct) for rh in range(tq // LANES) for ct in range(2 * tq // LANES)
            if not (rh == 1 and ct % 2 == 0)]


def _diff_scores(nkb, q, k_ref, mask_ref, s_ref, m_ref, *, tq):
    qq = _stack_halves(q)
    m = None
    for j in range(nkb):
        s = _dot_nt(k_ref[j * tq:(j + 1) * tq, :], qq)
        if j < nkb - 1:
            s_ref[j * tq:(j + 1) * tq, :] = s
            mj = jnp.max(s, axis=0, keepdims=True)
        else:
            m_tiles = [None] * (2 * tq // LANES)
            for rh, ct in _diag_tiles(tq):
                rows = slice(rh * LANES, (rh + 1) * LANES)
                lanes = slice(ct * LANES, (ct + 1) * LANES)
                st = s[rows, lanes] + mask_ref[rows, lanes]
                s_ref[j * tq + rh * LANES:j * tq + (rh + 1) * LANES, lanes] = st
                mt = jnp.max(st, axis=0, keepdims=True)
                m_tiles[ct] = mt if m_tiles[ct] is None else jnp.maximum(m_tiles[ct], mt)
            mj = jnp.concatenate(m_tiles, axis=1)
        m = mj if m is None else jnp.maximum(m, mj)
        yield
    m_ref[...] = m


def _diff_output(nkb, s_ref, m_ref, vt_ref, lam, g_ref, o_ref, *, tq, lambda_init):
    m = m_ref[...]
    acc = None
    for j in range(nkb):
        rows = slice(j * tq, (j + 1) * tq)
        if j < nkb - 1:
            p = jnp.exp2(s_ref[rows, :] - m).astype(BF16)
        else:
            vis = _diag_tiles(tq)
            p = jnp.concatenate([jnp.concatenate([
                jnp.exp2(s_ref[j * tq + rh * LANES:j * tq + (rh + 1) * LANES, ct * LANES:(ct + 1) * LANES]
                         - m[:, ct * LANES:(ct + 1) * LANES]).astype(BF16)
                if (rh, ct) in vis else jnp.zeros((LANES, LANES), BF16)
                for ct in range(2 * tq // LANES)], axis=1) for rh in range(tq // LANES)], axis=0)
        part = _dot(vt_ref[:, rows], p)
        acc = part if acc is None else acc + part
        yield
    o = acc[:DIFF_V_DIM, :] * (1.0 / acc[DIFF_V_DIM:DIFF_V_DIM + 1, :])
    o = o[:, :tq] - lam * o[:, tq:]
    y = (o * lax.rsqrt(jnp.mean(o * o, axis=0, keepdims=True) + LN_EPS)).T * g_ref[...]
    o_ref[...] = (y * (1.0 - lambda_init)).astype(o_ref.dtype)


def _diffattn_kernel(dl_ref, g_ref, mask_ref, q_ref, k_ref, v_ref, wf_ref, o_ref, wb_ref, vt_ref, s_ref,
                     m_ref, *, tq, nq, lambda_init):
    wb_ref[...] = wf_ref[...].astype(wb_ref.dtype)
    dl = dl_ref[...]
    lam = (jnp.exp(jnp.sum(dl[0:1, :] * dl[1:2, :], axis=1, keepdims=True))
           - jnp.exp(jnp.sum(dl[2:3, :] * dl[3:4, :], axis=1, keepdims=True)) + lambda_init)
    for hh in range(DA_HEADS):
        _fill_vt_ones(vt_ref.at[hh], v_ref[0, :, hh * LANES:(hh + 1) * LANES])
    stages = [(hh, c) for hh in range(DA_HEADS) for c in range(nq)]

    def scores(i):
        hh, c = stages[i]
        cols = slice(hh * LANES, (hh + 1) * LANES)
        return _diff_scores(c + 1, q_ref[0, c * tq:(c + 1) * tq, cols], k_ref.at[0, :, cols], mask_ref,
                            s_ref.at[i % DA_SLOTS], m_ref.at[i % DA_SLOTS], tq=tq)

    for i in range(DA_SLOTS - 1):
        _interleave(scores(i))
    for i, (hh, c) in enumerate(stages):
        nxt = [scores(i + DA_SLOTS - 1)] if i + DA_SLOTS - 1 < len(stages) else []
        _interleave(*nxt, _diff_output(c + 1, s_ref.at[i % DA_SLOTS], m_ref.at[i % DA_SLOTS], vt_ref.at[hh],
                                       lam, g_ref,
                                       o_ref.at[0, c * tq:(c + 1) * tq, hh * LANES:(hh + 1) * LANES], tq=tq,
                                       lambda_init=lambda_init))


def _diffattn_call(diff_lambda, g, h3, w_cast, tq, lambda_init):
    B, S, _ = h3.shape
    nq = S // tq
    wrows, wcols = w_cast.shape
    ngroups = N_HEADS_DIFF // DA_HEADS
    width = DA_HEADS * LANES
    wspec = pl.BlockSpec((wrows // (B * ngroups), wcols), lambda b, h: (b * ngroups + h, 0))
    key_chunk = np.arange(tq)[:, None] // CHUNK
    query_chunk = (np.arange(2 * tq)[None, :] % tq) // CHUNK
    diag_mask = jnp.asarray(np.where(key_chunk <= query_chunk, 0.0, NEG_INF), F32)
    return pl.pallas_call(
        functools.partial(_diffattn_kernel, tq=tq, nq=nq, lambda_init=lambda_init),
        grid=(B, ngroups),
        scratch_shapes=[pltpu.VMEM((DA_HEADS, DIFF_V_DIM + VT_ONES, S), BF16),
                        pltpu.VMEM((DA_SLOTS, S, 2 * tq), F32),
                        pltpu.VMEM((DA_SLOTS, 1, 2 * tq), F32)],
        in_specs=[
            pl.BlockSpec(diff_lambda.shape, lambda b, h: (0, 0)),
            pl.BlockSpec((1, DIFF_V_DIM), lambda b, h: (0, 0)),
            pl.BlockSpec((tq, 2 * tq), lambda b, h: (0, 0)),
            pl.BlockSpec((1, S, width), lambda b, h: (b, 0, QA_BLK // DA_HEADS + h)),
            pl.BlockSpec((1, S, width), lambda b, h: (b, 0, KA_BLK // DA_HEADS + h)),
            pl.BlockSpec((1, S, width), lambda b, h: (b, 0, VA_BLK // DA_HEADS + h)),
            wspec,
        ],
        out_specs=[pl.BlockSpec((1, S, width), lambda b, h: (b, 0, h)), wspec],
        out_shape=[jax.ShapeDtypeStruct((B, S, N_HEADS_DIFF * DIFF_V_DIM), BF16),
                   jax.ShapeDtypeStruct(w_cast.shape, BF16)],
        compiler_params=_params(("arbitrary", "arbitrary")),
        name="diffattn",
    )(diff_lambda, g, diag_mask, h3, h3, h3, w_cast)


CA_TQ = 256
CA_NKB = 3
CA_SLOTS = 3
CA_PAIRS = 2


def _chunk_bias_table(rel_bias):
    nh = rel_bias.shape[0]
    n_neg = CA_TQ - REL_CLIP + 1
    n_far = CA_NKB * CA_TQ - REL_CLIP
    by_dist = jnp.concatenate([
        jnp.broadcast_to(rel_bias[:, :1], (nh, n_neg)),
        rel_bias[:, 1:2 * REL_CLIP],
        jnp.broadcast_to(rel_bias[:, 2 * REL_CLIP:], (nh, n_far)),
    ], axis=1).astype(F32) * LOG2E
    assert by_dist.shape[1] == (CA_NKB + 1) * CA_TQ
    return by_dist.reshape(nh // 2, 2, (CA_NKB + 1) * CA_TQ)


def _build_bias_table(dist_ref, bias_ref):
    row = lax.broadcasted_iota(jnp.int32, (CA_TQ, CA_TQ), 0) // CHUNK
    col = lax.broadcasted_iota(jnp.int32, (CA_TQ, CA_TQ), 1) // CHUNK
    for t in range(2):
        for j in range(CA_NKB):
            d0 = (CA_NKB - 1 - j) * CA_TQ
            g = jnp.concatenate([dist_ref[t:t + 1, d0 + CA_TQ:d0 + 2 * CA_TQ],
                                 dist_ref[t:t + 1, d0:d0 + CA_TQ]], axis=1)
            rolled = pltpu.roll(jnp.broadcast_to(g, (CA_TQ, 2 * CA_TQ)), 0, 1, stride=1, stride_axis=0)
            blk = rolled[:, :CA_TQ]
            if j == 0:
                blk = jnp.where(row >= col, blk, NEG_INF)
            elif j == CA_NKB - 1:
                blk = jnp.where(row <= col, blk, NEG_INF)
            bias_ref[j * CA_TQ:(j + 1) * CA_TQ, t * CA_TQ:(t + 1) * CA_TQ] = blk


def _chunkattn_kernel(dist_ref, q_ref, k_ref, v_ref, wf_ref, o_ref, wb_ref, bias_ref, vt_ref, s_ref, m_ref):
    nq = q_ref.shape[1] // CA_TQ
    wb_ref[...] = wf_ref[...].astype(wb_ref.dtype)

    @pl.when(pl.program_id(1) == 0)
    def _():
        for pp in range(CA_PAIRS):
            _build_bias_table(dist_ref.at[pp], bias_ref.at[pp])

    def key_blocks(blk):
        return [(j, blk - (CA_NKB - 1) + j) for j in range(CA_NKB) if blk - (CA_NKB - 1) + j >= 0]

    def visible_tiles(j):
        tiles = [(rh, ct) for rh in range(CA_TQ // LANES) for ct in range(2 * CA_TQ // LANES)]
        if j == 0:
            return [(rh, ct) for rh, ct in tiles if not (rh == 0 and ct % 2 == 1)]
        if j == CA_NKB - 1:
            return [(rh, ct) for rh, ct in tiles if not (rh == 1 and ct % 2 == 0)]
        return tiles

    def scores(pp, blk, slot):
        cols = slice(pp * LANES, (pp + 1) * LANES)
        qq = _stack_halves(q_ref[0, blk * CA_TQ:(blk + 1) * CA_TQ, cols])
        m = [None] * (2 * CA_TQ // LANES)
        for j, kb in key_blocks(blk):
            s = _dot_nt(k_ref[0, kb * CA_TQ:(kb + 1) * CA_TQ, cols], qq)
            for rh, ct in visible_tiles(j):
                rows = slice(j * CA_TQ + rh * LANES, j * CA_TQ + (rh + 1) * LANES)
                lanes = slice(ct * LANES, (ct + 1) * LANES)
                st = s[rh * LANES:(rh + 1) * LANES, lanes] + bias_ref[pp, rows, lanes]
                s_ref[slot, rows, lanes] = st
                mt = jnp.max(st, axis=0, keepdims=True)
                m[ct] = mt if m[ct] is None else jnp.maximum(m[ct], mt)
            yield
        m_ref[slot] = jnp.concatenate(m, axis=1)

    def output(pp, blk, slot):
        m = m_ref[slot]
        acc = None
        for j, kb in key_blocks(blk):
            vis = visible_tiles(j)
            p_rows = []
            for rh in range(CA_TQ // LANES):
                p_tiles = []
                for ct in range(2 * CA_TQ // LANES):
                    if (rh, ct) in vis:
                        rows = slice(j * CA_TQ + rh * LANES, j * CA_TQ + (rh + 1) * LANES)
                        lanes = slice(ct * LANES, (ct + 1) * LANES)
                        p_tiles.append(jnp.exp2(s_ref[slot, rows, lanes] - m[:, lanes]).astype(BF16))
                    else:
                        p_tiles.append(jnp.zeros((LANES, LANES), BF16))
                p_rows.append(jnp.concatenate(p_tiles, axis=1))
            p = jnp.concatenate(p_rows, axis=0)
            part = _dot(vt_ref[pp, :, kb * CA_TQ:(kb + 1) * CA_TQ], p)
            acc = part if acc is None else acc + part
            yield
        inv_l = 1.0 / acc[LANES:LANES + 1, :]
        o_t = jnp.concatenate([acc[:HEAD_DIM, :CA_TQ] * inv_l[:, :CA_TQ],
                               acc[HEAD_DIM:LANES, CA_TQ:] * inv_l[:, CA_TQ:]], axis=0)
        o_ref[0, blk * CA_TQ:(blk + 1) * CA_TQ, pp * LANES:(pp + 1) * LANES] = o_t.T.astype(o_ref.dtype)

    for pp in range(CA_PAIRS):
        _fill_vt_ones(vt_ref.at[pp], v_ref[0, :, pp * LANES:(pp + 1) * LANES])
    stages = [(pp, blk) for pp in range(CA_PAIRS) for blk in range(nq)]
    ahead = CA_SLOTS - 1
    for i in range(ahead):
        _interleave(scores(*stages[i], i % CA_SLOTS))
    for i, stage in enumerate(stages):
        nxt = [scores(*stages[i + ahead], (i + ahead) % CA_SLOTS)] if i + ahead < len(stages) else []
        _interleave(*nxt, output(*stage, i % CA_SLOTS))


def _chunkattn_call(bias, h3, w_cast):
    B, S, _ = h3.shape
    ngroups = N_HEADS_CHUNK // 2 // CA_PAIRS
    width = CA_PAIRS * LANES
    wrows, wcols = w_cast.shape
    wspec = pl.BlockSpec((wrows // (ngroups * B), wcols), lambda p, b: (p * B + b, 0))
    return pl.pallas_call(
        _chunkattn_kernel,
        grid=(ngroups, B),
        in_specs=[
            pl.BlockSpec((CA_PAIRS, 2, (CA_NKB + 1) * CA_TQ), lambda p, b: (p, 0, 0)),
            pl.BlockSpec((1, S, width), lambda p, b: (b, 0, QB_BLK // CA_PAIRS + p)),
            pl.BlockSpec((1, S, width), lambda p, b: (b, 0, KB_BLK // CA_PAIRS + p)),
            pl.BlockSpec((1, S, width), lambda p, b: (b, 0, VB_BLK // CA_PAIRS + p)),
            wspec,
        ],
        out_specs=[pl.BlockSpec((1, S, width), lambda p, b: (b, 0, p)), wspec],
        out_shape=[jax.ShapeDtypeStruct((B, S, N_HEADS_CHUNK * HEAD_DIM), BF16),
                   jax.ShapeDtypeStruct(w_cast.shape, BF16)],
        scratch_shapes=[pltpu.VMEM((CA_PAIRS, CA_NKB * CA_TQ, 2 * CA_TQ), F32),
                        pltpu.VMEM((CA_PAIRS, LANES + VT_ONES, S), BF16),
                        pltpu.VMEM((CA_SLOTS, CA_NKB * CA_TQ, 2 * CA_TQ), F32),
                        pltpu.VMEM((CA_SLOTS, 1, 2 * CA_TQ), F32)],
        compiler_params=_params(("arbitrary", "arbitrary")),
        name="chunkattn",
    )(bias, h3, h3, h3, w_cast)


def _mid_kernel(x_ref, ya_ref, yb_ref, mem_ref, wo_ref, wq_ref, wk_ref, wv_ref, wmo_ref,
                g1_ref, b1_ref, g2_ref, b2_ref, x2_ref, k_ref, v_ref, x1_ref):
    hd = MEM_HEAD_DIM
    half = ya_ref.shape[2]

    @pl.when(pl.program_id(1) == 0)
    def _():
        mb = mem_ref[0].astype(BF16)
        k_ref[...] = _dot(mb, wk_ref[...]).astype(BF16)
        v_ref[...] = _dot(mb, wv_ref[...]).astype(BF16)

    def sub_tile(rows):
        y = _dot(ya_ref[0, rows, :], wo_ref[0:half, :]) + _dot(yb_ref[0, rows, :], wo_ref[half:, :])
        yield
        x1 = _layer_norm(DEEPNORM_ALPHA * x_ref[0, rows, :] + y, g1_ref[...], b1_ref[...])
        x1_ref[rows, :] = x1
        q = (_dot(x1.astype(BF16), wq_ref[...]) * (hd ** -0.5 * LOG2E)).astype(BF16)
        yield
        outs = []
        for h in range(N_HEADS_MEM):
            sl = slice(h * hd, (h + 1) * hd)
            s = _dot_nt(q[:, sl], k_ref[:, sl])
            p = jnp.exp2(s - jnp.max(s, axis=1, keepdims=True))
            l = jnp.sum(p, axis=1, keepdims=True)
            outs.append((_dot(p.astype(BF16), v_ref[:, sl]) / l).astype(BF16))
            yield
        o = jnp.concatenate(outs, axis=1)
        y2 = _dot(o, wmo_ref[...])
        yield
        x2_ref[0, rows, :] = _layer_norm(DEEPNORM_ALPHA * x1_ref[rows, :] + y2, g2_ref[...], b2_ref[...])

    _wavefront([sub_tile(rows) for rows in _sub_tiles(x_ref.shape[1], 256)], lag=0)


def _mid_call(x, ya, yb, mem, w_o, w_mq, w_mk, w_mv, w_mo, g1, b1, g2, b2, tm):
    B, S, _ = x.shape
    M = mem.shape[1]
    half = ya.shape[2]
    row = pl.BlockSpec((1, tm, D_MODEL), lambda bb, i: (bb, i, 0))
    hrow = pl.BlockSpec((1, tm, half), lambda bb, i: (bb, i, 0))
    memblk = pl.BlockSpec((1, M, D_MODEL), lambda bb, i: (bb, 0, 0))
    vec = pl.BlockSpec((1, D_MODEL), lambda bb, i: (0, 0))
    wspec = _resident((D_MODEL, D_MODEL))
    return pl.pallas_call(
        _mid_kernel,
        grid=(B, S // tm),
        in_specs=[row, hrow, hrow, memblk] + [wspec] * 5 + [vec] * 4,
        out_specs=row,
        out_shape=jax.ShapeDtypeStruct((B, S, D_MODEL), F32),
        scratch_shapes=[pltpu.VMEM((M, D_MODEL), BF16)] * 2 + [pltpu.VMEM((tm, D_MODEL), F32)],
        compiler_params=_params(("arbitrary", "arbitrary")),
        name="mid",
    )(x, ya, yb, mem, w_o, w_mq, w_mk, w_mv, w_mo, g1, b1, g2, b2)


def _mlp_kernel(x_ref, wu_ref, wd_ref, g_ref, b_ref, o_ref, h_ref, *, ff_chunk):
    def finish(rows, y):
        o_ref[rows, :] = _layer_norm(DEEPNORM_ALPHA * x_ref[rows, :] + y, g_ref[...], b_ref[...])

    pending = None
    for rows in _sub_tiles(x_ref.shape[0], 256):
        xb = x_ref[rows, :].astype(BF16)
        for c in range(D_FF // ff_chunk):
            sl = slice(c * ff_chunk, (c + 1) * ff_chunk)
            h = jnp.maximum(_dot(xb, wu_ref[:, sl]), 0.0)
            h_ref[rows, sl] = (h * h).astype(BF16)
            if c == 0 and pending is not None:
                prev = pending
                finish(prev, _dot(h_ref[prev, :], wd_ref[...]))
                pending = None
        pending = rows
    finish(pending, _dot(h_ref[pending, :], wd_ref[...]))


def _mlp_call(x2d, w_up, w_down, g, b, tm, ff_chunk):
    T = x2d.shape[0]
    row = pl.BlockSpec((tm, D_MODEL), lambda i: (i, 0))
    vec = pl.BlockSpec((1, D_MODEL), lambda i: (0, 0))
    return pl.pallas_call(
        functools.partial(_mlp_kernel, ff_chunk=ff_chunk),
        grid=(T // tm,),
        in_specs=[row,
                  pl.BlockSpec((D_MODEL, D_FF), lambda i: (0, 0), pipeline_mode=pl.Buffered(1)),
                  pl.BlockSpec((D_FF, D_MODEL), lambda i: (0, 0), pipeline_mode=pl.Buffered(1)),
                  vec, vec],
        out_specs=row,
        out_shape=jax.ShapeDtypeStruct((T, D_MODEL), F32),
        scratch_shapes=[pltpu.VMEM((tm, D_FF), BF16)],
        compiler_params=_params(("arbitrary",)),
        name="mlp",
    )(x2d, w_up, w_down, g, b)


def kernel(x, mem, positions, w_in, diff_lambda, subln_g, rel_bias, w_o, ln1_g, ln1_b,
           w_mq, w_mk, w_mv, w_mo, ln2_g, ln2_b, w_up, w_down, ln3_g, ln3_b):
    B, S, D = x.shape
    T = B * S
    depth = w_in.shape[0]
    assert depth == DEPTH and D == D_MODEL and S % CA_TQ == 0
    inv_freq = 1.0 / (ROPE_THETA ** (jnp.arange(0, HEAD_DIM, 2, dtype=F32) / HEAD_DIM))
    invf = jnp.tile(inv_freq, LANES // (HEAD_DIM // 2)).reshape(1, LANES)
    pos = positions.reshape(T // IN_TM, ROPE_PACK, IN_TM // ROPE_PACK).transpose(0, 2, 1)
    pos = jnp.repeat(pos, HEAD_DIM // 2, axis=2).reshape(T // ROPE_PACK, LANES)
    vec = lambda a: a.reshape(1, -1)

    for l in range(depth):
        lambda_init = 0.8 - 0.6 * math.exp(-0.3 * l)
        h, mid_w = _inproj_call(pos, invf, x.reshape(T, D), w_in[l],
                                [w_o[l], w_mq[l], w_mk[l], w_mv[l], w_mo[l]], tm=IN_TM)
        h3 = h.reshape(B, S, IN_WIDTH)
        ya, w_up_b = _diffattn_call(diff_lambda[l], vec(subln_g[l]), h3, w_up[l], tq=DA_TQ,
                                    lambda_init=lambda_init)
        yb, w_down_b = _chunkattn_call(_chunk_bias_table(rel_bias[l]), h3, w_down[l])
        x2 = _mid_call(x, ya, yb, mem, *mid_w, vec(ln1_g[l]), vec(ln1_b[l]), vec(ln2_g[l]), vec(ln2_b[l]),
                       tm=1024)
        out = _mlp_call(x2.reshape(T, D), w_up_b, w_down_b,
                        vec(ln3_g[l]), vec(ln3_b[l]), tm=1024, ff_chunk=1024)
        x = out.reshape(B, S, D)
    return x
```
